```python
import math
import jax
import jax.numpy as jnp
from jax import lax
import numpy as np

D_MODEL = 2048
BATCH = 16
SEQ = 2048
DEPTH = 2

CONV_WIDTH = 3
CONV_CHANNELS = D_MODEL // 2
RET_HEADS = 4
RET_HEAD_DIM = (D_MODEL // 2) // RET_HEADS
RET_CHUNK = 128
ATT_HEADS = 16
ATT_HEAD_DIM = D_MODEL // ATT_HEADS
IDX_HEADS = 16
IDX_DIM = 64
DSA_TOPK_MAX = 256
Q_BLOCK = 128
N_BUCKETS = 32
MAX_DISTANCE = 128
N_EXPERTS = 16
N_GROUPS = 4
EXPERTS_PER_GROUP = N_EXPERTS // N_GROUPS
TOP_K = 2
D_EXPERT = D_MODEL // 2
MOE_BLOCK = 128
RMS_EPS = 1e-6

kernel_name = 'hybrid_conv_retention_dsa_moe_block'


def rms_norm(x):
    xf = x.astype(jnp.float32)
    return (xf * lax.rsqrt(jnp.mean(xf * xf, axis=-1, keepdims=True) + RMS_EPS)).astype(x.dtype)


def ada_pre(x, cond, norm_w, ada_w, ada_b):
    mod = jax.nn.silu(cond) @ ada_w + ada_b
    shift, scale, gate = jnp.split(mod, 3, axis=-1)
    h = rms_norm(x) * norm_w * (1.0 + scale[:, None, :]) + shift[:, None, :]
    return h, gate[:, None, :]


def rotary(x, pos):
    half = x.shape[-1] // 2
    inv = 1.0 / (10000.0 ** jnp.linspace(0.0, 1.0, half, dtype=jnp.float32))
    ang = pos.astype(jnp.float32)[:, None] * inv[None, :]
    cos = jnp.cos(ang)[None, :, None, :]
    sin = jnp.sin(ang)[None, :, None, :]
    x1, x2 = x[..., :half], x[..., half:]
    return jnp.concatenate([x1 * cos - x2 * sin, x1 * sin + x2 * cos], axis=-1).astype(x.dtype)


def retention_chunkwise(q, k, v):
    bsz, seq, nh, dk = q.shape
    dv = v.shape[-1]
    nc = seq // RET_CHUNK
    log_g = jnp.log(1.0 - 2.0 ** (-5.0 - jnp.arange(nh, dtype=jnp.float32)))
    i = jnp.arange(RET_CHUNK, dtype=jnp.float32)
    diff = i[:, None] - i[None, :]
    d_intra = jnp.where(diff >= 0, jnp.exp(log_g[:, None, None] * jnp.maximum(diff, 0.0)), 0.0)
    d_cross = jnp.exp(log_g[:, None] * (i[None, :] + 1.0))
    d_state = jnp.exp(log_g[:, None] * (RET_CHUNK - 1.0 - i[None, :]))
    g_chunk = jnp.exp(log_g * RET_CHUNK)

    def chunks(a):
        return a.reshape(bsz, nc, RET_CHUNK, nh, a.shape[-1]).transpose(1, 0, 3, 2, 4)

    def step(state, inp):
        qc, kc, vc = inp
        intra = jnp.einsum('bhid,bhjd->bhij', qc, kc) * d_intra
        out = (jnp.einsum('bhij,bhje->bhie', intra, vc)
               + jnp.einsum('bhid,bhde->bhie', qc, state) * d_cross[..., None])
        state = state * g_chunk[:, None, None] + jnp.einsum('bhjd,bhje->bhde', kc * d_state[..., None], vc)
        return state, out

    state0 = jnp.zeros((bsz, nh, dk, dv), jnp.float32)
    _, out = lax.scan(step, state0, (chunks(q), chunks(k), chunks(v)))
    return out.transpose(1, 0, 3, 2, 4).reshape(bsz, seq, nh, dv).astype(q.dtype)


def mixer_conv_retention(h, w_in, conv_w, conv_b, w_out):
    bsz, seq, _ = h.shape
    a_b, a_c, a_v, r_q, r_k, r_v, r_g = jnp.split(h @ w_in, 7, axis=-1)
    u = a_c * a_v
    up = jnp.pad(u, ((0, 0), (CONV_WIDTH - 1, 0), (0, 0)))
    conv = conv_b + sum(up[:, j:j + seq] * conv_w[j] for j in range(CONV_WIDTH))
    y_a = a_b * conv
    pos = jnp.arange(seq)
    heads = lambda z: z.reshape(bsz, seq, RET_HEADS, RET_HEAD_DIM)
    q = rotary(heads(r_q), pos)
    k = rotary(heads(r_k), pos) * (RET_HEAD_DIM ** -0.5)
    o = retention_chunkwise(q, k, heads(r_v))
    y_b = (rms_norm(o) * jax.nn.silu(heads(r_g))).reshape(bsz, seq, -1)
    return jnp.concatenate([y_a, y_b], axis=-1) @ w_out


def t5_bucket(n):
    max_exact = N_BUCKETS // 2
    nf = jnp.maximum(n, 1).astype(jnp.float32)
    large = max_exact + (jnp.log(nf / max_exact) / math.log(MAX_DISTANCE / max_exact)
                         * (N_BUCKETS - max_exact)).astype(jnp.int32)
    large = jnp.minimum(large, N_BUCKETS - 1)
    return jnp.where(n < max_exact, n, large)


def dsa_attention(q, k, v, qi, ki, wi, rel_bias):
    bsz, seq = q.shape[:2]
    topk = min(DSA_TOPK_MAX, seq // 4)
    nb = seq // Q_BLOCK
    s_pos = jnp.arange(seq)
    scale = ATT_HEAD_DIM ** -0.5

    def to_blocks(a):
        return a.reshape(bsz, nb, Q_BLOCK, *a.shape[2:]).swapaxes(0, 1)

    def one_block(args):
        qb, qib, wb, blk = args
        t_pos = blk * Q_BLOCK + jnp.arange(Q_BLOCK)
        causal = s_pos[None, :] <= t_pos[:, None]
        rel = jax.nn.relu(jnp.einsum('bthd,bsd->bths', qib, ki).astype(jnp.float32))
        score = jnp.einsum('bth,bths->bts', wb.astype(jnp.float32), rel)
        score = jnp.where(causal[None], score, -jnp.inf)
        _, idx = lax.top_k(score, topk)
        gather = jax.vmap(lambda arr, ii: arr[ii])
        k_sel = gather(k, idx)
        v_sel = gather(v, idx)
        dist = t_pos[None, :, None] - idx
        valid = dist >= 0
        bias = jnp.moveaxis(rel_bias[t5_bucket(jnp.maximum(dist, 0))], -1, 2)
        logits = jnp.einsum('bthd,btkd->bthk', qb, k_sel).astype(jnp.float32) * scale + bias
        logits = jnp.where(valid[:, :, None, :], logits, -jnp.inf)
        p = jax.nn.softmax(logits, axis=-1)
        return jnp.einsum('bthk,btkd->bthd', p.astype(v_sel.dtype), v_sel)

    out = lax.map(one_block, (to_blocks(q), to_blocks(qi), to_blocks(wi), jnp.arange(nb)))
    return out.swapaxes(0, 1).reshape(bsz, seq, ATT_HEADS, ATT_HEAD_DIM)


def mixer_sparse_attention(h, w_in, q_norm, k_norm, w_out, rel_bias):
    bsz, seq, _ = h.shape
    sizes = [ATT_HEADS * ATT_HEAD_DIM, ATT_HEAD_DIM, ATT_HEAD_DIM, IDX_HEADS * IDX_DIM, IDX_DIM, IDX_HEADS]
    cuts, acc = [], 0
    for s in sizes[:-1]:
        acc += s
        cuts.append(acc)
    q, k, v, qi, ki, wi = jnp.split(h @ w_in, cuts, axis=-1)
    q = rms_norm(q.reshape(bsz, seq, ATT_HEADS, ATT_HEAD_DIM)) * q_norm
    k = rms_norm(k) * k_norm
    qi = qi.reshape(bsz, seq, IDX_HEADS, IDX_DIM)
    o = dsa_attention(q, k, v, qi, ki, wi, rel_bias)
    return o.reshape(bsz, seq, -1) @ w_out


def moe_ffn(h, w_router, b_router, w1, w3, w2):
    bsz, seq, d = h.shape
    t = bsz * seq
    hf = h.reshape(t, d)
    probs = jax.nn.softmax((hf @ w_router).astype(jnp.float32), axis=-1)
    sel = (probs + b_router.astype(jnp.float32)).reshape(t, N_GROUPS, EXPERTS_PER_GROUP)
    group_score = lax.top_k(sel, TOP_K)[0].sum(-1)
    grp = jnp.argmax(group_score, axis=-1)
    in_grp = sel[jnp.arange(t), grp]
    local = lax.top_k(in_grp, TOP_K)[1]
    expert = grp[:, None] * EXPERTS_PER_GROUP + local
    p = jnp.take_along_axis(probs, expert, axis=1)
    gates = p / jnp.sum(p, axis=-1, keepdims=True)
    a = t * TOP_K
    flat_e = expert.reshape(a)
    flat_tok = jnp.repeat(jnp.arange(t), TOP_K)
    flat_g = gates.reshape(a)
    order = jnp.argsort(flat_e)
    e_sorted = flat_e[order]
    tok_sorted = flat_tok[order]
    counts = jnp.bincount(flat_e, length=N_EXPERTS)
    padded = (counts + MOE_BLOCK - 1) // MOE_BLOCK * MOE_BLOCK
    start = jnp.cumsum(counts) - counts
    ends = jnp.cumsum(padded)
    pstart = ends - padded
    dest = pstart[e_sorted] + jnp.arange(a) - start[e_sorted]
    n_rows = (a + MOE_BLOCK - 1) // MOE_BLOCK * MOE_BLOCK + N_EXPERTS * MOE_BLOCK
    n_blocks = n_rows // MOE_BLOCK
    row_tok = jnp.zeros((n_rows,), jnp.int32).at[dest].set(tok_sorted)
    row_used = jnp.zeros((n_rows,), hf.dtype).at[dest].set(1)
    xs = (hf[row_tok] * row_used[:, None]).reshape(n_blocks, MOE_BLOCK, d)
    block_e = jnp.minimum(jnp.searchsorted(ends, jnp.arange(n_blocks) * MOE_BLOCK, side='right'),
                          N_EXPERTS - 1)

    def block_ffn(args):
        xb, e = args
        return (jax.nn.silu(xb @ w1[e]) * (xb @ w3[e])) @ w2[e]

    ys = lax.map(block_ffn, (xs, block_e)).reshape(n_rows, d)
    out_sorted = ys[dest] * flat_g[order][:, None].astype(ys.dtype)
    out = jnp.zeros((t, d), ys.dtype).at[tok_sorted].add(out_sorted)
    return out.reshape(bsz, seq, d)


def setup_inputs(seed: int = 0) -> dict:
    key = jax.random.key(seed)
    ks = jax.random.split(key, 24)
    ne = (DEPTH + 1) // 2
    no = DEPTH // 2
    f32 = jnp.float32
    nrm = lambda k, shape, s: jax.random.normal(k, shape, f32) * s
    d = D_MODEL
    ab_cols = 3 * CONV_CHANNELS + 4 * RET_HEADS * RET_HEAD_DIM
    c_cols = ATT_HEADS * ATT_HEAD_DIM + 2 * ATT_HEAD_DIM + IDX_HEADS * IDX_DIM + IDX_DIM + IDX_HEADS
    return {
        'x': nrm(ks[0], (BATCH, SEQ, d), 1.0),
        'c': nrm(ks[1], (BATCH, d), 1.0),
        'rel_bias': nrm(ks[2], (N_BUCKETS, ATT_HEADS), 0.5),
        'w_router': nrm(ks[3], (d, N_EXPERTS), d ** -0.5),
        'b_router': nrm(ks[4], (N_EXPERTS,), 0.01),
        'norm_mix': 1.0 + nrm(ks[5], (DEPTH, d), 0.02),
        'ada_w_mix': nrm(ks[6], (DEPTH, d, 3 * d), 0.5 * d ** -0.5),
        'ada_b_mix': nrm(ks[7], (DEPTH, 3 * d), 0.02),
        'norm_ffn': 1.0 + nrm(ks[8], (DEPTH, d), 0.02),
        'ada_w_ffn': nrm(ks[9], (DEPTH, d, 3 * d), 0.5 * d ** -0.5),
        'ada_b_ffn': nrm(ks[10], (DEPTH, 3 * d), 0.02),
        'ab_w_in': nrm(ks[11], (ne, d, ab_cols), d ** -0.5),
        'ab_conv_w': nrm(ks[12], (ne, CONV_WIDTH, CONV_CHANNELS), CONV_WIDTH ** -0.5),
        'ab_conv_b': nrm(ks[13], (ne, CONV_CHANNELS), 0.02),
        'ab_w_out': nrm(ks[14], (ne, CONV_CHANNELS + RET_HEADS * RET_HEAD_DIM, d), d ** -0.5),
        'c_w_in': nrm(ks[15], (no, d, c_cols), d ** -0.5),
        'c_q_norm': 1.0 + nrm(ks[16], (no, ATT_HEAD_DIM), 0.02),
        'c_k_norm': 1.0 + nrm(ks[17], (no, ATT_HEAD_DIM), 0.02),
        'c_w_out': nrm(ks[18], (no, ATT_HEADS * ATT_HEAD_DIM, d), d ** -0.5),
        'moe_w1': nrm(ks[19], (DEPTH, N_EXPERTS, d, D_EXPERT), d ** -0.5),
        'moe_w3': nrm(ks[20], (DEPTH, N_EXPERTS, d, D_EXPERT), d ** -0.5),
        'moe_w2': nrm(ks[21], (DEPTH, N_EXPERTS, D_EXPERT, d), D_EXPERT ** -0.5),
    }


def reference(x, c, rel_bias, w_router, b_router, norm_mix, ada_w_mix, ada_b_mix,
              norm_ffn, ada_w_ffn, ada_b_ffn, ab_w_in, ab_conv_w, ab_conv_b, ab_w_out,
              c_w_in, c_q_norm, c_k_norm, c_w_out, moe_w1, moe_w3, moe_w2):
    for layer in range(DEPTH):
        i = layer // 2
        h, gate = ada_pre(x, c, norm_mix[layer], ada_w_mix[layer], ada_b_mix[layer])
        if layer % 2 == 0:
            y = mixer_conv_retention(h, ab_w_in[i], ab_conv_w[i], ab_conv_b[i], ab_w_out[i])
        else:
            y = mixer_sparse_attention(h, c_w_in[i], c_q_norm[i], c_k_norm[i], c_w_out[i], rel_bias)
        x = x + gate * y
        h, gate = ada_pre(x, c, norm_ffn[layer], ada_w_ffn[layer], ada_b_ffn[layer])
        x = x + gate * moe_ffn(h, w_router, b_router, moe_w1[layer], moe_w3[layer], moe_w2[layer])
    return x
```

```python
import functools
import math

import jax
import jax.numpy as jnp
from jax import lax
from jax.experimental import pallas as pl
from jax.experimental.pallas import tpu as pltpu

F32 = jnp.float32
BF16 = jnp.bfloat16
I32 = jnp.int32

RMS_EPS = 1e-6
CONV_WIDTH = 3
RET_HEADS = 4
ATT_HEADS = 16
IDX_HEADS = 16
IDX_DIM = 64
DSA_TOPK_MAX = 256
Q_BLOCK = 128
N_BUCKETS = 32
MAX_DISTANCE = 128
N_EXPERTS = 16
N_GROUPS = 4
EXPERTS_PER_GROUP = N_EXPERTS // N_GROUPS
RET_CHUNK = 256
MOE_ROWS = 256
LANE = 128
SUBLANE = 8
VMEM_LIMIT = 48 * 1024 * 1024
NEG_BIG = -1e30
INT_MIN = -(2 ** 31)


def _cparams(sem, vmem=VMEM_LIMIT):
    return pltpu.CompilerParams(dimension_semantics=sem, vmem_limit_bytes=vmem)


def _dot(a, b):
    return jnp.dot(a, b, preferred_element_type=F32)


def _dot_nt(a, b):
    return lax.dot_general(a, b, (((1,), (1,)), ((), ())), preferred_element_type=F32)


def _dot_tn(a, b):
    return lax.dot_general(a, b, (((0,), (0,)), ((), ())), preferred_element_type=F32)


def _silu(x):
    return x * jax.nn.sigmoid(x)


def _ada_kernel(c_ref, w_ref, b_ref, o_ref):
    cs = _silu(c_ref[...])
    o_ref[0] = _dot(cs.astype(BF16), w_ref[0].astype(BF16)) + b_ref[0]


def ada_mod(c, ada_w, ada_b):
    nl, d, n3 = ada_w.shape
    bsz = c.shape[0]
    tn = min(512, n3)
    return pl.pallas_call(
        _ada_kernel,
        grid=(nl, n3 // tn),
        in_specs=[
            pl.BlockSpec((bsz, d), lambda l, j: (0, 0)),
            pl.BlockSpec((1, d, tn), lambda l, j: (l, 0, j)),
            pl.BlockSpec((1, 1, tn), lambda l, j: (l, 0, j)),
        ],
        out_specs=pl.BlockSpec((1, bsz, tn), lambda l, j: (l, 0, j)),
        out_shape=jax.ShapeDtypeStruct((nl, bsz, n3), F32),
        compiler_params=_cparams(("parallel", "parallel")),
        name="ada_mod",
    )(c, ada_w, ada_b.reshape(nl, 1, n3))


def _modulated_norm(x, mod_ref, nw_ref):
    r = lax.rsqrt(jnp.mean(x * x, axis=-1, keepdims=True) + RMS_EPS)
    return x * r * nw_ref[...] * (1.0 + mod_ref[0, 1:2, :]) + mod_ref[0, 0:1, :]


def _inproj_kernel(x_ref, mod_ref, nw_ref, w_ref, o_ref, h_ref):
    @pl.when(pl.program_id(1) == 0)
    def _():
        h_ref[...] = _modulated_norm(x_ref[...], mod_ref, nw_ref).astype(BF16)

    o_ref[...] = _dot(h_ref[...], w_ref[...]).astype(o_ref.dtype)


def inproj(x2, mod, norm_w, w_bf, seq, tm, tn):
    t, d = x2.shape
    n = w_bf.shape[1]
    per = seq // tm
    return pl.pallas_call(
        _inproj_kernel,
        grid=(t // tm, n // tn),
        in_specs=[
            pl.BlockSpec((tm, d), lambda i, j: (i, 0)),
            pl.BlockSpec((1, 3, d), lambda i, j: (i // per, 0, 0)),
            pl.BlockSpec((1, d), lambda i, j: (0, 0)),
            pl.BlockSpec((d, tn), lambda i, j: (0, j)),
        ],
        out_specs=pl.BlockSpec((tm, tn), lambda i, j: (i, j)),
        out_shape=jax.ShapeDtypeStruct((t, n), BF16),
        scratch_shapes=[pltpu.VMEM((tm, d), BF16)],
        compiler_params=_cparams(("parallel", "arbitrary")),
        name="inproj",
    )(x2, mod, norm_w.reshape(1, d), w_bf)


def _conv_kernel(b_ref, c_ref, v_ref, w_ref, cb_ref, o_ref, u_ref, *, tm):
    s = pl.program_id(1)
    u = c_ref[...].astype(F32) * v_ref[...].astype(F32)

    @pl.when(s == 0)
    def _():
        u_ref[0:SUBLANE, :] = jnp.zeros((SUBLANE, u.shape[1]), F32)

    @pl.when(s > 0)
    def _():
        u_ref[0:SUBLANE, :] = u_ref[tm:tm + SUBLANE, :]

    u_ref[SUBLANE:SUBLANE + tm, :] = u
    conv = (cb_ref[...]
            + u_ref[SUBLANE - 2:SUBLANE - 2 + tm, :] * w_ref[0:1, :]
            + u_ref[SUBLANE - 1:SUBLANE - 1 + tm, :] * w_ref[1:2, :]
            + u * w_ref[2:3, :])
    o_ref[...] = (b_ref[...].astype(F32) * conv).astype(o_ref.dtype)


def conv_mixer(z, conv_w, conv_b, bsz, seq, tm):
    t = z.shape[0]
    ch = conv_w.shape[1]
    per = seq // tm
    row = lambda g: (lambda b, s: (b * per + s, g))
    return pl.pallas_call(
        functools.partial(_conv_kernel, tm=tm),
        grid=(bsz, per),
        in_specs=[
            pl.BlockSpec((tm, ch), row(0)),
            pl.BlockSpec((tm, ch), row(1)),
            pl.BlockSpec((tm, ch), row(2)),
            pl.BlockSpec((CONV_WIDTH, ch), lambda b, s: (0, 0)),
            pl.BlockSpec((1, ch), lambda b, s: (0, 0)),
        ],
        out_specs=pl.BlockSpec((tm, ch), lambda b, s: (b * per + s, 0)),
        out_shape=jax.ShapeDtypeStruct((t, ch), BF16),
        scratch_shapes=[pltpu.VMEM((tm + SUBLANE, ch), F32)],
        compiler_params=_cparams(("parallel", "arbitrary")),
        name="conv_mixer",
    )(z, z, z, conv_w, conv_b.reshape(1, ch))


def _retention_kernel(q_ref, k_ref, v_ref, g_ref, cos_ref, sin_ref, din_ref, dcr_ref, dst_ref,
                      o_ref, st_ref, *, nc, ck, dk):
    half = dk // 2
    st_ref[...] = jnp.zeros(st_ref.shape, F32)
    d_intra = din_ref[0]
    d_cross = dcr_ref[0]
    d_state = dst_ref[0]
    g_chunk = dcr_ref[0, ck - 1:ck, :]

    def rot(x, cos, sin):
        x1, x2 = x[:, :half], x[:, half:]
        return jnp.concatenate([x1 * cos - x2 * sin, x1 * sin + x2 * cos], axis=-1)

    def body(c, carry):
        r0 = pl.multiple_of(c * ck, ck)
        cos = cos_ref[pl.ds(r0, ck), :]
        sin = sin_ref[pl.ds(r0, ck), :]
        q = rot(q_ref[pl.ds(r0, ck), :].astype(F32), cos, sin)
        k = rot(k_ref[pl.ds(r0, ck), :].astype(F32), cos, sin) * (dk ** -0.5)
        v = v_ref[pl.ds(r0, ck), :]
        intra = _dot_nt(q.astype(BF16), k.astype(BF16)) * d_intra
        state = st_ref[...]
        o = _dot(intra.astype(BF16), v) + _dot((q * d_cross).astype(BF16), state.astype(BF16))
        st_ref[...] = state * g_chunk + _dot_tn((k * d_state).astype(BF16), v)
        r = lax.rsqrt(jnp.mean(o * o, axis=-1, keepdims=True) + RMS_EPS)
        o_ref[pl.ds(r0, ck), :] = (o * r * _silu(g_ref[pl.ds(r0, ck), :].astype(F32))).astype(o_ref.dtype)
        return carry

    lax.fori_loop(0, nc, body, 0)


def retention(z, bsz, seq, ch, col0):
    t = z.shape[0]
    nh = RET_HEADS
    dk = ch // nh
    ck = min(RET_CHUNK, seq)
    nc = seq // ck
    half = dk // 2
    pos = jnp.arange(seq, dtype=F32)
    inv = 1.0 / (10000.0 ** jnp.linspace(0.0, 1.0, half, dtype=F32))
    ang = pos[:, None] * inv[None, :]
    cos, sin = jnp.cos(ang), jnp.sin(ang)
    log_g = jnp.log(1.0 - 2.0 ** (-5.0 - jnp.arange(nh, dtype=F32)))
    i = jnp.arange(ck, dtype=F32)
    diff = i[:, None] - i[None, :]
    d_intra = jnp.where(diff >= 0, jnp.exp(log_g[:, None, None] * jnp.maximum(diff, 0.0)), 0.0)
    d_cross = jnp.exp(log_g[:, None] * (i[None, :] + 1.0))[..., None]
    d_state = jnp.exp(log_g[:, None] * (ck - 1.0 - i[None, :]))[..., None]
    cb = col0 // dk
    col = lambda g: (lambda b, h: (b, cb + g * nh + h))
    return pl.pallas_call(
        functools.partial(_retention_kernel, nc=nc, ck=ck, dk=dk),
        grid=(bsz, nh),
        in_specs=[
            pl.BlockSpec((seq, dk), col(0)),
            pl.BlockSpec((seq, dk), col(1)),
            pl.BlockSpec((seq, dk), col(2)),
            pl.BlockSpec((seq, dk), col(3)),
            pl.BlockSpec((seq, half), lambda b, h: (0, 0)),
            pl.BlockSpec((seq, half), lambda b, h: (0, 0)),
            pl.BlockSpec((1, ck, ck), lambda b, h: (h, 0, 0)),
            pl.BlockSpec((1, ck, 1), lambda b, h: (h, 0, 0)),
            pl.BlockSpec((1, ck, 1), lambda b, h: (h, 0, 0)),
        ],
        out_specs=pl.BlockSpec((seq, dk), lambda b, h: (b, h)),
        out_shape=jax.ShapeDtypeStruct((t, ch), BF16),
        scratch_shapes=[pltpu.VMEM((dk, dk), F32)],
        compiler_params=_cparams(("parallel", "parallel")),
        name="retention",
    )(z, z, z, z, cos, sin, d_intra, d_cross, d_state)


def _outproj_kernel(ya_ref, yb_ref, wa_ref, wb_ref, x_ref, mod_ref, o_ref):
    y = _dot(ya_ref[...], wa_ref[...]) + _dot(yb_ref[...], wb_ref[...])
    o_ref[...] = x_ref[...] + mod_ref[0, 2:3, :] * y


def outproj(ya, yb, cols, w_bf, x2, mod, seq, tm):
    t, d = x2.shape
    kh = w_bf.shape[0] // 2
    per = seq // tm
    return pl.pallas_call(
        _outproj_kernel,
        grid=(t // tm,),
        in_specs=[
            pl.BlockSpec((tm, kh), lambda i: (i, cols[0])),
            pl.BlockSpec((tm, kh), lambda i: (i, cols[1])),
            pl.BlockSpec((kh, d), lambda i: (0, 0)),
            pl.BlockSpec((kh, d), lambda i: (1, 0)),
            pl.BlockSpec((tm, d), lambda i: (i, 0)),
            pl.BlockSpec((1, 3, d), lambda i: (i // per, 0, 0)),
        ],
        out_specs=pl.BlockSpec((tm, d), lambda i: (i, 0)),
        out_shape=jax.ShapeDtypeStruct((t, d), F32),
        compiler_params=_cparams(("parallel",)),
        name="outproj",
    )(ya, yb, w_bf, w_bf, x2, mod)


def _router_kernel(x_ref, mod_ref, nw_ref, wr_ref, br_ref, h_ref, ri_ref, rf_ref, cnt_ref, carry_ref,
                   *, tm):
    @pl.when(pl.program_id(0) == 0)
    def _():
        carry_ref[...] = jnp.zeros(carry_ref.shape, F32)

    h = _modulated_norm(x_ref[...], mod_ref, nw_ref)
    h_ref[...] = h
    logits = _dot_nt(wr_ref[...], h.astype(BF16))
    mx = jnp.max(logits, axis=0, keepdims=True)
    ex = jnp.exp(logits - mx)
    probs = ex / jnp.sum(ex, axis=0, keepdims=True)
    sel = probs + br_ref[...]
    s = [sel[e:e + 1, :] for e in range(N_EXPERTS)]
    p = [probs[e:e + 1, :] for e in range(N_EXPERTS)]
    epg = EXPERTS_PER_GROUP

    def first_argmax(vals, exclude=None):
        best = jnp.full_like(vals[0], -jnp.inf)
        idx = jnp.zeros(vals[0].shape, I32)
        for j, vj in enumerate(vals):
            better = vj > best
            if exclude is not None:
                better = better & (exclude != j)
            idx = jnp.where(better, j, idx)
            best = jnp.where(better, vj, best)
        return idx

    gscore = []
    for g in range(N_GROUPS):
        gs = s[g * epg:(g + 1) * epg]
        best = None
        for a in range(epg):
            for b in range(a + 1, epg):
                pair = gs[a] + gs[b]
                best = pair if best is None else jnp.maximum(best, pair)
        gscore.append(best)
    grp = first_argmax(gscore)

    def pick(rows, index, n):
        out = rows[n - 1]
        for j in range(n - 2, -1, -1):
            out = jnp.where(index == j, rows[j], out)
        return out

    in_s = [pick([s[g * epg + j] for g in range(N_GROUPS)], grp, N_GROUPS) for j in range(epg)]
    in_p = [pick([p[g * epg + j] for g in range(N_GROUPS)], grp, N_GROUPS) for j in range(epg)]
    i1 = first_argmax(in_s)
    i2 = first_argmax(in_s, exclude=i1)
    p1 = pick(in_p, i1, epg)
    p2 = pick(in_p, i2, epg)
    e1 = grp * epg + i1
    e2 = grp * epg + i2
    den = p1 + p2
    g1 = p1 / den
    g2 = p2 / den

    eidx = lax.broadcasted_iota(I32, (N_EXPERTS, tm), 0)
    member = (eidx == e1) | (eidx == e2)
    member_f = jnp.where(member, 1.0, 0.0)
    before = lax.broadcasted_iota(I32, (tm, tm), 0) < lax.broadcasted_iota(I32, (tm, tm), 1)
    prefix = _dot(member_f.astype(BF16), jnp.where(before, 1.0, 0.0).astype(BF16))
    base = prefix + carry_ref[:, 0:1]
    rank1 = jnp.sum(jnp.where(eidx == e1, base, 0.0), axis=0, keepdims=True).astype(I32)
    rank2 = jnp.sum(jnp.where(eidx == e2, base, 0.0), axis=0, keepdims=True).astype(I32)
    carry_ref[...] = carry_ref[...] + jnp.sum(member_f, axis=1, keepdims=True)
    cnt_ref[...] = carry_ref[...].astype(I32)

    zi = jnp.zeros((SUBLANE - 4, tm), I32)
    ri_ref[...] = jnp.concatenate([e1, e2, rank1, rank2, zi], axis=0)
    zf = jnp.zeros((SUBLANE - 2, tm), F32)
    rf_ref[...] = jnp.concatenate([g1, g2, zf], axis=0)


def router(x2, mod, norm_w, w_router, b_router, seq, tm):
    t, d = x2.shape
    per = seq // tm
    ne = N_EXPERTS
    return pl.pallas_call(
        functools.partial(_router_kernel, tm=tm),
        grid=(t // tm,),
        in_specs=[
            pl.BlockSpec((tm, d), lambda i: (i, 0)),
            pl.BlockSpec((1, 3, d), lambda i: (i // per, 0, 0)),
            pl.BlockSpec((1, d), lambda i: (0, 0)),
            pl.BlockSpec((ne, d), lambda i: (0, 0)),
            pl.BlockSpec((ne, 1), lambda i: (0, 0)),
        ],
        out_specs=[
            pl.BlockSpec((tm, d), lambda i: (i, 0)),
            pl.BlockSpec((SUBLANE, tm), lambda i: (0, i)),
            pl.BlockSpec((SUBLANE, tm), lambda i: (0, i)),
            pl.BlockSpec((ne, LANE), lambda i: (0, 0)),
        ],
        out_shape=[
            jax.ShapeDtypeStruct((t, d), F32),
            jax.ShapeDtypeStruct((SUBLANE, t), I32),
            jax.ShapeDtypeStruct((SUBLANE, t), F32),
            jax.ShapeDtypeStruct((ne, LANE), I32),
        ],
        scratch_shapes=[pltpu.VMEM((ne, LANE), F32)],
        compiler_params=_cparams(("arbitrary",)),
        name="router",
    )(x2, mod, norm_w.reshape(1, d), w_router.T.astype(BF16), b_router.reshape(ne, 1))


def _dispatch_kernel(dest_ref, h_ref, xs_ref, sem, *, tm):
    base = pl.program_id(0) * tm

    def row_copy(r, slot):
        d = dest_ref[2 * (base + r) + slot]
        return pltpu.make_async_copy(h_ref.at[pl.ds(r, 1), :], xs_ref.at[pl.ds(d, 1), :], sem)

    def start(r, carry):
        row_copy(r, 0).start()
        row_copy(r, 1).start()
        return carry

    def wait(r, carry):
        row_copy(r, 0).wait()
        row_copy(r, 1).wait()
        return carry

    lax.fori_loop(0, tm, start, 0)
    lax.fori_loop(0, tm, wait, 0)


def dispatch(h, dest, n_rows, tm):
    t, d = h.shape
    return pl.pallas_call(
        functools.partial(_dispatch_kernel, tm=tm),
        grid_spec=pltpu.PrefetchScalarGridSpec(
            num_scalar_prefetch=1,
            grid=(t // tm,),
            in_specs=[pl.BlockSpec((tm, d), lambda i, dest: (i, 0))],
            out_specs=pl.BlockSpec(memory_space=pl.ANY),
            scratch_shapes=[pltpu.SemaphoreType.DMA(())],
        ),
        out_shape=jax.ShapeDtypeStruct((n_rows, d), F32),
        compiler_params=_cparams(("arbitrary",)),
        name="moe_dispatch",
    )(dest, h)


def _ffn_kernel(be_ref, bv_ref, xs_ref, w1_ref, w3_ref, w2_ref, ys_ref):
    valid = bv_ref[pl.program_id(0)]

    @pl.when(valid > 0)
    def _():
        rows = lax.broadcasted_iota(I32, xs_ref.shape, 0)
        x = jnp.where(rows < valid, xs_ref[...], 0.0).astype(BF16)
        h1 = _dot(x, w1_ref[0])
        h3 = _dot(x, w3_ref[0])
        a = (_silu(h1) * h3).astype(BF16)
        ys_ref[...] = _dot(a, w2_ref[0])

    @pl.when(valid <= 0)
    def _():
        ys_ref[...] = jnp.zeros(ys_ref.shape, F32)


def grouped_ffn(xs, block_e, block_valid, w1, w3, w2, bm):
    n_rows, d = xs.shape
    f = w1.shape[2]
    return pl.pallas_call(
        _ffn_kernel,
        grid_spec=pltpu.PrefetchScalarGridSpec(
            num_scalar_prefetch=2,
            grid=(n_rows // bm,),
            in_specs=[
                pl.BlockSpec((bm, d), lambda i, be, bv: (i, 0)),
                pl.BlockSpec((1, d, f), lambda i, be, bv: (be[i], 0, 0)),
                pl.BlockSpec((1, d, f), lambda i, be, bv: (be[i], 0, 0)),
                pl.BlockSpec((1, f, d), lambda i, be, bv: (be[i], 0, 0)),
            ],
            out_specs=pl.BlockSpec((bm, d), lambda i, be, bv: (i, 0)),
        ),
        out_shape=jax.ShapeDtypeStruct((n_rows, d), F32),
        compiler_params=_cparams(("arbitrary",)),
        name="moe_ffn",
    )(block_e, block_valid, xs, w1, w3, w2)


def _combine_kernel(dest_ref, ys_ref, x_ref, mod_ref, gf_ref, o_ref, a_ref, b_ref, sem, *, tm):
    base = pl.program_id(0) * tm

    def row_copy(r, slot):
        d = dest_ref[2 * (base + r) + slot]
        buf = a_ref if slot == 0 else b_ref
        return pltpu.make_async_copy(ys_ref.at[pl.ds(d, 1), :], buf.at[pl.ds(r, 1), :], sem)

    def start(r, carry):
        row_copy(r, 0).start()
        row_copy(r, 1).start()
        return carry

    def wait(r, carry):
        row_copy(r, 0).wait()
        row_copy(r, 1).wait()
        return carry

    lax.fori_loop(0, tm, start, 0)
    lax.fori_loop(0, tm, wait, 0)
    y = gf_ref[:, 0:1] * a_ref[...] + gf_ref[:, 1:2] * b_ref[...]
    o_ref[...] = x_ref[...] + mod_ref[0, 2:3, :] * y


def combine(ys, dest, x2, mod, gates_t, seq, tm):
    t, d = x2.shape
    per = seq // tm
    return pl.pallas_call(
        functools.partial(_combine_kernel, tm=tm),
        grid_spec=pltpu.PrefetchScalarGridSpec(
            num_scalar_prefetch=1,
            grid=(t // tm,),
            in_specs=[
                pl.BlockSpec(memory_space=pl.ANY),
                pl.BlockSpec((tm, d), lambda i, dest: (i, 0)),
                pl.BlockSpec((1, 3, d), lambda i, dest: (i // per, 0, 0)),
                pl.BlockSpec((tm, SUBLANE), lambda i, dest: (i, 0)),
            ],
            out_specs=pl.BlockSpec((tm, d), lambda i, dest: (i, 0)),
            scratch_shapes=[pltpu.VMEM((tm, d), F32), pltpu.VMEM((tm, d), F32),
                            pltpu.SemaphoreType.DMA(())],
        ),
        out_shape=jax.ShapeDtypeStruct((t, d), F32),
        compiler_params=_cparams(("arbitrary",)),
        name="moe_combine",
    )(dest, ys, x2, mod, gates_t)


def moe(x2, mod, norm_w, w_router, b_router, w1, w3, w2, seq):
    t, d = x2.shape
    bm = MOE_ROWS
    tm = min(256, seq)
    h, ri, rf, cnt = router(x2, mod, norm_w, w_router, b_router, seq, tm)
    counts = cnt[:, 0]
    nblk = (counts + bm - 1) // bm
    blk_end = jnp.cumsum(nblk)
    blk_start = blk_end - nblk
    e12 = ri[0:2].T
    dest = (blk_start[e12] * bm + ri[2:4].T).reshape(2 * t).astype(I32)
    n_blocks = (2 * t) // bm + N_EXPERTS
    bidx = jnp.arange(n_blocks, dtype=I32)
    block_e = jnp.minimum(jnp.searchsorted(blk_end, bidx, side='right'), N_EXPERTS - 1).astype(I32)
    block_valid = jnp.clip(counts[block_e] - (bidx - blk_start[block_e]) * bm, 0, bm).astype(I32)
    xs = dispatch(h, dest, n_blocks * bm, tm)
    ys = grouped_ffn(xs, block_e, block_valid, w1.astype(BF16), w3.astype(BF16), w2.astype(BF16), bm)
    return combine(ys, dest, x2, mod, rf.T, seq, tm)


def _t5_bucket(n):
    max_exact = N_BUCKETS // 2
    nf = jnp.maximum(n, 1).astype(F32)
    large = max_exact + (jnp.log(nf / max_exact) / math.log(MAX_DISTANCE / max_exact)
                         * (N_BUCKETS - max_exact)).astype(I32)
    large = jnp.minimum(large, N_BUCKETS - 1)
    return jnp.where(n < max_exact, n, large)


def _sortable(score):
    bits = pltpu.bitcast(score, I32)
    return bits ^ ((bits >> 31) & 0x7FFFFFFF)


def _dsa_kernel(q_ref, k_ref, v_ref, qi_ref, kw_ref, wq_ref, tb_ref, qn_ref, kn_ref, o_ref,
                khat_ref, qr_ref, qir_ref, m_ref, mt_ref, acc_ref, mx_ref, l_ref,
                *, nh, dh, nih, di, topk, seq):
    j = pl.program_id(1)
    qb = Q_BLOCK
    nkb = j + 1

    @pl.when(j == 0)
    def _():
        k = k_ref[...].astype(F32)
        r = lax.rsqrt(jnp.mean(k * k, axis=-1, keepdims=True) + RMS_EPS)
        khat_ref[...] = (k * r * kn_ref[...]).astype(BF16)

    for h in range(nh):
        qh = q_ref[:, h * dh:(h + 1) * dh].astype(F32)
        r = lax.rsqrt(jnp.mean(qh * qh, axis=-1, keepdims=True) + RMS_EPS)
        qr_ref[h * qb:(h + 1) * qb, :] = (qh * r * qn_ref[...]).astype(BF16)
    for h in range(nih):
        qir_ref[h * qb:(h + 1) * qb, :] = qi_ref[:, h * di:(h + 1) * di]
    wq = wq_ref[:, di:di + nih].astype(F32)

    row_t = j * qb + lax.broadcasted_iota(I32, (qb, qb), 0)
    col_l = lax.broadcasted_iota(I32, (qb, qb), 1)

    def score_body(kb, carry):
        r0 = pl.multiple_of(kb * qb, qb)
        ki = kw_ref[pl.ds(r0, qb), 0:di]
        rel = _dot_nt(qir_ref[...], ki)
        acc = jnp.zeros((qb, qb), F32)
        for h in range(nih):
            acc = acc + wq[:, h:h + 1] * jnp.maximum(rel[h * qb:(h + 1) * qb, :], 0.0)
        causal = (kb * qb + col_l) <= row_t
        m = jnp.where(causal, _sortable(acc), INT_MIN)
        m_ref[kb] = m
        mt_ref[kb] = m.T
        return carry

    lax.fori_loop(0, nkb, score_body, 0)

    def count_ge(cand):
        def cbody(kb, acc):
            return acc + jnp.where(mt_ref[kb] >= cand, 1, 0)
        acc = lax.fori_loop(0, nkb, cbody, jnp.zeros((qb, qb), I32))
        return jnp.sum(acc, axis=0, keepdims=True)

    def search():
        zero = jnp.zeros((1, qb), I32)
        ans0 = jnp.where(count_ge(zero) >= topk, zero, INT_MIN)

        def bit_body(bi, ans):
            cand = ans | (1 << (30 - bi))
            return jnp.where(count_ge(cand) >= topk, cand, ans)

        return lax.fori_loop(0, 31, bit_body, ans0)

    thr_row = lax.cond(nkb * qb > topk, search, lambda: jnp.full((1, qb), INT_MIN + 1, I32))
    thr = jnp.broadcast_to(thr_row, (qb, qb)).T

    mx_ref[...] = jnp.full(mx_ref.shape, NEG_BIG, F32)
    l_ref[...] = jnp.zeros(l_ref.shape, F32)
    acc_ref[...] = jnp.zeros(acc_ref.shape, F32)
    scale = dh ** -0.5

    def attn_body(kb, carry):
        r0 = pl.multiple_of(kb * qb, qb)
        kh = khat_ref[pl.ds(r0, qb), :]
        vv = v_ref[pl.ds(r0, qb), :]
        sel = m_ref[kb] >= thr
        kind = jnp.minimum(j - kb, 2)
        for h in range(nh):
            rows = slice(h * qb, (h + 1) * qb)
            lg = _dot_nt(qr_ref[rows, :], kh) * scale + tb_ref[kind, h]
            lg = jnp.where(sel, lg, NEG_BIG)
            m_old = mx_ref[rows, :]
            m_new = jnp.maximum(m_old, jnp.max(lg, axis=-1, keepdims=True))
            p = jnp.where(sel, jnp.exp(lg - m_new), 0.0)
            alpha = jnp.exp(m_old - m_new)
            l_ref[rows, :] = alpha * l_ref[rows, :] + jnp.sum(p, axis=-1, keepdims=True)
            acc_ref[rows, :] = alpha * acc_ref[rows, :] + _dot(p.astype(BF16), vv)
            mx_ref[rows, :] = m_new
        return carry

    lax.fori_loop(0, nkb, attn_body, 0)
    for h in range(nh):
        rows = slice(h * qb, (h + 1) * qb)
        o_ref[:, h * dh:(h + 1) * dh] = (acc_ref[rows, :] / l_ref[rows, :]).astype(o_ref.dtype)


def dsa_attention(z, rel_bias, q_norm, k_norm, bsz, seq, d):
    t = z.shape[0]
    nh, dh, nih, di = ATT_HEADS, d // ATT_HEADS, IDX_HEADS, IDX_DIM
    qb = Q_BLOCK
    nb = seq // qb
    topk = min(DSA_TOPK_MAX, seq // 4)
    assert topk % qb == 0 and dh == LANE and nih * di == d // 2
    i = jnp.arange(qb)
    dist = i[:, None] - i[None, :]
    tabs = [rel_bias[_t5_bucket(jnp.maximum(dist + off * qb, 0))] for off in range(2)]
    far = jnp.broadcast_to(rel_bias[_t5_bucket(jnp.full((), 2 * qb, I32))], (qb, qb, nh))
    tb = jnp.stack(tabs + [far]).transpose(0, 3, 1, 2).astype(F32)
    cq = d // LANE
    return pl.pallas_call(
        functools.partial(_dsa_kernel, nh=nh, dh=dh, nih=nih, di=di, topk=topk, seq=seq),
        grid=(bsz, nb),
        in_specs=[
            pl.BlockSpec((qb, d), lambda b, j: (b * nb + j, 0)),
            pl.BlockSpec((seq, LANE), lambda b, j: (b, cq + cq // 2)),
            pl.BlockSpec((seq, LANE), lambda b, j: (b, cq + cq // 2 + 1)),
            pl.BlockSpec((qb, d // 2), lambda b, j: (b * nb + j, 2)),
            pl.BlockSpec((seq, LANE), lambda b, j: (b, cq + cq // 2 + 2)),
            pl.BlockSpec((qb, LANE), lambda b, j: (b * nb + j, cq + cq // 2 + 2)),
            pl.BlockSpec((3, nh, qb, qb), lambda b, j: (0, 0, 0, 0)),
            pl.BlockSpec((1, dh), lambda b, j: (0, 0)),
            pl.BlockSpec((1, dh), lambda b, j: (0, 0)),
        ],
        out_specs=pl.BlockSpec((qb, d), lambda b, j: (b * nb + j, 0)),
        out_shape=jax.ShapeDtypeStruct((t, d), BF16),
        scratch_shapes=[
            pltpu.VMEM((seq, dh), BF16),
            pltpu.VMEM((nh * qb, dh), BF16),
            pltpu.VMEM((nih * qb, di), BF16),
            pltpu.VMEM((nb, qb, qb), I32),
            pltpu.VMEM((nb, qb, qb), I32),
            pltpu.VMEM((nh * qb, dh), F32),
            pltpu.VMEM((nh * qb, 1), F32),
            pltpu.VMEM((nh * qb, 1), F32),
        ],
        compiler_params=_cparams(("arbitrary", "arbitrary")),
        name="dsa_attention",
    )(z, z, z, z, z, z, tb, q_norm.reshape(1, dh), k_norm.reshape(1, dh))


def _dsa_weight(c_w_in, d):
    dh = d // ATT_HEADS
    nqi = IDX_HEADS * IDX_DIM
    q, k, v, qi, ki, wi = jnp.split(
        c_w_in, [d, d + dh, d + 2 * dh, d + 2 * dh + nqi, d + 2 * dh + nqi + IDX_DIM], axis=1)
    used = d + nqi + 2 * dh + IDX_DIM + IDX_HEADS
    total = -(-used // 512) * 512
    pad = jnp.zeros((d, total - used), c_w_in.dtype)
    return jnp.concatenate([q, qi, k, v, ki, wi, pad], axis=1).astype(BF16)


def kernel(x, c, rel_bias, w_router, b_router, norm_mix, ada_w_mix, ada_b_mix, norm_ffn, ada_w_ffn,
           ada_b_ffn, ab_w_in, ab_conv_w, ab_conv_b, ab_w_out, c_w_in, c_q_norm, c_k_norm, c_w_out,
           moe_w1, moe_w3, moe_w2):
    bsz, seq, d = x.shape
    depth = norm_mix.shape[0]
    t = bsz * seq
    ch = d // 2
    tm = min(512, seq)
    x2 = x.reshape(t, d)
    mod_mix = ada_mod(c, ada_w_mix, ada_b_mix).reshape(depth, bsz, 3, d)
    mod_ffn = ada_mod(c, ada_w_ffn, ada_b_ffn).reshape(depth, bsz, 3, d)
    for layer in range(depth):
        i = layer // 2
        if layer % 2 == 0:
            z = inproj(x2, mod_mix[layer], norm_mix[layer], ab_w_in[i].astype(BF16), seq, tm, 512)
            y_a = conv_mixer(z, ab_conv_w[i], ab_conv_b[i], bsz, seq, tm)
            y_b = retention(z, bsz, seq, ch, 3 * ch)
            x2 = outproj(y_a, y_b, (0, 0), ab_w_out[i].astype(BF16), x2, mod_mix[layer], seq, tm)
        else:
            z = inproj(x2, mod_mix[layer], norm_mix[layer], _dsa_weight(c_w_in[i], d), seq, tm, 512)
            o = dsa_attention(z, rel_bias, c_q_norm[i], c_k_norm[i], bsz, seq, d)
            x2 = outproj(o, o, (0, 1), c_w_out[i].astype(BF16), x2, mod_mix[layer], seq, tm)
        x2 = moe(x2, mod_ffn[layer], norm_ffn[layer], w_router, b_router,
                 moe_w1[layer], moe_w3[layer], moe_w2[layer], seq)
    return x2.reshape(bsz, seq, d)
```

```python
import functools
import math

import jax
import jax.numpy as jnp
from jax import lax
from jax.experimental import pallas as pl
from jax.experimental.pallas import tpu as pltpu

F32 = jnp.float32
BF16 = jnp.bfloat16
I32 = jnp.int32

RMS_EPS = 1e-6
CONV_WIDTH = 3
RET_HEADS = 4
ATT_HEADS = 16
IDX_HEADS = 16
IDX_DIM = 64
DSA_TOPK_MAX = 256
Q_BLOCK = 128
N_BUCKETS = 32
MAX_DISTANCE = 128
N_EXPERTS = 16
N_GROUPS = 4
EXPERTS_PER_GROUP = N_EXPERTS // N_GROUPS
RET_CHUNK = 256
MOE_ROWS = 256
DMA_UNROLL = 8
LANE = 128
SUBLANE = 8
VMEM_LIMIT = 48 * 1024 * 1024
NEG_BIG = -1e30
INT_MIN = -(2 ** 31)


def _cparams(sem, vmem=VMEM_LIMIT):
    return pltpu.CompilerParams(dimension_semantics=sem, vmem_limit_bytes=vmem)


def _dot(a, b):
    return jnp.dot(a, b, preferred_element_type=F32)


def _dot_nt(a, b):
    return lax.dot_general(a, b, (((1,), (1,)), ((), ())), preferred_element_type=F32)


def _dot_tn(a, b):
    return lax.dot_general(a, b, (((0,), (0,)), ((), ())), preferred_element_type=F32)


def _silu(x):
    return x * jax.nn.sigmoid(x)


def _ada_kernel(c_ref, w_ref, b_ref, o_ref):
    cs = _silu(c_ref[...])
    o_ref[0] = _dot(cs.astype(BF16), w_ref[0].astype(BF16)) + b_ref[0]


def ada_mod(c, ada_w, ada_b):
    nl, d, n3 = ada_w.shape
    bsz = c.shape[0]
    tn = min(512, n3)
    return pl.pallas_call(
        _ada_kernel,
        grid=(nl, n3 // tn),
        in_specs=[
            pl.BlockSpec((bsz, d), lambda l, j: (0, 0)),
            pl.BlockSpec((1, d, tn), lambda l, j: (l, 0, j)),
            pl.BlockSpec((1, 1, tn), lambda l, j: (l, 0, j)),
        ],
        out_specs=pl.BlockSpec((1, bsz, tn), lambda l, j: (l, 0, j)),
        out_shape=jax.ShapeDtypeStruct((nl, bsz, n3), F32),
        compiler_params=_cparams(("parallel", "parallel")),
        name="ada_mod",
    )(c, ada_w, ada_b.reshape(nl, 1, n3))


def _modulated_norm(x, mod_ref, nw_ref):
    r = lax.rsqrt(jnp.mean(x * x, axis=-1, keepdims=True) + RMS_EPS)
    return x * r * nw_ref[...] * (1.0 + mod_ref[0, 1:2, :]) + mod_ref[0, 0:1, :]


def _inproj_kernel(x_ref, mod_ref, nw_ref, w_ref, o_ref, h_ref):
    @pl.when(pl.program_id(1) == 0)
    def _():
        h_ref[...] = _modulated_norm(x_ref[...], mod_ref, nw_ref).astype(BF16)

    o_ref[...] = _dot(h_ref[...], w_ref[...]).astype(o_ref.dtype)


def inproj(x2, mod, norm_w, w_bf, seq, tm, tn):
    t, d = x2.shape
    n = w_bf.shape[1]
    per = seq // tm
    return pl.pallas_call(
        _inproj_kernel,
        grid=(t // tm, n // tn),
        in_specs=[
            pl.BlockSpec((tm, d), lambda i, j: (i, 0)),
            pl.BlockSpec((1, 3, d), lambda i, j: (i // per, 0, 0)),
            pl.BlockSpec((1, d), lambda i, j: (0, 0)),
            pl.BlockSpec((d, tn), lambda i, j: (0, j)),
        ],
        out_specs=pl.BlockSpec((tm, tn), lambda i, j: (i, j)),
        out_shape=jax.ShapeDtypeStruct((t, n), BF16),
        scratch_shapes=[pltpu.VMEM((tm, d), BF16)],
        compiler_params=_cparams(("parallel", "arbitrary")),
        name="inproj",
    )(x2, mod, norm_w.reshape(1, d), w_bf)


def _conv_kernel(b_ref, c_ref, v_ref, w_ref, cb_ref, o_ref, u_ref, *, tm):
    s = pl.program_id(1)
    u = c_ref[...].astype(F32) * v_ref[...].astype(F32)

    @pl.when(s == 0)
    def _():
        u_ref[0:SUBLANE, :] = jnp.zeros((SUBLANE, u.shape[1]), F32)

    @pl.when(s > 0)
    def _():
        u_ref[0:SUBLANE, :] = u_ref[tm:tm + SUBLANE, :]

    u_ref[SUBLANE:SUBLANE + tm, :] = u
    conv = (cb_ref[...]
            + u_ref[SUBLANE - 2:SUBLANE - 2 + tm, :] * w_ref[0:1, :]
            + u_ref[SUBLANE - 1:SUBLANE - 1 + tm, :] * w_ref[1:2, :]
            + u * w_ref[2:3, :])
    o_ref[...] = (b_ref[...].astype(F32) * conv).astype(o_ref.dtype)


def conv_mixer(z, conv_w, conv_b, bsz, seq, tm):
    t = z.shape[0]
    ch = conv_w.shape[1]
    per = seq // tm
    row = lambda g: (lambda b, s: (b * per + s, g))
    return pl.pallas_call(
        functools.partial(_conv_kernel, tm=tm),
        grid=(bsz, per),
        in_specs=[
            pl.BlockSpec((tm, ch), row(0)),
            pl.BlockSpec((tm, ch), row(1)),
            pl.BlockSpec((tm, ch), row(2)),
            pl.BlockSpec((CONV_WIDTH, ch), lambda b, s: (0, 0)),
            pl.BlockSpec((1, ch), lambda b, s: (0, 0)),
        ],
        out_specs=pl.BlockSpec((tm, ch), lambda b, s: (b * per + s, 0)),
        out_shape=jax.ShapeDtypeStruct((t, ch), BF16),
        scratch_shapes=[pltpu.VMEM((tm + SUBLANE, ch), F32)],
        compiler_params=_cparams(("parallel", "arbitrary")),
        name="conv_mixer",
    )(z, z, z, conv_w, conv_b.reshape(1, ch))


def _retention_kernel(q_ref, k_ref, v_ref, g_ref, cos_ref, sin_ref, din_ref, dcr_ref, dst_ref,
                      o_ref, st_ref, *, nc, ck, dk):
    half = dk // 2
    st_ref[...] = jnp.zeros(st_ref.shape, F32)
    d_intra = din_ref[0]
    d_cross = dcr_ref[0]
    d_state = dst_ref[0]
    g_chunk = dcr_ref[0, ck - 1:ck, :]

    def rot(x, cos, sin):
        x1, x2 = x[:, :half], x[:, half:]
        return jnp.concatenate([x1 * cos - x2 * sin, x1 * sin + x2 * cos], axis=-1)

    def body(c, carry):
        r0 = pl.multiple_of(c * ck, ck)
        cos = cos_ref[pl.ds(r0, ck), :]
        sin = sin_ref[pl.ds(r0, ck), :]
        q = rot(q_ref[pl.ds(r0, ck), :].astype(F32), cos, sin)
        k = rot(k_ref[pl.ds(r0, ck), :].astype(F32), cos, sin) * (dk ** -0.5)
        v = v_ref[pl.ds(r0, ck), :]
        intra = _dot_nt(q.astype(BF16), k.astype(BF16)) * d_intra
        state = st_ref[...]
        o = _dot(intra.astype(BF16), v) + _dot((q * d_cross).astype(BF16), state.astype(BF16))
        st_ref[...] = state * g_chunk + _dot_tn((k * d_state).astype(BF16), v)
        r = lax.rsqrt(jnp.mean(o * o, axis=-1, keepdims=True) + RMS_EPS)
        o_ref[pl.ds(r0, ck), :] = (o * r * _silu(g_ref[pl.ds(r0, ck), :].astype(F32))).astype(o_ref.dtype)
        return carry

    lax.fori_loop(0, nc, body, 0)


def retention(z, bsz, seq, ch, col0):
    t = z.shape[0]
    nh = RET_HEADS
    dk = ch // nh
    ck = min(RET_CHUNK, seq)
    nc = seq // ck
    half = dk // 2
    pos = jnp.arange(seq, dtype=F32)
    inv = 1.0 / (10000.0 ** jnp.linspace(0.0, 1.0, half, dtype=F32))
    ang = pos[:, None] * inv[None, :]
    cos, sin = jnp.cos(ang), jnp.sin(ang)
    log_g = jnp.log(1.0 - 2.0 ** (-5.0 - jnp.arange(nh, dtype=F32)))
    i = jnp.arange(ck, dtype=F32)
    diff = i[:, None] - i[None, :]
    d_intra = jnp.where(diff >= 0, jnp.exp(log_g[:, None, None] * jnp.maximum(diff, 0.0)), 0.0)
    d_cross = jnp.exp(log_g[:, None] * (i[None, :] + 1.0))[..., None]
    d_state = jnp.exp(log_g[:, None] * (ck - 1.0 - i[None, :]))[..., None]
    cb = col0 // dk
    col = lambda g: (lambda b, h: (b, cb + g * nh + h))
    return pl.pallas_call(
        functools.partial(_retention_kernel, nc=nc, ck=ck, dk=dk),
        grid=(bsz, nh),
        in_specs=[
            pl.BlockSpec((seq, dk), col(0)),
            pl.BlockSpec((seq, dk), col(1)),
            pl.BlockSpec((seq, dk), col(2)),
            pl.BlockSpec((seq, dk), col(3)),
            pl.BlockSpec((seq, half), lambda b, h: (0, 0)),
            pl.BlockSpec((seq, half), lambda b, h: (0, 0)),
            pl.BlockSpec((1, ck, ck), lambda b, h: (h, 0, 0)),
            pl.BlockSpec((1, ck, 1), lambda b, h: (h, 0, 0)),
            pl.BlockSpec((1, ck, 1), lambda b, h: (h, 0, 0)),
        ],
        out_specs=pl.BlockSpec((seq, dk), lambda b, h: (b, h)),
        out_shape=jax.ShapeDtypeStruct((t, ch), BF16),
        scratch_shapes=[pltpu.VMEM((dk, dk), F32)],
        compiler_params=_cparams(("parallel", "parallel")),
        name="retention",
    )(z, z, z, z, cos, sin, d_intra, d_cross, d_state)


def _outproj_kernel(ya_ref, yb_ref, wa_ref, wb_ref, x_ref, mod_ref, o_ref):
    y = _dot(ya_ref[...], wa_ref[...]) + _dot(yb_ref[...], wb_ref[...])
    o_ref[...] = x_ref[...] + mod_ref[0, 2:3, :] * y


def outproj(ya, yb, cols, w_bf, x2, mod, seq, tm):
    t, d = x2.shape
    kh = w_bf.shape[0] // 2
    per = seq // tm
    return pl.pallas_call(
        _outproj_kernel,
        grid=(t // tm,),
        in_specs=[
            pl.BlockSpec((tm, kh), lambda i: (i, cols[0])),
            pl.BlockSpec((tm, kh), lambda i: (i, cols[1])),
            pl.BlockSpec((kh, d), lambda i: (0, 0)),
            pl.BlockSpec((kh, d), lambda i: (1, 0)),
            pl.BlockSpec((tm, d), lambda i: (i, 0)),
            pl.BlockSpec((1, 3, d), lambda i: (i // per, 0, 0)),
        ],
        out_specs=pl.BlockSpec((tm, d), lambda i: (i, 0)),
        out_shape=jax.ShapeDtypeStruct((t, d), F32),
        compiler_params=_cparams(("parallel",)),
        name="outproj",
    )(ya, yb, w_bf, w_bf, x2, mod)


def _router_kernel(x_ref, mod_ref, nw_ref, wr_ref, br_ref, h_ref, ri_ref, rf_ref, cnt_ref, carry_ref,
                   *, tm):
    @pl.when(pl.program_id(0) == 0)
    def _():
        carry_ref[...] = jnp.zeros(carry_ref.shape, F32)

    h = _modulated_norm(x_ref[...], mod_ref, nw_ref)
    h_ref[...] = h
    logits = _dot_nt(wr_ref[...], h.astype(BF16))
    mx = jnp.max(logits, axis=0, keepdims=True)
    ex = jnp.exp(logits - mx)
    probs = ex / jnp.sum(ex, axis=0, keepdims=True)
    sel = probs + br_ref[...]
    s = [sel[e:e + 1, :] for e in range(N_EXPERTS)]
    p = [probs[e:e + 1, :] for e in range(N_EXPERTS)]
    epg = EXPERTS_PER_GROUP

    def first_argmax(vals, exclude=None):
        best = jnp.full_like(vals[0], -jnp.inf)
        idx = jnp.zeros(vals[0].shape, I32)
        for j, vj in enumerate(vals):
            better = vj > best
            if exclude is not None:
                better = better & (exclude != j)
            idx = jnp.where(better, j, idx)
            best = jnp.where(better, vj, best)
        return idx

    gscore = []
    for g in range(N_GROUPS):
        gs = s[g * epg:(g + 1) * epg]
        best = None
        for a in range(epg):
            for b in range(a + 1, epg):
                pair = gs[a] + gs[b]
                best = pair if best is None else jnp.maximum(best, pair)
        gscore.append(best)
    grp = first_argmax(gscore)

    def pick(rows, index, n):
        out = rows[n - 1]
        for j in range(n - 2, -1, -1):
            out = jnp.where(index == j, rows[j], out)
        return out

    in_s = [pick([s[g * epg + j] for g in range(N_GROUPS)], grp, N_GROUPS) for j in range(epg)]
    in_p = [pick([p[g * epg + j] for g in range(N_GROUPS)], grp, N_GROUPS) for j in range(epg)]
    i1 = first_argmax(in_s)
    i2 = first_argmax(in_s, exclude=i1)
    p1 = pick(in_p, i1, epg)
    p2 = pick(in_p, i2, epg)
    e1 = grp * epg + i1
    e2 = grp * epg + i2
    den = p1 + p2
    g1 = p1 / den
    g2 = p2 / den

    eidx = lax.broadcasted_iota(I32, (N_EXPERTS, tm), 0)
    member = (eidx == e1) | (eidx == e2)
    member_f = jnp.where(member, 1.0, 0.0)
    before = lax.broadcasted_iota(I32, (tm, tm), 0) < lax.broadcasted_iota(I32, (tm, tm), 1)
    prefix = _dot(member_f.astype(BF16), jnp.where(before, 1.0, 0.0).astype(BF16))
    base = prefix + carry_ref[:, 0:1]
    rank1 = jnp.sum(jnp.where(eidx == e1, base, 0.0), axis=0, keepdims=True).astype(I32)
    rank2 = jnp.sum(jnp.where(eidx == e2, base, 0.0), axis=0, keepdims=True).astype(I32)
    carry_ref[...] = carry_ref[...] + jnp.sum(member_f, axis=1, keepdims=True)
    cnt_ref[...] = carry_ref[...].astype(I32)

    zi = jnp.zeros((SUBLANE - 4, tm), I32)
    ri_ref[...] = jnp.concatenate([e1, e2, rank1, rank2, zi], axis=0)
    zf = jnp.zeros((SUBLANE - 2, tm), F32)
    rf_ref[...] = jnp.concatenate([g1, g2, zf], axis=0)


def router(x2, mod, norm_w, w_router, b_router, seq, tm):
    t, d = x2.shape
    per = seq // tm
    ne = N_EXPERTS
    return pl.pallas_call(
        functools.partial(_router_kernel, tm=tm),
        grid=(t // tm,),
        in_specs=[
            pl.BlockSpec((tm, d), lambda i: (i, 0)),
            pl.BlockSpec((1, 3, d), lambda i: (i // per, 0, 0)),
            pl.BlockSpec((1, d), lambda i: (0, 0)),
            pl.BlockSpec((ne, d), lambda i: (0, 0)),
            pl.BlockSpec((ne, 1), lambda i: (0, 0)),
        ],
        out_specs=[
            pl.BlockSpec((tm, d), lambda i: (i, 0)),
            pl.BlockSpec((SUBLANE, tm), lambda i: (0, i)),
            pl.BlockSpec((SUBLANE, tm), lambda i: (0, i)),
            pl.BlockSpec((ne, LANE), lambda i: (0, 0)),
        ],
        out_shape=[
            jax.ShapeDtypeStruct((t, d), F32),
            jax.ShapeDtypeStruct((SUBLANE, t), I32),
            jax.ShapeDtypeStruct((SUBLANE, t), F32),
            jax.ShapeDtypeStruct((ne, LANE), I32),
        ],
        scratch_shapes=[pltpu.VMEM((ne, LANE), F32)],
        compiler_params=_cparams(("arbitrary",)),
        name="router",
    )(x2, mod, norm_w.reshape(1, d), w_router.T.astype(BF16), b_router.reshape(ne, 1))


def _dispatch_kernel(dest_ref, h_ref, xs_ref, sem, *, tm):
    base = pl.program_id(0) * tm

    def row_copy(r, slot):
        d = dest_ref[2 * (base + r) + slot]
        return pltpu.make_async_copy(h_ref.at[pl.ds(r, 1), :], xs_ref.at[pl.ds(d, 1), :], sem)

    def start(r, carry):
        row_copy(r, 0).start()
        row_copy(r, 1).start()
        return carry

    lax.fori_loop(0, tm, start, 0, unroll=DMA_UNROLL)
    for _ in range(2):
        pltpu.make_async_copy(h_ref, xs_ref.at[pl.ds(0, tm), :], sem).wait()


def dispatch(h, dest, n_rows, tm):
    t, d = h.shape
    return pl.pallas_call(
        functools.partial(_dispatch_kernel, tm=tm),
        grid_spec=pltpu.PrefetchScalarGridSpec(
            num_scalar_prefetch=1,
            grid=(t // tm,),
            in_specs=[pl.BlockSpec((tm, d), lambda i, dest: (i, 0))],
            out_specs=pl.BlockSpec(memory_space=pl.ANY),
            scratch_shapes=[pltpu.SemaphoreType.DMA(())],
        ),
        out_shape=jax.ShapeDtypeStruct((n_rows, d), F32),
        compiler_params=_cparams(("arbitrary",)),
        name="moe_dispatch",
    )(dest, h)


def _ffn_kernel(be_ref, bv_ref, xs_ref, w1_ref, w3_ref, w2_ref, ys_ref):
    valid = bv_ref[pl.program_id(0)]

    @pl.when(valid > 0)
    def _():
        rows = lax.broadcasted_iota(I32, xs_ref.shape, 0)
        x = jnp.where(rows < valid, xs_ref[...], 0.0).astype(BF16)
        h1 = _dot(x, w1_ref[0])
        h3 = _dot(x, w3_ref[0])
        a = (_silu(h1) * h3).astype(BF16)
        ys_ref[...] = _dot(a, w2_ref[0])

    @pl.when(valid <= 0)
    def _():
        ys_ref[...] = jnp.zeros(ys_ref.shape, F32)


def grouped_ffn(xs, block_e, block_valid, w1, w3, w2, bm):
    n_rows, d = xs.shape
    f = w1.shape[2]
    return pl.pallas_call(
        _ffn_kernel,
        grid_spec=pltpu.PrefetchScalarGridSpec(
            num_scalar_prefetch=2,
            grid=(n_rows // bm,),
            in_specs=[
                pl.BlockSpec((bm, d), lambda i, be, bv: (i, 0)),
                pl.BlockSpec((1, d, f), lambda i, be, bv: (be[i], 0, 0)),
                pl.BlockSpec((1, d, f), lambda i, be, bv: (be[i], 0, 0)),
                pl.BlockSpec((1, f, d), lambda i, be, bv: (be[i], 0, 0)),
            ],
            out_specs=pl.BlockSpec((bm, d), lambda i, be, bv: (i, 0)),
        ),
        out_shape=jax.ShapeDtypeStruct((n_rows, d), F32),
        compiler_params=_cparams(("arbitrary",)),
        name="moe_ffn",
    )(block_e, block_valid, xs, w1, w3, w2)


def _combine_kernel(dest_ref, ys_ref, x_ref, mod_ref, gf_ref, o_ref, a_ref, b_ref, sem, *, tm):
    base = pl.program_id(0) * tm

    def row_copy(r, slot):
        d = dest_ref[2 * (base + r) + slot]
        buf = a_ref if slot == 0 else b_ref
        return pltpu.make_async_copy(ys_ref.at[pl.ds(d, 1), :], buf.at[pl.ds(r, 1), :], sem)

    def start(r, carry):
        row_copy(r, 0).start()
        row_copy(r, 1).start()
        return carry

    lax.fori_loop(0, tm, start, 0, unroll=DMA_UNROLL)
    pltpu.make_async_copy(ys_ref.at[pl.ds(0, tm), :], a_ref, sem).wait()
    pltpu.make_async_copy(ys_ref.at[pl.ds(0, tm), :], b_ref, sem).wait()
    y = gf_ref[:, 0:1] * a_ref[...] + gf_ref[:, 1:2] * b_ref[...]
    o_ref[...] = x_ref[...] + mod_ref[0, 2:3, :] * y


def combine(ys, dest, x2, mod, gates_t, seq, tm):
    t, d = x2.shape
    per = seq // tm
    return pl.pallas_call(
        functools.partial(_combine_kernel, tm=tm),
        grid_spec=pltpu.PrefetchScalarGridSpec(
            num_scalar_prefetch=1,
            grid=(t // tm,),
            in_specs=[
                pl.BlockSpec(memory_space=pl.ANY),
                pl.BlockSpec((tm, d), lambda i, dest: (i, 0)),
                pl.BlockSpec((1, 3, d), lambda i, dest: (i // per, 0, 0)),
                pl.BlockSpec((tm, SUBLANE), lambda i, dest: (i, 0)),
            ],
            out_specs=pl.BlockSpec((tm, d), lambda i, dest: (i, 0)),
            scratch_shapes=[pltpu.VMEM((tm, d), F32), pltpu.VMEM((tm, d), F32),
                            pltpu.SemaphoreType.DMA(())],
        ),
        out_shape=jax.ShapeDtypeStruct((t, d), F32),
        compiler_params=_cparams(("arbitrary",)),
        name="moe_combine",
    )(dest, ys, x2, mod, gates_t)


def moe(x2, mod, norm_w, w_router, b_router, w1, w3, w2, seq):
    t, d = x2.shape
    bm = MOE_ROWS
    tm = min(256, seq)
    h, ri, rf, cnt = router(x2, mod, norm_w, w_router, b_router, seq, tm)
    counts = cnt[:, 0]
    nblk = (counts + bm - 1) // bm
    blk_end = jnp.cumsum(nblk)
    blk_start = blk_end - nblk
    e12 = ri[0:2].T
    dest = (blk_start[e12] * bm + ri[2:4].T).reshape(2 * t).astype(I32)
    n_blocks = (2 * t) // bm + N_EXPERTS
    bidx = jnp.arange(n_blocks, dtype=I32)
    block_e = jnp.minimum(jnp.searchsorted(blk_end, bidx, side='right'), N_EXPERTS - 1).astype(I32)
    block_valid = jnp.clip(counts[block_e] - (bidx - blk_start[block_e]) * bm, 0, bm).astype(I32)
    xs = dispatch(h, dest, n_blocks * bm, tm)
    ys = grouped_ffn(xs, block_e, block_valid, w1.astype(BF16), w3.astype(BF16), w2.astype(BF16), bm)
    return combine(ys, dest, x2, mod, rf.T, seq, tm)


def _t5_bucket(n):
    max_exact = N_BUCKETS // 2
    nf = jnp.maximum(n, 1).astype(F32)
    large = max_exact + (jnp.log(nf / max_exact) / math.log(MAX_DISTANCE / max_exact)
                         * (N_BUCKETS - max_exact)).astype(I32)
    large = jnp.minimum(large, N_BUCKETS - 1)
    return jnp.where(n < max_exact, n, large)


def _sortable(score):
    bits = pltpu.bitcast(score, I32)
    return bits ^ ((bits >> 31) & 0x7FFFFFFF)


def _dsa_kernel(q_ref, k_ref, v_ref, qi_ref, kw_ref, wq_ref, tb_ref, qn_ref, kn_ref, o_ref,
                khat_ref, vt_ref, qrt_ref, qirt_ref, wt_ref, mt_ref, nm_ref, lg_ref, acc_ref,
                *, nh, dh, nih, di, topk, nb):
    j = pl.program_id(1)
    qb = Q_BLOCK
    kc = 2 * qb
    nch = (j + 2) // 2
    pair = 2 * qb

    @pl.when(j == 0)
    def _():
        k = k_ref[...].astype(F32)
        r = lax.rsqrt(jnp.mean(k * k, axis=-1, keepdims=True) + RMS_EPS)
        khat_ref[...] = (k * r * kn_ref[...]).astype(BF16)
        for c in range(nb // 2):
            for s in range(2):
                blk = v_ref[(2 * c + s) * qb:(2 * c + s + 1) * qb, :].astype(F32)
                vt_ref[c, :, s * qb:(s + 1) * qb] = blk.T.astype(BF16)

    qscale = (dh ** -0.5) * math.log2(math.e)
    for h in range(nh):
        qh = q_ref[:, h * dh:(h + 1) * dh].astype(F32)
        r = lax.rsqrt(jnp.mean(qh * qh, axis=-1, keepdims=True) + RMS_EPS)
        qrt_ref[:, h * qb:(h + 1) * qb] = (qh * r * qn_ref[...] * qscale).T.astype(BF16)
    for g in range(nih * di // LANE):
        two = qi_ref[:, g * LANE:(g + 1) * LANE].astype(F32).T
        for s in range(LANE // di):
            h = g * (LANE // di) + s
            qirt_ref[:, h * qb:(h + 1) * qb] = two[s * di:(s + 1) * di, :].astype(BF16)
    wt_ref[...] = wq_ref[...].astype(F32).T

    key_l = lax.broadcasted_iota(I32, (kc, qb), 0)
    q_pos = j * qb + lax.broadcasted_iota(I32, (kc, qb), 1)

    def score_body(c, carry):
        r0 = pl.multiple_of(c * kc, kc)
        ki = kw_ref[pl.ds(r0, kc), 0:di]
        acc = jnp.zeros((kc, qb), F32)
        for g in range(nih // 2):
            rel = _dot(ki, qirt_ref[:, g * pair:(g + 1) * pair])
            for s in range(2):
                h = 2 * g + s
                acc = acc + wt_ref[di + h:di + h + 1, :] * jnp.maximum(rel[:, s * qb:(s + 1) * qb], 0.0)
        mt_ref[c] = jnp.where(r0 + key_l <= q_pos, _sortable(acc), INT_MIN)
        return carry

    lax.fori_loop(0, nch, score_body, 0)

    def count_ge(cand):
        def cbody(c, acc):
            hit = jnp.where(mt_ref[c] >= cand, 1, 0)
            return acc + jnp.sum(hit.reshape(kc // SUBLANE, SUBLANE, qb), axis=0)
        acc = lax.fori_loop(0, nch, cbody, jnp.zeros((SUBLANE, qb), I32))
        return jnp.sum(acc, axis=0, keepdims=True)

    def search():
        zero = jnp.zeros((1, qb), I32)
        ans0 = jnp.where(count_ge(zero) >= topk, zero, INT_MIN)

        def bit_body(bi, ans):
            cand = ans | (1 << (30 - bi))
            return jnp.where(count_ge(cand) >= topk, cand, ans)

        return lax.fori_loop(0, 31, bit_body, ans0)

    thr = lax.cond((j + 1) * qb > topk, search, lambda: jnp.full((1, qb), INT_MIN + 1, I32))

    def mask_body(c, carry):
        nm_ref[c] = jnp.where(mt_ref[c] >= thr, 0.0, NEG_BIG)
        return carry

    lax.fori_loop(0, nch, mask_body, 0)

    npair = nh // 2
    pcols = [slice(pr * pair, (pr + 1) * pair) for pr in range(npair)]

    def fold(x):
        return x.reshape(kc // SUBLANE, SUBLANE, pair)

    def sweep1(c, mruns):
        r0 = pl.multiple_of(c * kc, kc)
        kh = khat_ref[pl.ds(r0, kc), :]
        k0 = jnp.clip(j - 2 * c, 0, 2)
        k1 = jnp.clip(j - 2 * c - 1, 0, 2)
        nm = nm_ref[c]
        nm2 = jnp.concatenate([nm, nm], axis=1)
        out = []
        for pr in range(npair):
            bias = jnp.concatenate([tb_ref[k0, :, pcols[pr]], tb_ref[k1, :, pcols[pr]]], axis=0)
            lg = _dot(kh, qrt_ref[:, pcols[pr]]) + bias + nm2
            lg_ref[c, :, pcols[pr]] = lg
            out.append(jnp.maximum(mruns[pr], jnp.max(fold(lg), axis=0)))
        return tuple(out)

    mruns = lax.fori_loop(0, nch, sweep1,
                          tuple(jnp.full((SUBLANE, pair), NEG_BIG, F32) for _ in range(npair)))
    mrows = [jnp.max(m, axis=0, keepdims=True) for m in mruns]
    acc_ref[...] = jnp.zeros(acc_ref.shape, F32)

    def sweep2(c, lsums):
        vt = vt_ref[c]
        out = []
        for pr in range(npair):
            p = jnp.exp2(lg_ref[c, :, pcols[pr]] - mrows[pr])
            out.append(lsums[pr] + jnp.sum(fold(p), axis=0))
            acc_ref[:, pcols[pr]] += _dot(vt, p.astype(BF16))
        return tuple(out)

    lsums = lax.fori_loop(0, nch, sweep2,
                          tuple(jnp.zeros((SUBLANE, pair), F32) for _ in range(npair)))
    for pr in range(npair):
        out = acc_ref[:, pcols[pr]] / jnp.sum(lsums[pr], axis=0, keepdims=True)
        for s in range(2):
            h = 2 * pr + s
            o_ref[:, h * dh:(h + 1) * dh] = out[:, s * qb:(s + 1) * qb].T.astype(o_ref.dtype)


def dsa_attention(z, rel_bias, q_norm, k_norm, bsz, seq, d):
    t = z.shape[0]
    nh, dh, nih, di = ATT_HEADS, d // ATT_HEADS, IDX_HEADS, IDX_DIM
    qb = Q_BLOCK
    nb = seq // qb
    topk = min(DSA_TOPK_MAX, seq // 4)
    assert topk % qb == 0 and dh == LANE and nih * di == d // 2 and nb % 2 == 0 and LANE % di == 0
    i = jnp.arange(qb)
    dist = i[None, :] - i[:, None]
    tabs = [rel_bias[_t5_bucket(jnp.maximum(dist + off * qb, 0))] for off in range(2)]
    far = jnp.broadcast_to(rel_bias[_t5_bucket(jnp.full((), 2 * qb, I32))], (qb, qb, nh))
    tb = jnp.stack(tabs + [far]).transpose(0, 1, 3, 2).reshape(3, qb, nh * qb).astype(F32)
    tb = tb * math.log2(math.e)
    cq = d // LANE
    return pl.pallas_call(
        functools.partial(_dsa_kernel, nh=nh, dh=dh, nih=nih, di=di, topk=topk, nb=nb),
        grid=(bsz, nb),
        in_specs=[
            pl.BlockSpec((qb, d), lambda b, j: (b * nb + j, 0)),
            pl.BlockSpec((seq, LANE), lambda b, j: (b, cq + cq // 2)),
            pl.BlockSpec((seq, LANE), lambda b, j: (b, cq + cq // 2 + 1)),
            pl.BlockSpec((qb, d // 2), lambda b, j: (b * nb + j, 2)),
            pl.BlockSpec((seq, LANE), lambda b, j: (b, cq + cq // 2 + 2)),
            pl.BlockSpec((qb, LANE), lambda b, j: (b * nb + j, cq + cq // 2 + 2)),
            pl.BlockSpec((3, qb, nh * qb), lambda b, j: (0, 0, 0)),
            pl.BlockSpec((1, dh), lambda b, j: (0, 0)),
            pl.BlockSpec((1, dh), lambda b, j: (0, 0)),
        ],
        out_specs=pl.BlockSpec((qb, d), lambda b, j: (b * nb + j, 0)),
        out_shape=jax.ShapeDtypeStruct((t, d), BF16),
        scratch_shapes=[
            pltpu.VMEM((seq, dh), BF16),
            pltpu.VMEM((nb // 2, dh, 2 * qb), BF16),
            pltpu.VMEM((dh, nh * qb), BF16),
            pltpu.VMEM((di, nih * qb), BF16),
            pltpu.VMEM((LANE, qb), F32),
            pltpu.VMEM((nb // 2, 2 * qb, qb), I32),
            pltpu.VMEM((nb // 2, 2 * qb, qb), F32),
            pltpu.VMEM((nb // 2, 2 * qb, nh * qb), F32),
            pltpu.VMEM((dh, nh * qb), F32),
        ],
        compiler_params=_cparams(("arbitrary", "arbitrary")),
        name="dsa_attention",
    )(z, z, z, z, z, z, tb, q_norm.reshape(1, dh), k_norm.reshape(1, dh))


def _dsa_weight(c_w_in, d):
    dh = d // ATT_HEADS
    nqi = IDX_HEADS * IDX_DIM
    q, k, v, qi, ki, wi = jnp.split(
        c_w_in, [d, d + dh, d + 2 * dh, d + 2 * dh + nqi, d + 2 * dh + nqi + IDX_DIM], axis=1)
    used = d + nqi + 2 * dh + IDX_DIM + IDX_HEADS
    total = -(-used // 512) * 512
    pad = jnp.zeros((d, total - used), c_w_in.dtype)
    return jnp.concatenate([q, qi, k, v, ki, wi, pad], axis=1).astype(BF16)


def kernel(x, c, rel_bias, w_router, b_router, norm_mix, ada_w_mix, ada_b_mix, norm_ffn, ada_w_ffn,
           ada_b_ffn, ab_w_in, ab_conv_w, ab_conv_b, ab_w_out, c_w_in, c_q_norm, c_k_norm, c_w_out,
           moe_w1, moe_w3, moe_w2):
    bsz, seq, d = x.shape
    depth = norm_mix.shape[0]
    t = bsz * seq
    ch = d // 2
    tm = min(512, seq)
    x2 = x.reshape(t, d)
    mod_mix = ada_mod(c, ada_w_mix, ada_b_mix).reshape(depth, bsz, 3, d)
    mod_ffn = ada_mod(c, ada_w_ffn, ada_b_ffn).reshape(depth, bsz, 3, d)
    for layer in range(depth):
        i = layer // 2
        if layer % 2 == 0:
            z = inproj(x2, mod_mix[layer], norm_mix[layer], ab_w_in[i].astype(BF16), seq,
                       min(1024, seq), 1024)
            y_a = conv_mixer(z, ab_conv_w[i], ab_conv_b[i], bsz, seq, tm)
            y_b = retention(z, bsz, seq, ch, 3 * ch)
            x2 = outproj(y_a, y_b, (0, 0), ab_w_out[i].astype(BF16), x2, mod_mix[layer], seq, tm)
        else:
            w_c = _dsa_weight(c_w_in[i], d)
            z = inproj(x2, mod_mix[layer], norm_mix[layer], w_c, seq, tm, w_c.shape[1] // 2)
            o = dsa_attention(z, rel_bias, c_q_norm[i], c_k_norm[i], bsz, seq, d)
            x2 = outproj(o, o, (0, 1), c_w_out[i].astype(BF16), x2, mod_mix[layer], seq, tm)
        x2 = moe(x2, mod_ffn[layer], norm_ffn[layer], w_router, b_router,
                 moe_w1[layer], moe_w3[layer], moe_w2[layer], seq)
    return x2.reshape(bsz, seq, d)
```

```python
import functools
import math

import jax
import jax.numpy as jnp
from jax import lax
from jax.experimental import pallas as pl
from jax.experimental.pallas import tpu as pltpu

F32 = jnp.float32
BF16 = jnp.bfloat16
I32 = jnp.int32

RMS_EPS = 1e-6
CONV_WIDTH = 3
RET_HEADS = 4
ATT_HEADS = 16
IDX_HEADS = 16
IDX_DIM = 64
DSA_TOPK_MAX = 256
Q_BLOCK = 128
N_BUCKETS = 32
MAX_DISTANCE = 128
N_EXPERTS = 16
N_GROUPS = 4
EXPERTS_PER_GROUP = N_EXPERTS // N_GROUPS
RET_CHUNK = 256
MOE_ROWS = 256
DMA_UNROLL = 8
LANE = 128
SUBLANE = 8
VMEM_LIMIT = 48 * 1024 * 1024
NEG_BIG = -1e30
INT_MIN = -(2 ** 31)


def _cparams(sem, vmem=VMEM_LIMIT):
    return pltpu.CompilerParams(dimension_semantics=sem, vmem_limit_bytes=vmem)


def _dot(a, b):
    return jnp.dot(a, b, preferred_element_type=F32)


def _dot_nt(a, b):
    return lax.dot_general(a, b, (((1,), (1,)), ((), ())), preferred_element_type=F32)


def _dot_tn(a, b):
    return lax.dot_general(a, b, (((0,), (0,)), ((), ())), preferred_element_type=F32)


def _silu(x):
    return x * jax.nn.sigmoid(x)


def _ada_kernel(c_ref, w_ref, b_ref, o_ref):
    cs = _silu(c_ref[...])
    o_ref[0] = _dot(cs.astype(BF16), w_ref[0].astype(BF16)) + b_ref[0]


def ada_mod(c, ada_w, ada_b):
    nl, d, n3 = ada_w.shape
    bsz = c.shape[0]
    tn = min(512, n3)
    return pl.pallas_call(
        _ada_kernel,
        grid=(nl, n3 // tn),
        in_specs=[
            pl.BlockSpec((bsz, d), lambda l, j: (0, 0)),
            pl.BlockSpec((1, d, tn), lambda l, j: (l, 0, j)),
            pl.BlockSpec((1, 1, tn), lambda l, j: (l, 0, j)),
        ],
        out_specs=pl.BlockSpec((1, bsz, tn), lambda l, j: (l, 0, j)),
        out_shape=jax.ShapeDtypeStruct((nl, bsz, n3), F32),
        compiler_params=_cparams(("parallel", "parallel")),
        name="ada_mod",
    )(c, ada_w, ada_b.reshape(nl, 1, n3))


def _modulated_norm(x, mod_ref, nw_ref):
    r = lax.rsqrt(jnp.mean(x * x, axis=-1, keepdims=True) + RMS_EPS)
    return x * r * nw_ref[...] * (1.0 + mod_ref[0, 1:2, :]) + mod_ref[0, 0:1, :]


def _inproj_kernel(x_ref, mod_ref, nw_ref, w_ref, o_ref, h_ref):
    @pl.when(pl.program_id(1) == 0)
    def _():
        h_ref[...] = _modulated_norm(x_ref[...], mod_ref, nw_ref).astype(BF16)

    o_ref[...] = _dot(h_ref[...], w_ref[...]).astype(o_ref.dtype)


def inproj(x2, mod, norm_w, w_bf, seq, tm, tn):
    t, d = x2.shape
    n = w_bf.shape[1]
    per = seq // tm
    return pl.pallas_call(
        _inproj_kernel,
        grid=(t // tm, n // tn),
        in_specs=[
            pl.BlockSpec((tm, d), lambda i, j: (i, 0)),
            pl.BlockSpec((1, 3, d), lambda i, j: (i // per, 0, 0)),
            pl.BlockSpec((1, d), lambda i, j: (0, 0)),
            pl.BlockSpec((d, tn), lambda i, j: (0, j)),
        ],
        out_specs=pl.BlockSpec((tm, tn), lambda i, j: (i, j)),
        out_shape=jax.ShapeDtypeStruct((t, n), BF16),
        scratch_shapes=[pltpu.VMEM((tm, d), BF16)],
        compiler_params=_cparams(("parallel", "arbitrary")),
        name="inproj",
    )(x2, mod, norm_w.reshape(1, d), w_bf)


def _conv_kernel(b_ref, c_ref, v_ref, w_ref, cb_ref, o_ref, u_ref, *, tm):
    s = pl.program_id(1)
    u = c_ref[...].astype(F32) * v_ref[...].astype(F32)

    @pl.when(s == 0)
    def _():
        u_ref[0:SUBLANE, :] = jnp.zeros((SUBLANE, u.shape[1]), F32)

    @pl.when(s > 0)
    def _():
        u_ref[0:SUBLANE, :] = u_ref[tm:tm + SUBLANE, :]

    u_ref[SUBLANE:SUBLANE + tm, :] = u
    conv = (cb_ref[...]
            + u_ref[SUBLANE - 2:SUBLANE - 2 + tm, :] * w_ref[0:1, :]
            + u_ref[SUBLANE - 1:SUBLANE - 1 + tm, :] * w_ref[1:2, :]
            + u * w_ref[2:3, :])
    o_ref[...] = (b_ref[...].astype(F32) * conv).astype(o_ref.dtype)


def conv_mixer(z, conv_w, conv_b, bsz, seq, tm):
    t = z.shape[0]
    ch = conv_w.shape[1]
    per = seq // tm
    row = lambda g: (lambda b, s: (b * per + s, g))
    return pl.pallas_call(
        functools.partial(_conv_kernel, tm=tm),
        grid=(bsz, per),
        in_specs=[
            pl.BlockSpec((tm, ch), row(0)),
            pl.BlockSpec((tm, ch), row(1)),
            pl.BlockSpec((tm, ch), row(2)),
            pl.BlockSpec((CONV_WIDTH, ch), lambda b, s: (0, 0)),
            pl.BlockSpec((1, ch), lambda b, s: (0, 0)),
        ],
        out_specs=pl.BlockSpec((tm, ch), lambda b, s: (b * per + s, 0)),
        out_shape=jax.ShapeDtypeStruct((t, ch), BF16),
        scratch_shapes=[pltpu.VMEM((tm + SUBLANE, ch), F32)],
        compiler_params=_cparams(("parallel", "arbitrary")),
        name="conv_mixer",
    )(z, z, z, conv_w, conv_b.reshape(1, ch))


def _retention_kernel(q_ref, k_ref, v_ref, g_ref, cos_ref, sin_ref, din_ref, dcr_ref, dst_ref,
                      o_ref, st_ref, *, nc, ck, dk):
    half = dk // 2
    st_ref[...] = jnp.zeros(st_ref.shape, F32)
    d_intra = din_ref[0]
    d_cross = dcr_ref[0]
    d_state = dst_ref[0]
    g_chunk = dcr_ref[0, ck - 1:ck, :]

    def rot(x, cos, sin):
        x1, x2 = x[:, :half], x[:, half:]
        return jnp.concatenate([x1 * cos - x2 * sin, x1 * sin + x2 * cos], axis=-1)

    def body(c, carry):
        r0 = pl.multiple_of(c * ck, ck)
        cos = cos_ref[pl.ds(r0, ck), :]
        sin = sin_ref[pl.ds(r0, ck), :]
        q = rot(q_ref[pl.ds(r0, ck), :].astype(F32), cos, sin)
        k = rot(k_ref[pl.ds(r0, ck), :].astype(F32), cos, sin) * (dk ** -0.5)
        v = v_ref[pl.ds(r0, ck), :]
        intra = _dot_nt(q.astype(BF16), k.astype(BF16)) * d_intra
        state = st_ref[...]
        o = _dot(intra.astype(BF16), v) + _dot((q * d_cross).astype(BF16), state.astype(BF16))
        st_ref[...] = state * g_chunk + _dot_tn((k * d_state).astype(BF16), v)
        r = lax.rsqrt(jnp.mean(o * o, axis=-1, keepdims=True) + RMS_EPS)
        o_ref[pl.ds(r0, ck), :] = (o * r * _silu(g_ref[pl.ds(r0, ck), :].astype(F32))).astype(o_ref.dtype)
        return carry

    lax.fori_loop(0, nc, body, 0)


def retention(z, bsz, seq, ch, col0):
    t = z.shape[0]
    nh = RET_HEADS
    dk = ch // nh
    ck = min(RET_CHUNK, seq)
    nc = seq // ck
    half = dk // 2
    pos = jnp.arange(seq, dtype=F32)
    inv = 1.0 / (10000.0 ** jnp.linspace(0.0, 1.0, half, dtype=F32))
    ang = pos[:, None] * inv[None, :]
    cos, sin = jnp.cos(ang), jnp.sin(ang)
    log_g = jnp.log(1.0 - 2.0 ** (-5.0 - jnp.arange(nh, dtype=F32)))
    i = jnp.arange(ck, dtype=F32)
    diff = i[:, None] - i[None, :]
    d_intra = jnp.where(diff >= 0, jnp.exp(log_g[:, None, None] * jnp.maximum(diff, 0.0)), 0.0)
    d_cross = jnp.exp(log_g[:, None] * (i[None, :] + 1.0))[..., None]
    d_state = jnp.exp(log_g[:, None] * (ck - 1.0 - i[None, :]))[..., None]
    cb = col0 // dk
    col = lambda g: (lambda b, h: (b, cb + g * nh + h))
    return pl.pallas_call(
        functools.partial(_retention_kernel, nc=nc, ck=ck, dk=dk),
        grid=(bsz, nh),
        in_specs=[
            pl.BlockSpec((seq, dk), col(0)),
            pl.BlockSpec((seq, dk), col(1)),
            pl.BlockSpec((seq, dk), col(2)),
            pl.BlockSpec((seq, dk), col(3)),
            pl.BlockSpec((seq, half), lambda b, h: (0, 0)),
            pl.BlockSpec((seq, half), lambda b, h: (0, 0)),
            pl.BlockSpec((1, ck, ck), lambda b, h: (h, 0, 0)),
            pl.BlockSpec((1, ck, 1), lambda b, h: (h, 0, 0)),
            pl.BlockSpec((1, ck, 1), lambda b, h: (h, 0, 0)),
        ],
        out_specs=pl.BlockSpec((seq, dk), lambda b, h: (b, h)),
        out_shape=jax.ShapeDtypeStruct((t, ch), BF16),
        scratch_shapes=[pltpu.VMEM((dk, dk), F32)],
        compiler_params=_cparams(("parallel", "parallel")),
        name="retention",
    )(z, z, z, z, cos, sin, d_intra, d_cross, d_state)


def _outproj_kernel(ya_ref, yb_ref, wa_ref, wb_ref, x_ref, mod_ref, o_ref):
    y = _dot(ya_ref[...], wa_ref[...]) + _dot(yb_ref[...], wb_ref[...])
    o_ref[...] = x_ref[...] + mod_ref[0, 2:3, :] * y


def outproj(ya, yb, cols, w_bf, x2, mod, seq, tm):
    t, d = x2.shape
    kh = w_bf.shape[0] // 2
    per = seq // tm
    return pl.pallas_call(
        _outproj_kernel,
        grid=(t // tm,),
        in_specs=[
            pl.BlockSpec((tm, kh), lambda i: (i, cols[0])),
            pl.BlockSpec((tm, kh), lambda i: (i, cols[1])),
            pl.BlockSpec((kh, d), lambda i: (0, 0)),
            pl.BlockSpec((kh, d), lambda i: (1, 0)),
            pl.BlockSpec((tm, d), lambda i: (i, 0)),
            pl.BlockSpec((1, 3, d), lambda i: (i // per, 0, 0)),
        ],
        out_specs=pl.BlockSpec((tm, d), lambda i: (i, 0)),
        out_shape=jax.ShapeDtypeStruct((t, d), F32),
        compiler_params=_cparams(("parallel",)),
        name="outproj",
    )(ya, yb, w_bf, w_bf, x2, mod)


def _router_kernel(x_ref, mod_ref, nw_ref, wr_ref, br_ref, h_ref, ri_ref, rf_ref, cnt_ref, carry_ref,
                   *, tm):
    @pl.when(pl.program_id(0) == 0)
    def _():
        carry_ref[...] = jnp.zeros(carry_ref.shape, F32)

    h = _modulated_norm(x_ref[...], mod_ref, nw_ref)
    h_ref[...] = h
    logits = _dot_nt(wr_ref[...], h.astype(BF16))
    mx = jnp.max(logits, axis=0, keepdims=True)
    ex = jnp.exp(logits - mx)
    probs = ex / jnp.sum(ex, axis=0, keepdims=True)
    sel = probs + br_ref[...]
    s = [sel[e:e + 1, :] for e in range(N_EXPERTS)]
    p = [probs[e:e + 1, :] for e in range(N_EXPERTS)]
    epg = EXPERTS_PER_GROUP

    def first_argmax(vals, exclude=None):
        best = jnp.full_like(vals[0], -jnp.inf)
        idx = jnp.zeros(vals[0].shape, I32)
        for j, vj in enumerate(vals):
            better = vj > best
            if exclude is not None:
                better = better & (exclude != j)
            idx = jnp.where(better, j, idx)
            best = jnp.where(better, vj, best)
        return idx

    gscore = []
    for g in range(N_GROUPS):
        gs = s[g * epg:(g + 1) * epg]
        best = None
        for a in range(epg):
            for b in range(a + 1, epg):
                pair = gs[a] + gs[b]
                best = pair if best is None else jnp.maximum(best, pair)
        gscore.append(best)
    grp = first_argmax(gscore)

    def pick(rows, index, n):
        out = rows[n - 1]
        for j in range(n - 2, -1, -1):
            out = jnp.where(index == j, rows[j], out)
        return out

    in_s = [pick([s[g * epg + j] for g in range(N_GROUPS)], grp, N_GROUPS) for j in range(epg)]
    in_p = [pick([p[g * epg + j] for g in range(N_GROUPS)], grp, N_GROUPS) for j in range(epg)]
    i1 = first_argmax(in_s)
    i2 = first_argmax(in_s, exclude=i1)
    p1 = pick(in_p, i1, epg)
    p2 = pick(in_p, i2, epg)
    e1 = grp * epg + i1
    e2 = grp * epg + i2
    den = p1 + p2
    g1 = p1 / den
    g2 = p2 / den

    eidx = lax.broadcasted_iota(I32, (N_EXPERTS, tm), 0)
    member = (eidx == e1) | (eidx == e2)
    member_f = jnp.where(member, 1.0, 0.0)
    before = lax.broadcasted_iota(I32, (tm, tm), 0) < lax.broadcasted_iota(I32, (tm, tm), 1)
    prefix = _dot(member_f.astype(BF16), jnp.where(before, 1.0, 0.0).astype(BF16))
    base = prefix + carry_ref[:, 0:1]
    rank1 = jnp.sum(jnp.where(eidx == e1, base, 0.0), axis=0, keepdims=True).astype(I32)
    rank2 = jnp.sum(jnp.where(eidx == e2, base, 0.0), axis=0, keepdims=True).astype(I32)
    carry_ref[...] = carry_ref[...] + jnp.sum(member_f, axis=1, keepdims=True)
    cnt_ref[...] = carry_ref[...].astype(I32)

    zi = jnp.zeros((SUBLANE - 4, tm), I32)
    ri_ref[...] = jnp.concatenate([e1, e2, rank1, rank2, zi], axis=0)
    zf = jnp.zeros((SUBLANE - 2, tm), F32)
    rf_ref[...] = jnp.concatenate([g1, g2, zf], axis=0)


def router(x2, mod, norm_w, w_router, b_router, seq, tm):
    t, d = x2.shape
    per = seq // tm
    ne = N_EXPERTS
    return pl.pallas_call(
        functools.partial(_router_kernel, tm=tm),
        grid=(t // tm,),
        in_specs=[
            pl.BlockSpec((tm, d), lambda i: (i, 0)),
            pl.BlockSpec((1, 3, d), lambda i: (i // per, 0, 0)),
            pl.BlockSpec((1, d), lambda i: (0, 0)),
            pl.BlockSpec((ne, d), lambda i: (0, 0)),
            pl.BlockSpec((ne, 1), lambda i: (0, 0)),
        ],
        out_specs=[
            pl.BlockSpec((tm, d), lambda i: (i, 0)),
            pl.BlockSpec((SUBLANE, tm), lambda i: (0, i)),
            pl.BlockSpec((SUBLANE, tm), lambda i: (0, i)),
            pl.BlockSpec((ne, LANE), lambda i: (0, 0)),
        ],
        out_shape=[
            jax.ShapeDtypeStruct((t, d), F32),
            jax.ShapeDtypeStruct((SUBLANE, t), I32),
            jax.ShapeDtypeStruct((SUBLANE, t), F32),
            jax.ShapeDtypeStruct((ne, LANE), I32),
        ],
        scratch_shapes=[pltpu.VMEM((ne, LANE), F32)],
        compiler_params=_cparams(("arbitrary",)),
        name="router",
    )(x2, mod, norm_w.reshape(1, d), w_router.T.astype(BF16), b_router.reshape(ne, 1))


def _dispatch_kernel(dest_ref, pad0_ref, padn_ref, h_ref, xs_ref, hbuf_ref, zero_ref, sem, zsem,
                     *, tm, nt):
    i = pl.program_id(0)
    cur = i % 2
    base = i * tm

    def zero_row(e, r):
        return pltpu.make_async_copy(zero_ref.at[pl.ds(0, 1), :],
                                     xs_ref.at[pl.ds(pad0_ref[e] + r, 1), :], zsem)

    def tile_done(buf):
        for _ in range(2):
            pltpu.make_async_copy(hbuf_ref.at[buf], xs_ref.at[pl.ds(0, tm), :], sem.at[buf]).wait()

    @pl.when(i == 0)
    def _():
        zero_ref[...] = jnp.zeros(zero_ref.shape, F32)
        for e in range(N_EXPERTS):
            lax.fori_loop(0, padn_ref[e], lambda r, c, e=e: (zero_row(e, r).start(), c)[1], 0)

    hbuf_ref[cur] = h_ref[...]

    def start(r, carry):
        for slot in range(2):
            d = dest_ref[2 * (base + r) + slot]
            pltpu.make_async_copy(hbuf_ref.at[cur, pl.ds(r, 1), :], xs_ref.at[pl.ds(d, 1), :],
                                  sem.at[cur]).start()
        return carry

    lax.fori_loop(0, tm, start, 0, unroll=DMA_UNROLL)

    @pl.when(i > 0)
    def _():
        tile_done(1 - cur)

    @pl.when(i == nt - 1)
    def _():
        tile_done(cur)
        for e in range(N_EXPERTS):
            lax.fori_loop(0, padn_ref[e], lambda r, c, e=e: (zero_row(e, r).wait(), c)[1], 0)


def dispatch(h, dest, pad_start, pad_count, n_rows, tm):
    t, d = h.shape
    return pl.pallas_call(
        functools.partial(_dispatch_kernel, tm=tm, nt=t // tm),
        grid_spec=pltpu.PrefetchScalarGridSpec(
            num_scalar_prefetch=3,
            grid=(t // tm,),
            in_specs=[pl.BlockSpec((tm, d), lambda i, *_: (i, 0))],
            out_specs=pl.BlockSpec(memory_space=pl.ANY),
            scratch_shapes=[pltpu.VMEM((2, tm, d), F32), pltpu.VMEM((SUBLANE, d), F32),
                            pltpu.SemaphoreType.DMA((2,)), pltpu.SemaphoreType.DMA(())],
        ),
        out_shape=jax.ShapeDtypeStruct((n_rows, d), F32),
        compiler_params=_cparams(("arbitrary",)),
        name="moe_dispatch",
    )(dest, pad_start, pad_count, h)


def _ffn_kernel(be_ref, bv_ref, xs_ref, w1_ref, w3_ref, w2_ref, ys_ref):
    valid = bv_ref[pl.program_id(0)]

    @pl.when(valid > 0)
    def _():
        rows = lax.broadcasted_iota(I32, xs_ref.shape, 0)
        x = jnp.where(rows < valid, xs_ref[...], 0.0).astype(BF16)
        h1 = _dot(x, w1_ref[0])
        h3 = _dot(x, w3_ref[0])
        a = (_silu(h1) * h3).astype(BF16)
        ys_ref[...] = _dot(a, w2_ref[0])

    @pl.when(valid <= 0)
    def _():
        ys_ref[...] = jnp.zeros(ys_ref.shape, F32)


def grouped_ffn(xs, block_e, block_valid, w1, w3, w2, bm):
    n_rows, d = xs.shape
    f = w1.shape[2]
    return pl.pallas_call(
        _ffn_kernel,
        grid_spec=pltpu.PrefetchScalarGridSpec(
            num_scalar_prefetch=2,
            grid=(n_rows // bm,),
            in_specs=[
                pl.BlockSpec((bm, d), lambda i, be, bv: (i, 0)),
                pl.BlockSpec((1, d, f), lambda i, be, bv: (be[i], 0, 0)),
                pl.BlockSpec((1, d, f), lambda i, be, bv: (be[i], 0, 0)),
                pl.BlockSpec((1, f, d), lambda i, be, bv: (be[i], 0, 0)),
            ],
            out_specs=pl.BlockSpec((bm, d), lambda i, be, bv: (i, 0)),
        ),
        out_shape=jax.ShapeDtypeStruct((n_rows, d), F32),
        compiler_params=_cparams(("arbitrary",)),
        name="moe_ffn",
    )(block_e, block_valid, xs, w1, w3, w2)


def _combine_kernel(dest_ref, ys_ref, x_ref, mod_ref, gf_ref, o_ref, a_ref, b_ref, sem, *, tm, nt):
    i = pl.program_id(0)

    def fetch(tile, buf):
        base = tile * tm

        def start(r, carry):
            for slot, ref in ((0, a_ref), (1, b_ref)):
                d = dest_ref[2 * (base + r) + slot]
                pltpu.make_async_copy(ys_ref.at[pl.ds(d, 1), :], ref.at[buf, pl.ds(r, 1), :],
                                      sem.at[buf]).start()
            return carry

        lax.fori_loop(0, tm, start, 0, unroll=DMA_UNROLL)

    @pl.when(i == 0)
    def _():
        fetch(0, 0)

    @pl.when(i + 1 < nt)
    def _():
        fetch(i + 1, (i + 1) % 2)

    cur = i % 2
    pltpu.make_async_copy(ys_ref.at[pl.ds(0, tm), :], a_ref.at[cur], sem.at[cur]).wait()
    pltpu.make_async_copy(ys_ref.at[pl.ds(0, tm), :], b_ref.at[cur], sem.at[cur]).wait()
    y = gf_ref[:, 0:1] * a_ref[cur] + gf_ref[:, 1:2] * b_ref[cur]
    o_ref[...] = x_ref[...] + mod_ref[0, 2:3, :] * y


def combine(ys, dest, x2, mod, gates_t, seq, tm):
    t, d = x2.shape
    per = seq // tm
    return pl.pallas_call(
        functools.partial(_combine_kernel, tm=tm, nt=t // tm),
        grid_spec=pltpu.PrefetchScalarGridSpec(
            num_scalar_prefetch=1,
            grid=(t // tm,),
            in_specs=[
                pl.BlockSpec(memory_space=pl.ANY),
                pl.BlockSpec((tm, d), lambda i, dest: (i, 0)),
                pl.BlockSpec((1, 3, d), lambda i, dest: (i // per, 0, 0)),
                pl.BlockSpec((tm, SUBLANE), lambda i, dest: (i, 0)),
            ],
            out_specs=pl.BlockSpec((tm, d), lambda i, dest: (i, 0)),
            scratch_shapes=[pltpu.VMEM((2, tm, d), F32), pltpu.VMEM((2, tm, d), F32),
                            pltpu.SemaphoreType.DMA((2,))],
        ),
        out_shape=jax.ShapeDtypeStruct((t, d), F32),
        compiler_params=_cparams(("arbitrary",)),
        name="moe_combine",
    )(dest, ys, x2, mod, gates_t)


def moe(x2, mod, norm_w, w_router, b_router, w1, w3, w2, expert0, seq):
    t, d = x2.shape
    bm = MOE_ROWS
    tm = min(256, seq)
    h, ri, rf, cnt = router(x2, mod, norm_w, w_router, b_router, seq, tm)
    ne = N_EXPERTS
    counts = cnt[:, 0]
    nblk = (counts + bm - 1) // bm
    blk_end = jnp.cumsum(nblk)
    blk_start = blk_end - nblk
    eids = jnp.arange(ne, dtype=I32)
    e12 = ri[0:2].T
    row0 = jnp.sum(jnp.where(e12[..., None] == eids, blk_start * bm, 0), axis=-1)
    dest = (row0 + ri[2:4].T).reshape(2 * t).astype(I32)
    n_blocks = (2 * t) // bm + ne
    bidx = jnp.arange(n_blocks, dtype=I32)
    block_e = jnp.minimum(jnp.sum(bidx[:, None] >= blk_end[None, :], axis=1), ne - 1).astype(I32)
    onehot = block_e[:, None] == eids
    cnt_b = jnp.sum(jnp.where(onehot, counts, 0), axis=1)
    start_b = jnp.sum(jnp.where(onehot, blk_start, 0), axis=1)
    block_valid = jnp.clip(cnt_b - (bidx - start_b) * bm, 0, bm).astype(I32)
    pad_start = (blk_start * bm + counts).astype(I32)
    pad_end = jnp.where(eids == ne - 1, n_blocks * bm, blk_end * bm)
    xs = dispatch(h, dest, pad_start, (pad_end - pad_start).astype(I32), n_blocks * bm, tm)
    ys = grouped_ffn(xs, block_e + expert0, block_valid, w1, w3, w2, bm)
    return combine(ys, dest, x2, mod, rf.T, seq, tm)


def _t5_bucket(n):
    max_exact = N_BUCKETS // 2
    nf = jnp.maximum(n, 1).astype(F32)
    large = max_exact + (jnp.log(nf / max_exact) / math.log(MAX_DISTANCE / max_exact)
                         * (N_BUCKETS - max_exact)).astype(I32)
    large = jnp.minimum(large, N_BUCKETS - 1)
    return jnp.where(n < max_exact, n, large)


def _sortable(score):
    bits = pltpu.bitcast(score, I32)
    return bits ^ ((bits >> 31) & 0x7FFFFFFF)


def _dsa_kernel(q_ref, k_ref, v_ref, qi_ref, kw_ref, wq_ref, tb_ref, qn_ref, kn_ref, o_ref,
                khat_ref, vt_ref, qrt_ref, qirt_ref, wt_ref, mt_ref, nm_ref, lg_ref, acc_ref,
                *, nh, dh, nih, di, topk, nb):
    j = pl.program_id(1)
    qb = Q_BLOCK
    kc = 2 * qb
    nch = (j + 2) // 2
    pair = 2 * qb

    @pl.when(j == 0)
    def _():
        k = k_ref[...].astype(F32)
        r = lax.rsqrt(jnp.mean(k * k, axis=-1, keepdims=True) + RMS_EPS)
        khat_ref[...] = (k * r * kn_ref[...]).astype(BF16)
        for c in range(nb // 2):
            for s in range(2):
                blk = v_ref[(2 * c + s) * qb:(2 * c + s + 1) * qb, :].astype(F32)
                vt_ref[c, :, s * qb:(s + 1) * qb] = blk.T.astype(BF16)

    qscale = (dh ** -0.5) * math.log2(math.e)
    for h in range(nh):
        qh = q_ref[:, h * dh:(h + 1) * dh].astype(F32)
        r = lax.rsqrt(jnp.mean(qh * qh, axis=-1, keepdims=True) + RMS_EPS)
        qrt_ref[:, h * qb:(h + 1) * qb] = (qh * r * qn_ref[...] * qscale).T.astype(BF16)
    for g in range(nih * di // LANE):
        two = qi_ref[:, g * LANE:(g + 1) * LANE].astype(F32).T
        for s in range(LANE // di):
            h = g * (LANE // di) + s
            qirt_ref[:, h * qb:(h + 1) * qb] = two[s * di:(s + 1) * di, :].astype(BF16)
    wt_ref[...] = wq_ref[...].astype(F32).T

    key_l = lax.broadcasted_iota(I32, (kc, qb), 0)
    q_pos = j * qb + lax.broadcasted_iota(I32, (kc, qb), 1)

    def score_body(c, carry):
        r0 = pl.multiple_of(c * kc, kc)
        ki = kw_ref[pl.ds(r0, kc), 0:di]
        acc = jnp.zeros((kc, qb), F32)
        for g in range(nih // 2):
            rel = _dot(ki, qirt_ref[:, g * pair:(g + 1) * pair])
            for s in range(2):
                h = 2 * g + s
                acc = acc + wt_ref[di + h:di + h + 1, :] * jnp.maximum(rel[:, s * qb:(s + 1) * qb], 0.0)
        mt_ref[c] = jnp.where(r0 + key_l <= q_pos, _sortable(acc), INT_MIN)
        return carry

    lax.fori_loop(0, nch, score_body, 0)

    def count_ge(cand):
        def cbody(c, acc):
            hit = jnp.where(mt_ref[c] >= cand, 1, 0)
            return acc + jnp.sum(hit.reshape(kc // SUBLANE, SUBLANE, qb), axis=0)
        acc = lax.fori_loop(0, nch, cbody, jnp.zeros((SUBLANE, qb), I32))
        return jnp.sum(acc, axis=0, keepdims=True)

    def search():
        zero = jnp.zeros((1, qb), I32)
        ans0 = jnp.where(count_ge(zero) >= topk, zero, INT_MIN)

        def bit_body(bi, ans):
            cand = ans | (1 << (30 - bi))
            return jnp.where(count_ge(cand) >= topk, cand, ans)

        return lax.fori_loop(0, 31, bit_body, ans0)

    selecting = (j + 1) * qb > topk
    thr = lax.cond(selecting, search, lambda: jnp.full((1, qb), INT_MIN + 1, I32))

    def plain_mask():
        def body(c, carry):
            nm_ref[c] = jnp.where(mt_ref[c] >= thr, 0.0, NEG_BIG)
            return carry
        lax.fori_loop(0, nch, body, 0)

    def tie_mask():
        need = (topk - count_ge(thr + 1)).astype(F32)
        tri = jnp.where(lax.broadcasted_iota(I32, (kc, kc), 1) <= lax.broadcasted_iota(I32, (kc, kc), 0),
                        1.0, 0.0).astype(BF16)

        def body(c, seen):
            m = mt_ref[c]
            tie = m == thr
            upto = _dot(tri, jnp.where(tie, 1.0, 0.0).astype(BF16)) + seen
            nm_ref[c] = jnp.where((m > thr) | (tie & (upto <= need)), 0.0, NEG_BIG)
            return upto[kc - 1:kc, :]
        lax.fori_loop(0, nch, body, jnp.zeros((1, qb), F32))

    crowded = selecting & (jnp.max(count_ge(thr)) > topk)
    lax.cond(crowded, tie_mask, plain_mask)

    npair = nh // 2
    pcols = [slice(pr * pair, (pr + 1) * pair) for pr in range(npair)]

    def fold(x):
        return x.reshape(kc // SUBLANE, SUBLANE, pair)

    def sweep1(c, mruns):
        r0 = pl.multiple_of(c * kc, kc)
        kh = khat_ref[pl.ds(r0, kc), :]
        k0 = jnp.clip(j - 2 * c, 0, 2)
        k1 = jnp.clip(j - 2 * c - 1, 0, 2)
        nm = nm_ref[c]
        nm2 = jnp.concatenate([nm, nm], axis=1)
        out = []
        for pr in range(npair):
            bias = jnp.concatenate([tb_ref[k0, :, pcols[pr]], tb_ref[k1, :, pcols[pr]]], axis=0)
            lg = _dot(kh, qrt_ref[:, pcols[pr]]) + bias + nm2
            lg_ref[c, :, pcols[pr]] = lg
            out.append(jnp.maximum(mruns[pr], jnp.max(fold(lg), axis=0)))
        return tuple(out)

    mruns = lax.fori_loop(0, nch, sweep1,
                          tuple(jnp.full((SUBLANE, pair), NEG_BIG, F32) for _ in range(npair)))
    mrows = [jnp.max(m, axis=0, keepdims=True) for m in mruns]
    acc_ref[...] = jnp.zeros(acc_ref.shape, F32)

    def sweep2(c, lsums):
        vt = vt_ref[c]
        out = []
        for pr in range(npair):
            p = jnp.exp2(lg_ref[c, :, pcols[pr]] - mrows[pr])
            out.append(lsums[pr] + jnp.sum(fold(p), axis=0))
            acc_ref[:, pcols[pr]] += _dot(vt, p.astype(BF16))
        return tuple(out)

    lsums = lax.fori_loop(0, nch, sweep2,
                          tuple(jnp.zeros((SUBLANE, pair), F32) for _ in range(npair)))
    for pr in range(npair):
        out = acc_ref[:, pcols[pr]] / jnp.sum(lsums[pr], axis=0, keepdims=True)
        for s in range(2):
            h = 2 * pr + s
            o_ref[:, h * dh:(h + 1) * dh] = out[:, s * qb:(s + 1) * qb].T.astype(o_ref.dtype)


def dsa_attention(z, rel_bias, q_norm, k_norm, bsz, seq, d):
    t = z.shape[0]
    nh, dh, nih, di = ATT_HEADS, d // ATT_HEADS, IDX_HEADS, IDX_DIM
    qb = Q_BLOCK
    nb = seq // qb
    topk = min(DSA_TOPK_MAX, seq // 4)
    assert topk % qb == 0 and dh == LANE and nih * di == d // 2 and nb % 2 == 0 and LANE % di == 0
    i = jnp.arange(qb)
    dist = i[None, :] - i[:, None]
    tabs = [rel_bias[_t5_bucket(jnp.maximum(dist + off * qb, 0))] for off in range(2)]
    far = jnp.broadcast_to(rel_bias[_t5_bucket(jnp.full((), 2 * qb, I32))], (qb, qb, nh))
    tb = jnp.stack(tabs + [far]).transpose(0, 1, 3, 2).reshape(3, qb, nh * qb).astype(F32)
    tb = tb * math.log2(math.e)
    cq = d // LANE
    return pl.pallas_call(
        functools.partial(_dsa_kernel, nh=nh, dh=dh, nih=nih, di=di, topk=topk, nb=nb),
        grid=(bsz, nb),
        in_specs=[
            pl.BlockSpec((qb, d), lambda b, j: (b * nb + j, 0)),
            pl.BlockSpec((seq, LANE), lambda b, j: (b, cq + cq // 2)),
            pl.BlockSpec((seq, LANE), lambda b, j: (b, cq + cq // 2 + 1)),
            pl.BlockSpec((qb, d // 2), lambda b, j: (b * nb + j, 2)),
            pl.BlockSpec((seq, LANE), lambda b, j: (b, cq + cq // 2 + 2)),
            pl.BlockSpec((qb, LANE), lambda b, j: (b * nb + j, cq + cq // 2 + 2)),
            pl.BlockSpec((3, qb, nh * qb), lambda b, j: (0, 0, 0)),
            pl.BlockSpec((1, dh), lambda b, j: (0, 0)),
            pl.BlockSpec((1, dh), lambda b, j: (0, 0)),
        ],
        out_specs=pl.BlockSpec((qb, d), lambda b, j: (b * nb + j, 0)),
        out_shape=jax.ShapeDtypeStruct((t, d), BF16),
        scratch_shapes=[
            pltpu.VMEM((seq, dh), BF16),
            pltpu.VMEM((nb // 2, dh, 2 * qb), BF16),
            pltpu.VMEM((dh, nh * qb), BF16),
            pltpu.VMEM((di, nih * qb), BF16),
            pltpu.VMEM((LANE, qb), F32),
            pltpu.VMEM((nb // 2, 2 * qb, qb), I32),
            pltpu.VMEM((nb // 2, 2 * qb, qb), F32),
            pltpu.VMEM((nb // 2, 2 * qb, nh * qb), F32),
            pltpu.VMEM((dh, nh * qb), F32),
        ],
        compiler_params=_cparams(("arbitrary", "arbitrary")),
        name="dsa_attention",
    )(z, z, z, z, z, z, tb, q_norm.reshape(1, dh), k_norm.reshape(1, dh))


def _dsa_weight(c_w_in, d):
    dh = d // ATT_HEADS
    nqi = IDX_HEADS * IDX_DIM
    q, k, v, qi, ki, wi = jnp.split(
        c_w_in, [d, d + dh, d + 2 * dh, d + 2 * dh + nqi, d + 2 * dh + nqi + IDX_DIM], axis=1)
    used = d + nqi + 2 * dh + IDX_DIM + IDX_HEADS
    total = -(-used // 512) * 512
    pad = jnp.zeros((d, total - used), c_w_in.dtype)
    return jnp.concatenate([q, qi, k, v, ki, wi, pad], axis=1).astype(BF16)


def kernel(x, c, rel_bias, w_router, b_router, norm_mix, ada_w_mix, ada_b_mix, norm_ffn, ada_w_ffn,
           ada_b_ffn, ab_w_in, ab_conv_w, ab_conv_b, ab_w_out, c_w_in, c_q_norm, c_k_norm, c_w_out,
           moe_w1, moe_w3, moe_w2):
    bsz, seq, d = x.shape
    depth = norm_mix.shape[0]
    t = bsz * seq
    ch = d // 2
    tm = min(512, seq)
    x2 = x.reshape(t, d)
    mod_mix = ada_mod(c, ada_w_mix, ada_b_mix).reshape(depth, bsz, 3, d)
    mod_ffn = ada_mod(c, ada_w_ffn, ada_b_ffn).reshape(depth, bsz, 3, d)
    stack = lambda w: w.astype(BF16).reshape((depth * N_EXPERTS,) + w.shape[2:])
    w1_bf, w3_bf, w2_bf = stack(moe_w1), stack(moe_w3), stack(moe_w2)
    for layer in range(depth):
        i = layer // 2
        if layer % 2 == 0:
            z = inproj(x2, mod_mix[layer], norm_mix[layer], ab_w_in[i].astype(BF16), seq,
                       min(1024, seq), 1024)
            y_a = conv_mixer(z, ab_conv_w[i], ab_conv_b[i], bsz, seq, tm)
            y_b = retention(z, bsz, seq, ch, 3 * ch)
            x2 = outproj(y_a, y_b, (0, 0), ab_w_out[i].astype(BF16), x2, mod_mix[layer], seq, tm)
        else:
            w_c = _dsa_weight(c_w_in[i], d)
            z = inproj(x2, mod_mix[layer], norm_mix[layer], w_c, seq, tm, w_c.shape[1] // 2)
            o = dsa_attention(z, rel_bias, c_q_norm[i], c_k_norm[i], bsz, seq, d)
            x2 = outproj(o, o, (0, 1), c_w_out[i].astype(BF16), x2, mod_mix[layer], seq, tm)
        x2 = moe(x2, mod_ffn[layer], norm_ffn[layer], w_router, b_router,
                 w1_bf, w3_bf, w2_bf, layer * N_EXPERTS, seq)
    return x2.reshape(bsz, seq, d)
```

```python
import functools
import math

import jax
import jax.numpy as jnp
from jax import lax
from jax.experimental import pallas as pl
from jax.experimental.pallas import tpu as pltpu

F32 = jnp.float32
BF16 = jnp.bfloat16
I32 = jnp.int32
I16 = jnp.int16

RMS_EPS = 1e-6
CONV_WIDTH = 3
RET_HEADS = 4
ATT_HEADS = 16
IDX_HEADS = 16
IDX_DIM = 64
DSA_TOPK_MAX = 256
Q_BLOCK = 128
N_BUCKETS = 32
MAX_DISTANCE = 128
N_EXPERTS = 16
N_GROUPS = 4
EXPERTS_PER_GROUP = N_EXPERTS // N_GROUPS
RET_CHUNK = 256
MOE_ROWS = 256
DMA_UNROLL = 8
ONES_ROWS = 16
LANE = 128
SUBLANE = 8
VMEM_LIMIT = 48 * 1024 * 1024
NEG_BIG = -1e30
INT_MIN = -(2 ** 31)
I16_MIN = -(2 ** 15)
PACK16 = 16


def _cparams(sem, vmem=VMEM_LIMIT):
    return pltpu.CompilerParams(dimension_semantics=sem, vmem_limit_bytes=vmem)


def _dot(a, b):
    return jnp.dot(a, b, preferred_element_type=F32)


def _dot_nt(a, b):
    return lax.dot_general(a, b, (((1,), (1,)), ((), ())), preferred_element_type=F32)


def _dot_tn(a, b):
    return lax.dot_general(a, b, (((0,), (0,)), ((), ())), preferred_element_type=F32)


def _silu(x):
    return x * jax.nn.sigmoid(x)


def _ada_kernel(c_ref, w_ref, b_ref, o_ref):
    cs = _silu(c_ref[...])
    o_ref[0] = _dot(cs.astype(BF16), w_ref[0].astype(BF16)) + b_ref[0]


def ada_mod(c, ada_w, ada_b):
    nl, d, n3 = ada_w.shape
    bsz = c.shape[0]
    tn = min(512, n3)
    return pl.pallas_call(
        _ada_kernel,
        grid=(nl, n3 // tn),
        in_specs=[
            pl.BlockSpec((bsz, d), lambda l, j: (0, 0)),
            pl.BlockSpec((1, d, tn), lambda l, j: (l, 0, j)),
            pl.BlockSpec((1, 1, tn), lambda l, j: (l, 0, j)),
        ],
        out_specs=pl.BlockSpec((1, bsz, tn), lambda l, j: (l, 0, j)),
        out_shape=jax.ShapeDtypeStruct((nl, bsz, n3), F32),
        compiler_params=_cparams(("parallel", "parallel")),
        name="ada_mod",
    )(c, ada_w, ada_b.reshape(nl, 1, n3))


def _modulated_norm(x, mod_ref, nw_ref):
    r = lax.rsqrt(jnp.mean(x * x, axis=-1, keepdims=True) + RMS_EPS)
    return x * r * nw_ref[...] * (1.0 + mod_ref[0, 1:2, :]) + mod_ref[0, 0:1, :]


def _inproj_kernel(x_ref, mod_ref, nw_ref, w_ref, o_ref, h_ref):
    @pl.when(pl.program_id(1) == 0)
    def _():
        h_ref[...] = _modulated_norm(x_ref[...], mod_ref, nw_ref).astype(BF16)

    o_ref[...] = _dot(h_ref[...], w_ref[...]).astype(o_ref.dtype)


def inproj(x2, mod, norm_w, w_bf, seq, tm, tn):
    t, d = x2.shape
    n = w_bf.shape[1]
    per = seq // tm
    return pl.pallas_call(
        _inproj_kernel,
        grid=(t // tm, n // tn),
        in_specs=[
            pl.BlockSpec((tm, d), lambda i, j: (i, 0)),
            pl.BlockSpec((1, 3, d), lambda i, j: (i // per, 0, 0)),
            pl.BlockSpec((1, d), lambda i, j: (0, 0)),
            pl.BlockSpec((d, tn), lambda i, j: (0, j)),
        ],
        out_specs=pl.BlockSpec((tm, tn), lambda i, j: (i, j)),
        out_shape=jax.ShapeDtypeStruct((t, n), BF16),
        scratch_shapes=[pltpu.VMEM((tm, d), BF16)],
        compiler_params=_cparams(("parallel", "arbitrary")),
        name="inproj",
    )(x2, mod, norm_w.reshape(1, d), w_bf)


def _conv_kernel(b_ref, c_ref, v_ref, w_ref, cb_ref, o_ref, u_ref, *, tm):
    s = pl.program_id(1)
    u = c_ref[...].astype(F32) * v_ref[...].astype(F32)

    @pl.when(s == 0)
    def _():
        u_ref[0:SUBLANE, :] = jnp.zeros((SUBLANE, u.shape[1]), F32)

    @pl.when(s > 0)
    def _():
        u_ref[0:SUBLANE, :] = u_ref[tm:tm + SUBLANE, :]

    u_ref[SUBLANE:SUBLANE + tm, :] = u
    conv = (cb_ref[...]
            + u_ref[SUBLANE - 2:SUBLANE - 2 + tm, :] * w_ref[0:1, :]
            + u_ref[SUBLANE - 1:SUBLANE - 1 + tm, :] * w_ref[1:2, :]
            + u * w_ref[2:3, :])
    o_ref[...] = (b_ref[...].astype(F32) * conv).astype(o_ref.dtype)


def conv_mixer(z, conv_w, conv_b, bsz, seq, tm):
    t = z.shape[0]
    ch = conv_w.shape[1]
    per = seq // tm
    row = lambda g: (lambda b, s: (b * per + s, g))
    return pl.pallas_call(
        functools.partial(_conv_kernel, tm=tm),
        grid=(bsz, per),
        in_specs=[
            pl.BlockSpec((tm, ch), row(0)),
            pl.BlockSpec((tm, ch), row(1)),
            pl.BlockSpec((tm, ch), row(2)),
            pl.BlockSpec((CONV_WIDTH, ch), lambda b, s: (0, 0)),
            pl.BlockSpec((1, ch), lambda b, s: (0, 0)),
        ],
        out_specs=pl.BlockSpec((tm, ch), lambda b, s: (b * per + s, 0)),
        out_shape=jax.ShapeDtypeStruct((t, ch), BF16),
        scratch_shapes=[pltpu.VMEM((tm + SUBLANE, ch), F32)],
        compiler_params=_cparams(("parallel", "arbitrary")),
        name="conv_mixer",
    )(z, z, z, conv_w, conv_b.reshape(1, ch))


def _retention_kernel(q_ref, k_ref, v_ref, g_ref, cos_ref, sin_ref, din_ref, dcr_ref, dst_ref,
                      o_ref, st_ref, *, nc, ck, dk):
    half = dk // 2
    st_ref[...] = jnp.zeros(st_ref.shape, F32)
    d_intra = din_ref[0]
    d_cross = dcr_ref[0]
    d_state = dst_ref[0]
    g_chunk = dcr_ref[0, ck - 1:ck, :]

    def rot(x, cos, sin):
        x1, x2 = x[:, :half], x[:, half:]
        return jnp.concatenate([x1 * cos - x2 * sin, x1 * sin + x2 * cos], axis=-1)

    def body(c, carry):
        r0 = pl.multiple_of(c * ck, ck)
        cos = cos_ref[pl.ds(r0, ck), :]
        sin = sin_ref[pl.ds(r0, ck), :]
        q = rot(q_ref[pl.ds(r0, ck), :].astype(F32), cos, sin)
        k = rot(k_ref[pl.ds(r0, ck), :].astype(F32), cos, sin) * (dk ** -0.5)
        v = v_ref[pl.ds(r0, ck), :]
        intra = _dot_nt(q.astype(BF16), k.astype(BF16)) * d_intra
        state = st_ref[...]
        o = _dot(intra.astype(BF16), v) + _dot((q * d_cross).astype(BF16), state.astype(BF16))
        st_ref[...] = state * g_chunk + _dot_tn((k * d_state).astype(BF16), v)
        r = lax.rsqrt(jnp.mean(o * o, axis=-1, keepdims=True) + RMS_EPS)
        o_ref[pl.ds(r0, ck), :] = (o * r * _silu(g_ref[pl.ds(r0, ck), :].astype(F32))).astype(o_ref.dtype)
        return carry

    lax.fori_loop(0, nc, body, 0)


def retention(z, bsz, seq, ch, col0):
    t = z.shape[0]
    nh = RET_HEADS
    dk = ch // nh
    ck = min(RET_CHUNK, seq)
    nc = seq // ck
    half = dk // 2
    pos = jnp.arange(seq, dtype=F32)
    inv = 1.0 / (10000.0 ** jnp.linspace(0.0, 1.0, half, dtype=F32))
    ang = pos[:, None] * inv[None, :]
    cos, sin = jnp.cos(ang), jnp.sin(ang)
    log_g = jnp.log(1.0 - 2.0 ** (-5.0 - jnp.arange(nh, dtype=F32)))
    i = jnp.arange(ck, dtype=F32)
    diff = i[:, None] - i[None, :]
    d_intra = jnp.where(diff >= 0, jnp.exp(log_g[:, None, None] * jnp.maximum(diff, 0.0)), 0.0)
    d_cross = jnp.exp(log_g[:, None] * (i[None, :] + 1.0))[..., None]
    d_state = jnp.exp(log_g[:, None] * (ck - 1.0 - i[None, :]))[..., None]
    cb = col0 // dk
    col = lambda g: (lambda b, h: (b, cb + g * nh + h))
    return pl.pallas_call(
        functools.partial(_retention_kernel, nc=nc, ck=ck, dk=dk),
        grid=(bsz, nh),
        in_specs=[
            pl.BlockSpec((seq, dk), col(0)),
            pl.BlockSpec((seq, dk), col(1)),
            pl.BlockSpec((seq, dk), col(2)),
            pl.BlockSpec((seq, dk), col(3)),
            pl.BlockSpec((seq, half), lambda b, h: (0, 0)),
            pl.BlockSpec((seq, half), lambda b, h: (0, 0)),
            pl.BlockSpec((1, ck, ck), lambda b, h: (h, 0, 0)),
            pl.BlockSpec((1, ck, 1), lambda b, h: (h, 0, 0)),
            pl.BlockSpec((1, ck, 1), lambda b, h: (h, 0, 0)),
        ],
        out_specs=pl.BlockSpec((seq, dk), lambda b, h: (b, h)),
        out_shape=jax.ShapeDtypeStruct((t, ch), BF16),
        scratch_shapes=[pltpu.VMEM((dk, dk), F32)],
        compiler_params=_cparams(("parallel", "parallel")),
        name="retention",
    )(z, z, z, z, cos, sin, d_intra, d_cross, d_state)


def _outproj_kernel(ya_ref, yb_ref, wa_ref, wb_ref, x_ref, mod_ref, o_ref):
    y = _dot(ya_ref[...], wa_ref[...]) + _dot(yb_ref[...], wb_ref[...])
    o_ref[...] = x_ref[...] + mod_ref[0, 2:3, :] * y


def outproj(ya, yb, cols, w_bf, x2, mod, seq, tm):
    t, d = x2.shape
    kh = w_bf.shape[0] // 2
    per = seq // tm
    return pl.pallas_call(
        _outproj_kernel,
        grid=(t // tm,),
        in_specs=[
            pl.BlockSpec((tm, kh), lambda i: (i, cols[0])),
            pl.BlockSpec((tm, kh), lambda i: (i, cols[1])),
            pl.BlockSpec((kh, d), lambda i: (0, 0)),
            pl.BlockSpec((kh, d), lambda i: (1, 0)),
            pl.BlockSpec((tm, d), lambda i: (i, 0)),
            pl.BlockSpec((1, 3, d), lambda i: (i // per, 0, 0)),
        ],
        out_specs=pl.BlockSpec((tm, d), lambda i: (i, 0)),
        out_shape=jax.ShapeDtypeStruct((t, d), F32),
        compiler_params=_cparams(("parallel",)),
        name="outproj",
    )(ya, yb, w_bf, w_bf, x2, mod)


def _router_kernel(x_ref, mod_ref, nw_ref, wr_ref, br_ref, h_ref, ri_ref, rf_ref, cnt_ref, carry_ref,
                   *, tm):
    @pl.when(pl.program_id(0) == 0)
    def _():
        carry_ref[...] = jnp.zeros(carry_ref.shape, F32)

    h = _modulated_norm(x_ref[...], mod_ref, nw_ref)
    h_ref[...] = h
    logits = _dot_nt(wr_ref[...], h.astype(BF16))
    mx = jnp.max(logits, axis=0, keepdims=True)
    ex = jnp.exp(logits - mx)
    probs = ex / jnp.sum(ex, axis=0, keepdims=True)
    sel = probs + br_ref[...]
    s = [sel[e:e + 1, :] for e in range(N_EXPERTS)]
    p = [probs[e:e + 1, :] for e in range(N_EXPERTS)]
    epg = EXPERTS_PER_GROUP

    def first_argmax(vals, exclude=None):
        best = jnp.full_like(vals[0], -jnp.inf)
        idx = jnp.zeros(vals[0].shape, I32)
        for j, vj in enumerate(vals):
            better = vj > best
            if exclude is not None:
                better = better & (exclude != j)
            idx = jnp.where(better, j, idx)
            best = jnp.where(better, vj, best)
        return idx

    gscore = []
    for g in range(N_GROUPS):
        gs = s[g * epg:(g + 1) * epg]
        best = None
        for a in range(epg):
            for b in range(a + 1, epg):
                pair = gs[a] + gs[b]
                best = pair if best is None else jnp.maximum(best, pair)
        gscore.append(best)
    grp = first_argmax(gscore)

    def pick(rows, index, n):
        out = rows[n - 1]
        for j in range(n - 2, -1, -1):
            out = jnp.where(index == j, rows[j], out)
        return out

    in_s = [pick([s[g * epg + j] for g in range(N_GROUPS)], grp, N_GROUPS) for j in range(epg)]
    in_p = [pick([p[g * epg + j] for g in range(N_GROUPS)], grp, N_GROUPS) for j in range(epg)]
    i1 = first_argmax(in_s)
    i2 = first_argmax(in_s, exclude=i1)
    p1 = pick(in_p, i1, epg)
    p2 = pick(in_p, i2, epg)
    e1 = grp * epg + i1
    e2 = grp * epg + i2
    den = p1 + p2
    g1 = p1 / den
    g2 = p2 / den

    eidx = lax.broadcasted_iota(I32, (N_EXPERTS, tm), 0)
    member = (eidx == e1) | (eidx == e2)
    member_f = jnp.where(member, 1.0, 0.0)
    before = lax.broadcasted_iota(I32, (tm, tm), 0) < lax.broadcasted_iota(I32, (tm, tm), 1)
    prefix = _dot(member_f.astype(BF16), jnp.where(before, 1.0, 0.0).astype(BF16))
    base = prefix + carry_ref[:, 0:1]
    rank1 = jnp.sum(jnp.where(eidx == e1, base, 0.0), axis=0, keepdims=True).astype(I32)
    rank2 = jnp.sum(jnp.where(eidx == e2, base, 0.0), axis=0, keepdims=True).astype(I32)
    carry_ref[...] = carry_ref[...] + jnp.sum(member_f, axis=1, keepdims=True)
    cnt_ref[...] = carry_ref[...].astype(I32)

    zi = jnp.zeros((SUBLANE - 4, tm), I32)
    ri_ref[...] = jnp.concatenate([e1, e2, rank1, rank2, zi], axis=0)
    zf = jnp.zeros((SUBLANE - 2, tm), F32)
    rf_ref[...] = jnp.concatenate([g1, g2, zf], axis=0)


def router(x2, mod, norm_w, w_router, b_router, seq, tm):
    t, d = x2.shape
    per = seq // tm
    ne = N_EXPERTS
    return pl.pallas_call(
        functools.partial(_router_kernel, tm=tm),
        grid=(t // tm,),
        in_specs=[
            pl.BlockSpec((tm, d), lambda i: (i, 0)),
            pl.BlockSpec((1, 3, d), lambda i: (i // per, 0, 0)),
            pl.BlockSpec((1, d), lambda i: (0, 0)),
            pl.BlockSpec((ne, d), lambda i: (0, 0)),
            pl.BlockSpec((ne, 1), lambda i: (0, 0)),
        ],
        out_specs=[
            pl.BlockSpec((tm, d), lambda i: (i, 0)),
            pl.BlockSpec((SUBLANE, tm), lambda i: (0, i)),
            pl.BlockSpec((SUBLANE, tm), lambda i: (0, i)),
            pl.BlockSpec((ne, LANE), lambda i: (0, 0)),
        ],
        out_shape=[
            jax.ShapeDtypeStruct((t, d), F32),
            jax.ShapeDtypeStruct((SUBLANE, t), I32),
            jax.ShapeDtypeStruct((SUBLANE, t), F32),
            jax.ShapeDtypeStruct((ne, LANE), I32),
        ],
        scratch_shapes=[pltpu.VMEM((ne, LANE), F32)],
        compiler_params=_cparams(("arbitrary",)),
        name="router",
    )(x2, mod, norm_w.reshape(1, d), w_router.T.astype(BF16), b_router.reshape(ne, 1))


def _dispatch_kernel(dest_ref, pad0_ref, padn_ref, h_ref, xs_ref, hbuf_ref, zero_ref, sem, zsem,
                     *, tm, nt):
    i = pl.program_id(0)
    cur = i % 2
    base = i * tm

    def zero_row(e, r):
        return pltpu.make_async_copy(zero_ref.at[pl.ds(0, 1), :],
                                     xs_ref.at[pl.ds(pad0_ref[e] + r, 1), :], zsem)

    def tile_done(buf):
        for _ in range(2):
            pltpu.make_async_copy(hbuf_ref.at[buf], xs_ref.at[pl.ds(0, tm), :], sem.at[buf]).wait()

    @pl.when(i == 0)
    def _():
        zero_ref[...] = jnp.zeros(zero_ref.shape, F32)
        for e in range(N_EXPERTS):
            lax.fori_loop(0, padn_ref[e], lambda r, c, e=e: (zero_row(e, r).start(), c)[1], 0)

    hbuf_ref[cur] = h_ref[...]

    def start(r, carry):
        for slot in range(2):
            d = dest_ref[2 * (base + r) + slot]
            pltpu.make_async_copy(hbuf_ref.at[cur, pl.ds(r, 1), :], xs_ref.at[pl.ds(d, 1), :],
                                  sem.at[cur]).start()
        return carry

    lax.fori_loop(0, tm, start, 0, unroll=DMA_UNROLL)

    @pl.when(i > 0)
    def _():
        tile_done(1 - cur)

    @pl.when(i == nt - 1)
    def _():
        tile_done(cur)
        for e in range(N_EXPERTS):
            lax.fori_loop(0, padn_ref[e], lambda r, c, e=e: (zero_row(e, r).wait(), c)[1], 0)


def dispatch(h, dest, pad_start, pad_count, n_rows, tm):
    t, d = h.shape
    return pl.pallas_call(
        functools.partial(_dispatch_kernel, tm=tm, nt=t // tm),
        grid_spec=pltpu.PrefetchScalarGridSpec(
            num_scalar_prefetch=3,
            grid=(t // tm,),
            in_specs=[pl.BlockSpec((tm, d), lambda i, *_: (i, 0))],
            out_specs=pl.BlockSpec(memory_space=pl.ANY),
            scratch_shapes=[pltpu.VMEM((2, tm, d), F32), pltpu.VMEM((SUBLANE, d), F32),
                            pltpu.SemaphoreType.DMA((2,)), pltpu.SemaphoreType.DMA(())],
        ),
        out_shape=jax.ShapeDtypeStruct((n_rows, d), F32),
        compiler_params=_cparams(("arbitrary",)),
        name="moe_dispatch",
    )(dest, pad_start, pad_count, h)


def _ffn_kernel(be_ref, bv_ref, xs_ref, w1_ref, w3_ref, w2_ref, ys_ref):
    valid = bv_ref[pl.program_id(0)]

    @pl.when(valid > 0)
    def _():
        rows = lax.broadcasted_iota(I32, xs_ref.shape, 0)
        x = jnp.where(rows < valid, xs_ref[...], 0.0).astype(BF16)
        h1 = _dot(x, w1_ref[0])
        h3 = _dot(x, w3_ref[0])
        a = (_silu(h1) * h3).astype(BF16)
        ys_ref[...] = _dot(a, w2_ref[0])

    @pl.when(valid <= 0)
    def _():
        ys_ref[...] = jnp.zeros(ys_ref.shape, F32)


def grouped_ffn(xs, block_e, block_valid, w1, w3, w2, bm):
    n_rows, d = xs.shape
    f = w1.shape[2]
    return pl.pallas_call(
        _ffn_kernel,
        grid_spec=pltpu.PrefetchScalarGridSpec(
            num_scalar_prefetch=2,
            grid=(n_rows // bm,),
            in_specs=[
                pl.BlockSpec((bm, d), lambda i, be, bv: (i, 0)),
                pl.BlockSpec((1, d, f), lambda i, be, bv: (be[i], 0, 0)),
                pl.BlockSpec((1, d, f), lambda i, be, bv: (be[i], 0, 0)),
                pl.BlockSpec((1, f, d), lambda i, be, bv: (be[i], 0, 0)),
            ],
            out_specs=pl.BlockSpec((bm, d), lambda i, be, bv: (i, 0)),
        ),
        out_shape=jax.ShapeDtypeStruct((n_rows, d), F32),
        compiler_params=_cparams(("arbitrary",)),
        name="moe_ffn",
    )(block_e, block_valid, xs, w1, w3, w2)


def _combine_kernel(dest_ref, ys_ref, x_ref, mod_ref, gf_ref, o_ref, a_ref, b_ref, sem, *, tm, nt):
    i = pl.program_id(0)

    def fetch(tile, buf):
        base = tile * tm

        def start(r, carry):
            for slot, ref in ((0, a_ref), (1, b_ref)):
                d = dest_ref[2 * (base + r) + slot]
                pltpu.make_async_copy(ys_ref.at[pl.ds(d, 1), :], ref.at[buf, pl.ds(r, 1), :],
                                      sem.at[buf]).start()
            return carry

        lax.fori_loop(0, tm, start, 0, unroll=DMA_UNROLL)

    @pl.when(i == 0)
    def _():
        fetch(0, 0)

    @pl.when(i + 1 < nt)
    def _():
        fetch(i + 1, (i + 1) % 2)

    cur = i % 2
    pltpu.make_async_copy(ys_ref.at[pl.ds(0, tm), :], a_ref.at[cur], sem.at[cur]).wait()
    pltpu.make_async_copy(ys_ref.at[pl.ds(0, tm), :], b_ref.at[cur], sem.at[cur]).wait()
    y = gf_ref[:, 0:1] * a_ref[cur] + gf_ref[:, 1:2] * b_ref[cur]
    o_ref[...] = x_ref[...] + mod_ref[0, 2:3, :] * y


def combine(ys, dest, x2, mod, gates_t, seq, tm):
    t, d = x2.shape
    per = seq // tm
    return pl.pallas_call(
        functools.partial(_combine_kernel, tm=tm, nt=t // tm),
        grid_spec=pltpu.PrefetchScalarGridSpec(
            num_scalar_prefetch=1,
            grid=(t // tm,),
            in_specs=[
                pl.BlockSpec(memory_space=pl.ANY),
                pl.BlockSpec((tm, d), lambda i, dest: (i, 0)),
                pl.BlockSpec((1, 3, d), lambda i, dest: (i // per, 0, 0)),
                pl.BlockSpec((tm, SUBLANE), lambda i, dest: (i, 0)),
            ],
            out_specs=pl.BlockSpec((tm, d), lambda i, dest: (i, 0)),
            scratch_shapes=[pltpu.VMEM((2, tm, d), F32), pltpu.VMEM((2, tm, d), F32),
                            pltpu.SemaphoreType.DMA((2,))],
        ),
        out_shape=jax.ShapeDtypeStruct((t, d), F32),
        compiler_params=_cparams(("arbitrary",)),
        name="moe_combine",
    )(dest, ys, x2, mod, gates_t)


def moe(x2, mod, norm_w, w_router, b_router, w1, w3, w2, expert0, seq):
    t, d = x2.shape
    bm = MOE_ROWS
    tm = min(256, seq)
    h, ri, rf, cnt = router(x2, mod, norm_w, w_router, b_router, seq, tm)
    ne = N_EXPERTS
    counts = cnt[:, 0]
    nblk = (counts + bm - 1) // bm
    blk_end = jnp.cumsum(nblk)
    blk_start = blk_end - nblk
    eids = jnp.arange(ne, dtype=I32)
    e12 = ri[0:2].T
    row0 = jnp.sum(jnp.where(e12[..., None] == eids, blk_start * bm, 0), axis=-1)
    dest = (row0 + ri[2:4].T).reshape(2 * t).astype(I32)
    n_blocks = (2 * t) // bm + ne
    bidx = jnp.arange(n_blocks, dtype=I32)
    block_e = jnp.minimum(jnp.sum(bidx[:, None] >= blk_end[None, :], axis=1), ne - 1).astype(I32)
    onehot = block_e[:, None] == eids
    cnt_b = jnp.sum(jnp.where(onehot, counts, 0), axis=1)
    start_b = jnp.sum(jnp.where(onehot, blk_start, 0), axis=1)
    block_valid = jnp.clip(cnt_b - (bidx - start_b) * bm, 0, bm).astype(I32)
    pad_start = (blk_start * bm + counts).astype(I32)
    pad_end = jnp.where(eids == ne - 1, n_blocks * bm, blk_end * bm)
    xs = dispatch(h, dest, pad_start, (pad_end - pad_start).astype(I32), n_blocks * bm, tm)
    ys = grouped_ffn(xs, block_e + expert0, block_valid, w1, w3, w2, bm)
    return combine(ys, dest, x2, mod, rf.T, seq, tm)


def _t5_bucket(n):
    max_exact = N_BUCKETS // 2
    nf = jnp.maximum(n, 1).astype(F32)
    large = max_exact + (jnp.log(nf / max_exact) / math.log(MAX_DISTANCE / max_exact)
                         * (N_BUCKETS - max_exact)).astype(I32)
    large = jnp.minimum(large, N_BUCKETS - 1)
    return jnp.where(n < max_exact, n, large)


def _sortable(score):
    bits = pltpu.bitcast(score, I32)
    return bits ^ ((bits >> 31) & 0x7FFFFFFF)


def _dsa_kernel(q_ref, k_ref, v_ref, qi_ref, kw_ref, wq_ref, tb_ref, qn_ref, kn_ref, o_ref,
                khat_ref, vt_ref, qrt_ref, qirt_ref, wt_ref, mt_ref, hi_ref, lo_ref, lo2_ref, nm_ref, lg_ref,
                *acc_refs,
                nh, dh, nih, di, topk, nb):
    j = pl.program_id(1)
    qb = Q_BLOCK
    kc = 2 * qb
    nch = (j + 2) // 2
    pair = 2 * qb

    eye = jnp.where(lax.broadcasted_iota(I32, (qb, qb), 0) == lax.broadcasted_iota(I32, (qb, qb), 1),
                    1.0, 0.0).astype(BF16)

    def transpose(x):
        return _dot_nt(eye, x)

    @pl.when(j == 0)
    def _():
        k = k_ref[...].astype(F32)
        r = lax.rsqrt(jnp.mean(k * k, axis=-1, keepdims=True) + RMS_EPS)
        khat_ref[...] = (k * r * kn_ref[...]).astype(BF16)
        for c in range(nb // 2):
            for s in range(2):
                blk = v_ref[(2 * c + s) * qb:(2 * c + s + 1) * qb, :]
                vt_ref[c, 0:dh, s * qb:(s + 1) * qb] = transpose(blk).astype(BF16)
            vt_ref[c, dh:dh + ONES_ROWS, :] = jnp.ones((ONES_ROWS, kc), BF16)

    qscale = (dh ** -0.5) * math.log2(math.e)
    for h in range(nh):
        qh = q_ref[:, h * dh:(h + 1) * dh].astype(F32)
        r = lax.rsqrt(jnp.mean(qh * qh, axis=-1, keepdims=True) + RMS_EPS)
        qhat = (qh * r * qn_ref[...] * qscale).astype(BF16)
        qrt_ref[:, h * qb:(h + 1) * qb] = transpose(qhat).astype(BF16)
    for g in range(nih * di // LANE):
        two = transpose(qi_ref[:, g * LANE:(g + 1) * LANE])
        for s in range(LANE // di):
            h = g * (LANE // di) + s
            qirt_ref[:, h * qb:(h + 1) * qb] = two[s * di:(s + 1) * di, :].astype(BF16)
    wt_ref[...] = transpose(wq_ref[...])

    key_l = lax.broadcasted_iota(I32, (kc, qb), 0)
    q_pos = j * qb + lax.broadcasted_iota(I32, (kc, qb), 1)

    def score_body(c, carry):
        r0 = pl.multiple_of(c * kc, kc)
        ki = kw_ref[pl.ds(r0, kc), 0:di]
        acc = jnp.zeros((kc, qb), F32)
        for g in range(nih // 2):
            rel = _dot(ki, qirt_ref[:, g * pair:(g + 1) * pair])
            for s in range(2):
                h = 2 * g + s
                acc = acc + wt_ref[di + h:di + h + 1, :] * jnp.maximum(rel[:, s * qb:(s + 1) * qb], 0.0)
        m = jnp.where(r0 + key_l <= q_pos, _sortable(acc), INT_MIN)
        mt_ref[c] = m
        hi_ref[c] = (m >> 16).astype(I16)
        lo_ref[c] = ((m & 0xFFFF) + I16_MIN).astype(I16)
        return carry

    lax.fori_loop(0, nch, score_body, 0)

    def count_ge(cand):
        def cbody(c, acc):
            hit = jnp.where(mt_ref[c] >= cand, 1, 0)
            return acc + jnp.sum(hit.reshape(kc // SUBLANE, SUBLANE, qb), axis=0)
        acc = lax.fori_loop(0, nch, cbody, jnp.zeros((SUBLANE, qb), I32))
        return jnp.sum(acc, axis=0, keepdims=True)

    def fold16(hit):
        parts = [hit[PACK16 * i:PACK16 * (i + 1), :] for i in range(kc // PACK16)]
        while len(parts) > 1:
            parts = [parts[i] + parts[i + 1] for i in range(0, len(parts), 2)]
        return parts[0]

    def total16(acc):
        return jnp.sum(acc.astype(F32), axis=0, keepdims=True).astype(I32)

    one, nil = jnp.ones((), BF16), jnp.zeros((), BF16)

    def count16(src_ref, cand):
        c16 = cand.astype(I16)
        def cbody(c, acc):
            return acc + fold16(jnp.where(src_ref[c] >= c16, one, nil))
        return total16(lax.fori_loop(0, nch, cbody, jnp.zeros((PACK16, qb), BF16)))

    def search16(src_ref, kth):
        zero = jnp.zeros((1, qb), I32)
        ans0 = jnp.where(count16(src_ref, zero) >= kth, zero, I16_MIN)

        def bit_body(bi, ans):
            cand = ans | (1 << (14 - bi))
            return jnp.where(count16(src_ref, cand) >= kth, cand, ans)

        return lax.fori_loop(0, 15, bit_body, ans0)

    def search():
        top = search16(hi_ref, jnp.full((1, qb), topk, I32))
        top16 = top.astype(I16)

        def narrow(c, acc):
            hi = hi_ref[c]
            lo2_ref[c] = jnp.where(hi == top16, lo_ref[c], jnp.full((), I16_MIN, I16))
            return acc + fold16(jnp.where(hi > top16, one, nil))

        above = total16(lax.fori_loop(0, nch, narrow, jnp.zeros((PACK16, qb), BF16)))
        low = search16(lo2_ref, topk - above)
        return top * 65536 + (low - I16_MIN)

    selecting = (j + 1) * qb > topk
    thr = lax.cond(selecting, search, lambda: jnp.full((1, qb), INT_MIN + 1, I32))

    def plain_mask():
        def body(c, carry):
            nm_ref[c] = jnp.where(mt_ref[c] >= thr, 0.0, NEG_BIG)
            return carry
        lax.fori_loop(0, nch, body, 0)

    def tie_mask():
        need = (topk - count_ge(thr + 1)).astype(F32)
        tri = jnp.where(lax.broadcasted_iota(I32, (kc, kc), 1) <= lax.broadcasted_iota(I32, (kc, kc), 0),
                        1.0, 0.0).astype(BF16)

        def body(c, seen):
            m = mt_ref[c]
            tie = m == thr
            upto = _dot(tri, jnp.where(tie, 1.0, 0.0).astype(BF16)) + seen
            nm_ref[c] = jnp.where((m > thr) | (tie & (upto <= need)), 0.0, NEG_BIG)
            return upto[kc - 1:kc, :]
        lax.fori_loop(0, nch, body, jnp.zeros((1, qb), F32))

    crowded = selecting & (jnp.max(count_ge(thr)) > topk)
    lax.cond(crowded, tie_mask, plain_mask)

    npair = nh // 2
    pcols = [slice(pr * pair, (pr + 1) * pair) for pr in range(npair)]
    nfar = jnp.maximum((j - 1) // 2, 0)

    def sweep1(c, mruns, near):
        r0 = pl.multiple_of(c * kc, kc)
        kh = khat_ref[pl.ds(r0, kc), :]
        nm = nm_ref[c]
        nm2 = jnp.concatenate([nm, nm], axis=1)
        if near:
            k0 = jnp.clip(j - 2 * c, 0, 2)
            k1 = jnp.clip(j - 2 * c - 1, 0, 2)
        out = []
        for pr in range(npair):
            lg = _dot(kh, qrt_ref[:, pcols[pr]]) + nm2
            if near:
                lg = lg + jnp.concatenate([tb_ref[k0, :, pcols[pr]], tb_ref[k1, :, pcols[pr]]], axis=0)
            lg_ref[c, :, pcols[pr]] = lg
            out.append(jnp.maximum(mruns[pr], jnp.max(lg.reshape(kc // SUBLANE, SUBLANE, pair), axis=0)))
        return tuple(out)

    mruns = tuple(jnp.full((SUBLANE, pair), NEG_BIG, F32) for _ in range(npair))
    mruns = lax.fori_loop(0, nfar, functools.partial(sweep1, near=False), mruns)
    mruns = lax.fori_loop(nfar, nch, functools.partial(sweep1, near=True), mruns)
    mrows = [jnp.max(m, axis=0, keepdims=True) for m in mruns]
    for acc_ref in acc_refs:
        acc_ref[...] = jnp.zeros(acc_ref.shape, F32)

    def sweep2(c, carry):
        vt = vt_ref[c]
        for pr in range(npair):
            p = jnp.exp2((lg_ref[c, :, pcols[pr]] - mrows[pr]).astype(BF16))
            acc_refs[pr][...] += _dot(vt, p)
        return carry

    lax.fori_loop(0, nch, sweep2, 0)
    for pr in range(npair):
        out = acc_refs[pr][0:dh, :] / acc_refs[pr][dh:dh + 1, :]
        for s in range(2):
            h = 2 * pr + s
            o_ref[:, h * dh:(h + 1) * dh] = transpose(
                out[:, s * qb:(s + 1) * qb].astype(BF16)).astype(o_ref.dtype)


def dsa_attention(z, rel_bias, q_norm, k_norm, bsz, seq, d):
    t = z.shape[0]
    nh, dh, nih, di = ATT_HEADS, d // ATT_HEADS, IDX_HEADS, IDX_DIM
    qb = Q_BLOCK
    nb = seq // qb
    topk = min(DSA_TOPK_MAX, seq // 4)
    assert topk % qb == 0 and dh == LANE and nih * di == d // 2 and nb % 2 == 0 and LANE % di == 0
    assert (nb // 2) * (2 * qb // PACK16) <= 256
    i = jnp.arange(qb)
    dist = i[None, :] - i[:, None]
    tabs = [rel_bias[_t5_bucket(jnp.maximum(dist + off * qb, 0))] for off in range(2)]
    far = jnp.broadcast_to(rel_bias[_t5_bucket(jnp.full((), 2 * qb, I32))], (qb, qb, nh))
    tb = jnp.stack(tabs + [far]).transpose(0, 1, 3, 2).reshape(3, qb, nh * qb).astype(F32)
    tb = (tb - tb[2:3]) * math.log2(math.e)
    cq = d // LANE
    return pl.pallas_call(
        functools.partial(_dsa_kernel, nh=nh, dh=dh, nih=nih, di=di, topk=topk, nb=nb),
        grid=(bsz, nb),
        in_specs=[
            pl.BlockSpec((qb, d), lambda b, j: (b * nb + j, 0)),
            pl.BlockSpec((seq, LANE), lambda b, j: (b, cq + cq // 2)),
            pl.BlockSpec((seq, LANE), lambda b, j: (b, cq + cq // 2 + 1)),
            pl.BlockSpec((qb, d // 2), lambda b, j: (b * nb + j, 2)),
            pl.BlockSpec((seq, LANE), lambda b, j: (b, cq + cq // 2 + 2)),
            pl.BlockSpec((qb, LANE), lambda b, j: (b * nb + j, cq + cq // 2 + 2)),
            pl.BlockSpec((3, qb, nh * qb), lambda b, j: (0, 0, 0)),
            pl.BlockSpec((1, dh), lambda b, j: (0, 0)),
            pl.BlockSpec((1, dh), lambda b, j: (0, 0)),
        ],
        out_specs=pl.BlockSpec((qb, d), lambda b, j: (b * nb + j, 0)),
        out_shape=jax.ShapeDtypeStruct((t, d), BF16),
        scratch_shapes=[
            pltpu.VMEM((seq, dh), BF16),
            pltpu.VMEM((nb // 2, dh + ONES_ROWS, 2 * qb), BF16),
            pltpu.VMEM((dh, nh * qb), BF16),
            pltpu.VMEM((di, nih * qb), BF16),
            pltpu.VMEM((LANE, qb), F32),
            pltpu.VMEM((nb // 2, 2 * qb, qb), I32),
            pltpu.VMEM((nb // 2, 2 * qb, qb), I16),
            pltpu.VMEM((nb // 2, 2 * qb, qb), I16),
            pltpu.VMEM((nb // 2, 2 * qb, qb), I16),
            pltpu.VMEM((nb // 2, 2 * qb, qb), F32),
            pltpu.VMEM((nb // 2, 2 * qb, nh * qb), F32),
        ] + [pltpu.VMEM((dh + ONES_ROWS, 2 * qb), F32)
             for _ in range(nh // 2)],
        compiler_params=_cparams(("arbitrary", "arbitrary")),
        name="dsa_attention",
    )(z, z, z, z, z, z, tb, q_norm.reshape(1, dh), k_norm.reshape(1, dh))


def _dsa_weight(c_w_in, d):
    dh = d // ATT_HEADS
    nqi = IDX_HEADS * IDX_DIM
    q, k, v, qi, ki, wi = jnp.split(
        c_w_in, [d, d + dh, d + 2 * dh, d + 2 * dh + nqi, d + 2 * dh + nqi + IDX_DIM], axis=1)
    used = d + nqi + 2 * dh + IDX_DIM + IDX_HEADS
    total = -(-used // 512) * 512
    pad = jnp.zeros((d, total - used), c_w_in.dtype)
    return jnp.concatenate([q, qi, k, v, ki, wi, pad], axis=1).astype(BF16)


def kernel(x, c, rel_bias, w_router, b_router, norm_mix, ada_w_mix, ada_b_mix, norm_ffn, ada_w_ffn,
           ada_b_ffn, ab_w_in, ab_conv_w, ab_conv_b, ab_w_out, c_w_in, c_q_norm, c_k_norm, c_w_out,
           moe_w1, moe_w3, moe_w2):
    bsz, seq, d = x.shape
    depth = norm_mix.shape[0]
    t = bsz * seq
    ch = d // 2
    tm = min(512, seq)
    x2 = x.reshape(t, d)
    mod_mix = ada_mod(c, ada_w_mix, ada_b_mix).reshape(depth, bsz, 3, d)
    mod_ffn = ada_mod(c, ada_w_ffn, ada_b_ffn).reshape(depth, bsz, 3, d)
    stack = lambda w: w.astype(BF16).reshape((depth * N_EXPERTS,) + w.shape[2:])
    w1_bf, w3_bf, w2_bf = stack(moe_w1), stack(moe_w3), stack(moe_w2)
    for layer in range(depth):
        i = layer // 2
        if layer % 2 == 0:
            z = inproj(x2, mod_mix[layer], norm_mix[layer], ab_w_in[i].astype(BF16), seq,
                       min(1024, seq), 1024)
            y_a = conv_mixer(z, ab_conv_w[i], ab_conv_b[i], bsz, seq, tm)
            y_b = retention(z, bsz, seq, ch, 3 * ch)
            x2 = outproj(y_a, y_b, (0, 0), ab_w_out[i].astype(BF16), x2, mod_mix[layer], seq, tm)
        else:
            w_c = _dsa_weight(c_w_in[i], d)
            z = inproj(x2, mod_mix[layer], norm_mix[layer], w_c, seq, tm, w_c.shape[1] // 2)
            o = dsa_attention(z, rel_bias, c_q_norm[i], c_k_norm[i], bsz, seq, d)
            x2 = outproj(o, o, (0, 1), c_w_out[i].astype(BF16), x2, mod_mix[layer], seq, tm)
        x2 = moe(x2, mod_ffn[layer], norm_ffn[layer], w_router, b_router,
                 w1_bf, w3_bf, w2_bf, layer * N_EXPERTS, seq)
    return x2.reshape(bsz, seq, d)
```

```python
import functools
import math

import jax
import jax.numpy as jnp
from jax import lax
from jax.experimental import pallas as pl
from jax.experimental.pallas import tpu as pltpu

F32 = jnp.float32
BF16 = jnp.bfloat16
I32 = jnp.int32
I16 = jnp.int16

RMS_EPS = 1e-6
CONV_WIDTH = 3
RET_HEADS = 4
ATT_HEADS = 16
IDX_HEADS = 16
IDX_DIM = 64
DSA_TOPK_MAX = 256
Q_BLOCK = 128
N_BUCKETS = 32
MAX_DISTANCE = 128
N_EXPERTS = 16
N_GROUPS = 4
EXPERTS_PER_GROUP = N_EXPERTS // N_GROUPS
RET_CHUNK = 256
MOE_ROWS = 256
DMA_UNROLL = 8
ONES_ROWS = 16
LANE = 128
SUBLANE = 8
VMEM_LIMIT = 48 * 1024 * 1024
NEG_BIG = -1e30
INT_MIN = -(2 ** 31)
I16_MIN = -(2 ** 15)
PACK16 = 16


def _cparams(sem, vmem=VMEM_LIMIT):
    return pltpu.CompilerParams(dimension_semantics=sem, vmem_limit_bytes=vmem)


def _dot(a, b):
    return jnp.dot(a, b, preferred_element_type=F32)


def _dot_nt(a, b):
    return lax.dot_general(a, b, (((1,), (1,)), ((), ())), preferred_element_type=F32)


def _dot_tn(a, b):
    return lax.dot_general(a, b, (((0,), (0,)), ((), ())), preferred_element_type=F32)


def _silu(x):
    return x * jax.nn.sigmoid(x)


def _ada_kernel(c_ref, w_ref, b_ref, o_ref):
    cs = _silu(c_ref[...])
    o_ref[0] = _dot(cs.astype(BF16), w_ref[0].astype(BF16)) + b_ref[0]


def ada_mod(c, ada_w, ada_b):
    nl, d, n3 = ada_w.shape
    bsz = c.shape[0]
    tn = min(512, n3)
    return pl.pallas_call(
        _ada_kernel,
        grid=(nl, n3 // tn),
        in_specs=[
            pl.BlockSpec((bsz, d), lambda l, j: (0, 0)),
            pl.BlockSpec((1, d, tn), lambda l, j: (l, 0, j)),
            pl.BlockSpec((1, 1, tn), lambda l, j: (l, 0, j)),
        ],
        out_specs=pl.BlockSpec((1, bsz, tn), lambda l, j: (l, 0, j)),
        out_shape=jax.ShapeDtypeStruct((nl, bsz, n3), F32),
        compiler_params=_cparams(("parallel", "parallel")),
        name="ada_mod",
    )(c, ada_w, ada_b.reshape(nl, 1, n3))


def _modulated_norm(x, mod_ref, nw_ref):
    r = lax.rsqrt(jnp.mean(x * x, axis=-1, keepdims=True) + RMS_EPS)
    return x * r * nw_ref[...] * (1.0 + mod_ref[0, 1:2, :]) + mod_ref[0, 0:1, :]


def _inproj_kernel(x_ref, mod_ref, nw_ref, w_ref, o_ref, h_ref):
    @pl.when(pl.program_id(1) == 0)
    def _():
        h_ref[...] = _modulated_norm(x_ref[...], mod_ref, nw_ref).astype(BF16)

    o_ref[...] = _dot(h_ref[...], w_ref[...]).astype(o_ref.dtype)


def inproj(x2, mod, norm_w, w_bf, seq, tm, tn):
    t, d = x2.shape
    n = w_bf.shape[1]
    per = seq // tm
    return pl.pallas_call(
        _inproj_kernel,
        grid=(t // tm, n // tn),
        in_specs=[
            pl.BlockSpec((tm, d), lambda i, j: (i, 0)),
            pl.BlockSpec((1, 3, d), lambda i, j: (i // per, 0, 0)),
            pl.BlockSpec((1, d), lambda i, j: (0, 0)),
            pl.BlockSpec((d, tn), lambda i, j: (0, j)),
        ],
        out_specs=pl.BlockSpec((tm, tn), lambda i, j: (i, j)),
        out_shape=jax.ShapeDtypeStruct((t, n), BF16),
        scratch_shapes=[pltpu.VMEM((tm, d), BF16)],
        compiler_params=_cparams(("parallel", "arbitrary")),
        name="inproj",
    )(x2, mod, norm_w.reshape(1, d), w_bf)


def _conv_kernel(b_ref, c_ref, v_ref, w_ref, cb_ref, o_ref, u_ref, *, tm):
    s = pl.program_id(1)
    u = c_ref[...].astype(F32) * v_ref[...].astype(F32)

    @pl.when(s == 0)
    def _():
        u_ref[0:SUBLANE, :] = jnp.zeros((SUBLANE, u.shape[1]), F32)

    @pl.when(s > 0)
    def _():
        u_ref[0:SUBLANE, :] = u_ref[tm:tm + SUBLANE, :]

    u_ref[SUBLANE:SUBLANE + tm, :] = u
    conv = (cb_ref[...]
            + u_ref[SUBLANE - 2:SUBLANE - 2 + tm, :] * w_ref[0:1, :]
            + u_ref[SUBLANE - 1:SUBLANE - 1 + tm, :] * w_ref[1:2, :]
            + u * w_ref[2:3, :])
    o_ref[...] = (b_ref[...].astype(F32) * conv).astype(o_ref.dtype)


def conv_mixer(z, conv_w, conv_b, bsz, seq, tm):
    t = z.shape[0]
    ch = conv_w.shape[1]
    per = seq // tm
    row = lambda g: (lambda b, s: (b * per + s, g))
    return pl.pallas_call(
        functools.partial(_conv_kernel, tm=tm),
        grid=(bsz, per),
        in_specs=[
            pl.BlockSpec((tm, ch), row(0)),
            pl.BlockSpec((tm, ch), row(1)),
            pl.BlockSpec((tm, ch), row(2)),
            pl.BlockSpec((CONV_WIDTH, ch), lambda b, s: (0, 0)),
            pl.BlockSpec((1, ch), lambda b, s: (0, 0)),
        ],
        out_specs=pl.BlockSpec((tm, ch), lambda b, s: (b * per + s, 0)),
        out_shape=jax.ShapeDtypeStruct((t, ch), BF16),
        scratch_shapes=[pltpu.VMEM((tm + SUBLANE, ch), F32)],
        compiler_params=_cparams(("parallel", "arbitrary")),
        name="conv_mixer",
    )(z, z, z, conv_w, conv_b.reshape(1, ch))


def _retention_kernel(q_ref, k_ref, v_ref, g_ref, cos_ref, sin_ref, din_ref, dcr_ref, dst_ref,
                      o_ref, *st_refs, nc, ck, dk):
    half = dk // 2
    for st_ref in st_refs:
        st_ref[...] = jnp.zeros(st_ref.shape, F32)

    def rot(x, cos, sin):
        x1, x2 = x[:, :half], x[:, half:]
        return jnp.concatenate([x1 * cos - x2 * sin, x1 * sin + x2 * cos], axis=-1)

    def body(c, carry):
        r0 = pl.multiple_of(c * ck, ck)
        cos = cos_ref[pl.ds(r0, ck), :]
        sin = sin_ref[pl.ds(r0, ck), :]
        for h, st_ref in enumerate(st_refs):
            cols = slice(h * dk, (h + 1) * dk)
            d_cross = dcr_ref[h]
            g_chunk = dcr_ref[h, ck - 1:ck, :]
            q = rot(q_ref[pl.ds(r0, ck), cols].astype(F32), cos, sin)
            k = rot(k_ref[pl.ds(r0, ck), cols].astype(F32), cos, sin) * (dk ** -0.5)
            v = v_ref[pl.ds(r0, ck), cols]
            intra = _dot_nt(q.astype(BF16), k.astype(BF16)) * din_ref[h]
            state = st_ref[...]
            o = _dot(intra.astype(BF16), v) + _dot((q * d_cross).astype(BF16), state.astype(BF16))
            st_ref[...] = state * g_chunk + _dot_tn((k * dst_ref[h]).astype(BF16), v)
            r = lax.rsqrt(jnp.mean(o * o, axis=-1, keepdims=True) + RMS_EPS)
            gate = _silu(g_ref[pl.ds(r0, ck), cols].astype(F32))
            o_ref[pl.ds(r0, ck), cols] = (o * r * gate).astype(o_ref.dtype)
        return carry

    lax.fori_loop(0, nc, body, 0)


def retention(z, bsz, seq, ch, col0):
    t = z.shape[0]
    nh = RET_HEADS
    dk = ch // nh
    ck = min(RET_CHUNK, seq)
    nc = seq // ck
    half = dk // 2
    pos = jnp.arange(seq, dtype=F32)
    inv = 1.0 / (10000.0 ** jnp.linspace(0.0, 1.0, half, dtype=F32))
    ang = pos[:, None] * inv[None, :]
    cos, sin = jnp.cos(ang), jnp.sin(ang)
    log_g = jnp.log(1.0 - 2.0 ** (-5.0 - jnp.arange(nh, dtype=F32)))
    i = jnp.arange(ck, dtype=F32)
    diff = i[:, None] - i[None, :]
    d_intra = jnp.where(diff >= 0, jnp.exp(log_g[:, None, None] * jnp.maximum(diff, 0.0)), 0.0)
    d_cross = jnp.exp(log_g[:, None] * (i[None, :] + 1.0))[..., None]
    d_state = jnp.exp(log_g[:, None] * (ck - 1.0 - i[None, :]))[..., None]
    cb = col0 // ch
    col = lambda g: (lambda b: (b, cb + g))
    whole = lambda b: (0, 0, 0)
    return pl.pallas_call(
        functools.partial(_retention_kernel, nc=nc, ck=ck, dk=dk),
        grid=(bsz,),
        in_specs=[
            pl.BlockSpec((seq, ch), col(0)),
            pl.BlockSpec((seq, ch), col(1)),
            pl.BlockSpec((seq, ch), col(2)),
            pl.BlockSpec((seq, ch), col(3)),
            pl.BlockSpec((seq, half), lambda b: (0, 0)),
            pl.BlockSpec((seq, half), lambda b: (0, 0)),
            pl.BlockSpec((nh, ck, ck), whole),
            pl.BlockSpec((nh, ck, 1), whole),
            pl.BlockSpec((nh, ck, 1), whole),
        ],
        out_specs=pl.BlockSpec((seq, ch), lambda b: (b, 0)),
        out_shape=jax.ShapeDtypeStruct((t, ch), BF16),
        scratch_shapes=[pltpu.VMEM((dk, dk), F32) for _ in range(nh)],
        compiler_params=_cparams(("parallel",)),
        name="retention",
    )(z, z, z, z, cos, sin, d_intra, d_cross, d_state)


def _outproj_kernel(ya_ref, yb_ref, wa_ref, wb_ref, x_ref, mod_ref, o_ref):
    y = _dot(ya_ref[...], wa_ref[...]) + _dot(yb_ref[...], wb_ref[...])
    o_ref[...] = x_ref[...] + mod_ref[0, 2:3, :] * y


def outproj(ya, yb, cols, w_bf, x2, mod, seq, tm):
    t, d = x2.shape
    kh = w_bf.shape[0] // 2
    per = seq // tm
    return pl.pallas_call(
        _outproj_kernel,
        grid=(t // tm,),
        in_specs=[
            pl.BlockSpec((tm, kh), lambda i: (i, cols[0])),
            pl.BlockSpec((tm, kh), lambda i: (i, cols[1])),
            pl.BlockSpec((kh, d), lambda i: (0, 0)),
            pl.BlockSpec((kh, d), lambda i: (1, 0)),
            pl.BlockSpec((tm, d), lambda i: (i, 0)),
            pl.BlockSpec((1, 3, d), lambda i: (i // per, 0, 0)),
        ],
        out_specs=pl.BlockSpec((tm, d), lambda i: (i, 0)),
        out_shape=jax.ShapeDtypeStruct((t, d), F32),
        compiler_params=_cparams(("parallel",)),
        name="outproj",
    )(ya, yb, w_bf, w_bf, x2, mod)


def _router_kernel(x_ref, mod_ref, nw_ref, wr_ref, br_ref, h_ref, ri_ref, rf_ref, cnt_ref, carry_ref,
                   *, tm):
    @pl.when(pl.program_id(0) == 0)
    def _():
        carry_ref[...] = jnp.zeros(carry_ref.shape, F32)

    h = _modulated_norm(x_ref[...], mod_ref, nw_ref)
    h_ref[...] = h
    logits = _dot_nt(wr_ref[...], h.astype(BF16))
    mx = jnp.max(logits, axis=0, keepdims=True)
    ex = jnp.exp(logits - mx)
    probs = ex / jnp.sum(ex, axis=0, keepdims=True)
    sel = probs + br_ref[...]
    s = [sel[e:e + 1, :] for e in range(N_EXPERTS)]
    p = [probs[e:e + 1, :] for e in range(N_EXPERTS)]
    epg = EXPERTS_PER_GROUP

    def first_argmax(vals, exclude=None):
        best = jnp.full_like(vals[0], -jnp.inf)
        idx = jnp.zeros(vals[0].shape, I32)
        for j, vj in enumerate(vals):
            better = vj > best
            if exclude is not None:
                better = better & (exclude != j)
            idx = jnp.where(better, j, idx)
            best = jnp.where(better, vj, best)
        return idx

    gscore = []
    for g in range(N_GROUPS):
        gs = s[g * epg:(g + 1) * epg]
        best = None
        for a in range(epg):
            for b in range(a + 1, epg):
                pair = gs[a] + gs[b]
                best = pair if best is None else jnp.maximum(best, pair)
        gscore.append(best)
    grp = first_argmax(gscore)

    def pick(rows, index, n):
        out = rows[n - 1]
        for j in range(n - 2, -1, -1):
            out = jnp.where(index == j, rows[j], out)
        return out

    in_s = [pick([s[g * epg + j] for g in range(N_GROUPS)], grp, N_GROUPS) for j in range(epg)]
    in_p = [pick([p[g * epg + j] for g in range(N_GROUPS)], grp, N_GROUPS) for j in range(epg)]
    i1 = first_argmax(in_s)
    i2 = first_argmax(in_s, exclude=i1)
    p1 = pick(in_p, i1, epg)
    p2 = pick(in_p, i2, epg)
    e1 = grp * epg + i1
    e2 = grp * epg + i2
    den = p1 + p2
    g1 = p1 / den
    g2 = p2 / den

    eidx = lax.broadcasted_iota(I32, (N_EXPERTS, tm), 0)
    member = (eidx == e1) | (eidx == e2)
    member_f = jnp.where(member, 1.0, 0.0)
    before = lax.broadcasted_iota(I32, (tm, tm), 0) < lax.broadcasted_iota(I32, (tm, tm), 1)
    prefix = _dot(member_f.astype(BF16), jnp.where(before, 1.0, 0.0).astype(BF16))
    base = prefix + carry_ref[:, 0:1]
    rank1 = jnp.sum(jnp.where(eidx == e1, base, 0.0), axis=0, keepdims=True).astype(I32)
    rank2 = jnp.sum(jnp.where(eidx == e2, base, 0.0), axis=0, keepdims=True).astype(I32)
    carry_ref[...] = carry_ref[...] + jnp.sum(member_f, axis=1, keepdims=True)
    cnt_ref[...] = carry_ref[...].astype(I32)

    zi = jnp.zeros((SUBLANE - 4, tm), I32)
    ri_ref[...] = jnp.concatenate([e1, e2, rank1, rank2, zi], axis=0)
    zf = jnp.zeros((SUBLANE - 2, tm), F32)
    rf_ref[...] = jnp.concatenate([g1, g2, zf], axis=0)


def router(x2, mod, norm_w, w_router, b_router, seq, tm):
    t, d = x2.shape
    per = seq // tm
    ne = N_EXPERTS
    return pl.pallas_call(
        functools.partial(_router_kernel, tm=tm),
        grid=(t // tm,),
        in_specs=[
            pl.BlockSpec((tm, d), lambda i: (i, 0)),
            pl.BlockSpec((1, 3, d), lambda i: (i // per, 0, 0)),
            pl.BlockSpec((1, d), lambda i: (0, 0)),
            pl.BlockSpec((ne, d), lambda i: (0, 0)),
            pl.BlockSpec((ne, 1), lambda i: (0, 0)),
        ],
        out_specs=[
            pl.BlockSpec((tm, d), lambda i: (i, 0)),
            pl.BlockSpec((SUBLANE, tm), lambda i: (0, i)),
            pl.BlockSpec((SUBLANE, tm), lambda i: (0, i)),
            pl.BlockSpec((ne, LANE), lambda i: (0, 0)),
        ],
        out_shape=[
            jax.ShapeDtypeStruct((t, d), F32),
            jax.ShapeDtypeStruct((SUBLANE, t), I32),
            jax.ShapeDtypeStruct((SUBLANE, t), F32),
            jax.ShapeDtypeStruct((ne, LANE), I32),
        ],
        scratch_shapes=[pltpu.VMEM((ne, LANE), F32)],
        compiler_params=_cparams(("arbitrary",)),
        name="router",
    )(x2, mod, norm_w.reshape(1, d), w_router.T.astype(BF16), b_router.reshape(ne, 1))


def _dispatch_kernel(dest_ref, pad0_ref, padn_ref, h_ref, xs_ref, hbuf_ref, zero_ref, sem, zsem,
                     *, tm, nt):
    i = pl.program_id(0)
    cur = i % 2
    base = i * tm

    def zero_row(e, r):
        return pltpu.make_async_copy(zero_ref.at[pl.ds(0, 1), :],
                                     xs_ref.at[pl.ds(pad0_ref[e] + r, 1), :], zsem)

    def tile_done(buf):
        for _ in range(2):
            pltpu.make_async_copy(hbuf_ref.at[buf], xs_ref.at[pl.ds(0, tm), :], sem.at[buf]).wait()

    @pl.when(i == 0)
    def _():
        zero_ref[...] = jnp.zeros(zero_ref.shape, F32)
        for e in range(N_EXPERTS):
            lax.fori_loop(0, padn_ref[e], lambda r, c, e=e: (zero_row(e, r).start(), c)[1], 0)

    hbuf_ref[cur] = h_ref[...]

    def start(r, carry):
        for slot in range(2):
            d = dest_ref[2 * (base + r) + slot]
            pltpu.make_async_copy(hbuf_ref.at[cur, pl.ds(r, 1), :], xs_ref.at[pl.ds(d, 1), :],
                                  sem.at[cur]).start()
        return carry

    lax.fori_loop(0, tm, start, 0, unroll=DMA_UNROLL)

    @pl.when(i > 0)
    def _():
        tile_done(1 - cur)

    @pl.when(i == nt - 1)
    def _():
        tile_done(cur)
        for e in range(N_EXPERTS):
            lax.fori_loop(0, padn_ref[e], lambda r, c, e=e: (zero_row(e, r).wait(), c)[1], 0)


def dispatch(h, dest, pad_start, pad_count, n_rows, tm):
    t, d = h.shape
    return pl.pallas_call(
        functools.partial(_dispatch_kernel, tm=tm, nt=t // tm),
        grid_spec=pltpu.PrefetchScalarGridSpec(
            num_scalar_prefetch=3,
            grid=(t // tm,),
            in_specs=[pl.BlockSpec((tm, d), lambda i, *_: (i, 0))],
            out_specs=pl.BlockSpec(memory_space=pl.ANY),
            scratch_shapes=[pltpu.VMEM((2, tm, d), F32), pltpu.VMEM((SUBLANE, d), F32),
                            pltpu.SemaphoreType.DMA((2,)), pltpu.SemaphoreType.DMA(())],
        ),
        out_shape=jax.ShapeDtypeStruct((n_rows, d), F32),
        compiler_params=_cparams(("arbitrary",)),
        name="moe_dispatch",
    )(dest, pad_start, pad_count, h)


def _ffn_kernel(be_ref, bv_ref, xs_ref, w1_ref, w3_ref, w2_ref, ys_ref):
    valid = bv_ref[pl.program_id(0)]

    @pl.when(valid > 0)
    def _():
        rows = lax.broadcasted_iota(I32, xs_ref.shape, 0)
        x = jnp.where(rows < valid, xs_ref[...], 0.0).astype(BF16)
        h1 = _dot(x, w1_ref[0])
        h3 = _dot(x, w3_ref[0])
        a = (_silu(h1) * h3).astype(BF16)
        ys_ref[...] = _dot(a, w2_ref[0])

    @pl.when(valid <= 0)
    def _():
        ys_ref[...] = jnp.zeros(ys_ref.shape, F32)


def grouped_ffn(xs, block_e, block_valid, w1, w3, w2, bm):
    n_rows, d = xs.shape
    f = w1.shape[2]
    return pl.pallas_call(
        _ffn_kernel,
        grid_spec=pltpu.PrefetchScalarGridSpec(
            num_scalar_prefetch=2,
            grid=(n_rows // bm,),
            in_specs=[
                pl.BlockSpec((bm, d), lambda i, be, bv: (i, 0)),
                pl.BlockSpec((1, d, f), lambda i, be, bv: (be[i], 0, 0)),
                pl.BlockSpec((1, d, f), lambda i, be, bv: (be[i], 0, 0)),
                pl.BlockSpec((1, f, d), lambda i, be, bv: (be[i], 0, 0)),
            ],
            out_specs=pl.BlockSpec((bm, d), lambda i, be, bv: (i, 0)),
        ),
        out_shape=jax.ShapeDtypeStruct((n_rows, d), F32),
        compiler_params=_cparams(("arbitrary",)),
        name="moe_ffn",
    )(block_e, block_valid, xs, w1, w3, w2)


def _combine_kernel(dest_ref, ys_ref, x_ref, mod_ref, gf_ref, o_ref, a_ref, b_ref, sem, *, tm, nt):
    i = pl.program_id(0)

    def fetch(tile, buf):
        base = tile * tm

        def start(r, carry):
            for slot, ref in ((0, a_ref), (1, b_ref)):
                d = dest_ref[2 * (base + r) + slot]
                pltpu.make_async_copy(ys_ref.at[pl.ds(d, 1), :], ref.at[buf, pl.ds(r, 1), :],
                                      sem.at[buf]).start()
            return carry

        lax.fori_loop(0, tm, start, 0, unroll=DMA_UNROLL)

    @pl.when(i == 0)
    def _():
        fetch(0, 0)

    @pl.when(i + 1 < nt)
    def _():
        fetch(i + 1, (i + 1) % 2)

    cur = i % 2
    pltpu.make_async_copy(ys_ref.at[pl.ds(0, tm), :], a_ref.at[cur], sem.at[cur]).wait()
    pltpu.make_async_copy(ys_ref.at[pl.ds(0, tm), :], b_ref.at[cur], sem.at[cur]).wait()
    y = gf_ref[:, 0:1] * a_ref[cur] + gf_ref[:, 1:2] * b_ref[cur]
    o_ref[...] = x_ref[...] + mod_ref[0, 2:3, :] * y


def combine(ys, dest, x2, mod, gates_t, seq, tm):
    t, d = x2.shape
    per = seq // tm
    return pl.pallas_call(
        functools.partial(_combine_kernel, tm=tm, nt=t // tm),
        grid_spec=pltpu.PrefetchScalarGridSpec(
            num_scalar_prefetch=1,
            grid=(t // tm,),
            in_specs=[
                pl.BlockSpec(memory_space=pl.ANY),
                pl.BlockSpec((tm, d), lambda i, dest: (i, 0)),
                pl.BlockSpec((1, 3, d), lambda i, dest: (i // per, 0, 0)),
                pl.BlockSpec((tm, SUBLANE), lambda i, dest: (i, 0)),
            ],
            out_specs=pl.BlockSpec((tm, d), lambda i, dest: (i, 0)),
            scratch_shapes=[pltpu.VMEM((2, tm, d), F32), pltpu.VMEM((2, tm, d), F32),
                            pltpu.SemaphoreType.DMA((2,))],
        ),
        out_shape=jax.ShapeDtypeStruct((t, d), F32),
        compiler_params=_cparams(("arbitrary",)),
        name="moe_combine",
    )(dest, ys, x2, mod, gates_t)


def moe(x2, mod, norm_w, w_router, b_router, w1, w3, w2, expert0, seq):
    t, d = x2.shape
    bm = MOE_ROWS
    tm = min(256, seq)
    h, ri, rf, cnt = router(x2, mod, norm_w, w_router, b_router, seq, tm)
    ne = N_EXPERTS
    counts = cnt[:, 0]
    nblk = (counts + bm - 1) // bm
    blk_end = jnp.cumsum(nblk)
    blk_start = blk_end - nblk
    eids = jnp.arange(ne, dtype=I32)
    e12 = ri[0:2].T
    row0 = jnp.sum(jnp.where(e12[..., None] == eids, blk_start * bm, 0), axis=-1)
    dest = (row0 + ri[2:4].T).reshape(2 * t).astype(I32)
    n_blocks = (2 * t) // bm + ne
    bidx = jnp.arange(n_blocks, dtype=I32)
    block_e = jnp.minimum(jnp.sum(bidx[:, None] >= blk_end[None, :], axis=1), ne - 1).astype(I32)
    onehot = block_e[:, None] == eids
    cnt_b = jnp.sum(jnp.where(onehot, counts, 0), axis=1)
    start_b = jnp.sum(jnp.where(onehot, blk_start, 0), axis=1)
    block_valid = jnp.clip(cnt_b - (bidx - start_b) * bm, 0, bm).astype(I32)
    pad_start = (blk_start * bm + counts).astype(I32)
    pad_end = jnp.where(eids == ne - 1, n_blocks * bm, blk_end * bm)
    xs = dispatch(h, dest, pad_start, (pad_end - pad_start).astype(I32), n_blocks * bm, tm)
    ys = grouped_ffn(xs, block_e + expert0, block_valid, w1, w3, w2, bm)
    return combine(ys, dest, x2, mod, rf.T, seq, tm)


def _t5_bucket(n):
    max_exact = N_BUCKETS // 2
    nf = jnp.maximum(n, 1).astype(F32)
    large = max_exact + (jnp.log(nf / max_exact) / math.log(MAX_DISTANCE / max_exact)
                         * (N_BUCKETS - max_exact)).astype(I32)
    large = jnp.minimum(large, N_BUCKETS - 1)
    return jnp.where(n < max_exact, n, large)


def _sortable(score):
    bits = pltpu.bitcast(score, I32)
    return bits ^ ((bits >> 31) & 0x7FFFFFFF)


def _dsa_kernel(q_ref, k_ref, v_ref, qi_ref, kw_ref, wq_ref, tb_ref, qn_ref, kn_ref, o_ref,
                khat_ref, vt_ref, qrt_ref, qirt_ref, wt_ref, mt_ref, hi_ref, lo_ref, lo2_ref, nm_ref, lg_ref,
                *acc_refs,
                nh, dh, nih, di, topk, nb):
    j = pl.program_id(1)
    qb = Q_BLOCK
    kc = 2 * qb
    nch = (j + 2) // 2
    pair = 2 * qb

    def transpose(x):
        return x.astype(F32).T

    @pl.when(j == 0)
    def _():
        k = k_ref[...].astype(F32)
        r = lax.rsqrt(jnp.mean(k * k, axis=-1, keepdims=True) + RMS_EPS)
        khat_ref[...] = (k * r * kn_ref[...]).astype(BF16)
        for c in range(nb // 2):
            for s in range(2):
                blk = v_ref[(2 * c + s) * qb:(2 * c + s + 1) * qb, :]
                vt_ref[c, 0:dh, s * qb:(s + 1) * qb] = transpose(blk).astype(BF16)
            vt_ref[c, dh:dh + ONES_ROWS, :] = jnp.ones((ONES_ROWS, kc), BF16)
        for ref in (hi_ref, lo_ref, lo2_ref):
            ref[...] = jnp.full(ref.shape, I16_MIN, I16)

    qscale = (dh ** -0.5) * math.log2(math.e)
    for h in range(nh):
        qh = q_ref[:, h * dh:(h + 1) * dh].astype(F32)
        r = lax.rsqrt(jnp.mean(qh * qh, axis=-1, keepdims=True) + RMS_EPS)
        qhat = (qh * r * qn_ref[...] * qscale).astype(BF16)
        qrt_ref[:, h * qb:(h + 1) * qb] = transpose(qhat).astype(BF16)
    for g in range(nih * di // LANE):
        two = transpose(qi_ref[:, g * LANE:(g + 1) * LANE])
        for s in range(LANE // di):
            h = g * (LANE // di) + s
            qirt_ref[:, h * qb:(h + 1) * qb] = two[s * di:(s + 1) * di, :].astype(BF16)
    wt_ref[...] = transpose(wq_ref[...])

    key_l = lax.broadcasted_iota(I32, (kc, qb), 0)
    q_pos = j * qb + lax.broadcasted_iota(I32, (kc, qb), 1)

    def score_body(c, carry):
        r0 = pl.multiple_of(c * kc, kc)
        ki = kw_ref[pl.ds(r0, kc), 0:di]
        acc = jnp.zeros((kc, qb), F32)
        for g in range(nih // 2):
            rel = _dot(ki, qirt_ref[:, g * pair:(g + 1) * pair])
            for s in range(2):
                h = 2 * g + s
                acc = acc + wt_ref[di + h:di + h + 1, :] * jnp.maximum(rel[:, s * qb:(s + 1) * qb], 0.0)
        m = jnp.where(r0 + key_l <= q_pos, _sortable(acc), INT_MIN)
        mt_ref[c] = m
        hi_ref[c] = (m >> 16).astype(I16)
        lo_ref[c] = ((m & 0xFFFF) + I16_MIN).astype(I16)
        return carry

    lax.fori_loop(0, nch, score_body, 0)

    def count_ge(cand):
        def cbody(c, acc):
            hit = jnp.where(mt_ref[c] >= cand, 1, 0)
            return acc + jnp.sum(hit.reshape(kc // SUBLANE, SUBLANE, qb), axis=0)
        acc = lax.fori_loop(0, nch, cbody, jnp.zeros((SUBLANE, qb), I32))
        return jnp.sum(acc, axis=0, keepdims=True)

    def fold16(hit):
        parts = [hit[PACK16 * i:PACK16 * (i + 1), :] for i in range(kc // PACK16)]
        while len(parts) > 1:
            parts = [parts[i] + parts[i + 1] for i in range(0, len(parts), 2)]
        return parts[0]

    def total16(acc):
        return jnp.sum(acc.astype(F32), axis=0, keepdims=True).astype(I32)

    one, nil = jnp.ones((), BF16), jnp.zeros((), BF16)

    def count16(src_ref, cand, n):
        c16 = cand.astype(I16)
        acc = fold16(jnp.where(src_ref[0] >= c16, one, nil))
        for c in range(1, n):
            acc = acc + fold16(jnp.where(src_ref[c] >= c16, one, nil))
        return total16(acc)

    def search16(src_ref, kth, n):
        zero = jnp.zeros((1, qb), I32)
        ans0 = jnp.where(count16(src_ref, zero, n) >= kth, zero, I16_MIN)

        def bit_body(bi, ans):
            cand = ans | (1 << (14 - bi))
            return jnp.where(count16(src_ref, cand, n) >= kth, cand, ans)

        return lax.fori_loop(0, 15, bit_body, ans0)

    def search(n):
        top = search16(hi_ref, jnp.full((1, qb), topk, I32), n)
        top16 = top.astype(I16)
        acc = jnp.zeros((PACK16, qb), BF16)
        for c in range(n):
            hi = hi_ref[c]
            lo2_ref[c] = jnp.where(hi == top16, lo_ref[c], jnp.full((), I16_MIN, I16))
            acc = acc + fold16(jnp.where(hi > top16, one, nil))
        above = total16(acc)
        low = search16(lo2_ref, topk - above, n)
        return top * 65536 + (low - I16_MIN), above + count16(lo2_ref, low, n)

    def search_any():
        return lax.cond(nch <= nb // 4, functools.partial(search, nb // 4),
                        functools.partial(search, nb // 2))

    selecting = (j + 1) * qb > topk
    thr, n_ge = lax.cond(selecting, search_any,
                         lambda: (jnp.full((1, qb), INT_MIN + 1, I32), jnp.zeros((1, qb), I32)))

    def plain_mask():
        def body(c, carry):
            nm_ref[c] = jnp.where(mt_ref[c] >= thr, 0.0, NEG_BIG)
            return carry
        lax.fori_loop(0, nch, body, 0)

    def tie_mask():
        need = (topk - count_ge(thr + 1)).astype(F32)
        tri = jnp.where(lax.broadcasted_iota(I32, (kc, kc), 1) <= lax.broadcasted_iota(I32, (kc, kc), 0),
                        1.0, 0.0).astype(BF16)

        def body(c, seen):
            m = mt_ref[c]
            tie = m == thr
            upto = _dot(tri, jnp.where(tie, 1.0, 0.0).astype(BF16)) + seen
            nm_ref[c] = jnp.where((m > thr) | (tie & (upto <= need)), 0.0, NEG_BIG)
            return upto[kc - 1:kc, :]
        lax.fori_loop(0, nch, body, jnp.zeros((1, qb), F32))

    crowded = jnp.max(n_ge) > topk
    lax.cond(crowded, tie_mask, plain_mask)

    npair = nh // 2
    pcols = [slice(pr * pair, (pr + 1) * pair) for pr in range(npair)]
    nfar = jnp.maximum((j - 1) // 2, 0)

    def sweep1(c, mruns, near):
        r0 = pl.multiple_of(c * kc, kc)
        kh = khat_ref[pl.ds(r0, kc), :]
        nm = nm_ref[c]
        nm2 = jnp.concatenate([nm, nm], axis=1)
        if near:
            k0 = jnp.clip(j - 2 * c, 0, 2)
            k1 = jnp.clip(j - 2 * c - 1, 0, 2)
        out = []
        for pr in range(npair):
            lg = _dot(kh, qrt_ref[:, pcols[pr]]) + nm2
            if near:
                lg = lg + jnp.concatenate([tb_ref[k0, :, pcols[pr]], tb_ref[k1, :, pcols[pr]]], axis=0)
            lg_ref[c, :, pcols[pr]] = lg
            out.append(jnp.maximum(mruns[pr], jnp.max(lg.reshape(kc // SUBLANE, SUBLANE, pair), axis=0)))
        return tuple(out)

    mruns = tuple(jnp.full((SUBLANE, pair), NEG_BIG, F32) for _ in range(npair))
    mruns = lax.fori_loop(0, nfar, functools.partial(sweep1, near=False), mruns)
    mruns = lax.fori_loop(nfar, nch, functools.partial(sweep1, near=True), mruns)
    mrows = [jnp.max(m, axis=0, keepdims=True) for m in mruns]
    for acc_ref in acc_refs:
        acc_ref[...] = jnp.zeros(acc_ref.shape, F32)

    def sweep2(c, carry):
        vt = vt_ref[c]
        for pr in range(npair):
            p = jnp.exp2((lg_ref[c, :, pcols[pr]] - mrows[pr]).astype(BF16))
            acc_refs[pr][...] += _dot(vt, p)
        return carry

    lax.fori_loop(0, nch, sweep2, 0)
    for pr in range(npair):
        out = acc_refs[pr][0:dh, :] / acc_refs[pr][dh:dh + 1, :]
        for s in range(2):
            h = 2 * pr + s
            o_ref[:, h * dh:(h + 1) * dh] = transpose(
                out[:, s * qb:(s + 1) * qb].astype(BF16)).astype(o_ref.dtype)


def dsa_attention(z, rel_bias, q_norm, k_norm, bsz, seq, d):
    t = z.shape[0]
    nh, dh, nih, di = ATT_HEADS, d // ATT_HEADS, IDX_HEADS, IDX_DIM
    qb = Q_BLOCK
    nb = seq // qb
    topk = min(DSA_TOPK_MAX, seq // 4)
    assert topk % qb == 0 and dh == LANE and nih * di == d // 2 and nb % 2 == 0 and LANE % di == 0
    assert (nb // 2) * (2 * qb // PACK16) <= 256
    i = jnp.arange(qb)
    dist = i[None, :] - i[:, None]
    tabs = [rel_bias[_t5_bucket(jnp.maximum(dist + off * qb, 0))] for off in range(2)]
    far = jnp.broadcast_to(rel_bias[_t5_bucket(jnp.full((), 2 * qb, I32))], (qb, qb, nh))
    tb = jnp.stack(tabs + [far]).transpose(0, 1, 3, 2).reshape(3, qb, nh * qb).astype(F32)
    tb = (tb - tb[2:3]) * math.log2(math.e)
    cq = d // LANE
    return pl.pallas_call(
        functools.partial(_dsa_kernel, nh=nh, dh=dh, nih=nih, di=di, topk=topk, nb=nb),
        grid=(bsz, nb),
        in_specs=[
            pl.BlockSpec((qb, d), lambda b, j: (b * nb + j, 0)),
            pl.BlockSpec((seq, LANE), lambda b, j: (b, cq + cq // 2)),
            pl.BlockSpec((seq, LANE), lambda b, j: (b, cq + cq // 2 + 1)),
            pl.BlockSpec((qb, d // 2), lambda b, j: (b * nb + j, 2)),
            pl.BlockSpec((seq, LANE), lambda b, j: (b, cq + cq // 2 + 2)),
            pl.BlockSpec((qb, LANE), lambda b, j: (b * nb + j, cq + cq // 2 + 2)),
            pl.BlockSpec((3, qb, nh * qb), lambda b, j: (0, 0, 0)),
            pl.BlockSpec((1, dh), lambda b, j: (0, 0)),
            pl.BlockSpec((1, dh), lambda b, j: (0, 0)),
        ],
        out_specs=pl.BlockSpec((qb, d), lambda b, j: (b * nb + j, 0)),
        out_shape=jax.ShapeDtypeStruct((t, d), BF16),
        scratch_shapes=[
            pltpu.VMEM((seq, dh), BF16),
            pltpu.VMEM((nb // 2, dh + ONES_ROWS, 2 * qb), BF16),
            pltpu.VMEM((dh, nh * qb), BF16),
            pltpu.VMEM((di, nih * qb), BF16),
            pltpu.VMEM((LANE, qb), F32),
            pltpu.VMEM((nb // 2, 2 * qb, qb), I32),
            pltpu.VMEM((nb // 2, 2 * qb, qb), I16),
            pltpu.VMEM((nb // 2, 2 * qb, qb), I16),
            pltpu.VMEM((nb // 2, 2 * qb, qb), I16),
            pltpu.VMEM((nb // 2, 2 * qb, qb), F32),
            pltpu.VMEM((nb // 2, 2 * qb, nh * qb), F32),
        ] + [pltpu.VMEM((dh + ONES_ROWS, 2 * qb), F32)
             for _ in range(nh // 2)],
        compiler_params=_cparams(("arbitrary", "arbitrary")),
        name="dsa_attention",
    )(z, z, z, z, z, z, tb, q_norm.reshape(1, dh), k_norm.reshape(1, dh))


def _dsa_weight(c_w_in, d):
    dh = d // ATT_HEADS
    nqi = IDX_HEADS * IDX_DIM
    q, k, v, qi, ki, wi = jnp.split(
        c_w_in, [d, d + dh, d + 2 * dh, d + 2 * dh + nqi, d + 2 * dh + nqi + IDX_DIM], axis=1)
    used = d + nqi + 2 * dh + IDX_DIM + IDX_HEADS
    total = -(-used // 512) * 512
    pad = jnp.zeros((d, total - used), c_w_in.dtype)
    return jnp.concatenate([q, qi, k, v, ki, wi, pad], axis=1).astype(BF16)


def kernel(x, c, rel_bias, w_router, b_router, norm_mix, ada_w_mix, ada_b_mix, norm_ffn, ada_w_ffn,
           ada_b_ffn, ab_w_in, ab_conv_w, ab_conv_b, ab_w_out, c_w_in, c_q_norm, c_k_norm, c_w_out,
           moe_w1, moe_w3, moe_w2):
    bsz, seq, d = x.shape
    depth = norm_mix.shape[0]
    t = bsz * seq
    ch = d // 2
    tm = min(512, seq)
    x2 = x.reshape(t, d)
    mod_mix = ada_mod(c, ada_w_mix, ada_b_mix).reshape(depth, bsz, 3, d)
    mod_ffn = ada_mod(c, ada_w_ffn, ada_b_ffn).reshape(depth, bsz, 3, d)
    stack = lambda w: w.astype(BF16).reshape((depth * N_EXPERTS,) + w.shape[2:])
    w1_bf, w3_bf, w2_bf = stack(moe_w1), stack(moe_w3), stack(moe_w2)
    for layer in range(depth):
        i = layer // 2
        if layer % 2 == 0:
            z = inproj(x2, mod_mix[layer], norm_mix[layer], ab_w_in[i].astype(BF16), seq,
                       min(1024, seq), 1024)
            y_a = conv_mixer(z, ab_conv_w[i], ab_conv_b[i], bsz, seq, tm)
            y_b = retention(z, bsz, seq, ch, 3 * ch)
            x2 = outproj(y_a, y_b, (0, 0), ab_w_out[i].astype(BF16), x2, mod_mix[layer], seq, tm)
        else:
            w_c = _dsa_weight(c_w_in[i], d)
            z = inproj(x2, mod_mix[layer], norm_mix[layer], w_c, seq, tm, w_c.shape[1] // 2)
            o = dsa_attention(z, rel_bias, c_q_norm[i], c_k_norm[i], bsz, seq, d)
            x2 = outproj(o, o, (0, 1), c_w_out[i].astype(BF16), x2, mod_mix[layer], seq, tm)
        x2 = moe(x2, mod_ffn[layer], norm_ffn[layer], w_router, b_router,
                 w1_bf, w3_bf, w2_bf, layer * N_EXPERTS, seq)
    return x2.reshape(bsz, seq, d)
```

```python
import functools
import math

import jax
import jax.numpy as jnp
from jax import lax
from jax.experimental import pallas as pl
from jax.experimental.pallas import tpu as pltpu

F32 = jnp.float32
BF16 = jnp.bfloat16
I32 = jnp.int32
I16 = jnp.int16

RMS_EPS = 1e-6
CONV_WIDTH = 3
RET_HEADS = 4
ATT_HEADS = 16
IDX_HEADS = 16
IDX_DIM = 64
DSA_TOPK_MAX = 256
Q_BLOCK = 128
N_BUCKETS = 32
MAX_DISTANCE = 128
N_EXPERTS = 16
N_GROUPS = 4
EXPERTS_PER_GROUP = N_EXPERTS // N_GROUPS
RET_CHUNK = 256
MOE_ROWS = 256
DMA_UNROLL = 8
ONES_ROWS = 16
LANE = 128
SUBLANE = 8
VMEM_LIMIT = 48 * 1024 * 1024
NEG_BIG = -1e30
INT_MIN = -(2 ** 31)
I16_MIN = -(2 ** 15)
PACK16 = 16


def _cparams(sem, vmem=VMEM_LIMIT):
    return pltpu.CompilerParams(dimension_semantics=sem, vmem_limit_bytes=vmem)


def _dot(a, b):
    return jnp.dot(a, b, preferred_element_type=F32)


def _dot_nt(a, b):
    return lax.dot_general(a, b, (((1,), (1,)), ((), ())), preferred_element_type=F32)


def _dot_tn(a, b):
    return lax.dot_general(a, b, (((0,), (0,)), ((), ())), preferred_element_type=F32)


def _silu(x):
    return x * jax.nn.sigmoid(x)


def _ada_kernel(c_ref, w_ref, b_ref, o_ref):
    cs = _silu(c_ref[...])
    o_ref[0] = _dot(cs.astype(BF16), w_ref[0].astype(BF16)) + b_ref[0]


def ada_mod(c, ada_w, ada_b):
    nl, d, n3 = ada_w.shape
    bsz = c.shape[0]
    tn = min(512, n3)
    return pl.pallas_call(
        _ada_kernel,
        grid=(nl, n3 // tn),
        in_specs=[
            pl.BlockSpec((bsz, d), lambda l, j: (0, 0)),
            pl.BlockSpec((1, d, tn), lambda l, j: (l, 0, j)),
            pl.BlockSpec((1, 1, tn), lambda l, j: (l, 0, j)),
        ],
        out_specs=pl.BlockSpec((1, bsz, tn), lambda l, j: (l, 0, j)),
        out_shape=jax.ShapeDtypeStruct((nl, bsz, n3), F32),
        compiler_params=_cparams(("parallel", "parallel")),
        name="ada_mod",
    )(c, ada_w, ada_b.reshape(nl, 1, n3))


def _modulated_norm(x, mod_ref, nw_ref):
    r = lax.rsqrt(jnp.mean(x * x, axis=-1, keepdims=True) + RMS_EPS)
    return x * r * nw_ref[...] * (1.0 + mod_ref[0, 1:2, :]) + mod_ref[0, 0:1, :]


def _inproj_kernel(x_ref, mod_ref, nw_ref, w_ref, o_ref, h_ref):
    @pl.when(pl.program_id(1) == 0)
    def _():
        h_ref[...] = _modulated_norm(x_ref[...], mod_ref, nw_ref).astype(BF16)

    o_ref[...] = _dot(h_ref[...], w_ref[...]).astype(o_ref.dtype)


def _proj_kernel(h_ref, w_ref, o_ref):
    o_ref[...] = _dot(h_ref[...], w_ref[...]).astype(o_ref.dtype)


def proj(h_bf, w_bf, tm, tn):
    t, d = h_bf.shape
    n = w_bf.shape[1]
    return pl.pallas_call(
        _proj_kernel,
        grid=(t // tm, n // tn),
        in_specs=[pl.BlockSpec((tm, d), lambda i, j: (i, 0)), pl.BlockSpec((d, tn), lambda i, j: (0, j))],
        out_specs=pl.BlockSpec((tm, tn), lambda i, j: (i, j)),
        out_shape=jax.ShapeDtypeStruct((t, n), BF16),
        compiler_params=_cparams(("parallel", "arbitrary")),
        name="proj",
    )(h_bf, w_bf)


def inproj(x2, mod, norm_w, w_bf, seq, tm, tn):
    t, d = x2.shape
    n = w_bf.shape[1]
    per = seq // tm
    return pl.pallas_call(
        _inproj_kernel,
        grid=(t // tm, n // tn),
        in_specs=[
            pl.BlockSpec((tm, d), lambda i, j: (i, 0)),
            pl.BlockSpec((1, 3, d), lambda i, j: (i // per, 0, 0)),
            pl.BlockSpec((1, d), lambda i, j: (0, 0)),
            pl.BlockSpec((d, tn), lambda i, j: (0, j)),
        ],
        out_specs=pl.BlockSpec((tm, tn), lambda i, j: (i, j)),
        out_shape=jax.ShapeDtypeStruct((t, n), BF16),
        scratch_shapes=[pltpu.VMEM((tm, d), BF16)],
        compiler_params=_cparams(("parallel", "arbitrary")),
        name="inproj",
    )(x2, mod, norm_w.reshape(1, d), w_bf)


def _gated_conv(b_ref, c_ref, v_ref, w_ref, cb_ref, u_ref, s, tm):
    u = c_ref[...].astype(F32) * v_ref[...].astype(F32)

    @pl.when(s == 0)
    def _():
        u_ref[0:SUBLANE, :] = jnp.zeros((SUBLANE, u.shape[1]), F32)

    @pl.when(s > 0)
    def _():
        u_ref[0:SUBLANE, :] = u_ref[tm:tm + SUBLANE, :]

    u_ref[SUBLANE:SUBLANE + tm, :] = u
    conv = (cb_ref[...]
            + u_ref[SUBLANE - 2:SUBLANE - 2 + tm, :] * w_ref[0:1, :]
            + u_ref[SUBLANE - 1:SUBLANE - 1 + tm, :] * w_ref[1:2, :]
            + u * w_ref[2:3, :])
    return b_ref[...].astype(F32) * conv


def _retention_kernel(q_ref, k_ref, v_ref, g_ref, cos_ref, sin_ref, din_ref, dcr_ref, dst_ref,
                      o_ref, *st_refs, nc, ck, dk):
    half = dk // 2
    for st_ref in st_refs:
        st_ref[...] = jnp.zeros(st_ref.shape, F32)

    def rot(x, cos, sin):
        x1, x2 = x[:, :half], x[:, half:]
        return jnp.concatenate([x1 * cos - x2 * sin, x1 * sin + x2 * cos], axis=-1)

    def body(c, carry):
        r0 = pl.multiple_of(c * ck, ck)
        cos = cos_ref[pl.ds(r0, ck), :]
        sin = sin_ref[pl.ds(r0, ck), :]
        for h, st_ref in enumerate(st_refs):
            cols = slice(h * dk, (h + 1) * dk)
            d_cross = dcr_ref[h]
            g_chunk = dcr_ref[h, ck - 1:ck, :]
            q = rot(q_ref[pl.ds(r0, ck), cols].astype(F32), cos, sin)
            k = rot(k_ref[pl.ds(r0, ck), cols].astype(F32), cos, sin) * (dk ** -0.5)
            v = v_ref[pl.ds(r0, ck), cols]
            intra = _dot_nt(q.astype(BF16), k.astype(BF16)) * din_ref[h]
            state = st_ref[...]
            o = _dot(intra.astype(BF16), v) + _dot((q * d_cross).astype(BF16), state.astype(BF16))
            st_ref[...] = state * g_chunk + _dot_tn((k * dst_ref[h]).astype(BF16), v)
            r = lax.rsqrt(jnp.mean(o * o, axis=-1, keepdims=True) + RMS_EPS)
            gate = _silu(g_ref[pl.ds(r0, ck), cols].astype(F32))
            o_ref[pl.ds(r0, ck), cols] = (o * r * gate).astype(o_ref.dtype)
        return carry

    lax.fori_loop(0, nc, body, 0)


def retention(z, bsz, seq, ch, col0):
    t = z.shape[0]
    nh = RET_HEADS
    dk = ch // nh
    ck = min(RET_CHUNK, seq)
    nc = seq // ck
    half = dk // 2
    pos = jnp.arange(seq, dtype=F32)
    inv = 1.0 / (10000.0 ** jnp.linspace(0.0, 1.0, half, dtype=F32))
    ang = pos[:, None] * inv[None, :]
    cos, sin = jnp.cos(ang), jnp.sin(ang)
    log_g = jnp.log(1.0 - 2.0 ** (-5.0 - jnp.arange(nh, dtype=F32)))
    i = jnp.arange(ck, dtype=F32)
    diff = i[:, None] - i[None, :]
    d_intra = jnp.where(diff >= 0, jnp.exp(log_g[:, None, None] * jnp.maximum(diff, 0.0)), 0.0)
    d_cross = jnp.exp(log_g[:, None] * (i[None, :] + 1.0))[..., None]
    d_state = jnp.exp(log_g[:, None] * (ck - 1.0 - i[None, :]))[..., None]
    cb = col0 // ch
    col = lambda g: (lambda b: (b, cb + g))
    whole = lambda b: (0, 0, 0)
    return pl.pallas_call(
        functools.partial(_retention_kernel, nc=nc, ck=ck, dk=dk),
        grid=(bsz,),
        in_specs=[
            pl.BlockSpec((seq, ch), col(0)),
            pl.BlockSpec((seq, ch), col(1)),
            pl.BlockSpec((seq, ch), col(2)),
            pl.BlockSpec((seq, ch), col(3)),
            pl.BlockSpec((seq, half), lambda b: (0, 0)),
            pl.BlockSpec((seq, half), lambda b: (0, 0)),
            pl.BlockSpec((nh, ck, ck), whole),
            pl.BlockSpec((nh, ck, 1), whole),
            pl.BlockSpec((nh, ck, 1), whole),
        ],
        out_specs=pl.BlockSpec((seq, ch), lambda b: (b, 0)),
        out_shape=jax.ShapeDtypeStruct((t, ch), BF16),
        scratch_shapes=[pltpu.VMEM((dk, dk), F32) for _ in range(nh)],
        compiler_params=_cparams(("parallel",)),
        name="retention",
    )(z, z, z, z, cos, sin, d_intra, d_cross, d_state)


def _route(h, wr_ref, br_ref, ri_ref, rf_ref, cnt_ref, carry_ref, tm):
    logits = _dot_nt(wr_ref[...], h.astype(BF16))
    mx = jnp.max(logits, axis=0, keepdims=True)
    ex = jnp.exp(logits - mx)
    probs = ex / jnp.sum(ex, axis=0, keepdims=True)
    sel = probs + br_ref[...]
    s = [sel[e:e + 1, :] for e in range(N_EXPERTS)]
    p = [probs[e:e + 1, :] for e in range(N_EXPERTS)]
    epg = EXPERTS_PER_GROUP

    def first_argmax(vals, exclude=None):
        best = jnp.full_like(vals[0], -jnp.inf)
        idx = jnp.zeros(vals[0].shape, I32)
        for j, vj in enumerate(vals):
            better = vj > best
            if exclude is not None:
                better = better & (exclude != j)
            idx = jnp.where(better, j, idx)
            best = jnp.where(better, vj, best)
        return idx

    gscore = []
    for g in range(N_GROUPS):
        gs = s[g * epg:(g + 1) * epg]
        best = None
        for a in range(epg):
            for b in range(a + 1, epg):
                pair = gs[a] + gs[b]
                best = pair if best is None else jnp.maximum(best, pair)
        gscore.append(best)
    grp = first_argmax(gscore)

    def pick(rows, index, n):
        out = rows[n - 1]
        for j in range(n - 2, -1, -1):
            out = jnp.where(index == j, rows[j], out)
        return out

    in_s = [pick([s[g * epg + j] for g in range(N_GROUPS)], grp, N_GROUPS) for j in range(epg)]
    in_p = [pick([p[g * epg + j] for g in range(N_GROUPS)], grp, N_GROUPS) for j in range(epg)]
    i1 = first_argmax(in_s)
    i2 = first_argmax(in_s, exclude=i1)
    p1 = pick(in_p, i1, epg)
    p2 = pick(in_p, i2, epg)
    e1 = grp * epg + i1
    e2 = grp * epg + i2
    den = p1 + p2
    g1 = p1 / den
    g2 = p2 / den

    eidx = lax.broadcasted_iota(I32, (N_EXPERTS, tm), 0)
    member = (eidx == e1) | (eidx == e2)
    member_f = jnp.where(member, 1.0, 0.0)
    before = lax.broadcasted_iota(I32, (tm, tm), 0) < lax.broadcasted_iota(I32, (tm, tm), 1)
    prefix = _dot(member_f.astype(BF16), jnp.where(before, 1.0, 0.0).astype(BF16))
    base = prefix + carry_ref[:, 0:1]
    rank1 = jnp.sum(jnp.where(eidx == e1, base, 0.0), axis=0, keepdims=True).astype(I32)
    rank2 = jnp.sum(jnp.where(eidx == e2, base, 0.0), axis=0, keepdims=True).astype(I32)
    carry_ref[...] = carry_ref[...] + jnp.sum(member_f, axis=1, keepdims=True)
    cnt_ref[...] = carry_ref[...].astype(I32)

    zi = jnp.zeros((SUBLANE - 4, tm), I32)
    ri_ref[...] = jnp.concatenate([e1, e2, rank1, rank2, zi], axis=0)
    zf = jnp.zeros((SUBLANE - 2, tm), F32)
    rf_ref[...] = jnp.concatenate([g1, g2, zf], axis=0)


def _mixer_out_kernel(*refs, tm, conv):
    if conv:
        b_ref, c_ref, v_ref, cw_ref, cb_ref, *refs = refs
    else:
        ya_ref, *refs = refs
    (yb_ref, wa_ref, wb_ref, x_ref, modm_ref, modf_ref, nw_ref, wr_ref, br_ref,
     x1_ref, h_ref, ri_ref, rf_ref, cnt_ref, carry_ref, *conv_scratch) = refs
    s = pl.program_id(1)

    @pl.when((pl.program_id(0) == 0) & (s == 0))
    def _():
        carry_ref[...] = jnp.zeros(carry_ref.shape, F32)

    if conv:
        ya = _gated_conv(b_ref, c_ref, v_ref, cw_ref, cb_ref, conv_scratch[0], s, tm).astype(BF16)
    else:
        ya = ya_ref[...]
    y = _dot(ya, wa_ref[...]) + _dot(yb_ref[...], wb_ref[...])
    x1 = x_ref[...] + modm_ref[0, 2:3, :] * y
    x1_ref[...] = x1
    h = _modulated_norm(x1, modf_ref, nw_ref)
    h_ref[...] = h
    _route(h, wr_ref, br_ref, ri_ref, rf_ref, cnt_ref, carry_ref, tm)


def mixer_out(lhs, w_bf, x2, mod_mix, mod_ffn, norm_ffn_w, w_router, b_router, bsz, seq, tm, conv=None):
    t, d = x2.shape
    kh = w_bf.shape[0] // 2
    per = seq // tm
    ne = N_EXPERTS
    row = lambda g: (lambda b, s: (b * per + s, g))
    const = lambda b, s: (0, 0)
    if conv is None:
        (o,) = lhs
        lhs_args = (o, o)
        lhs_specs = [pl.BlockSpec((tm, kh), row(0)), pl.BlockSpec((tm, kh), row(1))]
        scratch = []
    else:
        z, yb = lhs
        conv_w, conv_b = conv
        lhs_args = (z, z, z, conv_w, conv_b.reshape(1, kh), yb)
        lhs_specs = [pl.BlockSpec((tm, kh), row(0)), pl.BlockSpec((tm, kh), row(1)),
                     pl.BlockSpec((tm, kh), row(2)), pl.BlockSpec((CONV_WIDTH, kh), const),
                     pl.BlockSpec((1, kh), const), pl.BlockSpec((tm, kh), row(0))]
        scratch = [pltpu.VMEM((tm + SUBLANE, kh), F32)]
    batch = lambda b, s: (b, 0, 0)
    return pl.pallas_call(
        functools.partial(_mixer_out_kernel, tm=tm, conv=conv is not None),
        grid=(bsz, per),
        in_specs=lhs_specs + [
            pl.BlockSpec((kh, d), const),
            pl.BlockSpec((kh, d), lambda b, s: (1, 0)),
            pl.BlockSpec((tm, d), row(0)),
            pl.BlockSpec((1, 3, d), batch),
            pl.BlockSpec((1, 3, d), batch),
            pl.BlockSpec((1, d), const),
            pl.BlockSpec((ne, d), const),
            pl.BlockSpec((ne, 1), const),
        ],
        out_specs=[
            pl.BlockSpec((tm, d), row(0)),
            pl.BlockSpec((tm, d), row(0)),
            pl.BlockSpec((SUBLANE, tm), lambda b, s: (0, b * per + s)),
            pl.BlockSpec((SUBLANE, tm), lambda b, s: (0, b * per + s)),
            pl.BlockSpec((ne, LANE), const),
        ],
        out_shape=[
            jax.ShapeDtypeStruct((t, d), F32),
            jax.ShapeDtypeStruct((t, d), F32),
            jax.ShapeDtypeStruct((SUBLANE, t), I32),
            jax.ShapeDtypeStruct((SUBLANE, t), F32),
            jax.ShapeDtypeStruct((ne, LANE), I32),
        ],
        scratch_shapes=[pltpu.VMEM((ne, LANE), F32)] + scratch,
        compiler_params=_cparams(("arbitrary", "arbitrary")),
        name="mixer_out",
    )(*lhs_args, w_bf, w_bf, x2, mod_mix, mod_ffn, norm_ffn_w.reshape(1, d),
      w_router.T.astype(BF16), b_router.reshape(ne, 1))


def _dispatch_kernel(dest_ref, pad0_ref, padn_ref, h_ref, xs_ref, hbuf_ref, zero_ref, sem, zsem,
                     *, tm, nt):
    i = pl.program_id(0)
    cur = i % 2
    base = i * tm

    def zero_row(e, r):
        return pltpu.make_async_copy(zero_ref.at[pl.ds(0, 1), :],
                                     xs_ref.at[pl.ds(pad0_ref[e] + r, 1), :], zsem)

    def tile_done(buf):
        for _ in range(2):
            pltpu.make_async_copy(hbuf_ref.at[buf], xs_ref.at[pl.ds(0, tm), :], sem.at[buf]).wait()

    @pl.when(i == 0)
    def _():
        zero_ref[...] = jnp.zeros(zero_ref.shape, F32)
        for e in range(N_EXPERTS):
            lax.fori_loop(0, padn_ref[e], lambda r, c, e=e: (zero_row(e, r).start(), c)[1], 0)

    hbuf_ref[cur] = h_ref[...]

    def start(r, carry):
        for slot in range(2):
            d = dest_ref[2 * (base + r) + slot]
            pltpu.make_async_copy(hbuf_ref.at[cur, pl.ds(r, 1), :], xs_ref.at[pl.ds(d, 1), :],
                                  sem.at[cur]).start()
        return carry

    lax.fori_loop(0, tm, start, 0, unroll=DMA_UNROLL)

    @pl.when(i > 0)
    def _():
        tile_done(1 - cur)

    @pl.when(i == nt - 1)
    def _():
        tile_done(cur)
        for e in range(N_EXPERTS):
            lax.fori_loop(0, padn_ref[e], lambda r, c, e=e: (zero_row(e, r).wait(), c)[1], 0)


def dispatch(h, dest, pad_start, pad_count, n_rows, tm):
    t, d = h.shape
    return pl.pallas_call(
        functools.partial(_dispatch_kernel, tm=tm, nt=t // tm),
        grid_spec=pltpu.PrefetchScalarGridSpec(
            num_scalar_prefetch=3,
            grid=(t // tm,),
            in_specs=[pl.BlockSpec((tm, d), lambda i, *_: (i, 0))],
            out_specs=pl.BlockSpec(memory_space=pl.ANY),
            scratch_shapes=[pltpu.VMEM((2, tm, d), F32), pltpu.VMEM((SUBLANE, d), F32),
                            pltpu.SemaphoreType.DMA((2,)), pltpu.SemaphoreType.DMA(())],
        ),
        out_shape=jax.ShapeDtypeStruct((n_rows, d), F32),
        compiler_params=_cparams(("arbitrary",)),
        name="moe_dispatch",
    )(dest, pad_start, pad_count, h)


def _ffn_kernel(be_ref, bv_ref, xs_ref, w1_ref, w3_ref, w2_ref, ys_ref):
    valid = bv_ref[pl.program_id(0)]

    @pl.when(valid > 0)
    def _():
        rows = lax.broadcasted_iota(I32, xs_ref.shape, 0)
        x = jnp.where(rows < valid, xs_ref[...], 0.0).astype(BF16)
        h1 = _dot(x, w1_ref[0])
        h3 = _dot(x, w3_ref[0])
        a = (_silu(h1) * h3).astype(BF16)
        ys_ref[...] = _dot(a, w2_ref[0])

    @pl.when(valid <= 0)
    def _():
        ys_ref[...] = jnp.zeros(ys_ref.shape, F32)


def grouped_ffn(xs, block_e, block_valid, w1, w3, w2, bm):
    n_rows, d = xs.shape
    f = w1.shape[2]
    return pl.pallas_call(
        _ffn_kernel,
        grid_spec=pltpu.PrefetchScalarGridSpec(
            num_scalar_prefetch=2,
            grid=(n_rows // bm,),
            in_specs=[
                pl.BlockSpec((bm, d), lambda i, be, bv: (i, 0)),
                pl.BlockSpec((1, d, f), lambda i, be, bv: (be[i], 0, 0)),
                pl.BlockSpec((1, d, f), lambda i, be, bv: (be[i], 0, 0)),
                pl.BlockSpec((1, f, d), lambda i, be, bv: (be[i], 0, 0)),
            ],
            out_specs=pl.BlockSpec((bm, d), lambda i, be, bv: (i, 0)),
        ),
        out_shape=jax.ShapeDtypeStruct((n_rows, d), F32),
        compiler_params=_cparams(("arbitrary",)),
        name="moe_ffn",
    )(block_e, block_valid, xs, w1, w3, w2)


def _combine_kernel(dest_ref, ys_ref, x_ref, mod_ref, gf_ref, *refs, tm, nt, emit_next):
    if emit_next:
        modn_ref, nwn_ref, o_ref, hn_ref, a_ref, b_ref, sem = refs
    else:
        o_ref, a_ref, b_ref, sem = refs
    i = pl.program_id(0)

    def fetch(tile, buf):
        base = tile * tm

        def start(r, carry):
            for slot, ref in ((0, a_ref), (1, b_ref)):
                d = dest_ref[2 * (base + r) + slot]
                pltpu.make_async_copy(ys_ref.at[pl.ds(d, 1), :], ref.at[buf, pl.ds(r, 1), :],
                                      sem.at[buf]).start()
            return carry

        lax.fori_loop(0, tm, start, 0, unroll=DMA_UNROLL)

    @pl.when(i == 0)
    def _():
        fetch(0, 0)

    @pl.when(i + 1 < nt)
    def _():
        fetch(i + 1, (i + 1) % 2)

    cur = i % 2
    pltpu.make_async_copy(ys_ref.at[pl.ds(0, tm), :], a_ref.at[cur], sem.at[cur]).wait()
    pltpu.make_async_copy(ys_ref.at[pl.ds(0, tm), :], b_ref.at[cur], sem.at[cur]).wait()
    y = gf_ref[:, 0:1] * a_ref[cur] + gf_ref[:, 1:2] * b_ref[cur]
    out = x_ref[...] + mod_ref[0, 2:3, :] * y
    o_ref[...] = out
    if emit_next:
        hn_ref[...] = _modulated_norm(out, modn_ref, nwn_ref).astype(BF16)


def combine(ys, dest, x2, mod, gates_t, seq, tm, next_norm=None):
    t, d = x2.shape
    per = seq // tm
    row = lambda i, dest: (i, 0)
    batch = lambda i, dest: (i // per, 0, 0)
    in_specs = [pl.BlockSpec(memory_space=pl.ANY), pl.BlockSpec((tm, d), row),
                pl.BlockSpec((1, 3, d), batch), pl.BlockSpec((tm, SUBLANE), row)]
    out_specs = [pl.BlockSpec((tm, d), row)]
    out_shape = [jax.ShapeDtypeStruct((t, d), F32)]
    args = [dest, ys, x2, mod, gates_t]
    if next_norm is not None:
        in_specs += [pl.BlockSpec((1, 3, d), batch), pl.BlockSpec((1, d), lambda i, dest: (0, 0))]
        out_specs += [pl.BlockSpec((tm, d), row)]
        out_shape += [jax.ShapeDtypeStruct((t, d), BF16)]
        args += [next_norm[0], next_norm[1].reshape(1, d)]
    return pl.pallas_call(
        functools.partial(_combine_kernel, tm=tm, nt=t // tm, emit_next=next_norm is not None),
        grid_spec=pltpu.PrefetchScalarGridSpec(
            num_scalar_prefetch=1,
            grid=(t // tm,),
            in_specs=in_specs,
            out_specs=out_specs,
            scratch_shapes=[pltpu.VMEM((2, tm, d), F32), pltpu.VMEM((2, tm, d), F32),
                            pltpu.SemaphoreType.DMA((2,))],
        ),
        out_shape=out_shape,
        compiler_params=_cparams(("arbitrary",)),
        name="moe_combine",
    )(*args)


def moe(x2, h, ri, rf, cnt, mod, w1, w3, w2, expert0, seq, next_norm=None):
    t, d = x2.shape
    bm = MOE_ROWS
    tm = min(256, seq)
    ne = N_EXPERTS
    counts = cnt[:, 0]
    nblk = (counts + bm - 1) // bm
    blk_end = jnp.cumsum(nblk)
    blk_start = blk_end - nblk
    eids = jnp.arange(ne, dtype=I32)
    e12 = ri[0:2].T
    row0 = jnp.sum(jnp.where(e12[..., None] == eids, blk_start * bm, 0), axis=-1)
    dest = (row0 + ri[2:4].T).reshape(2 * t).astype(I32)
    n_blocks = (2 * t) // bm + ne
    bidx = jnp.arange(n_blocks, dtype=I32)
    block_e = jnp.minimum(jnp.sum(bidx[:, None] >= blk_end[None, :], axis=1), ne - 1).astype(I32)
    onehot = block_e[:, None] == eids
    cnt_b = jnp.sum(jnp.where(onehot, counts, 0), axis=1)
    start_b = jnp.sum(jnp.where(onehot, blk_start, 0), axis=1)
    block_valid = jnp.clip(cnt_b - (bidx - start_b) * bm, 0, bm).astype(I32)
    pad_start = (blk_start * bm + counts).astype(I32)
    pad_end = jnp.where(eids == ne - 1, n_blocks * bm, blk_end * bm)
    xs = dispatch(h, dest, pad_start, (pad_end - pad_start).astype(I32), n_blocks * bm, tm)
    ys = grouped_ffn(xs, block_e + expert0, block_valid, w1, w3, w2, bm)
    return combine(ys, dest, x2, mod, rf.T, seq, tm, next_norm)


def _t5_bucket(n):
    max_exact = N_BUCKETS // 2
    nf = jnp.maximum(n, 1).astype(F32)
    large = max_exact + (jnp.log(nf / max_exact) / math.log(MAX_DISTANCE / max_exact)
                         * (N_BUCKETS - max_exact)).astype(I32)
    large = jnp.minimum(large, N_BUCKETS - 1)
    return jnp.where(n < max_exact, n, large)


def _sortable(score):
    bits = pltpu.bitcast(score, I32)
    return bits ^ ((bits >> 31) & 0x7FFFFFFF)


def _dsa_kernel(q_ref, k_ref, v_ref, qi_ref, kw_ref, wq_ref, tb_ref, qn_ref, kn_ref, o_ref,
                khat_ref, vt_ref, qrt_ref, qirt_ref, wt_ref, mt_ref, hi_ref, lo_ref, lo2_ref, nm_ref, lg_ref,
                *acc_refs,
                nh, dh, nih, di, topk, nb):
    j = pl.program_id(1)
    qb = Q_BLOCK
    kc = 2 * qb
    nch = (j + 2) // 2
    pair = 2 * qb

    def transpose(x):
        return x.astype(F32).T

    @pl.when(j == 0)
    def _():
        k = k_ref[...].astype(F32)
        r = lax.rsqrt(jnp.mean(k * k, axis=-1, keepdims=True) + RMS_EPS)
        khat_ref[...] = (k * r * kn_ref[...]).astype(BF16)
        for c in range(nb // 2):
            for s in range(2):
                blk = v_ref[(2 * c + s) * qb:(2 * c + s + 1) * qb, :]
                vt_ref[c, 0:dh, s * qb:(s + 1) * qb] = transpose(blk).astype(BF16)
            vt_ref[c, dh:dh + ONES_ROWS, :] = jnp.ones((ONES_ROWS, kc), BF16)
        for ref in (hi_ref, lo_ref, lo2_ref):
            ref[...] = jnp.full(ref.shape, I16_MIN, I16)

    qscale = (dh ** -0.5) * math.log2(math.e)
    for h in range(nh):
        qh = q_ref[:, h * dh:(h + 1) * dh].astype(F32)
        r = lax.rsqrt(jnp.mean(qh * qh, axis=-1, keepdims=True) + RMS_EPS)
        qhat = (qh * r * qn_ref[...] * qscale).astype(BF16)
        qrt_ref[:, h * qb:(h + 1) * qb] = transpose(qhat).astype(BF16)
    for g in range(nih * di // LANE):
        two = transpose(qi_ref[:, g * LANE:(g + 1) * LANE])
        for s in range(LANE // di):
            h = g * (LANE // di) + s
            qirt_ref[:, h * qb:(h + 1) * qb] = two[s * di:(s + 1) * di, :].astype(BF16)
    wt_ref[...] = transpose(wq_ref[...])

    key_l = lax.broadcasted_iota(I32, (kc, qb), 0)
    q_pos = j * qb + lax.broadcasted_iota(I32, (kc, qb), 1)

    def score_body(c, carry):
        r0 = pl.multiple_of(c * kc, kc)
        ki = kw_ref[pl.ds(r0, kc), 0:di]
        acc = jnp.zeros((kc, qb), F32)
        for g in range(nih // 2):
            rel = _dot(ki, qirt_ref[:, g * pair:(g + 1) * pair])
            for s in range(2):
                h = 2 * g + s
                acc = acc + wt_ref[di + h:di + h + 1, :] * jnp.maximum(rel[:, s * qb:(s + 1) * qb], 0.0)
        m = jnp.where(r0 + key_l <= q_pos, _sortable(acc), INT_MIN)
        mt_ref[c] = m
        hi_ref[c] = (m >> 16).astype(I16)
        lo_ref[c] = ((m & 0xFFFF) + I16_MIN).astype(I16)
        return carry

    lax.fori_loop(0, nch, score_body, 0)

    def count_ge(cand):
        def cbody(c, acc):
            hit = jnp.where(mt_ref[c] >= cand, 1, 0)
            return acc + jnp.sum(hit.reshape(kc // SUBLANE, SUBLANE, qb), axis=0)
        acc = lax.fori_loop(0, nch, cbody, jnp.zeros((SUBLANE, qb), I32))
        return jnp.sum(acc, axis=0, keepdims=True)

    def fold16(hit):
        parts = [hit[PACK16 * i:PACK16 * (i + 1), :] for i in range(kc // PACK16)]
        while len(parts) > 1:
            parts = [parts[i] + parts[i + 1] for i in range(0, len(parts), 2)]
        return parts[0]

    def total16(acc):
        return jnp.sum(acc.astype(F32), axis=0, keepdims=True).astype(I32)

    one, nil = jnp.ones((), BF16), jnp.zeros((), BF16)

    def count16(src_ref, cand, n):
        c16 = cand.astype(I16)
        acc = fold16(jnp.where(src_ref[0] >= c16, one, nil))
        for c in range(1, n):
            acc = acc + fold16(jnp.where(src_ref[c] >= c16, one, nil))
        return total16(acc)

    def search16(src_ref, kth, n):
        zero = jnp.zeros((1, qb), I32)
        ans0 = jnp.where(count16(src_ref, zero, n) >= kth, zero, I16_MIN)

        def bit_body(bi, ans):
            cand = ans | (1 << (14 - bi))
            return jnp.where(count16(src_ref, cand, n) >= kth, cand, ans)

        return lax.fori_loop(0, 15, bit_body, ans0)

    def search(n):
        top = search16(hi_ref, jnp.full((1, qb), topk, I32), n)
        top16 = top.astype(I16)
        acc = jnp.zeros((PACK16, qb), BF16)
        for c in range(n):
            hi = hi_ref[c]
            lo2_ref[c] = jnp.where(hi == top16, lo_ref[c], jnp.full((), I16_MIN, I16))
            acc = acc + fold16(jnp.where(hi > top16, one, nil))
        above = total16(acc)
        low = search16(lo2_ref, topk - above, n)
        return top * 65536 + (low - I16_MIN), above + count16(lo2_ref, low, n)

    def search_any():
        return lax.cond(nch <= nb // 4, functools.partial(search, nb // 4),
                        functools.partial(search, nb // 2))

    selecting = (j + 1) * qb > topk
    thr, n_ge = lax.cond(selecting, search_any,
                         lambda: (jnp.full((1, qb), INT_MIN + 1, I32), jnp.zeros((1, qb), I32)))

    def plain_mask():
        def body(c, carry):
            nm_ref[c] = jnp.where(mt_ref[c] >= thr, 0.0, NEG_BIG)
            return carry
        lax.fori_loop(0, nch, body, 0)

    def tie_mask():
        need = (topk - count_ge(thr + 1)).astype(F32)
        tri = jnp.where(lax.broadcasted_iota(I32, (kc, kc), 1) <= lax.broadcasted_iota(I32, (kc, kc), 0),
                        1.0, 0.0).astype(BF16)

        def body(c, seen):
            m = mt_ref[c]
            tie = m == thr
            upto = _dot(tri, jnp.where(tie, 1.0, 0.0).astype(BF16)) + seen
            nm_ref[c] = jnp.where((m > thr) | (tie & (upto <= need)), 0.0, NEG_BIG)
            return upto[kc - 1:kc, :]
        lax.fori_loop(0, nch, body, jnp.zeros((1, qb), F32))

    crowded = jnp.max(n_ge) > topk
    lax.cond(crowded, tie_mask, plain_mask)

    npair = nh // 2
    pcols = [slice(pr * pair, (pr + 1) * pair) for pr in range(npair)]
    nfar = jnp.maximum((j - 1) // 2, 0)

    def sweep1(c, mruns, near):
        r0 = pl.multiple_of(c * kc, kc)
        kh = khat_ref[pl.ds(r0, kc), :]
        nm = nm_ref[c]
        nm2 = jnp.concatenate([nm, nm], axis=1)
        if near:
            k0 = jnp.clip(j - 2 * c, 0, 2)
            k1 = jnp.clip(j - 2 * c - 1, 0, 2)
        out = []
        for pr in range(npair):
            lg = _dot(kh, qrt_ref[:, pcols[pr]]) + nm2
            if near:
                lg = lg + jnp.concatenate([tb_ref[k0, :, pcols[pr]], tb_ref[k1, :, pcols[pr]]], axis=0)
            lg_ref[c, :, pcols[pr]] = lg
            out.append(jnp.maximum(mruns[pr], jnp.max(lg.reshape(kc // SUBLANE, SUBLANE, pair), axis=0)))
        return tuple(out)

    mruns = tuple(jnp.full((SUBLANE, pair), NEG_BIG, F32) for _ in range(npair))
    mruns = lax.fori_loop(0, nfar, functools.partial(sweep1, near=False), mruns)
    mruns = lax.fori_loop(nfar, nch, functools.partial(sweep1, near=True), mruns)
    mrows = [jnp.max(m, axis=0, keepdims=True) for m in mruns]
    for acc_ref in acc_refs:
        acc_ref[...] = jnp.zeros(acc_ref.shape, F32)

    def sweep2(c, carry):
        vt = vt_ref[c]
        for pr in range(npair):
            p = jnp.exp2((lg_ref[c, :, pcols[pr]] - mrows[pr]).astype(BF16))
            acc_refs[pr][...] += _dot(vt, p)
        return carry

    lax.fori_loop(0, nch, sweep2, 0)
    for pr in range(npair):
        out = acc_refs[pr][0:dh, :] / acc_refs[pr][dh:dh + 1, :]
        for s in range(2):
            h = 2 * pr + s
            o_ref[:, h * dh:(h + 1) * dh] = transpose(
                out[:, s * qb:(s + 1) * qb].astype(BF16)).astype(o_ref.dtype)


def dsa_attention(z, rel_bias, q_norm, k_norm, bsz, seq, d):
    t = z.shape[0]
    nh, dh, nih, di = ATT_HEADS, d // ATT_HEADS, IDX_HEADS, IDX_DIM
    qb = Q_BLOCK
    nb = seq // qb
    topk = min(DSA_TOPK_MAX, seq // 4)
    assert topk % qb == 0 and dh == LANE and nih * di == d // 2 and nb % 2 == 0 and LANE % di == 0
    assert (nb // 2) * (2 * qb // PACK16) <= 256
    i = jnp.arange(qb)
    dist = i[None, :] - i[:, None]
    def lookup(bucket):
        hot = bucket[..., None] == jnp.arange(N_BUCKETS, dtype=I32)
        return jnp.sum(jnp.where(hot[..., None], rel_bias, 0.0), axis=-2)

    tabs = [lookup(_t5_bucket(jnp.maximum(dist + off * qb, 0))) for off in range(2)]
    far = jnp.broadcast_to(lookup(_t5_bucket(jnp.full((), 2 * qb, I32))), (qb, qb, nh))
    tb = jnp.stack(tabs + [far]).transpose(0, 1, 3, 2).reshape(3, qb, nh * qb).astype(F32)
    tb = (tb - tb[2:3]) * math.log2(math.e)
    cq = d // LANE
    return pl.pallas_call(
        functools.partial(_dsa_kernel, nh=nh, dh=dh, nih=nih, di=di, topk=topk, nb=nb),
        grid=(bsz, nb),
        in_specs=[
            pl.BlockSpec((qb, d), lambda b, j: (b * nb + j, 0)),
            pl.BlockSpec((seq, LANE), lambda b, j: (b, cq + cq // 2)),
            pl.BlockSpec((seq, LANE), lambda b, j: (b, cq + cq // 2 + 1)),
            pl.BlockSpec((qb, d // 2), lambda b, j: (b * nb + j, 2)),
            pl.BlockSpec((seq, LANE), lambda b, j: (b, cq + cq // 2 + 2)),
            pl.BlockSpec((qb, LANE), lambda b, j: (b * nb + j, cq + cq // 2 + 2)),
            pl.BlockSpec((3, qb, nh * qb), lambda b, j: (0, 0, 0)),
            pl.BlockSpec((1, dh), lambda b, j: (0, 0)),
            pl.BlockSpec((1, dh), lambda b, j: (0, 0)),
        ],
        out_specs=pl.BlockSpec((qb, d), lambda b, j: (b * nb + j, 0)),
        out_shape=jax.ShapeDtypeStruct((t, d), BF16),
        scratch_shapes=[
            pltpu.VMEM((seq, dh), BF16),
            pltpu.VMEM((nb // 2, dh + ONES_ROWS, 2 * qb), BF16),
            pltpu.VMEM((dh, nh * qb), BF16),
            pltpu.VMEM((di, nih * qb), BF16),
            pltpu.VMEM((LANE, qb), F32),
            pltpu.VMEM((nb // 2, 2 * qb, qb), I32),
            pltpu.VMEM((nb // 2, 2 * qb, qb), I16),
            pltpu.VMEM((nb // 2, 2 * qb, qb), I16),
            pltpu.VMEM((nb // 2, 2 * qb, qb), I16),
            pltpu.VMEM((nb // 2, 2 * qb, qb), F32),
            pltpu.VMEM((nb // 2, 2 * qb, nh * qb), F32),
        ] + [pltpu.VMEM((dh + ONES_ROWS, 2 * qb), F32)
             for _ in range(nh // 2)],
        compiler_params=_cparams(("arbitrary", "arbitrary")),
        name="dsa_attention",
    )(z, z, z, z, z, z, tb, q_norm.reshape(1, dh), k_norm.reshape(1, dh))


def _dsa_weight(c_w_in, d):
    dh = d // ATT_HEADS
    nqi = IDX_HEADS * IDX_DIM
    q, k, v, qi, ki, wi = jnp.split(
        c_w_in, [d, d + dh, d + 2 * dh, d + 2 * dh + nqi, d + 2 * dh + nqi + IDX_DIM], axis=1)
    used = d + nqi + 2 * dh + IDX_DIM + IDX_HEADS
    total = -(-used // 512) * 512
    pad = jnp.zeros((d, total - used), c_w_in.dtype)
    return jnp.concatenate([q, qi, k, v, ki, wi, pad], axis=1).astype(BF16)


def kernel(x, c, rel_bias, w_router, b_router, norm_mix, ada_w_mix, ada_b_mix, norm_ffn, ada_w_ffn,
           ada_b_ffn, ab_w_in, ab_conv_w, ab_conv_b, ab_w_out, c_w_in, c_q_norm, c_k_norm, c_w_out,
           moe_w1, moe_w3, moe_w2):
    bsz, seq, d = x.shape
    depth = norm_mix.shape[0]
    t = bsz * seq
    ch = d // 2
    tm_out = min(512, seq)
    x2 = x.reshape(t, d)
    mod_mix = ada_mod(c, ada_w_mix, ada_b_mix).reshape(depth, bsz, 3, d)
    mod_ffn = ada_mod(c, ada_w_ffn, ada_b_ffn).reshape(depth, bsz, 3, d)
    stack = lambda w: w.astype(BF16).reshape((depth * N_EXPERTS,) + w.shape[2:])
    w1_bf, w3_bf, w2_bf = stack(moe_w1), stack(moe_w3), stack(moe_w2)
    h_mix = None
    for layer in range(depth):
        i = layer // 2
        if layer % 2 == 0:
            w_in, tn = ab_w_in[i].astype(BF16), 1024
        else:
            w_in = _dsa_weight(c_w_in[i], d)
            tn = w_in.shape[1] // 2
        if h_mix is None:
            z = inproj(x2, mod_mix[layer], norm_mix[layer], w_in, seq, min(1024, seq), min(tn, 1024))
        else:
            z = proj(h_mix, w_in, min(1024, seq), tn)
        if layer % 2 == 0:
            y_b = retention(z, bsz, seq, ch, 3 * ch)
            lhs, w_out, conv = (z, y_b), ab_w_out[i], (ab_conv_w[i], ab_conv_b[i])
        else:
            o = dsa_attention(z, rel_bias, c_q_norm[i], c_k_norm[i], bsz, seq, d)
            lhs, w_out, conv = (o,), c_w_out[i], None
        x2, h, ri, rf, cnt = mixer_out(lhs, w_out.astype(BF16), x2, mod_mix[layer], mod_ffn[layer],
                                       norm_ffn[layer], w_router, b_router, bsz, seq, tm_out, conv)
        nxt = (mod_mix[layer + 1], norm_mix[layer + 1]) if layer + 1 < depth else None
        res = moe(x2, h, ri, rf, cnt, mod_ffn[layer], w1_bf, w3_bf, w2_bf, layer * N_EXPERTS, seq, nxt)
        x2, h_mix = (res[0], res[1]) if nxt is not None else (res[0], None)
    return x2.reshape(bsz, seq, d)
```

```python
import functools
import math
from typing import NamedTuple

import jax
import jax.numpy as jnp
from jax import lax
from jax.experimental import pallas as pl
from jax.experimental.pallas import tpu as pltpu

F32 = jnp.float32
BF16 = jnp.bfloat16
I32 = jnp.int32
I16 = jnp.int16

RMS_EPS = 1e-6
CONV_WIDTH = 3
RET_HEADS = 4
ATT_HEADS = 16
IDX_HEADS = 16
IDX_DIM = 64
DSA_TOPK_MAX = 256
Q_BLOCK = 128
N_BUCKETS = 32
MAX_DISTANCE = 128
N_EXPERTS = 16
N_GROUPS = 4
EXPERTS_PER_GROUP = N_EXPERTS // N_GROUPS
RET_CHUNK = 256
MOE_ROWS = 256
DMA_UNROLL = 8
RIDER_BLOCK_BYTES = 1024 * 1024
ONES_ROWS = 16
LANE = 128
SUBLANE = 8
VMEM_LIMIT = 48 * 1024 * 1024
VMEM_LIMIT_LARGE = 56 * 1024 * 1024
NEG_BIG = -1e30
INT_MIN = -(2 ** 31)
I16_MIN = -(2 ** 15)
PACK16 = 16


def _cparams(sem, vmem=VMEM_LIMIT):
    return pltpu.CompilerParams(dimension_semantics=sem, vmem_limit_bytes=vmem)


def _dot(a, b):
    return jnp.dot(a, b, preferred_element_type=F32)


def _dot_nt(a, b):
    return lax.dot_general(a, b, (((1,), (1,)), ((), ())), preferred_element_type=F32)


def _dot_tn(a, b):
    return lax.dot_general(a, b, (((0,), (0,)), ((), ())), preferred_element_type=F32)


def _silu(x):
    return x * jax.nn.sigmoid(x)


class _CastRider(NamedTuple):
    src: jax.Array
    rows: int
    first_blk: int
    nblk: int

    def in_spec(self, flat):
        return pl.BlockSpec((self.rows, self.src.shape[1]),
                            lambda *g: (jnp.minimum(flat(*g), self.nblk - 1) + self.first_blk, 0))

    def out_spec(self, flat):
        return pl.BlockSpec((self.rows, self.src.shape[1]),
                            lambda *g: (jnp.minimum(flat(*g), self.nblk - 1), 0))

    @property
    def out_shape(self):
        return jax.ShapeDtypeStruct((self.rows * self.nblk, self.src.shape[1]), BF16)


def _cast_riders(weights, layer, n_steps):
    riders = []
    for w in weights:
        per_layer, cols = w.shape[1] * w.shape[2], w.shape[3]
        rows = PACK16
        while rows * n_steps < per_layer:
            rows *= 2
        if per_layer % rows or rows * cols * 4 > RIDER_BLOCK_BYTES:
            return None
        nblk = per_layer // rows
        riders.append(_CastRider(w.reshape(-1, cols), rows, layer * nblk, nblk))
    return riders


def _ride_casts(nblks, src_refs, dst_refs, step):
    for nblk, src, dst in zip(nblks, src_refs, dst_refs):
        @pl.when(step < nblk)
        def _():
            dst[...] = src[...].astype(BF16)


def _ada_kernel(c_ref, w_ref, b_ref, o_ref):
    cs = _silu(c_ref[...])
    o_ref[0] = _dot(cs.astype(BF16), w_ref[0].astype(BF16)) + b_ref[0]


def ada_mod(c, ada_w, ada_b):
    nl, d, n3 = ada_w.shape
    bsz = c.shape[0]
    tn = min(512, n3)
    return pl.pallas_call(
        _ada_kernel,
        grid=(nl, n3 // tn),
        in_specs=[
            pl.BlockSpec((bsz, d), lambda l, j: (0, 0)),
            pl.BlockSpec((1, d, tn), lambda l, j: (l, 0, j)),
            pl.BlockSpec((1, 1, tn), lambda l, j: (l, 0, j)),
        ],
        out_specs=pl.BlockSpec((1, bsz, tn), lambda l, j: (l, 0, j)),
        out_shape=jax.ShapeDtypeStruct((nl, bsz, n3), F32),
        compiler_params=_cparams(("parallel", "parallel")),
        name="ada_mod",
    )(c, ada_w, ada_b.reshape(nl, 1, n3))


def _modulated_norm(x, mod_ref, nw_ref):
    r = lax.rsqrt(jnp.mean(x * x, axis=-1, keepdims=True) + RMS_EPS)
    return x * r * nw_ref[...] * (1.0 + mod_ref[0, 1:2, :]) + mod_ref[0, 0:1, :]


def _inproj_kernel(x_ref, mod_ref, nw_ref, w_ref, *refs, riders):
    o_ref, h_ref = refs[len(riders)], refs[-1]
    step = pl.program_id(0) * pl.num_programs(1) + pl.program_id(1)
    _ride_casts(riders, refs[:len(riders)], refs[len(riders) + 1:-1], step)

    @pl.when(pl.program_id(1) == 0)
    def _():
        h_ref[...] = _modulated_norm(x_ref[...], mod_ref, nw_ref).astype(BF16)

    o_ref[...] = _dot(h_ref[...], w_ref[...]).astype(o_ref.dtype)


def _proj_kernel(h_ref, w_ref, o_ref):
    o_ref[...] = _dot(h_ref[...], w_ref[...]).astype(o_ref.dtype)


def proj(h_bf, w_bf, tm, tn):
    t, d = h_bf.shape
    n = w_bf.shape[1]
    return pl.pallas_call(
        _proj_kernel,
        grid=(t // tm, n // tn),
        in_specs=[pl.BlockSpec((tm, d), lambda i, j: (i, 0)), pl.BlockSpec((d, tn), lambda i, j: (0, j))],
        out_specs=pl.BlockSpec((tm, tn), lambda i, j: (i, j)),
        out_shape=jax.ShapeDtypeStruct((t, n), BF16),
        compiler_params=_cparams(("parallel", "arbitrary")),
        name="proj",
    )(h_bf, w_bf)


def inproj(x2, mod, norm_w, w_bf, seq, tm, tn, riders=()):
    t, d = x2.shape
    n = w_bf.shape[1]
    per = seq // tm
    nj = n // tn
    flat = lambda i, j: i * nj + j
    return pl.pallas_call(
        functools.partial(_inproj_kernel, riders=tuple(r.nblk for r in riders)),
        grid=(t // tm, nj),
        in_specs=[
            pl.BlockSpec((tm, d), lambda i, j: (i, 0)),
            pl.BlockSpec((1, 3, d), lambda i, j: (i // per, 0, 0)),
            pl.BlockSpec((1, d), lambda i, j: (0, 0)),
            pl.BlockSpec((d, tn), lambda i, j: (0, j)),
        ] + [r.in_spec(flat) for r in riders],
        out_specs=[pl.BlockSpec((tm, tn), lambda i, j: (i, j))] + [r.out_spec(flat) for r in riders],
        out_shape=[jax.ShapeDtypeStruct((t, n), BF16)] + [r.out_shape for r in riders],
        scratch_shapes=[pltpu.VMEM((tm, d), BF16)],
        compiler_params=_cparams(("arbitrary", "arbitrary"), VMEM_LIMIT_LARGE if riders else VMEM_LIMIT),
        name="inproj",
    )(x2, mod, norm_w.reshape(1, d), w_bf, *[r.src for r in riders])


def _gated_conv(b_ref, c_ref, v_ref, w_ref, cb_ref, u_ref, s, tm):
    u = c_ref[...].astype(F32) * v_ref[...].astype(F32)

    @pl.when(s == 0)
    def _():
        u_ref[0:SUBLANE, :] = jnp.zeros((SUBLANE, u.shape[1]), F32)

    @pl.when(s > 0)
    def _():
        u_ref[0:SUBLANE, :] = u_ref[tm:tm + SUBLANE, :]

    u_ref[SUBLANE:SUBLANE + tm, :] = u
    conv = (cb_ref[...]
            + u_ref[SUBLANE - 2:SUBLANE - 2 + tm, :] * w_ref[0:1, :]
            + u_ref[SUBLANE - 1:SUBLANE - 1 + tm, :] * w_ref[1:2, :]
            + u * w_ref[2:3, :])
    return b_ref[...].astype(F32) * conv


def _retention_kernel(q_ref, k_ref, v_ref, g_ref, cos_ref, sin_ref, din_ref, dcr_ref, dst_ref,
                      o_ref, *st_refs, nc, ck, dk):
    half = dk // 2
    for st_ref in st_refs:
        st_ref[...] = jnp.zeros(st_ref.shape, F32)

    def rot(x, cos, sin):
        x1, x2 = x[:, :half], x[:, half:]
        return jnp.concatenate([x1 * cos - x2 * sin, x1 * sin + x2 * cos], axis=-1)

    def body(c, carry):
        r0 = pl.multiple_of(c * ck, ck)
        cos = cos_ref[pl.ds(r0, ck), :]
        sin = sin_ref[pl.ds(r0, ck), :]
        for h, st_ref in enumerate(st_refs):
            cols = slice(h * dk, (h + 1) * dk)
            d_cross = dcr_ref[h]
            g_chunk = dcr_ref[h, ck - 1:ck, :]
            q = rot(q_ref[pl.ds(r0, ck), cols].astype(F32), cos, sin)
            k = rot(k_ref[pl.ds(r0, ck), cols].astype(F32), cos, sin) * (dk ** -0.5)
            v = v_ref[pl.ds(r0, ck), cols]
            intra = _dot_nt(q.astype(BF16), k.astype(BF16)) * din_ref[h]
            state = st_ref[...]
            o = _dot(intra.astype(BF16), v) + _dot((q * d_cross).astype(BF16), state.astype(BF16))
            st_ref[...] = state * g_chunk + _dot_tn((k * dst_ref[h]).astype(BF16), v)
            r = lax.rsqrt(jnp.mean(o * o, axis=-1, keepdims=True) + RMS_EPS)
            gate = _silu(g_ref[pl.ds(r0, ck), cols].astype(F32))
            o_ref[pl.ds(r0, ck), cols] = (o * r * gate).astype(o_ref.dtype)
        return carry

    lax.fori_loop(0, nc, body, 0)


def retention(z, bsz, seq, ch, col0):
    t = z.shape[0]
    nh = RET_HEADS
    dk = ch // nh
    ck = min(RET_CHUNK, seq)
    nc = seq // ck
    half = dk // 2
    pos = jnp.arange(seq, dtype=F32)
    inv = 1.0 / (10000.0 ** jnp.linspace(0.0, 1.0, half, dtype=F32))
    ang = pos[:, None] * inv[None, :]
    cos, sin = jnp.cos(ang), jnp.sin(ang)
    log_g = jnp.log(1.0 - 2.0 ** (-5.0 - jnp.arange(nh, dtype=F32)))
    i = jnp.arange(ck, dtype=F32)
    diff = i[:, None] - i[None, :]
    d_intra = jnp.where(diff >= 0, jnp.exp(log_g[:, None, None] * jnp.maximum(diff, 0.0)), 0.0)
    d_cross = jnp.exp(log_g[:, None] * (i[None, :] + 1.0))[..., None]
    d_state = jnp.exp(log_g[:, None] * (ck - 1.0 - i[None, :]))[..., None]
    cb = col0 // ch
    col = lambda g: (lambda b: (b, cb + g))
    whole = lambda b: (0, 0, 0)
    return pl.pallas_call(
        functools.partial(_retention_kernel, nc=nc, ck=ck, dk=dk),
        grid=(bsz,),
        in_specs=[
            pl.BlockSpec((seq, ch), col(0)),
            pl.BlockSpec((seq, ch), col(1)),
            pl.BlockSpec((seq, ch), col(2)),
            pl.BlockSpec((seq, ch), col(3)),
            pl.BlockSpec((seq, half), lambda b: (0, 0)),
            pl.BlockSpec((seq, half), lambda b: (0, 0)),
            pl.BlockSpec((nh, ck, ck), whole),
            pl.BlockSpec((nh, ck, 1), whole),
            pl.BlockSpec((nh, ck, 1), whole),
        ],
        out_specs=pl.BlockSpec((seq, ch), lambda b: (b, 0)),
        out_shape=jax.ShapeDtypeStruct((t, ch), BF16),
        scratch_shapes=[pltpu.VMEM((dk, dk), F32) for _ in range(nh)],
        compiler_params=_cparams(("parallel",)),
        name="retention",
    )(z, z, z, z, cos, sin, d_intra, d_cross, d_state)


def _route(h, wr_ref, br_ref, ri_ref, rf_ref, cnt_ref, carry_ref, tm):
    logits = _dot_nt(wr_ref[...], h.astype(BF16))
    mx = jnp.max(logits, axis=0, keepdims=True)
    ex = jnp.exp(logits - mx)
    probs = ex / jnp.sum(ex, axis=0, keepdims=True)
    sel = probs + br_ref[...]
    s = [sel[e:e + 1, :] for e in range(N_EXPERTS)]
    p = [probs[e:e + 1, :] for e in range(N_EXPERTS)]
    epg = EXPERTS_PER_GROUP

    def first_argmax(vals, exclude=None):
        best = jnp.full_like(vals[0], -jnp.inf)
        idx = jnp.zeros(vals[0].shape, I32)
        for j, vj in enumerate(vals):
            better = vj > best
            if exclude is not None:
                better = better & (exclude != j)
            idx = jnp.where(better, j, idx)
            best = jnp.where(better, vj, best)
        return idx

    gscore = []
    for g in range(N_GROUPS):
        gs = s[g * epg:(g + 1) * epg]
        best = None
        for a in range(epg):
            for b in range(a + 1, epg):
                pair = gs[a] + gs[b]
                best = pair if best is None else jnp.maximum(best, pair)
        gscore.append(best)
    grp = first_argmax(gscore)

    def pick(rows, index, n):
        out = rows[n - 1]
        for j in range(n - 2, -1, -1):
            out = jnp.where(index == j, rows[j], out)
        return out

    in_s = [pick([s[g * epg + j] for g in range(N_GROUPS)], grp, N_GROUPS) for j in range(epg)]
    in_p = [pick([p[g * epg + j] for g in range(N_GROUPS)], grp, N_GROUPS) for j in range(epg)]
    i1 = first_argmax(in_s)
    i2 = first_argmax(in_s, exclude=i1)
    p1 = pick(in_p, i1, epg)
    p2 = pick(in_p, i2, epg)
    e1 = grp * epg + i1
    e2 = grp * epg + i2
    den = p1 + p2
    g1 = p1 / den
    g2 = p2 / den

    eidx = lax.broadcasted_iota(I32, (N_EXPERTS, tm), 0)
    member = (eidx == e1) | (eidx == e2)
    member_f = jnp.where(member, 1.0, 0.0)
    before = lax.broadcasted_iota(I32, (tm, tm), 0) < lax.broadcasted_iota(I32, (tm, tm), 1)
    prefix = _dot(member_f.astype(BF16), jnp.where(before, 1.0, 0.0).astype(BF16))
    base = prefix + carry_ref[:, 0:1]
    rank1 = jnp.sum(jnp.where(eidx == e1, base, 0.0), axis=0, keepdims=True).astype(I32)
    rank2 = jnp.sum(jnp.where(eidx == e2, base, 0.0), axis=0, keepdims=True).astype(I32)
    carry_ref[...] = carry_ref[...] + jnp.sum(member_f, axis=1, keepdims=True)
    cnt_ref[...] = carry_ref[...].astype(I32)

    zi = jnp.zeros((SUBLANE - 4, tm), I32)
    ri_ref[...] = jnp.concatenate([e1, e2, rank1, rank2, zi], axis=0)
    zf = jnp.zeros((SUBLANE - 2, tm), F32)
    rf_ref[...] = jnp.concatenate([g1, g2, zf], axis=0)


def _mixer_out_kernel(*refs, tm, conv):
    if conv:
        b_ref, c_ref, v_ref, cw_ref, cb_ref, *refs = refs
    else:
        ya_ref, *refs = refs
    (yb_ref, wa_ref, wb_ref, x_ref, modm_ref, modf_ref, nw_ref, wr_ref, br_ref,
     x1_ref, h_ref, ri_ref, rf_ref, cnt_ref, carry_ref, *conv_scratch) = refs
    s = pl.program_id(1)

    @pl.when((pl.program_id(0) == 0) & (s == 0))
    def _():
        carry_ref[...] = jnp.zeros(carry_ref.shape, F32)

    if conv:
        ya = _gated_conv(b_ref, c_ref, v_ref, cw_ref, cb_ref, conv_scratch[0], s, tm).astype(BF16)
    else:
        ya = ya_ref[...]
    y = _dot(ya, wa_ref[...]) + _dot(yb_ref[...], wb_ref[...])
    x1 = x_ref[...] + modm_ref[0, 2:3, :] * y
    x1_ref[...] = x1
    h = _modulated_norm(x1, modf_ref, nw_ref)
    h_ref[...] = h
    _route(h, wr_ref, br_ref, ri_ref, rf_ref, cnt_ref, carry_ref, tm)


def mixer_out(lhs, w_bf, x2, mod_mix, mod_ffn, norm_ffn_w, w_router, b_router, bsz, seq, tm, conv=None):
    t, d = x2.shape
    kh = w_bf.shape[0] // 2
    per = seq // tm
    ne = N_EXPERTS
    row = lambda g: (lambda b, s: (b * per + s, g))
    const = lambda b, s: (0, 0)
    if conv is None:
        (o,) = lhs
        lhs_args = (o, o)
        lhs_specs = [pl.BlockSpec((tm, kh), row(0)), pl.BlockSpec((tm, kh), row(1))]
        scratch = []
    else:
        z, yb = lhs
        conv_w, conv_b = conv
        lhs_args = (z, z, z, conv_w, conv_b.reshape(1, kh), yb)
        lhs_specs = [pl.BlockSpec((tm, kh), row(0)), pl.BlockSpec((tm, kh), row(1)),
                     pl.BlockSpec((tm, kh), row(2)), pl.BlockSpec((CONV_WIDTH, kh), const),
                     pl.BlockSpec((1, kh), const), pl.BlockSpec((tm, kh), row(0))]
        scratch = [pltpu.VMEM((tm + SUBLANE, kh), F32)]
    batch = lambda b, s: (b, 0, 0)
    return pl.pallas_call(
        functools.partial(_mixer_out_kernel, tm=tm, conv=conv is not None),
        grid=(bsz, per),
        in_specs=lhs_specs + [
            pl.BlockSpec((kh, d), const),
            pl.BlockSpec((kh, d), lambda b, s: (1, 0)),
            pl.BlockSpec((tm, d), row(0)),
            pl.BlockSpec((1, 3, d), batch),
            pl.BlockSpec((1, 3, d), batch),
            pl.BlockSpec((1, d), const),
            pl.BlockSpec((ne, d), const),
            pl.BlockSpec((ne, 1), const),
        ],
        out_specs=[
            pl.BlockSpec((tm, d), row(0)),
            pl.BlockSpec((tm, d), row(0)),
            pl.BlockSpec((SUBLANE, tm), lambda b, s: (0, b * per + s)),
            pl.BlockSpec((SUBLANE, tm), lambda b, s: (0, b * per + s)),
            pl.BlockSpec((ne, LANE), const),
        ],
        out_shape=[
            jax.ShapeDtypeStruct((t, d), F32),
            jax.ShapeDtypeStruct((t, d), F32),
            jax.ShapeDtypeStruct((SUBLANE, t), I32),
            jax.ShapeDtypeStruct((SUBLANE, t), F32),
            jax.ShapeDtypeStruct((ne, LANE), I32),
        ],
        scratch_shapes=[pltpu.VMEM((ne, LANE), F32)] + scratch,
        compiler_params=_cparams(("arbitrary", "arbitrary")),
        name="mixer_out",
    )(*lhs_args, w_bf, w_bf, x2, mod_mix, mod_ffn, norm_ffn_w.reshape(1, d),
      w_router.T.astype(BF16), b_router.reshape(ne, 1))


def _dispatch_kernel(dest_ref, pad0_ref, padn_ref, h_ref, xs_ref, hbuf_ref, zero_ref, sem, zsem,
                     *, tm, nt):
    i = pl.program_id(0)
    cur = i % 2
    base = i * tm

    def zero_row(e, r):
        return pltpu.make_async_copy(zero_ref.at[pl.ds(0, 1), :],
                                     xs_ref.at[pl.ds(pad0_ref[e] + r, 1), :], zsem)

    def tile_done(buf):
        for _ in range(2):
            pltpu.make_async_copy(hbuf_ref.at[buf], xs_ref.at[pl.ds(0, tm), :], sem.at[buf]).wait()

    @pl.when(i == 0)
    def _():
        zero_ref[...] = jnp.zeros(zero_ref.shape, F32)
        for e in range(N_EXPERTS):
            lax.fori_loop(0, padn_ref[e], lambda r, c, e=e: (zero_row(e, r).start(), c)[1], 0)

    hbuf_ref[cur] = h_ref[...]

    def start(r, carry):
        for slot in range(2):
            d = dest_ref[2 * (base + r) + slot]
            pltpu.make_async_copy(hbuf_ref.at[cur, pl.ds(r, 1), :], xs_ref.at[pl.ds(d, 1), :],
                                  sem.at[cur]).start()
        return carry

    lax.fori_loop(0, tm, start, 0, unroll=DMA_UNROLL)

    @pl.when(i > 0)
    def _():
        tile_done(1 - cur)

    @pl.when(i == nt - 1)
    def _():
        tile_done(cur)
        for e in range(N_EXPERTS):
            lax.fori_loop(0, padn_ref[e], lambda r, c, e=e: (zero_row(e, r).wait(), c)[1], 0)


def dispatch(h, dest, pad_start, pad_count, n_rows, tm):
    t, d = h.shape
    return pl.pallas_call(
        functools.partial(_dispatch_kernel, tm=tm, nt=t // tm),
        grid_spec=pltpu.PrefetchScalarGridSpec(
            num_scalar_prefetch=3,
            grid=(t // tm,),
            in_specs=[pl.BlockSpec((tm, d), lambda i, *_: (i, 0))],
            out_specs=pl.BlockSpec(memory_space=pl.ANY),
            scratch_shapes=[pltpu.VMEM((2, tm, d), F32), pltpu.VMEM((SUBLANE, d), F32),
                            pltpu.SemaphoreType.DMA((2,)), pltpu.SemaphoreType.DMA(())],
        ),
        out_shape=jax.ShapeDtypeStruct((n_rows, d), F32),
        compiler_params=_cparams(("arbitrary",)),
        name="moe_dispatch",
    )(dest, pad_start, pad_count, h)


def _ffn_kernel(be_ref, bv_ref, xs_ref, w1_ref, w3_ref, w2_ref, *refs, riders):
    ys_ref = refs[len(riders)]
    _ride_casts(riders, refs[:len(riders)], refs[len(riders) + 1:], pl.program_id(0))
    valid = bv_ref[pl.program_id(0)]

    @pl.when(valid > 0)
    def _():
        rows = lax.broadcasted_iota(I32, xs_ref.shape, 0)
        x = jnp.where(rows < valid, xs_ref[...], 0.0).astype(BF16)
        h1 = _dot(x, w1_ref[0])
        h3 = _dot(x, w3_ref[0])
        a = (_silu(h1) * h3).astype(BF16)
        ys_ref[...] = _dot(a, w2_ref[0])

    @pl.when(valid <= 0)
    def _():
        ys_ref[...] = jnp.zeros(ys_ref.shape, F32)


def grouped_ffn(xs, block_e, block_valid, w1, w3, w2, bm, riders=()):
    n_rows, d = xs.shape
    f = w1.shape[2]
    flat = lambda i, be, bv: i
    return pl.pallas_call(
        functools.partial(_ffn_kernel, riders=tuple(r.nblk for r in riders)),
        grid_spec=pltpu.PrefetchScalarGridSpec(
            num_scalar_prefetch=2,
            grid=(n_rows // bm,),
            in_specs=[
                pl.BlockSpec((bm, d), lambda i, be, bv: (i, 0)),
                pl.BlockSpec((1, d, f), lambda i, be, bv: (be[i], 0, 0)),
                pl.BlockSpec((1, d, f), lambda i, be, bv: (be[i], 0, 0)),
                pl.BlockSpec((1, f, d), lambda i, be, bv: (be[i], 0, 0)),
            ] + [r.in_spec(flat) for r in riders],
            out_specs=[pl.BlockSpec((bm, d), lambda i, be, bv: (i, 0))] + [r.out_spec(flat) for r in riders],
        ),
        out_shape=[jax.ShapeDtypeStruct((n_rows, d), F32)] + [r.out_shape for r in riders],
        compiler_params=_cparams(("arbitrary",)),
        name="moe_ffn",
    )(block_e, block_valid, xs, w1, w3, w2, *[r.src for r in riders])


def _combine_kernel(dest_ref, ys_ref, x_ref, mod_ref, gf_ref, *refs, tm, nt, emit_next):
    if emit_next:
        modn_ref, nwn_ref, o_ref, hn_ref, a_ref, b_ref, sem = refs
    else:
        o_ref, a_ref, b_ref, sem = refs
    i = pl.program_id(0)

    def fetch(tile, buf):
        base = tile * tm

        def start(r, carry):
            for slot, ref in ((0, a_ref), (1, b_ref)):
                d = dest_ref[2 * (base + r) + slot]
                pltpu.make_async_copy(ys_ref.at[pl.ds(d, 1), :], ref.at[buf, pl.ds(r, 1), :],
                                      sem.at[buf]).start()
            return carry

        lax.fori_loop(0, tm, start, 0, unroll=DMA_UNROLL)

    @pl.when(i == 0)
    def _():
        fetch(0, 0)

    @pl.when(i + 1 < nt)
    def _():
        fetch(i + 1, (i + 1) % 2)

    cur = i % 2
    pltpu.make_async_copy(ys_ref.at[pl.ds(0, tm), :], a_ref.at[cur], sem.at[cur]).wait()
    pltpu.make_async_copy(ys_ref.at[pl.ds(0, tm), :], b_ref.at[cur], sem.at[cur]).wait()
    y = gf_ref[:, 0:1] * a_ref[cur] + gf_ref[:, 1:2] * b_ref[cur]
    out = x_ref[...] + mod_ref[0, 2:3, :] * y
    o_ref[...] = out
    if emit_next:
        hn_ref[...] = _modulated_norm(out, modn_ref, nwn_ref).astype(BF16)


def combine(ys, dest, x2, mod, gates_t, seq, tm, next_norm=None):
    t, d = x2.shape
    per = seq // tm
    row = lambda i, dest: (i, 0)
    batch = lambda i, dest: (i // per, 0, 0)
    in_specs = [pl.BlockSpec(memory_space=pl.ANY), pl.BlockSpec((tm, d), row),
                pl.BlockSpec((1, 3, d), batch), pl.BlockSpec((tm, SUBLANE), row)]
    out_specs = [pl.BlockSpec((tm, d), row)]
    out_shape = [jax.ShapeDtypeStruct((t, d), F32)]
    args = [dest, ys, x2, mod, gates_t]
    if next_norm is not None:
        in_specs += [pl.BlockSpec((1, 3, d), batch), pl.BlockSpec((1, d), lambda i, dest: (0, 0))]
        out_specs += [pl.BlockSpec((tm, d), row)]
        out_shape += [jax.ShapeDtypeStruct((t, d), BF16)]
        args += [next_norm[0], next_norm[1].reshape(1, d)]
    return pl.pallas_call(
        functools.partial(_combine_kernel, tm=tm, nt=t // tm, emit_next=next_norm is not None),
        grid_spec=pltpu.PrefetchScalarGridSpec(
            num_scalar_prefetch=1,
            grid=(t // tm,),
            in_specs=in_specs,
            out_specs=out_specs,
            scratch_shapes=[pltpu.VMEM((2, tm, d), F32), pltpu.VMEM((2, tm, d), F32),
                            pltpu.SemaphoreType.DMA((2,))],
        ),
        out_shape=out_shape,
        compiler_params=_cparams(("arbitrary",)),
        name="moe_combine",
    )(*args)


def _moe_blocks(t):
    return (2 * t) // MOE_ROWS + N_EXPERTS


def moe(x2, h, ri, rf, cnt, mod, w1, w3, w2, seq, next_norm=None, riders=()):
    t, d = x2.shape
    bm = MOE_ROWS
    tm = min(256, seq)
    ne = N_EXPERTS
    counts = cnt[:, 0]
    nblk = (counts + bm - 1) // bm
    blk_end = jnp.cumsum(nblk)
    blk_start = blk_end - nblk
    eids = jnp.arange(ne, dtype=I32)
    e12 = ri[0:2].T
    row0 = jnp.sum(jnp.where(e12[..., None] == eids, blk_start * bm, 0), axis=-1)
    dest = (row0 + ri[2:4].T).reshape(2 * t).astype(I32)
    n_blocks = _moe_blocks(t)
    bidx = jnp.arange(n_blocks, dtype=I32)
    block_e = jnp.minimum(jnp.sum(bidx[:, None] >= blk_end[None, :], axis=1), ne - 1).astype(I32)
    onehot = block_e[:, None] == eids
    cnt_b = jnp.sum(jnp.where(onehot, counts, 0), axis=1)
    start_b = jnp.sum(jnp.where(onehot, blk_start, 0), axis=1)
    block_valid = jnp.clip(cnt_b - (bidx - start_b) * bm, 0, bm).astype(I32)
    pad_start = (blk_start * bm + counts).astype(I32)
    pad_end = jnp.where(eids == ne - 1, n_blocks * bm, blk_end * bm)
    xs = dispatch(h, dest, pad_start, (pad_end - pad_start).astype(I32), n_blocks * bm, tm)
    ys, *casts = grouped_ffn(xs, block_e, block_valid, w1, w3, w2, bm, riders)
    return combine(ys, dest, x2, mod, rf.T, seq, tm, next_norm), casts


def _t5_bucket(n):
    max_exact = N_BUCKETS // 2
    nf = jnp.maximum(n, 1).astype(F32)
    large = max_exact + (jnp.log(nf / max_exact) / math.log(MAX_DISTANCE / max_exact)
                         * (N_BUCKETS - max_exact)).astype(I32)
    large = jnp.minimum(large, N_BUCKETS - 1)
    return jnp.where(n < max_exact, n, large)


def _sortable(score):
    bits = pltpu.bitcast(score, I32)
    return bits ^ ((bits >> 31) & 0x7FFFFFFF)


def _dsa_kernel(q_ref, k_ref, v_ref, qi_ref, kw_ref, wq_ref, tb_ref, qn_ref, kn_ref, o_ref,
                khat_ref, vt_ref, qrt_ref, qirt_ref, wt_ref, mt_ref, hi_ref, lo_ref, lo2_ref, nm_ref, lg_ref,
                *acc_refs,
                nh, dh, nih, di, topk, nb):
    j = pl.program_id(1)
    qb = Q_BLOCK
    kc = 2 * qb
    nch = (j + 2) // 2
    pair = 2 * qb

    def transpose(x):
        return x.astype(F32).T

    @pl.when(j == 0)
    def _():
        k = k_ref[...].astype(F32)
        r = lax.rsqrt(jnp.mean(k * k, axis=-1, keepdims=True) + RMS_EPS)
        khat_ref[...] = (k * r * kn_ref[...]).astype(BF16)
        for c in range(nb // 2):
            for s in range(2):
                blk = v_ref[(2 * c + s) * qb:(2 * c + s + 1) * qb, :]
                vt_ref[c, 0:dh, s * qb:(s + 1) * qb] = transpose(blk).astype(BF16)
            vt_ref[c, dh:dh + ONES_ROWS, :] = jnp.ones((ONES_ROWS, kc), BF16)
        for ref in (hi_ref, lo_ref, lo2_ref):
            ref[...] = jnp.full(ref.shape, I16_MIN, I16)

    qscale = (dh ** -0.5) * math.log2(math.e)
    for h in range(nh):
        qh = q_ref[:, h * dh:(h + 1) * dh].astype(F32)
        r = lax.rsqrt(jnp.mean(qh * qh, axis=-1, keepdims=True) + RMS_EPS)
        qhat = (qh * r * qn_ref[...] * qscale).astype(BF16)
        qrt_ref[:, h * qb:(h + 1) * qb] = transpose(qhat).astype(BF16)
    for g in range(nih * di // LANE):
        two = transpose(qi_ref[:, g * LANE:(g + 1) * LANE])
        for s in range(LANE // di):
            h = g * (LANE // di) + s
            qirt_ref[:, h * qb:(h + 1) * qb] = two[s * di:(s + 1) * di, :].astype(BF16)
    wt_ref[...] = transpose(wq_ref[...])

    key_l = lax.broadcasted_iota(I32, (kc, qb), 0)
    q_pos = j * qb + lax.broadcasted_iota(I32, (kc, qb), 1)

    def score_body(c, carry):
        r0 = pl.multiple_of(c * kc, kc)
        ki = kw_ref[pl.ds(r0, kc), 0:di]
        acc = jnp.zeros((kc, qb), F32)
        for g in range(nih // 2):
            rel = _dot(ki, qirt_ref[:, g * pair:(g + 1) * pair])
            for s in range(2):
                h = 2 * g + s
                acc = acc + wt_ref[di + h:di + h + 1, :] * jnp.maximum(rel[:, s * qb:(s + 1) * qb], 0.0)
        m = jnp.where(r0 + key_l <= q_pos, _sortable(acc), INT_MIN)
        mt_ref[c] = m
        hi_ref[c] = (m >> 16).astype(I16)
        lo_ref[c] = ((m & 0xFFFF) + I16_MIN).astype(I16)
        return carry

    lax.fori_loop(0, nch, score_body, 0)

    def count_ge(cand):
        def cbody(c, acc):
            hit = jnp.where(mt_ref[c] >= cand, 1, 0)
            return acc + jnp.sum(hit.reshape(kc // SUBLANE, SUBLANE, qb), axis=0)
        acc = lax.fori_loop(0, nch, cbody, jnp.zeros((SUBLANE, qb), I32))
        return jnp.sum(acc, axis=0, keepdims=True)

    def fold16(hit):
        parts = [hit[PACK16 * i:PACK16 * (i + 1), :] for i in range(kc // PACK16)]
        while len(parts) > 1:
            parts = [parts[i] + parts[i + 1] for i in range(0, len(parts), 2)]
        return parts[0]

    def total16(acc):
        return jnp.sum(acc.astype(F32), axis=0, keepdims=True).astype(I32)

    one, nil = jnp.ones((), BF16), jnp.zeros((), BF16)

    def count16(src_ref, cand, n):
        c16 = cand.astype(I16)
        acc = fold16(jnp.where(src_ref[0] >= c16, one, nil))
        for c in range(1, n):
            acc = acc + fold16(jnp.where(src_ref[c] >= c16, one, nil))
        return total16(acc)

    def search16(src_ref, kth, n):
        zero = jnp.zeros((1, qb), I32)
        ans0 = jnp.where(count16(src_ref, zero, n) >= kth, zero, I16_MIN)

        def bit_body(bi, ans):
            cand = ans | (1 << (14 - bi))
            return jnp.where(count16(src_ref, cand, n) >= kth, cand, ans)

        return lax.fori_loop(0, 15, bit_body, ans0)

    def search(n):
        top = search16(hi_ref, jnp.full((1, qb), topk, I32), n)
        top16 = top.astype(I16)
        acc = jnp.zeros((PACK16, qb), BF16)
        for c in range(n):
            hi = hi_ref[c]
            lo2_ref[c] = jnp.where(hi == top16, lo_ref[c], jnp.full((), I16_MIN, I16))
            acc = acc + fold16(jnp.where(hi > top16, one, nil))
        above = total16(acc)
        low = search16(lo2_ref, topk - above, n)
        return top * 65536 + (low - I16_MIN), above + count16(lo2_ref, low, n)

    def search_any():
        return lax.cond(nch <= nb // 4, functools.partial(search, nb // 4),
                        functools.partial(search, nb // 2))

    selecting = (j + 1) * qb > topk
    thr, n_ge = lax.cond(selecting, search_any,
                         lambda: (jnp.full((1, qb), INT_MIN + 1, I32), jnp.zeros((1, qb), I32)))

    def plain_mask():
        def body(c, carry):
            nm_ref[c] = jnp.where(mt_ref[c] >= thr, 0.0, NEG_BIG)
            return carry
        lax.fori_loop(0, nch, body, 0)

    def tie_mask():
        need = (topk - count_ge(thr + 1)).astype(F32)
        tri = jnp.where(lax.broadcasted_iota(I32, (kc, kc), 1) <= lax.broadcasted_iota(I32, (kc, kc), 0),
                        1.0, 0.0).astype(BF16)

        def body(c, seen):
            m = mt_ref[c]
            tie = m == thr
            upto = _dot(tri, jnp.where(tie, 1.0, 0.0).astype(BF16)) + seen
            nm_ref[c] = jnp.where((m > thr) | (tie & (upto <= need)), 0.0, NEG_BIG)
            return upto[kc - 1:kc, :]
        lax.fori_loop(0, nch, body, jnp.zeros((1, qb), F32))

    crowded = jnp.max(n_ge) > topk
    lax.cond(crowded, tie_mask, plain_mask)

    npair = nh // 2
    pcols = [slice(pr * pair, (pr + 1) * pair) for pr in range(npair)]
    nfar = jnp.maximum((j - 1) // 2, 0)

    def sweep1(c, mruns, near):
        r0 = pl.multiple_of(c * kc, kc)
        kh = khat_ref[pl.ds(r0, kc), :]
        nm = nm_ref[c]
        nm2 = jnp.concatenate([nm, nm], axis=1)
        if near:
            k0 = jnp.clip(j - 2 * c, 0, 2)
            k1 = jnp.clip(j - 2 * c - 1, 0, 2)
        out = []
        for pr in range(npair):
            lg = _dot(kh, qrt_ref[:, pcols[pr]]) + nm2
            if near:
                lg = lg + jnp.concatenate([tb_ref[k0, :, pcols[pr]], tb_ref[k1, :, pcols[pr]]], axis=0)
            lg_ref[c, :, pcols[pr]] = lg
            out.append(jnp.maximum(mruns[pr], jnp.max(lg.reshape(kc // SUBLANE, SUBLANE, pair), axis=0)))
        return tuple(out)

    mruns = tuple(jnp.full((SUBLANE, pair), NEG_BIG, F32) for _ in range(npair))
    mruns = lax.fori_loop(0, nfar, functools.partial(sweep1, near=False), mruns)
    mruns = lax.fori_loop(nfar, nch, functools.partial(sweep1, near=True), mruns)
    mrows = [jnp.max(m, axis=0, keepdims=True) for m in mruns]
    for acc_ref in acc_refs:
        acc_ref[...] = jnp.zeros(acc_ref.shape, F32)

    def sweep2(c, carry):
        vt = vt_ref[c]
        for pr in range(npair):
            p = jnp.exp2((lg_ref[c, :, pcols[pr]] - mrows[pr]).astype(BF16))
            acc_refs[pr][...] += _dot(vt, p)
        return carry

    lax.fori_loop(0, nch, sweep2, 0)
    for pr in range(npair):
        out = acc_refs[pr][0:dh, :] / acc_refs[pr][dh:dh + 1, :]
        for s in range(2):
            h = 2 * pr + s
            o_ref[:, h * dh:(h + 1) * dh] = transpose(
                out[:, s * qb:(s + 1) * qb].astype(BF16)).astype(o_ref.dtype)


def dsa_attention(z, rel_bias, q_norm, k_norm, bsz, seq, d):
    t = z.shape[0]
    nh, dh, nih, di = ATT_HEADS, d // ATT_HEADS, IDX_HEADS, IDX_DIM
    qb = Q_BLOCK
    nb = seq // qb
    topk = min(DSA_TOPK_MAX, seq // 4)
    assert topk % qb == 0 and dh == LANE and nih * di == d // 2 and nb % 2 == 0 and LANE % di == 0
    assert (nb // 2) * (2 * qb // PACK16) <= 256
    i = jnp.arange(qb)
    dist = i[None, :] - i[:, None]
    def lookup(bucket):
        hot = bucket[..., None] == jnp.arange(N_BUCKETS, dtype=I32)
        return jnp.sum(jnp.where(hot[..., None], rel_bias, 0.0), axis=-2)

    tabs = [lookup(_t5_bucket(jnp.maximum(dist + off * qb, 0))) for off in range(2)]
    far = jnp.broadcast_to(lookup(_t5_bucket(jnp.full((), 2 * qb, I32))), (qb, qb, nh))
    tb = jnp.stack(tabs + [far]).transpose(0, 1, 3, 2).reshape(3, qb, nh * qb).astype(F32)
    tb = (tb - tb[2:3]) * math.log2(math.e)
    cq = d // LANE
    return pl.pallas_call(
        functools.partial(_dsa_kernel, nh=nh, dh=dh, nih=nih, di=di, topk=topk, nb=nb),
        grid=(bsz, nb),
        in_specs=[
            pl.BlockSpec((qb, d), lambda b, j: (b * nb + j, 0)),
            pl.BlockSpec((seq, LANE), lambda b, j: (b, cq + cq // 2)),
            pl.BlockSpec((seq, LANE), lambda b, j: (b, cq + cq // 2 + 1)),
            pl.BlockSpec((qb, d // 2), lambda b, j: (b * nb + j, 2)),
            pl.BlockSpec((seq, LANE), lambda b, j: (b, cq + cq // 2 + 2)),
            pl.BlockSpec((qb, LANE), lambda b, j: (b * nb + j, cq + cq // 2 + 2)),
            pl.BlockSpec((3, qb, nh * qb), lambda b, j: (0, 0, 0)),
            pl.BlockSpec((1, dh), lambda b, j: (0, 0)),
            pl.BlockSpec((1, dh), lambda b, j: (0, 0)),
        ],
        out_specs=pl.BlockSpec((qb, d), lambda b, j: (b * nb + j, 0)),
        out_shape=jax.ShapeDtypeStruct((t, d), BF16),
        scratch_shapes=[
            pltpu.VMEM((seq, dh), BF16),
            pltpu.VMEM((nb // 2, dh + ONES_ROWS, 2 * qb), BF16),
            pltpu.VMEM((dh, nh * qb), BF16),
            pltpu.VMEM((di, nih * qb), BF16),
            pltpu.VMEM((LANE, qb), F32),
            pltpu.VMEM((nb // 2, 2 * qb, qb), I32),
            pltpu.VMEM((nb // 2, 2 * qb, qb), I16),
            pltpu.VMEM((nb // 2, 2 * qb, qb), I16),
            pltpu.VMEM((nb // 2, 2 * qb, qb), I16),
            pltpu.VMEM((nb // 2, 2 * qb, qb), F32),
            pltpu.VMEM((nb // 2, 2 * qb, nh * qb), F32),
        ] + [pltpu.VMEM((dh + ONES_ROWS, 2 * qb), F32)
             for _ in range(nh // 2)],
        compiler_params=_cparams(("arbitrary", "arbitrary")),
        name="dsa_attention",
    )(z, z, z, z, z, z, tb, q_norm.reshape(1, dh), k_norm.reshape(1, dh))


def _dsa_weight(c_w_in, d):
    dh = d // ATT_HEADS
    nqi = IDX_HEADS * IDX_DIM
    q, k, v, qi, ki, wi = jnp.split(
        c_w_in, [d, d + dh, d + 2 * dh, d + 2 * dh + nqi, d + 2 * dh + nqi + IDX_DIM], axis=1)
    used = d + nqi + 2 * dh + IDX_DIM + IDX_HEADS
    total = -(-used // 512) * 512
    pad = jnp.zeros((d, total - used), c_w_in.dtype)
    return jnp.concatenate([q, qi, k, v, ki, wi, pad], axis=1).astype(BF16)


def kernel(x, c, rel_bias, w_router, b_router, norm_mix, ada_w_mix, ada_b_mix, norm_ffn, ada_w_ffn,
           ada_b_ffn, ab_w_in, ab_conv_w, ab_conv_b, ab_w_out, c_w_in, c_q_norm, c_k_norm, c_w_out,
           moe_w1, moe_w3, moe_w2):
    bsz, seq, d = x.shape
    depth = norm_mix.shape[0]
    t = bsz * seq
    ch = d // 2
    tm_out = min(512, seq)
    x2 = x.reshape(t, d)
    mod_mix = ada_mod(c, ada_w_mix, ada_b_mix).reshape(depth, bsz, 3, d)
    mod_ffn = ada_mod(c, ada_w_ffn, ada_b_ffn).reshape(depth, bsz, 3, d)
    experts = (moe_w1, moe_w3, moe_w2)
    as_experts = lambda casts: [cw.reshape(w.shape[1:]) for cw, w in zip(casts, experts)]
    tm_in = min(1024, seq)
    h_mix = None
    w_moe = None
    for layer in range(depth):
        i = layer // 2
        if layer % 2 == 0:
            w_in, tn = ab_w_in[i].astype(BF16), 1024
        else:
            w_in = _dsa_weight(c_w_in[i], d)
            tn = w_in.shape[1] // 2
        if h_mix is None:
            tn = min(tn, 1024)
            riders = None if w_moe is not None else _cast_riders(
                experts, layer, (t // tm_in) * (w_in.shape[1] // tn))
            z, *casts = inproj(x2, mod_mix[layer], norm_mix[layer], w_in, seq, tm_in, tn, riders or ())
            if riders:
                w_moe = as_experts(casts)
        else:
            z = proj(h_mix, w_in, tm_in, tn)
        if w_moe is None:
            w_moe = [w[layer].astype(BF16) for w in experts]
        if layer % 2 == 0:
            y_b = retention(z, bsz, seq, ch, 3 * ch)
            lhs, w_out, conv = (z, y_b), ab_w_out[i], (ab_conv_w[i], ab_conv_b[i])
        else:
            o = dsa_attention(z, rel_bias, c_q_norm[i], c_k_norm[i], bsz, seq, d)
            lhs, w_out, conv = (o,), c_w_out[i], None
        x2, h, ri, rf, cnt = mixer_out(lhs, w_out.astype(BF16), x2, mod_mix[layer], mod_ffn[layer],
                                       norm_ffn[layer], w_router, b_router, bsz, seq, tm_out, conv)
        last = layer + 1 == depth
        nxt = None if last else (mod_mix[layer + 1], norm_mix[layer + 1])
        riders = None if last else _cast_riders(experts, layer + 1, _moe_blocks(t))
        res, casts = moe(x2, h, ri, rf, cnt, mod_ffn[layer], *w_moe, seq, nxt, riders or ())
        x2, h_mix = (res[0], None) if last else (res[0], res[1])
        w_moe = as_experts(casts) if riders else None
    return x2.reshape(bsz, seq, d)
```

```python
import functools
import math
from typing import NamedTuple

import jax
import jax.numpy as jnp
from jax import lax
from jax.experimental import pallas as pl
from jax.experimental.pallas import tpu as pltpu

F32 = jnp.float32
BF16 = jnp.bfloat16
I32 = jnp.int32
I16 = jnp.int16

RMS_EPS = 1e-6
CONV_WIDTH = 3
RET_HEADS = 4
ATT_HEADS = 16
IDX_HEADS = 16
IDX_DIM = 64
DSA_TOPK_MAX = 256
Q_BLOCK = 128
N_BUCKETS = 32
MAX_DISTANCE = 128
N_EXPERTS = 16
N_GROUPS = 4
EXPERTS_PER_GROUP = N_EXPERTS // N_GROUPS
RET_CHUNK = 256
MOE_ROWS = 256
DMA_UNROLL = 8
RIDER_BLOCK_BYTES = 1024 * 1024
ONES_ROWS = 16
LANE = 128
SUBLANE = 8
VMEM_LIMIT = 48 * 1024 * 1024
VMEM_LIMIT_LARGE = 56 * 1024 * 1024
NEG_BIG = -1e30
INT_MIN = -(2 ** 31)
I16_MIN = -(2 ** 15)
PACK16 = 16


def _cparams(sem, vmem=VMEM_LIMIT):
    return pltpu.CompilerParams(dimension_semantics=sem, vmem_limit_bytes=vmem)


def _dot(a, b):
    return jnp.dot(a, b, preferred_element_type=F32)


def _dot_nt(a, b):
    return lax.dot_general(a, b, (((1,), (1,)), ((), ())), preferred_element_type=F32)


def _dot_tn(a, b):
    return lax.dot_general(a, b, (((0,), (0,)), ((), ())), preferred_element_type=F32)


def _silu(x):
    return x * jax.nn.sigmoid(x)


class _CastRider(NamedTuple):
    src: jax.Array
    rows: int
    first_blk: int
    nblk: int

    def in_spec(self, flat):
        return pl.BlockSpec((self.rows, self.src.shape[1]),
                            lambda *g: (jnp.minimum(flat(*g), self.nblk - 1) + self.first_blk, 0))

    def out_spec(self, flat):
        return pl.BlockSpec((self.rows, self.src.shape[1]),
                            lambda *g: (jnp.minimum(flat(*g), self.nblk - 1), 0))

    @property
    def out_shape(self):
        return jax.ShapeDtypeStruct((self.rows * self.nblk, self.src.shape[1]), BF16)


def _cast_riders(weights, layer, n_steps):
    riders = []
    for w in weights:
        per_layer, cols = w.shape[1] * w.shape[2], w.shape[3]
        rows = PACK16
        while rows * n_steps < per_layer:
            rows *= 2
        if per_layer % rows or rows * cols * 4 > RIDER_BLOCK_BYTES:
            return None
        nblk = per_layer // rows
        riders.append(_CastRider(w.reshape(-1, cols), rows, layer * nblk, nblk))
    return riders


def _ride_casts(nblks, src_refs, dst_refs, step):
    for nblk, src, dst in zip(nblks, src_refs, dst_refs):
        @pl.when(step < nblk)
        def _():
            dst[...] = src[...].astype(BF16)


def _ada_kernel(c_ref, w_ref, b_ref, o_ref):
    cs = _silu(c_ref[...])
    o_ref[0] = _dot(cs.astype(BF16), w_ref[0].astype(BF16)) + b_ref[0]


def ada_mod(c, ada_w, ada_b):
    nl, d, n3 = ada_w.shape
    bsz = c.shape[0]
    tn = min(512, n3)
    return pl.pallas_call(
        _ada_kernel,
        grid=(nl, n3 // tn),
        in_specs=[
            pl.BlockSpec((bsz, d), lambda l, j: (0, 0)),
            pl.BlockSpec((1, d, tn), lambda l, j: (l, 0, j)),
            pl.BlockSpec((1, 1, tn), lambda l, j: (l, 0, j)),
        ],
        out_specs=pl.BlockSpec((1, bsz, tn), lambda l, j: (l, 0, j)),
        out_shape=jax.ShapeDtypeStruct((nl, bsz, n3), F32),
        compiler_params=_cparams(("parallel", "parallel")),
        name="ada_mod",
    )(c, ada_w, ada_b.reshape(nl, 1, n3))


def _modulated_norm(x, mod_ref, nw_ref):
    r = lax.rsqrt(jnp.mean(x * x, axis=-1, keepdims=True) + RMS_EPS)
    return x * r * nw_ref[...] * (1.0 + mod_ref[0, 1:2, :]) + mod_ref[0, 0:1, :]


def _inproj_kernel(x_ref, mod_ref, nw_ref, w_ref, *refs, riders):
    o_ref, h_ref = refs[len(riders)], refs[-1]
    step = pl.program_id(0) * pl.num_programs(1) + pl.program_id(1)
    _ride_casts(riders, refs[:len(riders)], refs[len(riders) + 1:-1], step)

    @pl.when(pl.program_id(1) == 0)
    def _():
        h_ref[...] = _modulated_norm(x_ref[...], mod_ref, nw_ref).astype(BF16)

    o_ref[...] = _dot(h_ref[...], w_ref[...]).astype(o_ref.dtype)


def inproj(x2, mod, norm_w, w_bf, seq, tm, tn, riders=()):
    t, d = x2.shape
    n = w_bf.shape[1]
    per = seq // tm
    nj = n // tn
    flat = lambda i, j: i * nj + j
    return pl.pallas_call(
        functools.partial(_inproj_kernel, riders=tuple(r.nblk for r in riders)),
        grid=(t // tm, nj),
        in_specs=[
            pl.BlockSpec((tm, d), lambda i, j: (i, 0)),
            pl.BlockSpec((1, 3, d), lambda i, j: (i // per, 0, 0)),
            pl.BlockSpec((1, d), lambda i, j: (0, 0)),
            pl.BlockSpec((d, tn), lambda i, j: (0, j)),
        ] + [r.in_spec(flat) for r in riders],
        out_specs=[pl.BlockSpec((tm, tn), lambda i, j: (i, j))] + [r.out_spec(flat) for r in riders],
        out_shape=[jax.ShapeDtypeStruct((t, n), BF16)] + [r.out_shape for r in riders],
        scratch_shapes=[pltpu.VMEM((tm, d), BF16)],
        compiler_params=_cparams(("arbitrary", "arbitrary"), VMEM_LIMIT_LARGE if riders else VMEM_LIMIT),
        name="inproj",
    )(x2, mod, norm_w.reshape(1, d), w_bf, *[r.src for r in riders])


def _gated_conv(b_ref, c_ref, v_ref, w_ref, cb_ref, u_ref, s, tm):
    u = c_ref[...].astype(F32) * v_ref[...].astype(F32)

    @pl.when(s == 0)
    def _():
        u_ref[0:SUBLANE, :] = jnp.zeros((SUBLANE, u.shape[1]), F32)

    @pl.when(s > 0)
    def _():
        u_ref[0:SUBLANE, :] = u_ref[tm:tm + SUBLANE, :]

    u_ref[SUBLANE:SUBLANE + tm, :] = u
    conv = (cb_ref[...]
            + u_ref[SUBLANE - 2:SUBLANE - 2 + tm, :] * w_ref[0:1, :]
            + u_ref[SUBLANE - 1:SUBLANE - 1 + tm, :] * w_ref[1:2, :]
            + u * w_ref[2:3, :])
    return b_ref[...].astype(F32) * conv


def _retention_kernel(q_ref, k_ref, v_ref, g_ref, cos_ref, sin_ref, din_ref, dcr_ref, dst_ref,
                      o_ref, *st_refs, nc, ck, dk):
    half = dk // 2
    for st_ref in st_refs:
        st_ref[...] = jnp.zeros(st_ref.shape, F32)

    def rot(x, cos, sin):
        x1, x2 = x[:, :half], x[:, half:]
        return jnp.concatenate([x1 * cos - x2 * sin, x1 * sin + x2 * cos], axis=-1)

    def body(c, carry):
        r0 = pl.multiple_of(c * ck, ck)
        cos = cos_ref[pl.ds(r0, ck), :]
        sin = sin_ref[pl.ds(r0, ck), :]
        for h, st_ref in enumerate(st_refs):
            cols = slice(h * dk, (h + 1) * dk)
            d_cross = dcr_ref[h]
            g_chunk = dcr_ref[h, ck - 1:ck, :]
            q = rot(q_ref[pl.ds(r0, ck), cols].astype(F32), cos, sin)
            k = rot(k_ref[pl.ds(r0, ck), cols].astype(F32), cos, sin) * (dk ** -0.5)
            v = v_ref[pl.ds(r0, ck), cols]
            intra = _dot_nt(q.astype(BF16), k.astype(BF16)) * din_ref[h]
            state = st_ref[...]
            o = _dot(intra.astype(BF16), v) + _dot((q * d_cross).astype(BF16), state.astype(BF16))
            st_ref[...] = state * g_chunk + _dot_tn((k * dst_ref[h]).astype(BF16), v)
            r = lax.rsqrt(jnp.mean(o * o, axis=-1, keepdims=True) + RMS_EPS)
            gate = _silu(g_ref[pl.ds(r0, ck), cols].astype(F32))
            o_ref[pl.ds(r0, ck), cols] = (o * r * gate).astype(o_ref.dtype)
        return carry

    lax.fori_loop(0, nc, body, 0)


def retention(z, bsz, seq, ch, col0):
    t = z.shape[0]
    nh = RET_HEADS
    dk = ch // nh
    ck = min(RET_CHUNK, seq)
    nc = seq // ck
    half = dk // 2
    pos = jnp.arange(seq, dtype=F32)
    inv = 1.0 / (10000.0 ** jnp.linspace(0.0, 1.0, half, dtype=F32))
    ang = pos[:, None] * inv[None, :]
    cos, sin = jnp.cos(ang), jnp.sin(ang)
    log_g = jnp.log(1.0 - 2.0 ** (-5.0 - jnp.arange(nh, dtype=F32)))
    i = jnp.arange(ck, dtype=F32)
    diff = i[:, None] - i[None, :]
    d_intra = jnp.where(diff >= 0, jnp.exp(log_g[:, None, None] * jnp.maximum(diff, 0.0)), 0.0)
    d_cross = jnp.exp(log_g[:, None] * (i[None, :] + 1.0))[..., None]
    d_state = jnp.exp(log_g[:, None] * (ck - 1.0 - i[None, :]))[..., None]
    cb = col0 // ch
    col = lambda g: (lambda b: (b, cb + g))
    whole = lambda b: (0, 0, 0)
    return pl.pallas_call(
        functools.partial(_retention_kernel, nc=nc, ck=ck, dk=dk),
        grid=(bsz,),
        in_specs=[
            pl.BlockSpec((seq, ch), col(0)),
            pl.BlockSpec((seq, ch), col(1)),
            pl.BlockSpec((seq, ch), col(2)),
            pl.BlockSpec((seq, ch), col(3)),
            pl.BlockSpec((seq, half), lambda b: (0, 0)),
            pl.BlockSpec((seq, half), lambda b: (0, 0)),
            pl.BlockSpec((nh, ck, ck), whole),
            pl.BlockSpec((nh, ck, 1), whole),
            pl.BlockSpec((nh, ck, 1), whole),
        ],
        out_specs=pl.BlockSpec((seq, ch), lambda b: (b, 0)),
        out_shape=jax.ShapeDtypeStruct((t, ch), BF16),
        scratch_shapes=[pltpu.VMEM((dk, dk), F32) for _ in range(nh)],
        compiler_params=_cparams(("parallel",)),
        name="retention",
    )(z, z, z, z, cos, sin, d_intra, d_cross, d_state)


def _route(h, wr_ref, br_ref, ri_ref, rf_ref, cnt_ref, carry_ref, tm):
    logits = _dot_nt(wr_ref[...], h.astype(BF16))
    mx = jnp.max(logits, axis=0, keepdims=True)
    ex = jnp.exp(logits - mx)
    probs = ex / jnp.sum(ex, axis=0, keepdims=True)
    sel = probs + br_ref[...]
    s = [sel[e:e + 1, :] for e in range(N_EXPERTS)]
    p = [probs[e:e + 1, :] for e in range(N_EXPERTS)]
    epg = EXPERTS_PER_GROUP

    def first_argmax(vals, exclude=None):
        best = jnp.full_like(vals[0], -jnp.inf)
        idx = jnp.zeros(vals[0].shape, I32)
        for j, vj in enumerate(vals):
            better = vj > best
            if exclude is not None:
                better = better & (exclude != j)
            idx = jnp.where(better, j, idx)
            best = jnp.where(better, vj, best)
        return idx

    gscore = []
    for g in range(N_GROUPS):
        gs = s[g * epg:(g + 1) * epg]
        best = None
        for a in range(epg):
            for b in range(a + 1, epg):
                pair = gs[a] + gs[b]
                best = pair if best is None else jnp.maximum(best, pair)
        gscore.append(best)
    grp = first_argmax(gscore)

    def pick(rows, index, n):
        out = rows[n - 1]
        for j in range(n - 2, -1, -1):
            out = jnp.where(index == j, rows[j], out)
        return out

    in_s = [pick([s[g * epg + j] for g in range(N_GROUPS)], grp, N_GROUPS) for j in range(epg)]
    in_p = [pick([p[g * epg + j] for g in range(N_GROUPS)], grp, N_GROUPS) for j in range(epg)]
    i1 = first_argmax(in_s)
    i2 = first_argmax(in_s, exclude=i1)
    p1 = pick(in_p, i1, epg)
    p2 = pick(in_p, i2, epg)
    e1 = grp * epg + i1
    e2 = grp * epg + i2
    den = p1 + p2
    g1 = p1 / den
    g2 = p2 / den

    eidx = lax.broadcasted_iota(I32, (N_EXPERTS, tm), 0)
    member = (eidx == e1) | (eidx == e2)
    member_f = jnp.where(member, 1.0, 0.0)
    before = lax.broadcasted_iota(I32, (tm, tm), 0) < lax.broadcasted_iota(I32, (tm, tm), 1)
    prefix = _dot(member_f.astype(BF16), jnp.where(before, 1.0, 0.0).astype(BF16))
    base = prefix + carry_ref[:, 0:1]
    rank1 = jnp.sum(jnp.where(eidx == e1, base, 0.0), axis=0, keepdims=True).astype(I32)
    rank2 = jnp.sum(jnp.where(eidx == e2, base, 0.0), axis=0, keepdims=True).astype(I32)
    carry_ref[...] = carry_ref[...] + jnp.sum(member_f, axis=1, keepdims=True)
    cnt_ref[...] = carry_ref[...].astype(I32)

    zi = jnp.zeros((SUBLANE - 4, tm), I32)
    ri_ref[...] = jnp.concatenate([e1, e2, rank1, rank2, zi], axis=0)
    zf = jnp.zeros((SUBLANE - 2, tm), F32)
    rf_ref[...] = jnp.concatenate([g1, g2, zf], axis=0)


def _mixer_out_kernel(*refs, tm, conv):
    if conv:
        b_ref, c_ref, v_ref, cw_ref, cb_ref, *refs = refs
    else:
        ya_ref, *refs = refs
    (yb_ref, wa_ref, wb_ref, x_ref, modm_ref, modf_ref, nw_ref, wr_ref, br_ref,
     x1_ref, h_ref, ri_ref, rf_ref, cnt_ref, carry_ref, *conv_scratch) = refs
    s = pl.program_id(1)

    @pl.when((pl.program_id(0) == 0) & (s == 0))
    def _():
        carry_ref[...] = jnp.zeros(carry_ref.shape, F32)

    if conv:
        ya = _gated_conv(b_ref, c_ref, v_ref, cw_ref, cb_ref, conv_scratch[0], s, tm).astype(BF16)
    else:
        ya = ya_ref[...]
    y = _dot(ya, wa_ref[...]) + _dot(yb_ref[...], wb_ref[...])
    x1 = x_ref[...] + modm_ref[0, 2:3, :] * y
    x1_ref[...] = x1
    h = _modulated_norm(x1, modf_ref, nw_ref)
    h_ref[...] = h
    _route(h, wr_ref, br_ref, ri_ref, rf_ref, cnt_ref, carry_ref, tm)


def mixer_out(lhs, w_bf, x2, mod_mix, mod_ffn, norm_ffn_w, w_router, b_router, bsz, seq, tm, conv=None):
    t, d = x2.shape
    kh = w_bf.shape[0] // 2
    per = seq // tm
    ne = N_EXPERTS
    row = lambda g: (lambda b, s: (b * per + s, g))
    const = lambda b, s: (0, 0)
    if conv is None:
        (o,) = lhs
        lhs_args = (o, o)
        lhs_specs = [pl.BlockSpec((tm, kh), row(0)), pl.BlockSpec((tm, kh), row(1))]
        scratch = []
    else:
        z, yb = lhs
        conv_w, conv_b = conv
        lhs_args = (z, z, z, conv_w, conv_b.reshape(1, kh), yb)
        lhs_specs = [pl.BlockSpec((tm, kh), row(0)), pl.BlockSpec((tm, kh), row(1)),
                     pl.BlockSpec((tm, kh), row(2)), pl.BlockSpec((CONV_WIDTH, kh), const),
                     pl.BlockSpec((1, kh), const), pl.BlockSpec((tm, kh), row(0))]
        scratch = [pltpu.VMEM((tm + SUBLANE, kh), F32)]
    batch = lambda b, s: (b, 0, 0)
    return pl.pallas_call(
        functools.partial(_mixer_out_kernel, tm=tm, conv=conv is not None),
        grid=(bsz, per),
        in_specs=lhs_specs + [
            pl.BlockSpec((kh, d), const),
            pl.BlockSpec((kh, d), lambda b, s: (1, 0)),
            pl.BlockSpec((tm, d), row(0)),
            pl.BlockSpec((1, 3, d), batch),
            pl.BlockSpec((1, 3, d), batch),
            pl.BlockSpec((1, d), const),
            pl.BlockSpec((ne, d), const),
            pl.BlockSpec((ne, 1), const),
        ],
        out_specs=[
            pl.BlockSpec((tm, d), row(0)),
            pl.BlockSpec((tm, d), row(0)),
            pl.BlockSpec((SUBLANE, tm), lambda b, s: (0, b * per + s)),
            pl.BlockSpec((SUBLANE, tm), lambda b, s: (0, b * per + s)),
            pl.BlockSpec((ne, LANE), const),
        ],
        out_shape=[
            jax.ShapeDtypeStruct((t, d), F32),
            jax.ShapeDtypeStruct((t, d), F32),
            jax.ShapeDtypeStruct((SUBLANE, t), I32),
            jax.ShapeDtypeStruct((SUBLANE, t), F32),
            jax.ShapeDtypeStruct((ne, LANE), I32),
        ],
        scratch_shapes=[pltpu.VMEM((ne, LANE), F32)] + scratch,
        compiler_params=_cparams(("arbitrary", "arbitrary")),
        name="mixer_out",
    )(*lhs_args, w_bf, w_bf, x2, mod_mix, mod_ffn, norm_ffn_w.reshape(1, d),
      w_router.T.astype(BF16), b_router.reshape(ne, 1))


def _dispatch_kernel(dest_ref, pad0_ref, padn_ref, h_ref, xs_ref, hbuf_ref, zero_ref, sem, zsem,
                     *, tm, nt):
    i = pl.program_id(0)
    cur = i % 2
    base = i * tm

    def zero_row(e, r):
        return pltpu.make_async_copy(zero_ref.at[pl.ds(0, 1), :],
                                     xs_ref.at[pl.ds(pad0_ref[e] + r, 1), :], zsem)

    def tile_done(buf):
        for _ in range(2):
            pltpu.make_async_copy(hbuf_ref.at[buf], xs_ref.at[pl.ds(0, tm), :], sem.at[buf]).wait()

    @pl.when(i == 0)
    def _():
        zero_ref[...] = jnp.zeros(zero_ref.shape, F32)
        for e in range(N_EXPERTS):
            lax.fori_loop(0, padn_ref[e], lambda r, c, e=e: (zero_row(e, r).start(), c)[1], 0)

    hbuf_ref[cur] = h_ref[...]

    def start(r, carry):
        for slot in range(2):
            d = dest_ref[2 * (base + r) + slot]
            pltpu.make_async_copy(hbuf_ref.at[cur, pl.ds(r, 1), :], xs_ref.at[pl.ds(d, 1), :],
                                  sem.at[cur]).start()
        return carry

    lax.fori_loop(0, tm, start, 0, unroll=DMA_UNROLL)

    @pl.when(i > 0)
    def _():
        tile_done(1 - cur)

    @pl.when(i == nt - 1)
    def _():
        tile_done(cur)
        for e in range(N_EXPERTS):
            lax.fori_loop(0, padn_ref[e], lambda r, c, e=e: (zero_row(e, r).wait(), c)[1], 0)


def dispatch(h, dest, pad_start, pad_count, n_rows, tm):
    t, d = h.shape
    return pl.pallas_call(
        functools.partial(_dispatch_kernel, tm=tm, nt=t // tm),
        grid_spec=pltpu.PrefetchScalarGridSpec(
            num_scalar_prefetch=3,
            grid=(t // tm,),
            in_specs=[pl.BlockSpec((tm, d), lambda i, *_: (i, 0))],
            out_specs=pl.BlockSpec(memory_space=pl.ANY),
            scratch_shapes=[pltpu.VMEM((2, tm, d), F32), pltpu.VMEM((SUBLANE, d), F32),
                            pltpu.SemaphoreType.DMA((2,)), pltpu.SemaphoreType.DMA(())],
        ),
        out_shape=jax.ShapeDtypeStruct((n_rows, d), F32),
        compiler_params=_cparams(("arbitrary",)),
        name="moe_dispatch",
    )(dest, pad_start, pad_count, h)


def _ffn_kernel(be_ref, bv_ref, xs_ref, w1_ref, w3_ref, w2_ref, *refs, riders):
    ys_ref = refs[len(riders)]
    _ride_casts(riders, refs[:len(riders)], refs[len(riders) + 1:], pl.program_id(0))
    valid = bv_ref[pl.program_id(0)]

    @pl.when(valid > 0)
    def _():
        rows = lax.broadcasted_iota(I32, xs_ref.shape, 0)
        x = jnp.where(rows < valid, xs_ref[...], 0.0).astype(BF16)
        h1 = _dot(x, w1_ref[0])
        h3 = _dot(x, w3_ref[0])
        a = (_silu(h1) * h3).astype(BF16)
        ys_ref[...] = _dot(a, w2_ref[0])

    @pl.when(valid <= 0)
    def _():
        ys_ref[...] = jnp.zeros(ys_ref.shape, F32)


def grouped_ffn(xs, block_e, block_valid, w1, w3, w2, bm, riders=()):
    n_rows, d = xs.shape
    f = w1.shape[2]
    flat = lambda i, be, bv: i
    return pl.pallas_call(
        functools.partial(_ffn_kernel, riders=tuple(r.nblk for r in riders)),
        grid_spec=pltpu.PrefetchScalarGridSpec(
            num_scalar_prefetch=2,
            grid=(n_rows // bm,),
            in_specs=[
                pl.BlockSpec((bm, d), lambda i, be, bv: (i, 0)),
                pl.BlockSpec((1, d, f), lambda i, be, bv: (be[i], 0, 0)),
                pl.BlockSpec((1, d, f), lambda i, be, bv: (be[i], 0, 0)),
                pl.BlockSpec((1, f, d), lambda i, be, bv: (be[i], 0, 0)),
            ] + [r.in_spec(flat) for r in riders],
            out_specs=[pl.BlockSpec((bm, d), lambda i, be, bv: (i, 0))] + [r.out_spec(flat) for r in riders],
        ),
        out_shape=[jax.ShapeDtypeStruct((n_rows, d), F32)] + [r.out_shape for r in riders],
        compiler_params=_cparams(("arbitrary",)),
        name="moe_ffn",
    )(block_e, block_valid, xs, w1, w3, w2, *[r.src for r in riders])


def _combine_kernel(dest_ref, ys_ref, x_ref, mod_ref, gf_ref, *refs, tm, nt, project):
    if project:
        modn_ref, nwn_ref, w_ref, o_ref, z_ref, a_ref, b_ref, sem = refs
    else:
        o_ref, a_ref, b_ref, sem = refs
    i = pl.program_id(0)
    cur = i % 2

    def fetch_row(base, r, buf):
        for slot, ref in ((0, a_ref), (1, b_ref)):
            d = dest_ref[2 * (base + r) + slot]
            pltpu.make_async_copy(ys_ref.at[pl.ds(d, 1), :], ref.at[buf, pl.ds(r, 1), :],
                                  sem.at[buf]).start()

    def fetch(tile, buf):
        lax.fori_loop(0, tm, lambda r, c: (fetch_row(tile * tm, r, buf), c)[1], 0, unroll=DMA_UNROLL)

    def fetched(buf):
        pltpu.make_async_copy(ys_ref.at[pl.ds(0, tm), :], a_ref.at[buf], sem.at[buf]).wait()
        pltpu.make_async_copy(ys_ref.at[pl.ds(0, tm), :], b_ref.at[buf], sem.at[buf]).wait()

    @pl.when(i == 0)
    def _():
        fetch(0, 0)

    if not project:
        @pl.when(i + 1 < nt)
        def _():
            fetch(i + 1, 1 - cur)

    fetched(cur)
    y = gf_ref[:, 0:1] * a_ref[cur] + gf_ref[:, 1:2] * b_ref[cur]
    out = x_ref[...] + mod_ref[0, 2:3, :] * y
    o_ref[...] = out
    if project:
        h = _modulated_norm(out, modn_ref, nwn_ref).astype(BF16)
        base = jnp.minimum(i + 1, nt - 1) * tm
        for r in range(tm):
            fetch_row(base, r, 1 - cur)
        z_ref[...] = _dot(h, w_ref[...]).astype(z_ref.dtype)

        @pl.when(i == nt - 1)
        def _():
            fetched(1 - cur)


def combine(ys, dest, x2, mod, gates_t, seq, tm, project=None):
    t, d = x2.shape
    per = seq // tm
    row = lambda i, dest: (i, 0)
    batch = lambda i, dest: (i // per, 0, 0)
    const = lambda i, dest: (0, 0)
    in_specs = [pl.BlockSpec(memory_space=pl.ANY), pl.BlockSpec((tm, d), row),
                pl.BlockSpec((1, 3, d), batch), pl.BlockSpec((tm, SUBLANE), row)]
    out_specs = [pl.BlockSpec((tm, d), row)]
    out_shape = [jax.ShapeDtypeStruct((t, d), F32)]
    args = [dest, ys, x2, mod, gates_t]
    if project is not None:
        mod_next, nw_next, w_bf = project
        n = w_bf.shape[1]
        in_specs += [pl.BlockSpec((1, 3, d), batch), pl.BlockSpec((1, d), const), pl.BlockSpec((d, n), const)]
        out_specs += [pl.BlockSpec((tm, n), row)]
        out_shape += [jax.ShapeDtypeStruct((t, n), BF16)]
        args += [mod_next, nw_next.reshape(1, d), w_bf]
    return pl.pallas_call(
        functools.partial(_combine_kernel, tm=tm, nt=t // tm, project=project is not None),
        grid_spec=pltpu.PrefetchScalarGridSpec(
            num_scalar_prefetch=1,
            grid=(t // tm,),
            in_specs=in_specs,
            out_specs=out_specs,
            scratch_shapes=[pltpu.VMEM((2, tm, d), F32), pltpu.VMEM((2, tm, d), F32),
                            pltpu.SemaphoreType.DMA((2,))],
        ),
        out_shape=out_shape,
        compiler_params=_cparams(("arbitrary",)),
        name="moe_combine",
    )(*args)


def _moe_blocks(t):
    return (2 * t) // MOE_ROWS + N_EXPERTS


def moe(x2, h, ri, rf, cnt, mod, w1, w3, w2, seq, project=None, riders=()):
    t, d = x2.shape
    bm = MOE_ROWS
    tm = min(256, seq)
    ne = N_EXPERTS
    counts = cnt[:, 0]
    nblk = (counts + bm - 1) // bm
    blk_end = jnp.cumsum(nblk)
    blk_start = blk_end - nblk
    eids = jnp.arange(ne, dtype=I32)
    e12 = ri[0:2].T
    row0 = jnp.sum(jnp.where(e12[..., None] == eids, blk_start * bm, 0), axis=-1)
    dest = (row0 + ri[2:4].T).reshape(2 * t).astype(I32)
    n_blocks = _moe_blocks(t)
    bidx = jnp.arange(n_blocks, dtype=I32)
    block_e = jnp.minimum(jnp.sum(bidx[:, None] >= blk_end[None, :], axis=1), ne - 1).astype(I32)
    onehot = block_e[:, None] == eids
    cnt_b = jnp.sum(jnp.where(onehot, counts, 0), axis=1)
    start_b = jnp.sum(jnp.where(onehot, blk_start, 0), axis=1)
    block_valid = jnp.clip(cnt_b - (bidx - start_b) * bm, 0, bm).astype(I32)
    pad_start = (blk_start * bm + counts).astype(I32)
    pad_end = jnp.where(eids == ne - 1, n_blocks * bm, blk_end * bm)
    xs = dispatch(h, dest, pad_start, (pad_end - pad_start).astype(I32), n_blocks * bm, tm)
    ys, *casts = grouped_ffn(xs, block_e, block_valid, w1, w3, w2, bm, riders)
    return combine(ys, dest, x2, mod, rf.T, seq, tm, project), casts


def _t5_bucket(n):
    max_exact = N_BUCKETS // 2
    nf = jnp.maximum(n, 1).astype(F32)
    large = max_exact + (jnp.log(nf / max_exact) / math.log(MAX_DISTANCE / max_exact)
                         * (N_BUCKETS - max_exact)).astype(I32)
    large = jnp.minimum(large, N_BUCKETS - 1)
    return jnp.where(n < max_exact, n, large)


def _sortable(score):
    bits = pltpu.bitcast(score, I32)
    return bits ^ ((bits >> 31) & 0x7FFFFFFF)


def _dsa_kernel(q_ref, k_ref, v_ref, qi_ref, kw_ref, wq_ref, tb_ref, qn_ref, kn_ref, o_ref,
                khat_ref, vt_ref, qrt_ref, qirt_ref, wt_ref, mt_ref, hi_ref, lo_ref, lo2_ref, nm_ref, lg_ref,
                *acc_refs,
                nh, dh, nih, di, topk, nb):
    j = pl.program_id(1)
    qb = Q_BLOCK
    kc = 2 * qb
    nch = (j + 2) // 2
    pair = 2 * qb

    def transpose(x):
        return x.astype(F32).T

    @pl.when(j == 0)
    def _():
        k = k_ref[...].astype(F32)
        r = lax.rsqrt(jnp.mean(k * k, axis=-1, keepdims=True) + RMS_EPS)
        khat_ref[...] = (k * r * kn_ref[...]).astype(BF16)
        for c in range(nb // 2):
            for s in range(2):
                blk = v_ref[(2 * c + s) * qb:(2 * c + s + 1) * qb, :]
                vt_ref[c, 0:dh, s * qb:(s + 1) * qb] = transpose(blk).astype(BF16)
            vt_ref[c, dh:dh + ONES_ROWS, :] = jnp.ones((ONES_ROWS, kc), BF16)
        for ref in (hi_ref, lo_ref, lo2_ref):
            ref[...] = jnp.full(ref.shape, I16_MIN, I16)

    qscale = (dh ** -0.5) * math.log2(math.e)
    for h in range(nh):
        qh = q_ref[:, h * dh:(h + 1) * dh].astype(F32)
        r = lax.rsqrt(jnp.mean(qh * qh, axis=-1, keepdims=True) + RMS_EPS)
        qhat = (qh * r * qn_ref[...] * qscale).astype(BF16)
        qrt_ref[:, h * qb:(h + 1) * qb] = transpose(qhat).astype(BF16)
    for g in range(nih * di // LANE):
        two = transpose(qi_ref[:, g * LANE:(g + 1) * LANE])
        for s in range(LANE // di):
            h = g * (LANE // di) + s
            qirt_ref[:, h * qb:(h + 1) * qb] = two[s * di:(s + 1) * di, :].astype(BF16)
    wt_ref[...] = transpose(wq_ref[...])

    key_l = lax.broadcasted_iota(I32, (kc, qb), 0)
    q_pos = j * qb + lax.broadcasted_iota(I32, (kc, qb), 1)

    def score_body(c, carry):
        r0 = pl.multiple_of(c * kc, kc)
        ki = kw_ref[pl.ds(r0, kc), 0:di]
        acc = jnp.zeros((kc, qb), F32)
        for g in range(nih // 2):
            rel = _dot(ki, qirt_ref[:, g * pair:(g + 1) * pair])
            for s in range(2):
                h = 2 * g + s
                acc = acc + wt_ref[di + h:di + h + 1, :] * jnp.maximum(rel[:, s * qb:(s + 1) * qb], 0.0)
        m = jnp.where(r0 + key_l <= q_pos, _sortable(acc), INT_MIN)
        mt_ref[c] = m
        hi_ref[c] = (m >> 16).astype(I16)
        lo_ref[c] = ((m & 0xFFFF) + I16_MIN).astype(I16)
        return carry

    lax.fori_loop(0, nch, score_body, 0)

    def count_ge(cand):
        def cbody(c, acc):
            hit = jnp.where(mt_ref[c] >= cand, 1, 0)
            return acc + jnp.sum(hit.reshape(kc // SUBLANE, SUBLANE, qb), axis=0)
        acc = lax.fori_loop(0, nch, cbody, jnp.zeros((SUBLANE, qb), I32))
        return jnp.sum(acc, axis=0, keepdims=True)

    def fold16(hit):
        parts = [hit[PACK16 * i:PACK16 * (i + 1), :] for i in range(kc // PACK16)]
        while len(parts) > 1:
            parts = [parts[i] + parts[i + 1] for i in range(0, len(parts), 2)]
        return parts[0]

    def total16(acc):
        return jnp.sum(acc.astype(F32), axis=0, keepdims=True).astype(I32)

    one, nil = jnp.ones((), BF16), jnp.zeros((), BF16)

    def count16(src_ref, cand, n):
        c16 = cand.astype(I16)
        acc = fold16(jnp.where(src_ref[0] >= c16, one, nil))
        for c in range(1, n):
            acc = acc + fold16(jnp.where(src_ref[c] >= c16, one, nil))
        return total16(acc)

    def search16(src_ref, kth, n):
        zero = jnp.zeros((1, qb), I32)
        ans0 = jnp.where(count16(src_ref, zero, n) >= kth, zero, I16_MIN)

        def bit_body(bi, ans):
            cand = ans | (1 << (14 - bi))
            return jnp.where(count16(src_ref, cand, n) >= kth, cand, ans)

        return lax.fori_loop(0, 15, bit_body, ans0)

    def search(n):
        top = search16(hi_ref, jnp.full((1, qb), topk, I32), n)
        top16 = top.astype(I16)
        acc = jnp.zeros((PACK16, qb), BF16)
        for c in range(n):
            hi = hi_ref[c]
            lo2_ref[c] = jnp.where(hi == top16, lo_ref[c], jnp.full((), I16_MIN, I16))
            acc = acc + fold16(jnp.where(hi > top16, one, nil))
        above = total16(acc)
        low = search16(lo2_ref, topk - above, n)
        return top * 65536 + (low - I16_MIN), above + count16(lo2_ref, low, n)

    def search_any():
        return lax.cond(nch <= nb // 4, functools.partial(search, nb // 4),
                        functools.partial(search, nb // 2))

    selecting = (j + 1) * qb > topk
    thr, n_ge = lax.cond(selecting, search_any,
                         lambda: (jnp.full((1, qb), INT_MIN + 1, I32), jnp.zeros((1, qb), I32)))

    def plain_mask():
        def body(c, carry):
            nm_ref[c] = jnp.where(mt_ref[c] >= thr, 0.0, NEG_BIG)
            return carry
        lax.fori_loop(0, nch, body, 0)

    def tie_mask():
        need = (topk - count_ge(thr + 1)).astype(F32)
        tri = jnp.where(lax.broadcasted_iota(I32, (kc, kc), 1) <= lax.broadcasted_iota(I32, (kc, kc), 0),
                        1.0, 0.0).astype(BF16)

        def body(c, seen):
            m = mt_ref[c]
            tie = m == thr
            upto = _dot(tri, jnp.where(tie, 1.0, 0.0).astype(BF16)) + seen
            nm_ref[c] = jnp.where((m > thr) | (tie & (upto <= need)), 0.0, NEG_BIG)
            return upto[kc - 1:kc, :]
        lax.fori_loop(0, nch, body, jnp.zeros((1, qb), F32))

    crowded = jnp.max(n_ge) > topk
    lax.cond(crowded, tie_mask, plain_mask)

    npair = nh // 2
    pcols = [slice(pr * pair, (pr + 1) * pair) for pr in range(npair)]
    nfar = jnp.maximum((j - 1) // 2, 0)

    def sweep1(c, mruns, near):
        r0 = pl.multiple_of(c * kc, kc)
        kh = khat_ref[pl.ds(r0, kc), :]
        nm = nm_ref[c]
        nm2 = jnp.concatenate([nm, nm], axis=1)
        if near:
            k0 = jnp.clip(j - 2 * c, 0, 2)
            k1 = jnp.clip(j - 2 * c - 1, 0, 2)
        out = []
        for pr in range(npair):
            lg = _dot(kh, qrt_ref[:, pcols[pr]]) + nm2
            if near:
                lg = lg + jnp.concatenate([tb_ref[k0, :, pcols[pr]], tb_ref[k1, :, pcols[pr]]], axis=0)
            lg_ref[c, :, pcols[pr]] = lg
            out.append(jnp.maximum(mruns[pr], jnp.max(lg.reshape(kc // SUBLANE, SUBLANE, pair), axis=0)))
        return tuple(out)

    mruns = tuple(jnp.full((SUBLANE, pair), NEG_BIG, F32) for _ in range(npair))
    mruns = lax.fori_loop(0, nfar, functools.partial(sweep1, near=False), mruns)
    mruns = lax.fori_loop(nfar, nch, functools.partial(sweep1, near=True), mruns)
    mrows = [jnp.max(m, axis=0, keepdims=True) for m in mruns]
    for acc_ref in acc_refs:
        acc_ref[...] = jnp.zeros(acc_ref.shape, F32)

    def sweep2(c, carry):
        vt = vt_ref[c]
        for pr in range(npair):
            p = jnp.exp2((lg_ref[c, :, pcols[pr]] - mrows[pr]).astype(BF16))
            acc_refs[pr][...] += _dot(vt, p)
        return carry

    lax.fori_loop(0, nch, sweep2, 0)
    for pr in range(npair):
        out = acc_refs[pr][0:dh, :] / acc_refs[pr][dh:dh + 1, :]
        for s in range(2):
            h = 2 * pr + s
            o_ref[:, h * dh:(h + 1) * dh] = transpose(
                out[:, s * qb:(s + 1) * qb].astype(BF16)).astype(o_ref.dtype)


def dsa_attention(z, rel_bias, q_norm, k_norm, bsz, seq, d):
    t = z.shape[0]
    nh, dh, nih, di = ATT_HEADS, d // ATT_HEADS, IDX_HEADS, IDX_DIM
    qb = Q_BLOCK
    nb = seq // qb
    topk = min(DSA_TOPK_MAX, seq // 4)
    assert topk % qb == 0 and dh == LANE and nih * di == d // 2 and nb % 2 == 0 and LANE % di == 0
    assert (nb // 2) * (2 * qb // PACK16) <= 256
    i = jnp.arange(qb)
    dist = i[None, :] - i[:, None]
    def lookup(bucket):
        hot = bucket[..., None] == jnp.arange(N_BUCKETS, dtype=I32)
        return jnp.sum(jnp.where(hot[..., None], rel_bias, 0.0), axis=-2)

    tabs = [lookup(_t5_bucket(jnp.maximum(dist + off * qb, 0))) for off in range(2)]
    far = jnp.broadcast_to(lookup(_t5_bucket(jnp.full((), 2 * qb, I32))), (qb, qb, nh))
    tb = jnp.stack(tabs + [far]).transpose(0, 1, 3, 2).reshape(3, qb, nh * qb).astype(F32)
    tb = (tb - tb[2:3]) * math.log2(math.e)
    cq = d // LANE
    return pl.pallas_call(
        functools.partial(_dsa_kernel, nh=nh, dh=dh, nih=nih, di=di, topk=topk, nb=nb),
        grid=(bsz, nb),
        in_specs=[
            pl.BlockSpec((qb, d), lambda b, j: (b * nb + j, 0)),
            pl.BlockSpec((seq, LANE), lambda b, j: (b, cq + cq // 2)),
            pl.BlockSpec((seq, LANE), lambda b, j: (b, cq + cq // 2 + 1)),
            pl.BlockSpec((qb, d // 2), lambda b, j: (b * nb + j, 2)),
            pl.BlockSpec((seq, LANE), lambda b, j: (b, cq + cq // 2 + 2)),
            pl.BlockSpec((qb, LANE), lambda b, j: (b * nb + j, cq + cq // 2 + 2)),
            pl.BlockSpec((3, qb, nh * qb), lambda b, j: (0, 0, 0)),
            pl.BlockSpec((1, dh), lambda b, j: (0, 0)),
            pl.BlockSpec((1, dh), lambda b, j: (0, 0)),
        ],
        out_specs=pl.BlockSpec((qb, d), lambda b, j: (b * nb + j, 0)),
        out_shape=jax.ShapeDtypeStruct((t, d), BF16),
        scratch_shapes=[
            pltpu.VMEM((seq, dh), BF16),
            pltpu.VMEM((nb // 2, dh + ONES_ROWS, 2 * qb), BF16),
            pltpu.VMEM((dh, nh * qb), BF16),
            pltpu.VMEM((di, nih * qb), BF16),
            pltpu.VMEM((LANE, qb), F32),
            pltpu.VMEM((nb // 2, 2 * qb, qb), I32),
            pltpu.VMEM((nb // 2, 2 * qb, qb), I16),
            pltpu.VMEM((nb // 2, 2 * qb, qb), I16),
            pltpu.VMEM((nb // 2, 2 * qb, qb), I16),
            pltpu.VMEM((nb // 2, 2 * qb, qb), F32),
            pltpu.VMEM((nb // 2, 2 * qb, nh * qb), F32),
        ] + [pltpu.VMEM((dh + ONES_ROWS, 2 * qb), F32)
             for _ in range(nh // 2)],
        compiler_params=_cparams(("arbitrary", "arbitrary")),
        name="dsa_attention",
    )(z, z, z, z, z, z, tb, q_norm.reshape(1, dh), k_norm.reshape(1, dh))


def _dsa_weight(c_w_in, d):
    dh = d // ATT_HEADS
    nqi = IDX_HEADS * IDX_DIM
    q, k, v, qi, ki, wi = jnp.split(
        c_w_in, [d, d + dh, d + 2 * dh, d + 2 * dh + nqi, d + 2 * dh + nqi + IDX_DIM], axis=1)
    used = d + nqi + 2 * dh + IDX_DIM + IDX_HEADS
    total = -(-used // 512) * 512
    pad = jnp.zeros((d, total - used), c_w_in.dtype)
    return jnp.concatenate([q, qi, k, v, ki, wi, pad], axis=1).astype(BF16)


def kernel(x, c, rel_bias, w_router, b_router, norm_mix, ada_w_mix, ada_b_mix, norm_ffn, ada_w_ffn,
           ada_b_ffn, ab_w_in, ab_conv_w, ab_conv_b, ab_w_out, c_w_in, c_q_norm, c_k_norm, c_w_out,
           moe_w1, moe_w3, moe_w2):
    bsz, seq, d = x.shape
    depth = norm_mix.shape[0]
    t = bsz * seq
    ch = d // 2
    tm_out = min(512, seq)
    x2 = x.reshape(t, d)
    mod_mix = ada_mod(c, ada_w_mix, ada_b_mix).reshape(depth, bsz, 3, d)
    mod_ffn = ada_mod(c, ada_w_ffn, ada_b_ffn).reshape(depth, bsz, 3, d)
    experts = (moe_w1, moe_w3, moe_w2)
    as_experts = lambda casts: [cw.reshape(w.shape[1:]) for cw, w in zip(casts, experts)]
    w_in = [ab_w_in[l // 2].astype(BF16) if l % 2 == 0 else _dsa_weight(c_w_in[l // 2], d)
            for l in range(depth)]
    tm_in, tn_in = min(1024, seq), 1024
    z = None
    w_moe = None
    for layer in range(depth):
        i = layer // 2
        if z is None:
            riders = _cast_riders(experts, layer, (t // tm_in) * (w_in[layer].shape[1] // tn_in))
            z, *casts = inproj(x2, mod_mix[layer], norm_mix[layer], w_in[layer], seq, tm_in, tn_in,
                               riders or ())
            if riders:
                w_moe = as_experts(casts)
        if w_moe is None:
            w_moe = [w[layer].astype(BF16) for w in experts]
        if layer % 2 == 0:
            y_b = retention(z, bsz, seq, ch, 3 * ch)
            lhs, w_out, conv = (z, y_b), ab_w_out[i], (ab_conv_w[i], ab_conv_b[i])
        else:
            o = dsa_attention(z, rel_bias, c_q_norm[i], c_k_norm[i], bsz, seq, d)
            lhs, w_out, conv = (o,), c_w_out[i], None
        x2, h, ri, rf, cnt = mixer_out(lhs, w_out.astype(BF16), x2, mod_mix[layer], mod_ffn[layer],
                                       norm_ffn[layer], w_router, b_router, bsz, seq, tm_out, conv)
        last = layer + 1 == depth
        nxt = None if last else (mod_mix[layer + 1], norm_mix[layer + 1], w_in[layer + 1])
        riders = None if last else _cast_riders(experts, layer + 1, _moe_blocks(t))
        res, casts = moe(x2, h, ri, rf, cnt, mod_ffn[layer], *w_moe, seq, nxt, riders or ())
        x2, z = (res[0], None) if last else (res[0], res[1])
        w_moe = as_experts(casts) if riders else None
    return x2.reshape(bsz, seq, d)
```

```python
import functools
import math
from typing import NamedTuple

import jax
import jax.numpy as jnp
from jax import lax
from jax.experimental import pallas as pl
from jax.experimental.pallas import tpu as pltpu

F32 = jnp.float32
BF16 = jnp.bfloat16
I32 = jnp.int32
I16 = jnp.int16

RMS_EPS = 1e-6
CONV_WIDTH = 3
RET_HEADS = 4
ATT_HEADS = 16
IDX_HEADS = 16
IDX_DIM = 64
DSA_TOPK_MAX = 256
Q_BLOCK = 128
N_BUCKETS = 32
MAX_DISTANCE = 128
N_EXPERTS = 16
N_GROUPS = 4
EXPERTS_PER_GROUP = N_EXPERTS // N_GROUPS
RET_CHUNK = 256
MOE_ROWS = 256
DMA_UNROLL = 8
RIDER_BLOCK_BYTES = 1024 * 1024
ONES_ROWS = 16
LANE = 128
SUBLANE = 8
VMEM_LIMIT = 48 * 1024 * 1024
NEG_BIG = -1e30
INT_MIN = -(2 ** 31)
I16_MIN = -(2 ** 15)
PACK16 = 16


def _cparams(sem, vmem=VMEM_LIMIT):
    return pltpu.CompilerParams(dimension_semantics=sem, vmem_limit_bytes=vmem)


def _dot(a, b):
    return jnp.dot(a, b, preferred_element_type=F32)


def _dot_nt(a, b):
    return lax.dot_general(a, b, (((1,), (1,)), ((), ())), preferred_element_type=F32)


def _dot_tn(a, b):
    return lax.dot_general(a, b, (((0,), (0,)), ((), ())), preferred_element_type=F32)


def _silu(x):
    return x * jax.nn.sigmoid(x)


class _CastRider(NamedTuple):
    src: jax.Array
    rows: int
    first_blk: int
    nblk: int

    def in_spec(self, flat):
        return pl.BlockSpec((self.rows, self.src.shape[1]),
                            lambda *g: (jnp.minimum(flat(*g), self.nblk - 1) + self.first_blk, 0))

    def out_spec(self, flat):
        return pl.BlockSpec((self.rows, self.src.shape[1]),
                            lambda *g: (jnp.minimum(flat(*g), self.nblk - 1), 0))

    @property
    def out_shape(self):
        return jax.ShapeDtypeStruct((self.rows * self.nblk, self.src.shape[1]), BF16)


def _cast_riders(weights, layer, n_steps):
    riders = []
    for w in weights:
        per_layer, cols = w.shape[1] * w.shape[2], w.shape[3]
        rows = PACK16
        while rows * n_steps < per_layer:
            rows *= 2
        if per_layer % rows or rows * cols * 4 > RIDER_BLOCK_BYTES:
            return None
        nblk = per_layer // rows
        riders.append(_CastRider(w.reshape(-1, cols), rows, layer * nblk, nblk))
    return riders


def _ride_casts(nblks, src_refs, dst_refs, step):
    for nblk, src, dst in zip(nblks, src_refs, dst_refs):
        @pl.when(step < nblk)
        def _():
            dst[...] = src[...].astype(BF16)


def _ada_kernel(c_ref, w_ref, b_ref, o_ref):
    cs = _silu(c_ref[...])
    o_ref[0] = _dot(cs.astype(BF16), w_ref[0].astype(BF16)) + b_ref[0]


def ada_mod(c, ada_w, ada_b):
    nl, d, n3 = ada_w.shape
    bsz = c.shape[0]
    tn = min(512, n3)
    return pl.pallas_call(
        _ada_kernel,
        grid=(nl, n3 // tn),
        in_specs=[
            pl.BlockSpec((bsz, d), lambda l, j: (0, 0)),
            pl.BlockSpec((1, d, tn), lambda l, j: (l, 0, j)),
            pl.BlockSpec((1, 1, tn), lambda l, j: (l, 0, j)),
        ],
        out_specs=pl.BlockSpec((1, bsz, tn), lambda l, j: (l, 0, j)),
        out_shape=jax.ShapeDtypeStruct((nl, bsz, n3), F32),
        compiler_params=_cparams(("parallel", "parallel")),
        name="ada_mod",
    )(c, ada_w, ada_b.reshape(nl, 1, n3))


def _modulated_norm(x, mod_ref, nw_ref):
    r = lax.rsqrt(jnp.mean(x * x, axis=-1, keepdims=True) + RMS_EPS)
    return x * r * nw_ref[...] * (1.0 + mod_ref[0, 1:2, :]) + mod_ref[0, 0:1, :]


def _prenorm_kernel(x_ref, mod_ref, nw_ref, h_ref):
    h_ref[...] = _modulated_norm(x_ref[...], mod_ref, nw_ref).astype(h_ref.dtype)


def prenorm(x2, mod, norm_w, seq, tm):
    t, d = x2.shape
    per = seq // tm
    return pl.pallas_call(
        _prenorm_kernel,
        grid=(t // tm,),
        in_specs=[
            pl.BlockSpec((tm, d), lambda i: (i, 0)),
            pl.BlockSpec((1, 3, d), lambda i: (i // per, 0, 0)),
            pl.BlockSpec((1, d), lambda i: (0, 0)),
        ],
        out_specs=pl.BlockSpec((tm, d), lambda i: (i, 0)),
        out_shape=jax.ShapeDtypeStruct((t, d), BF16),
        compiler_params=_cparams(("parallel",)),
        name="prenorm",
    )(x2, mod, norm_w.reshape(1, d))


def _inproj_kernel(h_ref, w_ref, *refs, riders):
    o_ref = refs[len(riders)]
    step = pl.program_id(0) * pl.num_programs(1) + pl.program_id(1)
    _ride_casts(riders, refs[:len(riders)], refs[len(riders) + 1:], step)
    o_ref[...] = _dot(h_ref[...], w_ref[...]).astype(o_ref.dtype)


def inproj(h_bf, w_bf, tm, tn, riders=()):
    t, d = h_bf.shape
    n = w_bf.shape[1]
    nj = n // tn
    flat = lambda i, j: i * nj + j
    return pl.pallas_call(
        functools.partial(_inproj_kernel, riders=tuple(r.nblk for r in riders)),
        grid=(t // tm, nj),
        in_specs=[
            pl.BlockSpec((tm, d), lambda i, j: (i, 0)),
            pl.BlockSpec((d, tn), lambda i, j: (0, j)),
        ] + [r.in_spec(flat) for r in riders],
        out_specs=[pl.BlockSpec((tm, tn), lambda i, j: (i, j))] + [r.out_spec(flat) for r in riders],
        out_shape=[jax.ShapeDtypeStruct((t, n), BF16)] + [r.out_shape for r in riders],
        compiler_params=_cparams(("arbitrary", "arbitrary")),
        name="inproj",
    )(h_bf, w_bf, *[r.src for r in riders])


def _gated_conv(b_ref, c_ref, v_ref, w_ref, cb_ref, u_ref, s, tm):
    u = c_ref[...].astype(F32) * v_ref[...].astype(F32)

    @pl.when(s == 0)
    def _():
        u_ref[0:SUBLANE, :] = jnp.zeros((SUBLANE, u.shape[1]), F32)

    @pl.when(s > 0)
    def _():
        u_ref[0:SUBLANE, :] = u_ref[tm:tm + SUBLANE, :]

    u_ref[SUBLANE:SUBLANE + tm, :] = u
    conv = (cb_ref[...]
            + u_ref[SUBLANE - 2:SUBLANE - 2 + tm, :] * w_ref[0:1, :]
            + u_ref[SUBLANE - 1:SUBLANE - 1 + tm, :] * w_ref[1:2, :]
            + u * w_ref[2:3, :])
    return b_ref[...].astype(F32) * conv


def _retention_kernel(q_ref, k_ref, v_ref, g_ref, cos_ref, sin_ref, din_ref, dcr_ref, dst_ref,
                      o_ref, *st_refs, nc, ck, dk):
    half = dk // 2
    for st_ref in st_refs:
        st_ref[...] = jnp.zeros(st_ref.shape, F32)

    def rot(x, cos, sin):
        x1, x2 = x[:, :half], x[:, half:]
        return jnp.concatenate([x1 * cos - x2 * sin, x1 * sin + x2 * cos], axis=-1)

    def body(c, carry):
        r0 = pl.multiple_of(c * ck, ck)
        cos = cos_ref[pl.ds(r0, ck), :]
        sin = sin_ref[pl.ds(r0, ck), :]
        for h, st_ref in enumerate(st_refs):
            cols = slice(h * dk, (h + 1) * dk)
            d_cross = dcr_ref[h]
            g_chunk = dcr_ref[h, ck - 1:ck, :]
            q = rot(q_ref[pl.ds(r0, ck), cols].astype(F32), cos, sin)
            k = rot(k_ref[pl.ds(r0, ck), cols].astype(F32), cos, sin) * (dk ** -0.5)
            v = v_ref[pl.ds(r0, ck), cols]
            intra = _dot_nt(q.astype(BF16), k.astype(BF16)) * din_ref[h]
            state = st_ref[...]
            o = _dot(intra.astype(BF16), v) + _dot((q * d_cross).astype(BF16), state.astype(BF16))
            st_ref[...] = state * g_chunk + _dot_tn((k * dst_ref[h]).astype(BF16), v)
            r = lax.rsqrt(jnp.mean(o * o, axis=-1, keepdims=True) + RMS_EPS)
            gate = _silu(g_ref[pl.ds(r0, ck), cols].astype(F32))
            o_ref[pl.ds(r0, ck), cols] = (o * r * gate).astype(o_ref.dtype)
        return carry

    lax.fori_loop(0, nc, body, 0)


def retention(z, bsz, seq, ch, col0):
    t = z.shape[0]
    nh = RET_HEADS
    dk = ch // nh
    ck = min(RET_CHUNK, seq)
    nc = seq // ck
    half = dk // 2
    pos = jnp.arange(seq, dtype=F32)
    inv = 1.0 / (10000.0 ** jnp.linspace(0.0, 1.0, half, dtype=F32))
    ang = pos[:, None] * inv[None, :]
    cos, sin = jnp.cos(ang), jnp.sin(ang)
    log_g = jnp.log(1.0 - 2.0 ** (-5.0 - jnp.arange(nh, dtype=F32)))
    i = jnp.arange(ck, dtype=F32)
    diff = i[:, None] - i[None, :]
    d_intra = jnp.where(diff >= 0, jnp.exp(log_g[:, None, None] * jnp.maximum(diff, 0.0)), 0.0)
    d_cross = jnp.exp(log_g[:, None] * (i[None, :] + 1.0))[..., None]
    d_state = jnp.exp(log_g[:, None] * (ck - 1.0 - i[None, :]))[..., None]
    cb = col0 // ch
    col = lambda g: (lambda b: (b, cb + g))
    whole = lambda b: (0, 0, 0)
    return pl.pallas_call(
        functools.partial(_retention_kernel, nc=nc, ck=ck, dk=dk),
        grid=(bsz,),
        in_specs=[
            pl.BlockSpec((seq, ch), col(0)),
            pl.BlockSpec((seq, ch), col(1)),
            pl.BlockSpec((seq, ch), col(2)),
            pl.BlockSpec((seq, ch), col(3)),
            pl.BlockSpec((seq, half), lambda b: (0, 0)),
            pl.BlockSpec((seq, half), lambda b: (0, 0)),
            pl.BlockSpec((nh, ck, ck), whole),
            pl.BlockSpec((nh, ck, 1), whole),
            pl.BlockSpec((nh, ck, 1), whole),
        ],
        out_specs=pl.BlockSpec((seq, ch), lambda b: (b, 0)),
        out_shape=jax.ShapeDtypeStruct((t, ch), BF16),
        scratch_shapes=[pltpu.VMEM((dk, dk), F32) for _ in range(nh)],
        compiler_params=_cparams(("parallel",)),
        name="retention",
    )(z, z, z, z, cos, sin, d_intra, d_cross, d_state)


def _route(h, wr_ref, br_ref, ri_ref, rf_ref, cnt_ref, carry_ref, tm):
    logits = _dot_nt(wr_ref[...], h.astype(BF16))
    mx = jnp.max(logits, axis=0, keepdims=True)
    ex = jnp.exp(logits - mx)
    probs = ex / jnp.sum(ex, axis=0, keepdims=True)
    sel = probs + br_ref[...]
    s = [sel[e:e + 1, :] for e in range(N_EXPERTS)]
    p = [probs[e:e + 1, :] for e in range(N_EXPERTS)]
    epg = EXPERTS_PER_GROUP

    def first_argmax(vals, exclude=None):
        best = jnp.full_like(vals[0], -jnp.inf)
        idx = jnp.zeros(vals[0].shape, I32)
        for j, vj in enumerate(vals):
            better = vj > best
            if exclude is not None:
                better = better & (exclude != j)
            idx = jnp.where(better, j, idx)
            best = jnp.where(better, vj, best)
        return idx

    gscore = []
    for g in range(N_GROUPS):
        gs = s[g * epg:(g + 1) * epg]
        best = None
        for a in range(epg):
            for b in range(a + 1, epg):
                pair = gs[a] + gs[b]
                best = pair if best is None else jnp.maximum(best, pair)
        gscore.append(best)
    grp = first_argmax(gscore)

    def pick(rows, index, n):
        out = rows[n - 1]
        for j in range(n - 2, -1, -1):
            out = jnp.where(index == j, rows[j], out)
        return out

    in_s = [pick([s[g * epg + j] for g in range(N_GROUPS)], grp, N_GROUPS) for j in range(epg)]
    in_p = [pick([p[g * epg + j] for g in range(N_GROUPS)], grp, N_GROUPS) for j in range(epg)]
    i1 = first_argmax(in_s)
    i2 = first_argmax(in_s, exclude=i1)
    p1 = pick(in_p, i1, epg)
    p2 = pick(in_p, i2, epg)
    e1 = grp * epg + i1
    e2 = grp * epg + i2
    den = p1 + p2
    g1 = p1 / den
    g2 = p2 / den

    eidx = lax.broadcasted_iota(I32, (N_EXPERTS, tm), 0)
    member = (eidx == e1) | (eidx == e2)
    member_f = jnp.where(member, 1.0, 0.0)
    before = lax.broadcasted_iota(I32, (tm, tm), 0) < lax.broadcasted_iota(I32, (tm, tm), 1)
    prefix = _dot(member_f.astype(BF16), jnp.where(before, 1.0, 0.0).astype(BF16))
    base = prefix + carry_ref[:, 0:1]
    rank1 = jnp.sum(jnp.where(eidx == e1, base, 0.0), axis=0, keepdims=True).astype(I32)
    rank2 = jnp.sum(jnp.where(eidx == e2, base, 0.0), axis=0, keepdims=True).astype(I32)
    carry_ref[...] = carry_ref[...] + jnp.sum(member_f, axis=1, keepdims=True)
    cnt_ref[...] = carry_ref[...].astype(I32)

    zi = jnp.zeros((SUBLANE - 4, tm), I32)
    ri_ref[...] = jnp.concatenate([e1, e2, rank1, rank2, zi], axis=0)
    zf = jnp.zeros((SUBLANE - 2, tm), F32)
    rf_ref[...] = jnp.concatenate([g1, g2, zf], axis=0)


def _mixer_out_kernel(*refs, tm, conv):
    if conv:
        b_ref, c_ref, v_ref, cw_ref, cb_ref, *refs = refs
    else:
        ya_ref, *refs = refs
    (yb_ref, wa_ref, wb_ref, x_ref, modm_ref, modf_ref, nw_ref, wr_ref, br_ref,
     x1_ref, h_ref, ri_ref, rf_ref, cnt_ref, carry_ref, *conv_scratch) = refs
    s = pl.program_id(1)

    @pl.when((pl.program_id(0) == 0) & (s == 0))
    def _():
        carry_ref[...] = jnp.zeros(carry_ref.shape, F32)

    if conv:
        ya = _gated_conv(b_ref, c_ref, v_ref, cw_ref, cb_ref, conv_scratch[0], s, tm).astype(BF16)
    else:
        ya = ya_ref[...]
    y = _dot(ya, wa_ref[...]) + _dot(yb_ref[...], wb_ref[...])
    x1 = x_ref[...] + modm_ref[0, 2:3, :] * y
    x1_ref[...] = x1
    h = _modulated_norm(x1, modf_ref, nw_ref)
    h_ref[...] = h
    _route(h, wr_ref, br_ref, ri_ref, rf_ref, cnt_ref, carry_ref, tm)


def mixer_out(lhs, w_bf, x2, mod_mix, mod_ffn, norm_ffn_w, w_router, b_router, bsz, seq, tm, conv=None):
    t, d = x2.shape
    kh = w_bf.shape[0] // 2
    per = seq // tm
    ne = N_EXPERTS
    row = lambda g: (lambda b, s: (b * per + s, g))
    const = lambda b, s: (0, 0)
    if conv is None:
        (o,) = lhs
        lhs_args = (o, o)
        lhs_specs = [pl.BlockSpec((tm, kh), row(0)), pl.BlockSpec((tm, kh), row(1))]
        scratch = []
    else:
        z, yb = lhs
        conv_w, conv_b = conv
        lhs_args = (z, z, z, conv_w, conv_b.reshape(1, kh), yb)
        lhs_specs = [pl.BlockSpec((tm, kh), row(0)), pl.BlockSpec((tm, kh), row(1)),
                     pl.BlockSpec((tm, kh), row(2)), pl.BlockSpec((CONV_WIDTH, kh), const),
                     pl.BlockSpec((1, kh), const), pl.BlockSpec((tm, kh), row(0))]
        scratch = [pltpu.VMEM((tm + SUBLANE, kh), F32)]
    batch = lambda b, s: (b, 0, 0)
    return pl.pallas_call(
        functools.partial(_mixer_out_kernel, tm=tm, conv=conv is not None),
        grid=(bsz, per),
        in_specs=lhs_specs + [
            pl.BlockSpec((kh, d), const),
            pl.BlockSpec((kh, d), lambda b, s: (1, 0)),
            pl.BlockSpec((tm, d), row(0)),
            pl.BlockSpec((1, 3, d), batch),
            pl.BlockSpec((1, 3, d), batch),
            pl.BlockSpec((1, d), const),
            pl.BlockSpec((ne, d), const),
            pl.BlockSpec((ne, 1), const),
        ],
        out_specs=[
            pl.BlockSpec((tm, d), row(0)),
            pl.BlockSpec((tm, d), row(0)),
            pl.BlockSpec((SUBLANE, tm), lambda b, s: (0, b * per + s)),
            pl.BlockSpec((SUBLANE, tm), lambda b, s: (0, b * per + s)),
            pl.BlockSpec((ne, LANE), const),
        ],
        out_shape=[
            jax.ShapeDtypeStruct((t, d), F32),
            jax.ShapeDtypeStruct((t, d), F32),
            jax.ShapeDtypeStruct((SUBLANE, t), I32),
            jax.ShapeDtypeStruct((SUBLANE, t), F32),
            jax.ShapeDtypeStruct((ne, LANE), I32),
        ],
        scratch_shapes=[pltpu.VMEM((ne, LANE), F32)] + scratch,
        compiler_params=_cparams(("arbitrary", "arbitrary")),
        name="mixer_out",
    )(*lhs_args, w_bf, w_bf, x2, mod_mix, mod_ffn, norm_ffn_w.reshape(1, d),
      w_router.T.astype(BF16), b_router.reshape(ne, 1))


def _dispatch_kernel(dest_ref, pad0_ref, padn_ref, h_ref, xs_ref, hbuf_ref, zero_ref, sem, zsem,
                     *, tm, nt):
    i = pl.program_id(0)
    cur = i % 2
    base = i * tm

    def zero_row(e, r):
        return pltpu.make_async_copy(zero_ref.at[pl.ds(0, 1), :],
                                     xs_ref.at[pl.ds(pad0_ref[e] + r, 1), :], zsem)

    def tile_done(buf):
        for _ in range(2):
            pltpu.make_async_copy(hbuf_ref.at[buf], xs_ref.at[pl.ds(0, tm), :], sem.at[buf]).wait()

    @pl.when(i == 0)
    def _():
        zero_ref[...] = jnp.zeros(zero_ref.shape, F32)
        for e in range(N_EXPERTS):
            lax.fori_loop(0, padn_ref[e], lambda r, c, e=e: (zero_row(e, r).start(), c)[1], 0)

    hbuf_ref[cur] = h_ref[...]

    def start(r, carry):
        for slot in range(2):
            d = dest_ref[2 * (base + r) + slot]
            pltpu.make_async_copy(hbuf_ref.at[cur, pl.ds(r, 1), :], xs_ref.at[pl.ds(d, 1), :],
                                  sem.at[cur]).start()
        return carry

    lax.fori_loop(0, tm, start, 0, unroll=DMA_UNROLL)

    @pl.when(i > 0)
    def _():
        tile_done(1 - cur)

    @pl.when(i == nt - 1)
    def _():
        tile_done(cur)
        for e in range(N_EXPERTS):
            lax.fori_loop(0, padn_ref[e], lambda r, c, e=e: (zero_row(e, r).wait(), c)[1], 0)


def dispatch(h, dest, pad_start, pad_count, n_rows, tm):
    t, d = h.shape
    return pl.pallas_call(
        functools.partial(_dispatch_kernel, tm=tm, nt=t // tm),
        grid_spec=pltpu.PrefetchScalarGridSpec(
            num_scalar_prefetch=3,
            grid=(t // tm,),
            in_specs=[pl.BlockSpec((tm, d), lambda i, *_: (i, 0))],
            out_specs=pl.BlockSpec(memory_space=pl.ANY),
            scratch_shapes=[pltpu.VMEM((2, tm, d), F32), pltpu.VMEM((SUBLANE, d), F32),
                            pltpu.SemaphoreType.DMA((2,)), pltpu.SemaphoreType.DMA(())],
        ),
        out_shape=jax.ShapeDtypeStruct((n_rows, d), F32),
        compiler_params=_cparams(("arbitrary",)),
        name="moe_dispatch",
    )(dest, pad_start, pad_count, h)


def _ffn_kernel(be_ref, bv_ref, xs_ref, w1_ref, w3_ref, w2_ref, *refs, riders):
    ys_ref = refs[len(riders)]
    _ride_casts(riders, refs[:len(riders)], refs[len(riders) + 1:], pl.program_id(0))
    valid = bv_ref[pl.program_id(0)]

    @pl.when(valid > 0)
    def _():
        rows = lax.broadcasted_iota(I32, xs_ref.shape, 0)
        x = jnp.where(rows < valid, xs_ref[...], 0.0).astype(BF16)
        h1 = _dot(x, w1_ref[0])
        h3 = _dot(x, w3_ref[0])
        a = (_silu(h1) * h3).astype(BF16)
        ys_ref[...] = _dot(a, w2_ref[0])

    @pl.when(valid <= 0)
    def _():
        ys_ref[...] = jnp.zeros(ys_ref.shape, F32)


def grouped_ffn(xs, block_e, block_valid, w1, w3, w2, bm, riders=()):
    n_rows, d = xs.shape
    f = w1.shape[2]
    flat = lambda i, be, bv: i
    return pl.pallas_call(
        functools.partial(_ffn_kernel, riders=tuple(r.nblk for r in riders)),
        grid_spec=pltpu.PrefetchScalarGridSpec(
            num_scalar_prefetch=2,
            grid=(n_rows // bm,),
            in_specs=[
                pl.BlockSpec((bm, d), lambda i, be, bv: (i, 0)),
                pl.BlockSpec((1, d, f), lambda i, be, bv: (be[i], 0, 0)),
                pl.BlockSpec((1, d, f), lambda i, be, bv: (be[i], 0, 0)),
                pl.BlockSpec((1, f, d), lambda i, be, bv: (be[i], 0, 0)),
            ] + [r.in_spec(flat) for r in riders],
            out_specs=[pl.BlockSpec((bm, d), lambda i, be, bv: (i, 0))] + [r.out_spec(flat) for r in riders],
        ),
        out_shape=[jax.ShapeDtypeStruct((n_rows, d), F32)] + [r.out_shape for r in riders],
        compiler_params=_cparams(("arbitrary",)),
        name="moe_ffn",
    )(block_e, block_valid, xs, w1, w3, w2, *[r.src for r in riders])


def _combine_kernel(dest_ref, ys_ref, x_ref, mod_ref, gf_ref, *refs, tm, nt, project):
    if project:
        modn_ref, nwn_ref, w_ref, o_ref, z_ref, a_ref, b_ref, sem = refs
    else:
        o_ref, a_ref, b_ref, sem = refs
    i = pl.program_id(0)
    cur = i % 2

    def fetch_row(base, r, buf):
        for slot, ref in ((0, a_ref), (1, b_ref)):
            d = dest_ref[2 * (base + r) + slot]
            pltpu.make_async_copy(ys_ref.at[pl.ds(d, 1), :], ref.at[buf, pl.ds(r, 1), :],
                                  sem.at[buf]).start()

    def fetch(tile, buf):
        lax.fori_loop(0, tm, lambda r, c: (fetch_row(tile * tm, r, buf), c)[1], 0, unroll=DMA_UNROLL)

    def fetched(buf):
        pltpu.make_async_copy(ys_ref.at[pl.ds(0, tm), :], a_ref.at[buf], sem.at[buf]).wait()
        pltpu.make_async_copy(ys_ref.at[pl.ds(0, tm), :], b_ref.at[buf], sem.at[buf]).wait()

    @pl.when(i == 0)
    def _():
        fetch(0, 0)

    if not project:
        @pl.when(i + 1 < nt)
        def _():
            fetch(i + 1, 1 - cur)

    fetched(cur)
    y = gf_ref[:, 0:1] * a_ref[cur] + gf_ref[:, 1:2] * b_ref[cur]
    out = x_ref[...] + mod_ref[0, 2:3, :] * y
    o_ref[...] = out
    if project:
        h = _modulated_norm(out, modn_ref, nwn_ref).astype(BF16)
        base = jnp.minimum(i + 1, nt - 1) * tm
        for r in range(tm):
            fetch_row(base, r, 1 - cur)
        z_ref[...] = _dot(h, w_ref[...]).astype(z_ref.dtype)

        @pl.when(i == nt - 1)
        def _():
            fetched(1 - cur)


def combine(ys, dest, x2, mod, gates_t, seq, tm, project=None):
    t, d = x2.shape
    per = seq // tm
    row = lambda i, dest: (i, 0)
    batch = lambda i, dest: (i // per, 0, 0)
    const = lambda i, dest: (0, 0)
    in_specs = [pl.BlockSpec(memory_space=pl.ANY), pl.BlockSpec((tm, d), row),
                pl.BlockSpec((1, 3, d), batch), pl.BlockSpec((tm, SUBLANE), row)]
    out_specs = [pl.BlockSpec((tm, d), row)]
    out_shape = [jax.ShapeDtypeStruct((t, d), F32)]
    args = [dest, ys, x2, mod, gates_t]
    if project is not None:
        mod_next, nw_next, w_bf = project
        n = w_bf.shape[1]
        in_specs += [pl.BlockSpec((1, 3, d), batch), pl.BlockSpec((1, d), const), pl.BlockSpec((d, n), const)]
        out_specs += [pl.BlockSpec((tm, n), row)]
        out_shape += [jax.ShapeDtypeStruct((t, n), BF16)]
        args += [mod_next, nw_next.reshape(1, d), w_bf]
    return pl.pallas_call(
        functools.partial(_combine_kernel, tm=tm, nt=t // tm, project=project is not None),
        grid_spec=pltpu.PrefetchScalarGridSpec(
            num_scalar_prefetch=1,
            grid=(t // tm,),
            in_specs=in_specs,
            out_specs=out_specs,
            scratch_shapes=[pltpu.VMEM((2, tm, d), F32), pltpu.VMEM((2, tm, d), F32),
                            pltpu.SemaphoreType.DMA((2,))],
        ),
        out_shape=out_shape,
        compiler_params=_cparams(("arbitrary",)),
        name="moe_combine",
    )(*args)


def _moe_blocks(t):
    return (2 * t) // MOE_ROWS + N_EXPERTS


def moe(x2, h, ri, rf, cnt, mod, w1, w3, w2, seq, project=None, riders=()):
    t, d = x2.shape
    bm = MOE_ROWS
    tm = min(256, seq)
    ne = N_EXPERTS
    counts = cnt[:, 0]
    nblk = (counts + bm - 1) // bm
    blk_end = jnp.cumsum(nblk)
    blk_start = blk_end - nblk
    eids = jnp.arange(ne, dtype=I32)
    e12 = ri[0:2].T
    row0 = jnp.sum(jnp.where(e12[..., None] == eids, blk_start * bm, 0), axis=-1)
    dest = (row0 + ri[2:4].T).reshape(2 * t).astype(I32)
    n_blocks = _moe_blocks(t)
    bidx = jnp.arange(n_blocks, dtype=I32)
    block_e = jnp.minimum(jnp.sum(bidx[:, None] >= blk_end[None, :], axis=1), ne - 1).astype(I32)
    onehot = block_e[:, None] == eids
    cnt_b = jnp.sum(jnp.where(onehot, counts, 0), axis=1)
    start_b = jnp.sum(jnp.where(onehot, blk_start, 0), axis=1)
    block_valid = jnp.clip(cnt_b - (bidx - start_b) * bm, 0, bm).astype(I32)
    pad_start = (blk_start * bm + counts).astype(I32)
    pad_end = jnp.where(eids == ne - 1, n_blocks * bm, blk_end * bm)
    xs = dispatch(h, dest, pad_start, (pad_end - pad_start).astype(I32), n_blocks * bm, tm)
    ys, *casts = grouped_ffn(xs, block_e, block_valid, w1, w3, w2, bm, riders)
    return combine(ys, dest, x2, mod, rf.T, seq, tm, project), casts


def _t5_bucket(n):
    max_exact = N_BUCKETS // 2
    nf = jnp.maximum(n, 1).astype(F32)
    large = max_exact + (jnp.log(nf / max_exact) / math.log(MAX_DISTANCE / max_exact)
                         * (N_BUCKETS - max_exact)).astype(I32)
    large = jnp.minimum(large, N_BUCKETS - 1)
    return jnp.where(n < max_exact, n, large)


def _sortable(score):
    bits = pltpu.bitcast(score, I32)
    return bits ^ ((bits >> 31) & 0x7FFFFFFF)


def _dsa_kernel(q_ref, k_ref, v_ref, qi_ref, kw_ref, wq_ref, tb_ref, qn_ref, kn_ref, o_ref,
                khat_ref, vt_ref, qrt_ref, qirt_ref, wt_ref, mt_ref, hi_ref, lo_ref, lo2_ref, nm_ref, lg_ref,
                *acc_refs,
                nh, dh, nih, di, topk, nb):
    j = pl.program_id(1)
    qb = Q_BLOCK
    kc = 2 * qb
    nch = (j + 2) // 2
    pair = 2 * qb

    def transpose(x):
        return x.astype(F32).T

    @pl.when(j == 0)
    def _():
        k = k_ref[...].astype(F32)
        r = lax.rsqrt(jnp.mean(k * k, axis=-1, keepdims=True) + RMS_EPS)
        khat_ref[...] = (k * r * kn_ref[...]).astype(BF16)
        for c in range(nb // 2):
            for s in range(2):
                blk = v_ref[(2 * c + s) * qb:(2 * c + s + 1) * qb, :]
                vt_ref[c, 0:dh, s * qb:(s + 1) * qb] = transpose(blk).astype(BF16)
            vt_ref[c, dh:dh + ONES_ROWS, :] = jnp.ones((ONES_ROWS, kc), BF16)
        for ref in (hi_ref, lo_ref, lo2_ref):
            ref[...] = jnp.full(ref.shape, I16_MIN, I16)

    qscale = (dh ** -0.5) * math.log2(math.e)
    for h in range(nh):
        qh = q_ref[:, h * dh:(h + 1) * dh].astype(F32)
        r = lax.rsqrt(jnp.mean(qh * qh, axis=-1, keepdims=True) + RMS_EPS)
        qhat = (qh * r * qn_ref[...] * qscale).astype(BF16)
        qrt_ref[:, h * qb:(h + 1) * qb] = transpose(qhat).astype(BF16)
    for g in range(nih * di // LANE):
        two = transpose(qi_ref[:, g * LANE:(g + 1) * LANE])
        for s in range(LANE // di):
            h = g * (LANE // di) + s
            qirt_ref[:, h * qb:(h + 1) * qb] = two[s * di:(s + 1) * di, :].astype(BF16)
    wt_ref[...] = transpose(wq_ref[...])

    key_l = lax.broadcasted_iota(I32, (kc, qb), 0)
    q_pos = j * qb + lax.broadcasted_iota(I32, (kc, qb), 1)

    def score_body(c, carry):
        r0 = pl.multiple_of(c * kc, kc)
        ki = kw_ref[pl.ds(r0, kc), 0:di]
        acc = jnp.zeros((kc, qb), F32)
        for g in range(nih // 2):
            rel = _dot(ki, qirt_ref[:, g * pair:(g + 1) * pair])
            for s in range(2):
                h = 2 * g + s
                acc = acc + wt_ref[di + h:di + h + 1, :] * jnp.maximum(rel[:, s * qb:(s + 1) * qb], 0.0)
        m = jnp.where(r0 + key_l <= q_pos, _sortable(acc), INT_MIN)
        mt_ref[c] = m
        hi_ref[c] = (m >> 16).astype(I16)
        lo_ref[c] = ((m & 0xFFFF) + I16_MIN).astype(I16)
        return carry

    lax.fori_loop(0, nch, score_body, 0)

    def count_ge(cand):
        def cbody(c, acc):
            hit = jnp.where(mt_ref[c] >= cand, 1, 0)
            return acc + jnp.sum(hit.reshape(kc // SUBLANE, SUBLANE, qb), axis=0)
        acc = lax.fori_loop(0, nch, cbody, jnp.zeros((SUBLANE, qb), I32))
        return jnp.sum(acc, axis=0, keepdims=True)

    def fold16(hit):
        parts = [hit[PACK16 * i:PACK16 * (i + 1), :] for i in range(kc // PACK16)]
        while len(parts) > 1:
            parts = [parts[i] + parts[i + 1] for i in range(0, len(parts), 2)]
        return parts[0]

    def total16(acc):
        return jnp.sum(acc.astype(F32), axis=0, keepdims=True).astype(I32)

    one, nil = jnp.ones((), BF16), jnp.zeros((), BF16)

    def count16(src_ref, cand, n):
        c16 = cand.astype(I16)
        acc = fold16(jnp.where(src_ref[0] >= c16, one, nil))
        for c in range(1, n):
            acc = acc + fold16(jnp.where(src_ref[c] >= c16, one, nil))
        return total16(acc)

    def search16(src_ref, kth, n):
        zero = jnp.zeros((1, qb), I32)
        ans0 = jnp.where(count16(src_ref, zero, n) >= kth, zero, I16_MIN)

        def bit_body(bi, ans):
            cand = ans | (1 << (14 - bi))
            return jnp.where(count16(src_ref, cand, n) >= kth, cand, ans)

        return lax.fori_loop(0, 15, bit_body, ans0)

    def search(n):
        top = search16(hi_ref, jnp.full((1, qb), topk, I32), n)
        top16 = top.astype(I16)
        acc = jnp.zeros((PACK16, qb), BF16)
        for c in range(n):
            hi = hi_ref[c]
            lo2_ref[c] = jnp.where(hi == top16, lo_ref[c], jnp.full((), I16_MIN, I16))
            acc = acc + fold16(jnp.where(hi > top16, one, nil))
        above = total16(acc)
        low = search16(lo2_ref, topk - above, n)
        return top * 65536 + (low - I16_MIN), above + count16(lo2_ref, low, n)

    def search_any():
        return lax.cond(nch <= nb // 4, functools.partial(search, nb // 4),
                        functools.partial(search, nb // 2))

    selecting = (j + 1) * qb > topk
    thr, n_ge = lax.cond(selecting, search_any,
                         lambda: (jnp.full((1, qb), INT_MIN + 1, I32), jnp.zeros((1, qb), I32)))

    def plain_mask():
        def body(c, carry):
            nm_ref[c] = jnp.where(mt_ref[c] >= thr, 0.0, NEG_BIG)
            return carry
        lax.fori_loop(0, nch, body, 0)

    def tie_mask():
        need = (topk - count_ge(thr + 1)).astype(F32)
        tri = jnp.where(lax.broadcasted_iota(I32, (kc, kc), 1) <= lax.broadcasted_iota(I32, (kc, kc), 0),
                        1.0, 0.0).astype(BF16)

        def body(c, seen):
            m = mt_ref[c]
            tie = m == thr
            upto = _dot(tri, jnp.where(tie, 1.0, 0.0).astype(BF16)) + seen
            nm_ref[c] = jnp.where((m > thr) | (tie & (upto <= need)), 0.0, NEG_BIG)
            return upto[kc - 1:kc, :]
        lax.fori_loop(0, nch, body, jnp.zeros((1, qb), F32))

    crowded = jnp.max(n_ge) > topk
    lax.cond(crowded, tie_mask, plain_mask)

    npair = nh // 2
    pcols = [slice(pr * pair, (pr + 1) * pair) for pr in range(npair)]
    nfar = jnp.maximum((j - 1) // 2, 0)

    def sweep1(c, mruns, near):
        r0 = pl.multiple_of(c * kc, kc)
        kh = khat_ref[pl.ds(r0, kc), :]
        nm = nm_ref[c]
        nm2 = jnp.concatenate([nm, nm], axis=1)
        if near:
            k0 = jnp.clip(j - 2 * c, 0, 2)
            k1 = jnp.clip(j - 2 * c - 1, 0, 2)
        out = []
        for pr in range(npair):
            lg = _dot(kh, qrt_ref[:, pcols[pr]]) + nm2
            if near:
                lg = lg + jnp.concatenate([tb_ref[k0, :, pcols[pr]], tb_ref[k1, :, pcols[pr]]], axis=0)
            lg_ref[c, :, pcols[pr]] = lg
            out.append(jnp.maximum(mruns[pr], jnp.max(lg.reshape(kc // SUBLANE, SUBLANE, pair), axis=0)))
        return tuple(out)

    mruns = tuple(jnp.full((SUBLANE, pair), NEG_BIG, F32) for _ in range(npair))
    mruns = lax.fori_loop(0, nfar, functools.partial(sweep1, near=False), mruns)
    mruns = lax.fori_loop(nfar, nch, functools.partial(sweep1, near=True), mruns)
    mrows = [jnp.max(m, axis=0, keepdims=True) for m in mruns]
    for acc_ref in acc_refs:
        acc_ref[...] = jnp.zeros(acc_ref.shape, F32)

    def sweep2(c, carry):
        vt = vt_ref[c]
        for pr in range(npair):
            p = jnp.exp2((lg_ref[c, :, pcols[pr]] - mrows[pr]).astype(BF16))
            acc_refs[pr][...] += _dot(vt, p)
        return carry

    lax.fori_loop(0, nch, sweep2, 0)
    for pr in range(npair):
        out = acc_refs[pr][0:dh, :] / acc_refs[pr][dh:dh + 1, :]
        for s in range(2):
            h = 2 * pr + s
            o_ref[:, h * dh:(h + 1) * dh] = transpose(
                out[:, s * qb:(s + 1) * qb].astype(BF16)).astype(o_ref.dtype)


def dsa_attention(z, rel_bias, q_norm, k_norm, bsz, seq, d):
    t = z.shape[0]
    nh, dh, nih, di = ATT_HEADS, d // ATT_HEADS, IDX_HEADS, IDX_DIM
    qb = Q_BLOCK
    nb = seq // qb
    topk = min(DSA_TOPK_MAX, seq // 4)
    assert topk % qb == 0 and dh == LANE and nih * di == d // 2 and nb % 2 == 0 and LANE % di == 0
    assert (nb // 2) * (2 * qb // PACK16) <= 256
    i = jnp.arange(qb)
    dist = i[None, :] - i[:, None]
    def lookup(bucket):
        hot = bucket[..., None] == jnp.arange(N_BUCKETS, dtype=I32)
        return jnp.sum(jnp.where(hot[..., None], rel_bias, 0.0), axis=-2)

    tabs = [lookup(_t5_bucket(jnp.maximum(dist + off * qb, 0))) for off in range(2)]
    far = jnp.broadcast_to(lookup(_t5_bucket(jnp.full((), 2 * qb, I32))), (qb, qb, nh))
    tb = jnp.stack(tabs + [far]).transpose(0, 1, 3, 2).reshape(3, qb, nh * qb).astype(F32)
    tb = (tb - tb[2:3]) * math.log2(math.e)
    cq = d // LANE
    return pl.pallas_call(
        functools.partial(_dsa_kernel, nh=nh, dh=dh, nih=nih, di=di, topk=topk, nb=nb),
        grid=(bsz, nb),
        in_specs=[
            pl.BlockSpec((qb, d), lambda b, j: (b * nb + j, 0)),
            pl.BlockSpec((seq, LANE), lambda b, j: (b, cq + cq // 2)),
            pl.BlockSpec((seq, LANE), lambda b, j: (b, cq + cq // 2 + 1)),
            pl.BlockSpec((qb, d // 2), lambda b, j: (b * nb + j, 2)),
            pl.BlockSpec((seq, LANE), lambda b, j: (b, cq + cq // 2 + 2)),
            pl.BlockSpec((qb, LANE), lambda b, j: (b * nb + j, cq + cq // 2 + 2)),
            pl.BlockSpec((3, qb, nh * qb), lambda b, j: (0, 0, 0)),
            pl.BlockSpec((1, dh), lambda b, j: (0, 0)),
            pl.BlockSpec((1, dh), lambda b, j: (0, 0)),
        ],
        out_specs=pl.BlockSpec((qb, d), lambda b, j: (b * nb + j, 0)),
        out_shape=jax.ShapeDtypeStruct((t, d), BF16),
        scratch_shapes=[
            pltpu.VMEM((seq, dh), BF16),
            pltpu.VMEM((nb // 2, dh + ONES_ROWS, 2 * qb), BF16),
            pltpu.VMEM((dh, nh * qb), BF16),
            pltpu.VMEM((di, nih * qb), BF16),
            pltpu.VMEM((LANE, qb), F32),
            pltpu.VMEM((nb // 2, 2 * qb, qb), I32),
            pltpu.VMEM((nb // 2, 2 * qb, qb), I16),
            pltpu.VMEM((nb // 2, 2 * qb, qb), I16),
            pltpu.VMEM((nb // 2, 2 * qb, qb), I16),
            pltpu.VMEM((nb // 2, 2 * qb, qb), F32),
            pltpu.VMEM((nb // 2, 2 * qb, nh * qb), F32),
        ] + [pltpu.VMEM((dh + ONES_ROWS, 2 * qb), F32)
             for _ in range(nh // 2)],
        compiler_params=_cparams(("arbitrary", "arbitrary")),
        name="dsa_attention",
    )(z, z, z, z, z, z, tb, q_norm.reshape(1, dh), k_norm.reshape(1, dh))


def _dsa_weight(c_w_in, d):
    dh = d // ATT_HEADS
    nqi = IDX_HEADS * IDX_DIM
    q, k, v, qi, ki, wi = jnp.split(
        c_w_in, [d, d + dh, d + 2 * dh, d + 2 * dh + nqi, d + 2 * dh + nqi + IDX_DIM], axis=1)
    used = d + nqi + 2 * dh + IDX_DIM + IDX_HEADS
    total = -(-used // 512) * 512
    pad = jnp.zeros((d, total - used), c_w_in.dtype)
    return jnp.concatenate([q, qi, k, v, ki, wi, pad], axis=1).astype(BF16)


def kernel(x, c, rel_bias, w_router, b_router, norm_mix, ada_w_mix, ada_b_mix, norm_ffn, ada_w_ffn,
           ada_b_ffn, ab_w_in, ab_conv_w, ab_conv_b, ab_w_out, c_w_in, c_q_norm, c_k_norm, c_w_out,
           moe_w1, moe_w3, moe_w2):
    bsz, seq, d = x.shape
    depth = norm_mix.shape[0]
    t = bsz * seq
    ch = d // 2
    tm_out = min(512, seq)
    x2 = x.reshape(t, d)
    mod_mix = ada_mod(c, ada_w_mix, ada_b_mix).reshape(depth, bsz, 3, d)
    mod_ffn = ada_mod(c, ada_w_ffn, ada_b_ffn).reshape(depth, bsz, 3, d)
    experts = (moe_w1, moe_w3, moe_w2)
    as_experts = lambda casts: [cw.reshape(w.shape[1:]) for cw, w in zip(casts, experts)]
    w_in = [ab_w_in[l // 2].astype(BF16) if l % 2 == 0 else _dsa_weight(c_w_in[l // 2], d)
            for l in range(depth)]
    tm_in, tn_in = min(1024, seq), 1024
    z = None
    w_moe = None
    for layer in range(depth):
        i = layer // 2
        if z is None:
            riders = _cast_riders(experts, layer, (t // tm_in) * (w_in[layer].shape[1] // tn_in))
            h_mix = prenorm(x2, mod_mix[layer], norm_mix[layer], seq, tm_out)
            z, *casts = inproj(h_mix, w_in[layer], tm_in, tn_in, riders or ())
            if riders:
                w_moe = as_experts(casts)
        if w_moe is None:
            w_moe = [w[layer].astype(BF16) for w in experts]
        if layer % 2 == 0:
            y_b = retention(z, bsz, seq, ch, 3 * ch)
            lhs, w_out, conv = (z, y_b), ab_w_out[i], (ab_conv_w[i], ab_conv_b[i])
        else:
            o = dsa_attention(z, rel_bias, c_q_norm[i], c_k_norm[i], bsz, seq, d)
            lhs, w_out, conv = (o,), c_w_out[i], None
        x2, h, ri, rf, cnt = mixer_out(lhs, w_out.astype(BF16), x2, mod_mix[layer], mod_ffn[layer],
                                       norm_ffn[layer], w_router, b_router, bsz, seq, tm_out, conv)
        last = layer + 1 == depth
        nxt = None if last else (mod_mix[layer + 1], norm_mix[layer + 1], w_in[layer + 1])
        riders = None if last else _cast_riders(experts, layer + 1, _moe_blocks(t))
        res, casts = moe(x2, h, ri, rf, cnt, mod_ffn[layer], *w_moe, seq, nxt, riders or ())
        x2, z = (res[0], None) if last else (res[0], res[1])
        w_moe = as_experts(casts) if riders else None
    return x2.reshape(bsz, seq, d)
```

```python
import functools
import math
from typing import NamedTuple

import jax
import jax.numpy as jnp
from jax import lax
from jax.experimental import pallas as pl
from jax.experimental.pallas import tpu as pltpu

F32 = jnp.float32
BF16 = jnp.bfloat16
I32 = jnp.int32
I16 = jnp.int16

RMS_EPS = 1e-6
CONV_WIDTH = 3
RET_HEADS = 4
ATT_HEADS = 16
IDX_HEADS = 16
IDX_DIM = 64
DSA_TOPK_MAX = 256
Q_BLOCK = 128
N_BUCKETS = 32
MAX_DISTANCE = 128
N_EXPERTS = 16
N_GROUPS = 4
EXPERTS_PER_GROUP = N_EXPERTS // N_GROUPS
RET_CHUNK = 256
MOE_ROWS = 256
DMA_UNROLL = 8
RIDER_BLOCK_BYTES = 1024 * 1024
ONES_ROWS = 16
LANE = 128
SUBLANE = 8
VMEM_LIMIT = 48 * 1024 * 1024
NEG_BIG = -1e30
INT_MIN = -(2 ** 31)
I16_MIN = -(2 ** 15)
HIGH_HALF = -(2 ** 16)
PACK16 = 16


def _cparams(sem, vmem=VMEM_LIMIT):
    return pltpu.CompilerParams(dimension_semantics=sem, vmem_limit_bytes=vmem)


def _dot(a, b):
    return jnp.dot(a, b, preferred_element_type=F32)


def _dot_nt(a, b):
    return lax.dot_general(a, b, (((1,), (1,)), ((), ())), preferred_element_type=F32)


def _dot_tn(a, b):
    return lax.dot_general(a, b, (((0,), (0,)), ((), ())), preferred_element_type=F32)


def _silu(x):
    return x * jax.nn.sigmoid(x)


def _pack_pairs(x):
    n = x.shape[1] // 2
    xb = x.astype(BF16).astype(F32)
    lo = pltpu.bitcast(xb[:, :n], I32)
    hi = pltpu.bitcast(xb[:, n:], I32)
    return lax.shift_right_logical(lo, 16) | (hi & HIGH_HALF)


def _unpack_pairs(u):
    lo = pltpu.bitcast(u << 16, F32)
    hi = pltpu.bitcast(u & HIGH_HALF, F32)
    return jnp.concatenate([lo, hi], axis=1).astype(BF16)


class _CastRider(NamedTuple):
    src: jax.Array
    rows: int
    first_blk: int
    nblk: int

    def in_spec(self, flat):
        return pl.BlockSpec((self.rows, self.src.shape[1]),
                            lambda *g: (jnp.minimum(flat(*g), self.nblk - 1) + self.first_blk, 0))

    def out_spec(self, flat):
        return pl.BlockSpec((self.rows, self.src.shape[1]),
                            lambda *g: (jnp.minimum(flat(*g), self.nblk - 1), 0))

    @property
    def out_shape(self):
        return jax.ShapeDtypeStruct((self.rows * self.nblk, self.src.shape[1]), BF16)


def _cast_riders(weights, layer, n_steps):
    riders = []
    for w in weights:
        per_layer, cols = w.shape[1] * w.shape[2], w.shape[3]
        rows = PACK16
        while rows * n_steps < per_layer:
            rows *= 2
        if per_layer % rows or rows * cols * 4 > RIDER_BLOCK_BYTES:
            return None
        nblk = per_layer // rows
        riders.append(_CastRider(w.reshape(-1, cols), rows, layer * nblk, nblk))
    return riders


def _ride_casts(nblks, src_refs, dst_refs, step):
    for nblk, src, dst in zip(nblks, src_refs, dst_refs):
        @pl.when(step < nblk)
        def _():
            dst[...] = src[...].astype(BF16)


def _ada_kernel(c_ref, w_ref, b_ref, o_ref):
    cs = _silu(c_ref[...])
    o_ref[0] = _dot(cs.astype(BF16), w_ref[0].astype(BF16)) + b_ref[0]


def ada_mod(c, ada_w, ada_b):
    nl, d, n3 = ada_w.shape
    bsz = c.shape[0]
    tn = min(512, n3)
    return pl.pallas_call(
        _ada_kernel,
        grid=(nl, n3 // tn),
        in_specs=[
            pl.BlockSpec((bsz, d), lambda l, j: (0, 0)),
            pl.BlockSpec((1, d, tn), lambda l, j: (l, 0, j)),
            pl.BlockSpec((1, 1, tn), lambda l, j: (l, 0, j)),
        ],
        out_specs=pl.BlockSpec((1, bsz, tn), lambda l, j: (l, 0, j)),
        out_shape=jax.ShapeDtypeStruct((nl, bsz, n3), F32),
        compiler_params=_cparams(("parallel", "parallel")),
        name="ada_mod",
    )(c, ada_w, ada_b.reshape(nl, 1, n3))


def _modulated_norm(x, mod_ref, nw_ref):
    r = lax.rsqrt(jnp.mean(x * x, axis=-1, keepdims=True) + RMS_EPS)
    return x * r * nw_ref[...] * (1.0 + mod_ref[0, 1:2, :]) + mod_ref[0, 0:1, :]


def _prenorm_kernel(x_ref, mod_ref, nw_ref, h_ref):
    h_ref[...] = _modulated_norm(x_ref[...], mod_ref, nw_ref).astype(h_ref.dtype)


def prenorm(x2, mod, norm_w, seq, tm):
    t, d = x2.shape
    per = seq // tm
    return pl.pallas_call(
        _prenorm_kernel,
        grid=(t // tm,),
        in_specs=[
            pl.BlockSpec((tm, d), lambda i: (i, 0)),
            pl.BlockSpec((1, 3, d), lambda i: (i // per, 0, 0)),
            pl.BlockSpec((1, d), lambda i: (0, 0)),
        ],
        out_specs=pl.BlockSpec((tm, d), lambda i: (i, 0)),
        out_shape=jax.ShapeDtypeStruct((t, d), BF16),
        compiler_params=_cparams(("parallel",)),
        name="prenorm",
    )(x2, mod, norm_w.reshape(1, d))


def _inproj_kernel(h_ref, w_ref, *refs, riders):
    o_ref = refs[len(riders)]
    step = pl.program_id(0) * pl.num_programs(1) + pl.program_id(1)
    _ride_casts(riders, refs[:len(riders)], refs[len(riders) + 1:], step)
    o_ref[...] = _dot(h_ref[...], w_ref[...]).astype(o_ref.dtype)


def inproj(h_bf, w_bf, tm, tn, riders=()):
    t, d = h_bf.shape
    n = w_bf.shape[1]
    nj = n // tn
    flat = lambda i, j: i * nj + j
    return pl.pallas_call(
        functools.partial(_inproj_kernel, riders=tuple(r.nblk for r in riders)),
        grid=(t // tm, nj),
        in_specs=[
            pl.BlockSpec((tm, d), lambda i, j: (i, 0)),
            pl.BlockSpec((d, tn), lambda i, j: (0, j)),
        ] + [r.in_spec(flat) for r in riders],
        out_specs=[pl.BlockSpec((tm, tn), lambda i, j: (i, j))] + [r.out_spec(flat) for r in riders],
        out_shape=[jax.ShapeDtypeStruct((t, n), BF16)] + [r.out_shape for r in riders],
        compiler_params=_cparams(("arbitrary", "arbitrary")),
        name="inproj",
    )(h_bf, w_bf, *[r.src for r in riders])


def _gated_conv(b_ref, c_ref, v_ref, w_ref, cb_ref, u_ref, s, tm):
    u = c_ref[...].astype(F32) * v_ref[...].astype(F32)

    @pl.when(s == 0)
    def _():
        u_ref[0:SUBLANE, :] = jnp.zeros((SUBLANE, u.shape[1]), F32)

    @pl.when(s > 0)
    def _():
        u_ref[0:SUBLANE, :] = u_ref[tm:tm + SUBLANE, :]

    u_ref[SUBLANE:SUBLANE + tm, :] = u
    conv = (cb_ref[...]
            + u_ref[SUBLANE - 2:SUBLANE - 2 + tm, :] * w_ref[0:1, :]
            + u_ref[SUBLANE - 1:SUBLANE - 1 + tm, :] * w_ref[1:2, :]
            + u * w_ref[2:3, :])
    return b_ref[...].astype(F32) * conv


def _retention_kernel(q_ref, k_ref, v_ref, g_ref, cos_ref, sin_ref, din_ref, dcr_ref, dst_ref,
                      o_ref, *st_refs, nc, ck, dk):
    half = dk // 2
    for st_ref in st_refs:
        st_ref[...] = jnp.zeros(st_ref.shape, F32)

    def rot(x, cos, sin):
        x1, x2 = x[:, :half], x[:, half:]
        return jnp.concatenate([x1 * cos - x2 * sin, x1 * sin + x2 * cos], axis=-1)

    def body(c, carry):
        r0 = pl.multiple_of(c * ck, ck)
        cos = cos_ref[pl.ds(r0, ck), :]
        sin = sin_ref[pl.ds(r0, ck), :]
        for h, st_ref in enumerate(st_refs):
            cols = slice(h * dk, (h + 1) * dk)
            d_cross = dcr_ref[h]
            g_chunk = dcr_ref[h, ck - 1:ck, :]
            q = rot(q_ref[pl.ds(r0, ck), cols].astype(F32), cos, sin)
            k = rot(k_ref[pl.ds(r0, ck), cols].astype(F32), cos, sin) * (dk ** -0.5)
            v = v_ref[pl.ds(r0, ck), cols]
            intra = _dot_nt(q.astype(BF16), k.astype(BF16)) * din_ref[h]
            state = st_ref[...]
            o = _dot(intra.astype(BF16), v) + _dot((q * d_cross).astype(BF16), state.astype(BF16))
            st_ref[...] = state * g_chunk + _dot_tn((k * dst_ref[h]).astype(BF16), v)
            r = lax.rsqrt(jnp.mean(o * o, axis=-1, keepdims=True) + RMS_EPS)
            gate = _silu(g_ref[pl.ds(r0, ck), cols].astype(F32))
            o_ref[pl.ds(r0, ck), cols] = (o * r * gate).astype(o_ref.dtype)
        return carry

    lax.fori_loop(0, nc, body, 0)


def retention(z, bsz, seq, ch, col0):
    t = z.shape[0]
    nh = RET_HEADS
    dk = ch // nh
    ck = min(RET_CHUNK, seq)
    nc = seq // ck
    half = dk // 2
    pos = jnp.arange(seq, dtype=F32)
    inv = 1.0 / (10000.0 ** jnp.linspace(0.0, 1.0, half, dtype=F32))
    ang = pos[:, None] * inv[None, :]
    cos, sin = jnp.cos(ang), jnp.sin(ang)
    log_g = jnp.log(1.0 - 2.0 ** (-5.0 - jnp.arange(nh, dtype=F32)))
    i = jnp.arange(ck, dtype=F32)
    diff = i[:, None] - i[None, :]
    d_intra = jnp.where(diff >= 0, jnp.exp(log_g[:, None, None] * jnp.maximum(diff, 0.0)), 0.0)
    d_cross = jnp.exp(log_g[:, None] * (i[None, :] + 1.0))[..., None]
    d_state = jnp.exp(log_g[:, None] * (ck - 1.0 - i[None, :]))[..., None]
    cb = col0 // ch
    col = lambda g: (lambda b: (b, cb + g))
    whole = lambda b: (0, 0, 0)
    return pl.pallas_call(
        functools.partial(_retention_kernel, nc=nc, ck=ck, dk=dk),
        grid=(bsz,),
        in_specs=[
            pl.BlockSpec((seq, ch), col(0)),
            pl.BlockSpec((seq, ch), col(1)),
            pl.BlockSpec((seq, ch), col(2)),
            pl.BlockSpec((seq, ch), col(3)),
            pl.BlockSpec((seq, half), lambda b: (0, 0)),
            pl.BlockSpec((seq, half), lambda b: (0, 0)),
            pl.BlockSpec((nh, ck, ck), whole),
            pl.BlockSpec((nh, ck, 1), whole),
            pl.BlockSpec((nh, ck, 1), whole),
        ],
        out_specs=pl.BlockSpec((seq, ch), lambda b: (b, 0)),
        out_shape=jax.ShapeDtypeStruct((t, ch), BF16),
        scratch_shapes=[pltpu.VMEM((dk, dk), F32) for _ in range(nh)],
        compiler_params=_cparams(("parallel",)),
        name="retention",
    )(z, z, z, z, cos, sin, d_intra, d_cross, d_state)


def _route(h, wr_ref, br_ref, ri_ref, rf_ref, cnt_ref, carry_ref, tm):
    logits = _dot_nt(wr_ref[...], h.astype(BF16))
    mx = jnp.max(logits, axis=0, keepdims=True)
    ex = jnp.exp(logits - mx)
    probs = ex / jnp.sum(ex, axis=0, keepdims=True)
    sel = probs + br_ref[...]
    s = [sel[e:e + 1, :] for e in range(N_EXPERTS)]
    p = [probs[e:e + 1, :] for e in range(N_EXPERTS)]
    epg = EXPERTS_PER_GROUP

    def first_argmax(vals, exclude=None):
        best = jnp.full_like(vals[0], -jnp.inf)
        idx = jnp.zeros(vals[0].shape, I32)
        for j, vj in enumerate(vals):
            better = vj > best
            if exclude is not None:
                better = better & (exclude != j)
            idx = jnp.where(better, j, idx)
            best = jnp.where(better, vj, best)
        return idx

    gscore = []
    for g in range(N_GROUPS):
        gs = s[g * epg:(g + 1) * epg]
        best = None
        for a in range(epg):
            for b in range(a + 1, epg):
                pair = gs[a] + gs[b]
                best = pair if best is None else jnp.maximum(best, pair)
        gscore.append(best)
    grp = first_argmax(gscore)

    def pick(rows, index, n):
        out = rows[n - 1]
        for j in range(n - 2, -1, -1):
            out = jnp.where(index == j, rows[j], out)
        return out

    in_s = [pick([s[g * epg + j] for g in range(N_GROUPS)], grp, N_GROUPS) for j in range(epg)]
    in_p = [pick([p[g * epg + j] for g in range(N_GROUPS)], grp, N_GROUPS) for j in range(epg)]
    i1 = first_argmax(in_s)
    i2 = first_argmax(in_s, exclude=i1)
    p1 = pick(in_p, i1, epg)
    p2 = pick(in_p, i2, epg)
    e1 = grp * epg + i1
    e2 = grp * epg + i2
    den = p1 + p2
    g1 = p1 / den
    g2 = p2 / den

    eidx = lax.broadcasted_iota(I32, (N_EXPERTS, tm), 0)
    member = (eidx == e1) | (eidx == e2)
    member_f = jnp.where(member, 1.0, 0.0)
    before = lax.broadcasted_iota(I32, (tm, tm), 0) < lax.broadcasted_iota(I32, (tm, tm), 1)
    prefix = _dot(member_f.astype(BF16), jnp.where(before, 1.0, 0.0).astype(BF16))
    base = prefix + carry_ref[:, 0:1]
    rank1 = jnp.sum(jnp.where(eidx == e1, base, 0.0), axis=0, keepdims=True).astype(I32)
    rank2 = jnp.sum(jnp.where(eidx == e2, base, 0.0), axis=0, keepdims=True).astype(I32)
    carry_ref[...] = carry_ref[...] + jnp.sum(member_f, axis=1, keepdims=True)
    cnt_ref[...] = carry_ref[...].astype(I32)

    zi = jnp.zeros((SUBLANE - 4, tm), I32)
    ri_ref[...] = jnp.concatenate([e1, e2, rank1, rank2, zi], axis=0)
    zf = jnp.zeros((SUBLANE - 2, tm), F32)
    rf_ref[...] = jnp.concatenate([g1, g2, zf], axis=0)


def _mixer_out_kernel(*refs, tm, conv):
    if conv:
        b_ref, c_ref, v_ref, cw_ref, cb_ref, *refs = refs
    else:
        ya_ref, *refs = refs
    (yb_ref, wa_ref, wb_ref, x_ref, modm_ref, modf_ref, nw_ref, wr_ref, br_ref,
     x1_ref, h_ref, ri_ref, rf_ref, cnt_ref, carry_ref, *conv_scratch) = refs
    s = pl.program_id(1)

    @pl.when((pl.program_id(0) == 0) & (s == 0))
    def _():
        carry_ref[...] = jnp.zeros(carry_ref.shape, F32)

    if conv:
        ya = _gated_conv(b_ref, c_ref, v_ref, cw_ref, cb_ref, conv_scratch[0], s, tm).astype(BF16)
    else:
        ya = ya_ref[...]
    y = _dot(ya, wa_ref[...]) + _dot(yb_ref[...], wb_ref[...])
    x1 = x_ref[...] + modm_ref[0, 2:3, :] * y
    x1_ref[...] = x1
    h = _modulated_norm(x1, modf_ref, nw_ref)
    h_ref[...] = _pack_pairs(h)
    _route(h, wr_ref, br_ref, ri_ref, rf_ref, cnt_ref, carry_ref, tm)


def mixer_out(lhs, w_bf, x2, mod_mix, mod_ffn, norm_ffn_w, w_router, b_router, bsz, seq, tm, conv=None):
    t, d = x2.shape
    kh = w_bf.shape[0] // 2
    per = seq // tm
    ne = N_EXPERTS
    row = lambda g: (lambda b, s: (b * per + s, g))
    const = lambda b, s: (0, 0)
    if conv is None:
        (o,) = lhs
        lhs_args = (o, o)
        lhs_specs = [pl.BlockSpec((tm, kh), row(0)), pl.BlockSpec((tm, kh), row(1))]
        scratch = []
    else:
        z, yb = lhs
        conv_w, conv_b = conv
        lhs_args = (z, z, z, conv_w, conv_b.reshape(1, kh), yb)
        lhs_specs = [pl.BlockSpec((tm, kh), row(0)), pl.BlockSpec((tm, kh), row(1)),
                     pl.BlockSpec((tm, kh), row(2)), pl.BlockSpec((CONV_WIDTH, kh), const),
                     pl.BlockSpec((1, kh), const), pl.BlockSpec((tm, kh), row(0))]
        scratch = [pltpu.VMEM((tm + SUBLANE, kh), F32)]
    batch = lambda b, s: (b, 0, 0)
    return pl.pallas_call(
        functools.partial(_mixer_out_kernel, tm=tm, conv=conv is not None),
        grid=(bsz, per),
        in_specs=lhs_specs + [
            pl.BlockSpec((kh, d), const),
            pl.BlockSpec((kh, d), lambda b, s: (1, 0)),
            pl.BlockSpec((tm, d), row(0)),
            pl.BlockSpec((1, 3, d), batch),
            pl.BlockSpec((1, 3, d), batch),
            pl.BlockSpec((1, d), const),
            pl.BlockSpec((ne, d), const),
            pl.BlockSpec((ne, 1), const),
        ],
        out_specs=[
            pl.BlockSpec((tm, d), row(0)),
            pl.BlockSpec((tm, d // 2), row(0)),
            pl.BlockSpec((SUBLANE, tm), lambda b, s: (0, b * per + s)),
            pl.BlockSpec((SUBLANE, tm), lambda b, s: (0, b * per + s)),
            pl.BlockSpec((ne, LANE), const),
        ],
        out_shape=[
            jax.ShapeDtypeStruct((t, d), F32),
            jax.ShapeDtypeStruct((t, d // 2), I32),
            jax.ShapeDtypeStruct((SUBLANE, t), I32),
            jax.ShapeDtypeStruct((SUBLANE, t), F32),
            jax.ShapeDtypeStruct((ne, LANE), I32),
        ],
        scratch_shapes=[pltpu.VMEM((ne, LANE), F32)] + scratch,
        compiler_params=_cparams(("arbitrary", "arbitrary")),
        name="mixer_out",
    )(*lhs_args, w_bf, w_bf, x2, mod_mix, mod_ffn, norm_ffn_w.reshape(1, d),
      w_router.T.astype(BF16), b_router.reshape(ne, 1))


def _dispatch_kernel(dest_ref, pad0_ref, padn_ref, h_ref, xs_ref, hbuf_ref, zero_ref, sem, zsem,
                     *, tm, nt):
    i = pl.program_id(0)
    cur = i % 2
    base = i * tm

    def zero_row(e, r):
        return pltpu.make_async_copy(zero_ref.at[pl.ds(0, 1), :],
                                     xs_ref.at[pl.ds(pad0_ref[e] + r, 1), :], zsem)

    def tile_done(buf):
        for _ in range(2):
            pltpu.make_async_copy(hbuf_ref.at[buf], xs_ref.at[pl.ds(0, tm), :], sem.at[buf]).wait()

    @pl.when(i == 0)
    def _():
        zero_ref[...] = jnp.zeros(zero_ref.shape, zero_ref.dtype)
        for e in range(N_EXPERTS):
            lax.fori_loop(0, padn_ref[e], lambda r, c, e=e: (zero_row(e, r).start(), c)[1], 0)

    hbuf_ref[cur] = h_ref[...]

    def start(r, carry):
        for slot in range(2):
            d = dest_ref[2 * (base + r) + slot]
            pltpu.make_async_copy(hbuf_ref.at[cur, pl.ds(r, 1), :], xs_ref.at[pl.ds(d, 1), :],
                                  sem.at[cur]).start()
        return carry

    lax.fori_loop(0, tm, start, 0, unroll=DMA_UNROLL)

    @pl.when(i > 0)
    def _():
        tile_done(1 - cur)

    @pl.when(i == nt - 1)
    def _():
        tile_done(cur)
        for e in range(N_EXPERTS):
            lax.fori_loop(0, padn_ref[e], lambda r, c, e=e: (zero_row(e, r).wait(), c)[1], 0)


def dispatch(h, dest, pad_start, pad_count, n_rows, tm):
    t, d = h.shape
    return pl.pallas_call(
        functools.partial(_dispatch_kernel, tm=tm, nt=t // tm),
        grid_spec=pltpu.PrefetchScalarGridSpec(
            num_scalar_prefetch=3,
            grid=(t // tm,),
            in_specs=[pl.BlockSpec((tm, d), lambda i, *_: (i, 0))],
            out_specs=pl.BlockSpec(memory_space=pl.ANY),
            scratch_shapes=[pltpu.VMEM((2, tm, d), h.dtype), pltpu.VMEM((SUBLANE, d), h.dtype),
                            pltpu.SemaphoreType.DMA((2,)), pltpu.SemaphoreType.DMA(())],
        ),
        out_shape=jax.ShapeDtypeStruct((n_rows, d), h.dtype),
        compiler_params=_cparams(("arbitrary",)),
        name="moe_dispatch",
    )(dest, pad_start, pad_count, h)


def _ffn_kernel(be_ref, bv_ref, xs_ref, w1_ref, w3_ref, w2_ref, *refs, riders):
    ys_ref = refs[len(riders)]
    _ride_casts(riders, refs[:len(riders)], refs[len(riders) + 1:], pl.program_id(0))
    valid = bv_ref[pl.program_id(0)]

    @pl.when(valid > 0)
    def _():
        rows = lax.broadcasted_iota(I32, xs_ref.shape, 0)
        x = _unpack_pairs(jnp.where(rows < valid, xs_ref[...], 0))
        h1 = _dot(x, w1_ref[0])
        h3 = _dot(x, w3_ref[0])
        a = (_silu(h1) * h3).astype(BF16)
        ys_ref[...] = _pack_pairs(_dot(a, w2_ref[0]))

    @pl.when(valid <= 0)
    def _():
        ys_ref[...] = jnp.zeros(ys_ref.shape, ys_ref.dtype)


def grouped_ffn(xs, block_e, block_valid, w1, w3, w2, bm, riders=()):
    n_rows = xs.shape[0]
    d, f = w1.shape[1], w1.shape[2]
    flat = lambda i, be, bv: i
    return pl.pallas_call(
        functools.partial(_ffn_kernel, riders=tuple(r.nblk for r in riders)),
        grid_spec=pltpu.PrefetchScalarGridSpec(
            num_scalar_prefetch=2,
            grid=(n_rows // bm,),
            in_specs=[
                pl.BlockSpec((bm, d // 2), lambda i, be, bv: (i, 0)),
                pl.BlockSpec((1, d, f), lambda i, be, bv: (be[i], 0, 0)),
                pl.BlockSpec((1, d, f), lambda i, be, bv: (be[i], 0, 0)),
                pl.BlockSpec((1, f, d), lambda i, be, bv: (be[i], 0, 0)),
            ] + [r.in_spec(flat) for r in riders],
            out_specs=[pl.BlockSpec((bm, d // 2), lambda i, be, bv: (i, 0))] + [r.out_spec(flat) for r in riders],
        ),
        out_shape=[jax.ShapeDtypeStruct((n_rows, d // 2), I32)] + [r.out_shape for r in riders],
        compiler_params=_cparams(("arbitrary",)),
        name="moe_ffn",
    )(block_e, block_valid, xs, w1, w3, w2, *[r.src for r in riders])


def _combine_kernel(dest_ref, ys_ref, x_ref, mod_ref, gf_ref, *refs, tm, nt, project):
    if project:
        modn_ref, nwn_ref, w_ref, o_ref, z_ref, a_ref, b_ref, sem = refs
    else:
        o_ref, a_ref, b_ref, sem = refs
    i = pl.program_id(0)
    cur = i % 2

    def fetch_row(base, r, buf):
        for slot, ref in ((0, a_ref), (1, b_ref)):
            d = dest_ref[2 * (base + r) + slot]
            pltpu.make_async_copy(ys_ref.at[pl.ds(d, 1), :], ref.at[buf, pl.ds(r, 1), :],
                                  sem.at[buf]).start()

    def fetch(tile, buf):
        lax.fori_loop(0, tm, lambda r, c: (fetch_row(tile * tm, r, buf), c)[1], 0, unroll=DMA_UNROLL)

    def fetched(buf):
        pltpu.make_async_copy(ys_ref.at[pl.ds(0, tm), :], a_ref.at[buf], sem.at[buf]).wait()
        pltpu.make_async_copy(ys_ref.at[pl.ds(0, tm), :], b_ref.at[buf], sem.at[buf]).wait()

    @pl.when(i == 0)
    def _():
        fetch(0, 0)

    if not project:
        @pl.when(i + 1 < nt)
        def _():
            fetch(i + 1, 1 - cur)

    fetched(cur)
    y = (gf_ref[:, 0:1] * _unpack_pairs(a_ref[cur]).astype(F32)
         + gf_ref[:, 1:2] * _unpack_pairs(b_ref[cur]).astype(F32))
    out = x_ref[...] + mod_ref[0, 2:3, :] * y
    o_ref[...] = out
    if project:
        h = _modulated_norm(out, modn_ref, nwn_ref).astype(BF16)
        base = jnp.minimum(i + 1, nt - 1) * tm
        for r in range(tm):
            fetch_row(base, r, 1 - cur)
        z_ref[...] = _dot(h, w_ref[...]).astype(z_ref.dtype)

        @pl.when(i == nt - 1)
        def _():
            fetched(1 - cur)


def combine(ys, dest, x2, mod, gates_t, seq, tm, project=None):
    t, d = x2.shape
    per = seq // tm
    row = lambda i, dest: (i, 0)
    batch = lambda i, dest: (i // per, 0, 0)
    const = lambda i, dest: (0, 0)
    in_specs = [pl.BlockSpec(memory_space=pl.ANY), pl.BlockSpec((tm, d), row),
                pl.BlockSpec((1, 3, d), batch), pl.BlockSpec((tm, SUBLANE), row)]
    out_specs = [pl.BlockSpec((tm, d), row)]
    out_shape = [jax.ShapeDtypeStruct((t, d), F32)]
    args = [dest, ys, x2, mod, gates_t]
    if project is not None:
        mod_next, nw_next, w_bf = project
        n = w_bf.shape[1]
        in_specs += [pl.BlockSpec((1, 3, d), batch), pl.BlockSpec((1, d), const), pl.BlockSpec((d, n), const)]
        out_specs += [pl.BlockSpec((tm, n), row)]
        out_shape += [jax.ShapeDtypeStruct((t, n), BF16)]
        args += [mod_next, nw_next.reshape(1, d), w_bf]
    return pl.pallas_call(
        functools.partial(_combine_kernel, tm=tm, nt=t // tm, project=project is not None),
        grid_spec=pltpu.PrefetchScalarGridSpec(
            num_scalar_prefetch=1,
            grid=(t // tm,),
            in_specs=in_specs,
            out_specs=out_specs,
            scratch_shapes=[pltpu.VMEM((2, tm, d // 2), I32), pltpu.VMEM((2, tm, d // 2), I32),
                            pltpu.SemaphoreType.DMA((2,))],
        ),
        out_shape=out_shape,
        compiler_params=_cparams(("arbitrary",)),
        name="moe_combine",
    )(*args)


def _moe_blocks(t):
    return (2 * t) // MOE_ROWS + N_EXPERTS


def moe(x2, h, ri, rf, cnt, mod, w1, w3, w2, seq, project=None, riders=()):
    t, d = x2.shape
    bm = MOE_ROWS
    tm = min(256, seq)
    ne = N_EXPERTS
    counts = cnt[:, 0]
    nblk = (counts + bm - 1) // bm
    blk_end = jnp.cumsum(nblk)
    blk_start = blk_end - nblk
    eids = jnp.arange(ne, dtype=I32)
    e12 = ri[0:2].T
    row0 = jnp.sum(jnp.where(e12[..., None] == eids, blk_start * bm, 0), axis=-1)
    dest = (row0 + ri[2:4].T).reshape(2 * t).astype(I32)
    n_blocks = _moe_blocks(t)
    bidx = jnp.arange(n_blocks, dtype=I32)
    block_e = jnp.minimum(jnp.sum(bidx[:, None] >= blk_end[None, :], axis=1), ne - 1).astype(I32)
    onehot = block_e[:, None] == eids
    cnt_b = jnp.sum(jnp.where(onehot, counts, 0), axis=1)
    start_b = jnp.sum(jnp.where(onehot, blk_start, 0), axis=1)
    block_valid = jnp.clip(cnt_b - (bidx - start_b) * bm, 0, bm).astype(I32)
    pad_start = (blk_start * bm + counts).astype(I32)
    pad_end = jnp.where(eids == ne - 1, n_blocks * bm, blk_end * bm)
    xs = dispatch(h, dest, pad_start, (pad_end - pad_start).astype(I32), n_blocks * bm, tm)
    ys, *casts = grouped_ffn(xs, block_e, block_valid, w1, w3, w2, bm, riders)
    return combine(ys, dest, x2, mod, rf.T, seq, tm, project), casts


def _t5_bucket(n):
    max_exact = N_BUCKETS // 2
    nf = jnp.maximum(n, 1).astype(F32)
    large = max_exact + (jnp.log(nf / max_exact) / math.log(MAX_DISTANCE / max_exact)
                         * (N_BUCKETS - max_exact)).astype(I32)
    large = jnp.minimum(large, N_BUCKETS - 1)
    return jnp.where(n < max_exact, n, large)


def _sortable(score):
    bits = pltpu.bitcast(score, I32)
    return bits ^ ((bits >> 31) & 0x7FFFFFFF)


def _dsa_kernel(q_ref, k_ref, v_ref, qi_ref, kw_ref, wq_ref, tb_ref, qn_ref, kn_ref, o_ref,
                khat_ref, vt_ref, qrt_ref, qirt_ref, wt_ref, mt_ref, hi_ref, lo_ref, lo2_ref, nm_ref, lg_ref,
                *acc_refs,
                nh, dh, nih, di, topk, nb):
    j = pl.program_id(1)
    qb = Q_BLOCK
    kc = 2 * qb
    nch = (j + 2) // 2
    pair = 2 * qb

    def transpose(x):
        return x.astype(F32).T

    @pl.when(j == 0)
    def _():
        k = k_ref[...].astype(F32)
        r = lax.rsqrt(jnp.mean(k * k, axis=-1, keepdims=True) + RMS_EPS)
        khat_ref[...] = (k * r * kn_ref[...]).astype(BF16)
        for c in range(nb // 2):
            for s in range(2):
                blk = v_ref[(2 * c + s) * qb:(2 * c + s + 1) * qb, :]
                vt_ref[c, 0:dh, s * qb:(s + 1) * qb] = transpose(blk).astype(BF16)
            vt_ref[c, dh:dh + ONES_ROWS, :] = jnp.ones((ONES_ROWS, kc), BF16)
        for ref in (hi_ref, lo_ref, lo2_ref):
            ref[...] = jnp.full(ref.shape, I16_MIN, I16)

    qscale = (dh ** -0.5) * math.log2(math.e)
    for h in range(nh):
        qh = q_ref[:, h * dh:(h + 1) * dh].astype(F32)
        r = lax.rsqrt(jnp.mean(qh * qh, axis=-1, keepdims=True) + RMS_EPS)
        qhat = (qh * r * qn_ref[...] * qscale).astype(BF16)
        qrt_ref[:, h * qb:(h + 1) * qb] = transpose(qhat).astype(BF16)
    for g in range(nih * di // LANE):
        two = transpose(qi_ref[:, g * LANE:(g + 1) * LANE])
        for s in range(LANE // di):
            h = g * (LANE // di) + s
            qirt_ref[:, h * qb:(h + 1) * qb] = two[s * di:(s + 1) * di, :].astype(BF16)
    wt_ref[...] = transpose(wq_ref[...])

    key_l = lax.broadcasted_iota(I32, (kc, qb), 0)
    q_pos = j * qb + lax.broadcasted_iota(I32, (kc, qb), 1)

    def score_body(c, carry):
        r0 = pl.multiple_of(c * kc, kc)
        ki = kw_ref[pl.ds(r0, kc), 0:di]
        acc = jnp.zeros((kc, qb), F32)
        for g in range(nih // 2):
            rel = _dot(ki, qirt_ref[:, g * pair:(g + 1) * pair])
            for s in range(2):
                h = 2 * g + s
                acc = acc + wt_ref[di + h:di + h + 1, :] * jnp.maximum(rel[:, s * qb:(s + 1) * qb], 0.0)
        m = jnp.where(r0 + key_l <= q_pos, _sortable(acc), INT_MIN)
        mt_ref[c] = m
        hi_ref[c] = (m >> 16).astype(I16)
        lo_ref[c] = ((m & 0xFFFF) + I16_MIN).astype(I16)
        return carry

    lax.fori_loop(0, nch, score_body, 0)

    def count_ge(cand):
        def cbody(c, acc):
            hit = jnp.where(mt_ref[c] >= cand, 1, 0)
            return acc + jnp.sum(hit.reshape(kc // SUBLANE, SUBLANE, qb), axis=0)
        acc = lax.fori_loop(0, nch, cbody, jnp.zeros((SUBLANE, qb), I32))
        return jnp.sum(acc, axis=0, keepdims=True)

    def fold16(hit):
        parts = [hit[PACK16 * i:PACK16 * (i + 1), :] for i in range(kc // PACK16)]
        while len(parts) > 1:
            parts = [parts[i] + parts[i + 1] for i in range(0, len(parts), 2)]
        return parts[0]

    def total16(acc):
        return jnp.sum(acc.astype(F32), axis=0, keepdims=True).astype(I32)

    one, nil = jnp.ones((), BF16), jnp.zeros((), BF16)

    def count16(src_ref, cand, n):
        c16 = cand.astype(I16)
        acc = fold16(jnp.where(src_ref[0] >= c16, one, nil))
        for c in range(1, n):
            acc = acc + fold16(jnp.where(src_ref[c] >= c16, one, nil))
        return total16(acc)

    def search16(src_ref, kth, n):
        zero = jnp.zeros((1, qb), I32)
        ans0 = jnp.where(count16(src_ref, zero, n) >= kth, zero, I16_MIN)

        def bit_body(bi, ans):
            cand = ans | (1 << (14 - bi))
            return jnp.where(count16(src_ref, cand, n) >= kth, cand, ans)

        return lax.fori_loop(0, 15, bit_body, ans0)

    def search(n):
        top = search16(hi_ref, jnp.full((1, qb), topk, I32), n)
        top16 = top.astype(I16)
        acc = jnp.zeros((PACK16, qb), BF16)
        for c in range(n):
            hi = hi_ref[c]
            lo2_ref[c] = jnp.where(hi == top16, lo_ref[c], jnp.full((), I16_MIN, I16))
            acc = acc + fold16(jnp.where(hi > top16, one, nil))
        above = total16(acc)
        low = search16(lo2_ref, topk - above, n)
        return top * 65536 + (low - I16_MIN), above + count16(lo2_ref, low, n)

    def search_any():
        return lax.cond(nch <= nb // 4, functools.partial(search, nb // 4),
                        functools.partial(search, nb // 2))

    selecting = (j + 1) * qb > topk
    thr, n_ge = lax.cond(selecting, search_any,
                         lambda: (jnp.full((1, qb), INT_MIN + 1, I32), jnp.zeros((1, qb), I32)))

    def plain_mask():
        def body(c, carry):
            nm_ref[c] = jnp.where(mt_ref[c] >= thr, 0.0, NEG_BIG)
            return carry
        lax.fori_loop(0, nch, body, 0)

    def tie_mask():
        need = (topk - count_ge(thr + 1)).astype(F32)
        tri = jnp.where(lax.broadcasted_iota(I32, (kc, kc), 1) <= lax.broadcasted_iota(I32, (kc, kc), 0),
                        1.0, 0.0).astype(BF16)

        def body(c, seen):
            m = mt_ref[c]
            tie = m == thr
            upto = _dot(tri, jnp.where(tie, 1.0, 0.0).astype(BF16)) + seen
            nm_ref[c] = jnp.where((m > thr) | (tie & (upto <= need)), 0.0, NEG_BIG)
            return upto[kc - 1:kc, :]
        lax.fori_loop(0, nch, body, jnp.zeros((1, qb), F32))

    crowded = jnp.max(n_ge) > topk
    lax.cond(crowded, tie_mask, plain_mask)

    npair = nh // 2
    pcols = [slice(pr * pair, (pr + 1) * pair) for pr in range(npair)]
    nfar = jnp.maximum((j - 1) // 2, 0)

    def sweep1(c, mruns, near):
        r0 = pl.multiple_of(c * kc, kc)
        kh = khat_ref[pl.ds(r0, kc), :]
        nm = nm_ref[c]
        nm2 = jnp.concatenate([nm, nm], axis=1)
        if near:
            k0 = jnp.clip(j - 2 * c, 0, 2)
            k1 = jnp.clip(j - 2 * c - 1, 0, 2)
        out = []
        for pr in range(npair):
            lg = _dot(kh, qrt_ref[:, pcols[pr]]) + nm2
            if near:
                lg = lg + jnp.concatenate([tb_ref[k0, :, pcols[pr]], tb_ref[k1, :, pcols[pr]]], axis=0)
            lg_ref[c, :, pcols[pr]] = lg
            out.append(jnp.maximum(mruns[pr], jnp.max(lg.reshape(kc // SUBLANE, SUBLANE, pair), axis=0)))
        return tuple(out)

    mruns = tuple(jnp.full((SUBLANE, pair), NEG_BIG, F32) for _ in range(npair))
    mruns = lax.fori_loop(0, nfar, functools.partial(sweep1, near=False), mruns)
    mruns = lax.fori_loop(nfar, nch, functools.partial(sweep1, near=True), mruns)
    mrows = [jnp.max(m, axis=0, keepdims=True) for m in mruns]
    for acc_ref in acc_refs:
        acc_ref[...] = jnp.zeros(acc_ref.shape, F32)

    def sweep2(c, carry):
        vt = vt_ref[c]
        for pr in range(npair):
            p = jnp.exp2((lg_ref[c, :, pcols[pr]] - mrows[pr]).astype(BF16))
            acc_refs[pr][...] += _dot(vt, p)
        return carry

    lax.fori_loop(0, nch, sweep2, 0)
    for pr in range(npair):
        out = acc_refs[pr][0:dh, :] / acc_refs[pr][dh:dh + 1, :]
        for s in range(2):
            h = 2 * pr + s
            o_ref[:, h * dh:(h + 1) * dh] = transpose(
                out[:, s * qb:(s + 1) * qb].astype(BF16)).astype(o_ref.dtype)


def dsa_attention(z, rel_bias, q_norm, k_norm, bsz, seq, d):
    t = z.shape[0]
    nh, dh, nih, di = ATT_HEADS, d // ATT_HEADS, IDX_HEADS, IDX_DIM
    qb = Q_BLOCK
    nb = seq // qb
    topk = min(DSA_TOPK_MAX, seq // 4)
    assert topk % qb == 0 and dh == LANE and nih * di == d // 2 and nb % 2 == 0 and LANE % di == 0
    assert (nb // 2) * (2 * qb // PACK16) <= 256
    i = jnp.arange(qb)
    dist = i[None, :] - i[:, None]
    def lookup(bucket):
        hot = bucket[..., None] == jnp.arange(N_BUCKETS, dtype=I32)
        return jnp.sum(jnp.where(hot[..., None], rel_bias, 0.0), axis=-2)

    tabs = [lookup(_t5_bucket(jnp.maximum(dist + off * qb, 0))) for off in range(2)]
    far = jnp.broadcast_to(lookup(_t5_bucket(jnp.full((), 2 * qb, I32))), (qb, qb, nh))
    tb = jnp.stack(tabs + [far]).transpose(0, 1, 3, 2).reshape(3, qb, nh * qb).astype(F32)
    tb = (tb - tb[2:3]) * math.log2(math.e)
    cq = d // LANE
    return pl.pallas_call(
        functools.partial(_dsa_kernel, nh=nh, dh=dh, nih=nih, di=di, topk=topk, nb=nb),
        grid=(bsz, nb),
        in_specs=[
            pl.BlockSpec((qb, d), lambda b, j: (b * nb + j, 0)),
            pl.BlockSpec((seq, LANE), lambda b, j: (b, cq + cq // 2)),
            pl.BlockSpec((seq, LANE), lambda b, j: (b, cq + cq // 2 + 1)),
            pl.BlockSpec((qb, d // 2), lambda b, j: (b * nb + j, 2)),
            pl.BlockSpec((seq, LANE), lambda b, j: (b, cq + cq // 2 + 2)),
            pl.BlockSpec((qb, LANE), lambda b, j: (b * nb + j, cq + cq // 2 + 2)),
            pl.BlockSpec((3, qb, nh * qb), lambda b, j: (0, 0, 0)),
            pl.BlockSpec((1, dh), lambda b, j: (0, 0)),
            pl.BlockSpec((1, dh), lambda b, j: (0, 0)),
        ],
        out_specs=pl.BlockSpec((qb, d), lambda b, j: (b * nb + j, 0)),
        out_shape=jax.ShapeDtypeStruct((t, d), BF16),
        scratch_shapes=[
            pltpu.VMEM((seq, dh), BF16),
            pltpu.VMEM((nb // 2, dh + ONES_ROWS, 2 * qb), BF16),
            pltpu.VMEM((dh, nh * qb), BF16),
            pltpu.VMEM((di, nih * qb), BF16),
            pltpu.VMEM((LANE, qb), F32),
            pltpu.VMEM((nb // 2, 2 * qb, qb), I32),
            pltpu.VMEM((nb // 2, 2 * qb, qb), I16),
            pltpu.VMEM((nb // 2, 2 * qb, qb), I16),
            pltpu.VMEM((nb // 2, 2 * qb, qb), I16),
            pltpu.VMEM((nb // 2, 2 * qb, qb), F32),
            pltpu.VMEM((nb // 2, 2 * qb, nh * qb), F32),
        ] + [pltpu.VMEM((dh + ONES_ROWS, 2 * qb), F32)
             for _ in range(nh // 2)],
        compiler_params=_cparams(("arbitrary", "arbitrary")),
        name="dsa_attention",
    )(z, z, z, z, z, z, tb, q_norm.reshape(1, dh), k_norm.reshape(1, dh))


def _dsa_weight(c_w_in, d):
    dh = d // ATT_HEADS
    nqi = IDX_HEADS * IDX_DIM
    q, k, v, qi, ki, wi = jnp.split(
        c_w_in, [d, d + dh, d + 2 * dh, d + 2 * dh + nqi, d + 2 * dh + nqi + IDX_DIM], axis=1)
    used = d + nqi + 2 * dh + IDX_DIM + IDX_HEADS
    total = -(-used // 512) * 512
    pad = jnp.zeros((d, total - used), c_w_in.dtype)
    return jnp.concatenate([q, qi, k, v, ki, wi, pad], axis=1).astype(BF16)


def kernel(x, c, rel_bias, w_router, b_router, norm_mix, ada_w_mix, ada_b_mix, norm_ffn, ada_w_ffn,
           ada_b_ffn, ab_w_in, ab_conv_w, ab_conv_b, ab_w_out, c_w_in, c_q_norm, c_k_norm, c_w_out,
           moe_w1, moe_w3, moe_w2):
    bsz, seq, d = x.shape
    depth = norm_mix.shape[0]
    t = bsz * seq
    ch = d // 2
    tm_out = min(512, seq)
    x2 = x.reshape(t, d)
    mod_mix = ada_mod(c, ada_w_mix, ada_b_mix).reshape(depth, bsz, 3, d)
    mod_ffn = ada_mod(c, ada_w_ffn, ada_b_ffn).reshape(depth, bsz, 3, d)
    experts = (moe_w1, moe_w3, moe_w2)
    as_experts = lambda casts: [cw.reshape(w.shape[1:]) for cw, w in zip(casts, experts)]
    w_in = [ab_w_in[l // 2].astype(BF16) if l % 2 == 0 else _dsa_weight(c_w_in[l // 2], d)
            for l in range(depth)]
    tm_in, tn_in = min(1024, seq), 1024
    z = None
    w_moe = None
    for layer in range(depth):
        i = layer // 2
        if z is None:
            riders = _cast_riders(experts, layer, (t // tm_in) * (w_in[layer].shape[1] // tn_in))
            h_mix = prenorm(x2, mod_mix[layer], norm_mix[layer], seq, tm_out)
            z, *casts = inproj(h_mix, w_in[layer], tm_in, tn_in, riders or ())
            if riders:
                w_moe = as_experts(casts)
        if w_moe is None:
            w_moe = [w[layer].astype(BF16) for w in experts]
        if layer % 2 == 0:
            y_b = retention(z, bsz, seq, ch, 3 * ch)
            lhs, w_out, conv = (z, y_b), ab_w_out[i], (ab_conv_w[i], ab_conv_b[i])
        else:
            o = dsa_attention(z, rel_bias, c_q_norm[i], c_k_norm[i], bsz, seq, d)
            lhs, w_out, conv = (o,), c_w_out[i], None
        x2, h, ri, rf, cnt = mixer_out(lhs, w_out.astype(BF16), x2, mod_mix[layer], mod_ffn[layer],
                                       norm_ffn[layer], w_router, b_router, bsz, seq, tm_out, conv)
        last = layer + 1 == depth
        nxt = None if last else (mod_mix[layer + 1], norm_mix[layer + 1], w_in[layer + 1])
        riders = None if last else _cast_riders(experts, layer + 1, _moe_blocks(t))
        res, casts = moe(x2, h, ri, rf, cnt, mod_ffn[layer], *w_moe, seq, nxt, riders or ())
        x2, z = (res[0], None) if last else (res[0], res[1])
        w_moe = as_experts(casts) if riders else None
    return x2.reshape(bsz, seq, d)
```

```python
import functools
import math
from typing import NamedTuple

import jax
import jax.numpy as jnp
from jax import lax
from jax.experimental import pallas as pl
from jax.experimental.pallas import tpu as pltpu

F32 = jnp.float32
BF16 = jnp.bfloat16
I32 = jnp.int32
I16 = jnp.int16

RMS_EPS = 1e-6
CONV_WIDTH = 3
RET_HEADS = 4
ATT_HEADS = 16
IDX_HEADS = 16
IDX_DIM = 64
DSA_TOPK_MAX = 256
Q_BLOCK = 128
N_BUCKETS = 32
MAX_DISTANCE = 128
N_EXPERTS = 16
N_GROUPS = 4
EXPERTS_PER_GROUP = N_EXPERTS // N_GROUPS
RET_CHUNK = 256
MOE_ROWS = 256
DMA_UNROLL = 8
RIDER_BLOCK_BYTES = 1024 * 1024
ONES_ROWS = 16
LANE = 128
SUBLANE = 8
VMEM_LIMIT = 48 * 1024 * 1024
NEG_BIG = -1e30
INT_MIN = -(2 ** 31)
I16_MIN = -(2 ** 15)
HIGH_HALF = -(2 ** 16)
PACK16 = 16


def _cparams(sem, vmem=VMEM_LIMIT):
    return pltpu.CompilerParams(dimension_semantics=sem, vmem_limit_bytes=vmem)


def _dot(a, b):
    return jnp.dot(a, b, preferred_element_type=F32)


def _dot_nt(a, b):
    return lax.dot_general(a, b, (((1,), (1,)), ((), ())), preferred_element_type=F32)


def _dot_tn(a, b):
    return lax.dot_general(a, b, (((0,), (0,)), ((), ())), preferred_element_type=F32)


def _silu(x):
    return x * jax.nn.sigmoid(x)


def _pack_pairs(x):
    n = x.shape[1] // 2
    xb = x.astype(BF16).astype(F32)
    lo = pltpu.bitcast(xb[:, :n], I32)
    hi = pltpu.bitcast(xb[:, n:], I32)
    return lax.shift_right_logical(lo, 16) | (hi & HIGH_HALF)


def _unpack_pairs(u):
    lo = pltpu.bitcast(u << 16, F32)
    hi = pltpu.bitcast(u & HIGH_HALF, F32)
    return jnp.concatenate([lo, hi], axis=1).astype(BF16)


class _CastRider(NamedTuple):
    src: jax.Array
    rows: int
    first_blk: int
    nblk: int

    def in_spec(self, flat):
        return pl.BlockSpec((self.rows, self.src.shape[1]),
                            lambda *g: (jnp.minimum(flat(*g), self.nblk - 1) + self.first_blk, 0))

    def out_spec(self, flat):
        return pl.BlockSpec((self.rows, self.src.shape[1]),
                            lambda *g: (jnp.minimum(flat(*g), self.nblk - 1), 0))

    @property
    def out_shape(self):
        return jax.ShapeDtypeStruct((self.rows * self.nblk, self.src.shape[1]), BF16)


def _cast_riders(weights, layer, n_steps):
    riders = []
    for w in weights:
        per_layer, cols = w.shape[1] * w.shape[2], w.shape[3]
        rows = PACK16
        while rows * n_steps < per_layer:
            rows *= 2
        if per_layer % rows or rows * cols * 4 > RIDER_BLOCK_BYTES:
            return None
        nblk = per_layer // rows
        riders.append(_CastRider(w.reshape(-1, cols), rows, layer * nblk, nblk))
    return riders


def _ride_casts(nblks, src_refs, dst_refs, step):
    for nblk, src, dst in zip(nblks, src_refs, dst_refs):
        @pl.when(step < nblk)
        def _():
            dst[...] = src[...].astype(BF16)


def _ada_kernel(c_ref, w_ref, b_ref, o_ref):
    cs = _silu(c_ref[...])
    o_ref[0] = _dot(cs.astype(BF16), w_ref[0].astype(BF16)) + b_ref[0]


def ada_mod(c, ada_w, ada_b):
    nl, d, n3 = ada_w.shape
    bsz = c.shape[0]
    tn = min(512, n3)
    return pl.pallas_call(
        _ada_kernel,
        grid=(nl, n3 // tn),
        in_specs=[
            pl.BlockSpec((bsz, d), lambda l, j: (0, 0)),
            pl.BlockSpec((1, d, tn), lambda l, j: (l, 0, j)),
            pl.BlockSpec((1, 1, tn), lambda l, j: (l, 0, j)),
        ],
        out_specs=pl.BlockSpec((1, bsz, tn), lambda l, j: (l, 0, j)),
        out_shape=jax.ShapeDtypeStruct((nl, bsz, n3), F32),
        compiler_params=_cparams(("parallel", "parallel")),
        name="ada_mod",
    )(c, ada_w, ada_b.reshape(nl, 1, n3))


def _modulated_norm(x, mod_ref, nw_ref):
    r = lax.rsqrt(jnp.mean(x * x, axis=-1, keepdims=True) + RMS_EPS)
    return x * r * nw_ref[...] * (1.0 + mod_ref[0, 1:2, :]) + mod_ref[0, 0:1, :]


def _prenorm_kernel(x_ref, mod_ref, nw_ref, h_ref):
    h_ref[...] = _modulated_norm(x_ref[...], mod_ref, nw_ref).astype(h_ref.dtype)


def prenorm(x2, mod, norm_w, seq, tm):
    t, d = x2.shape
    per = seq // tm
    return pl.pallas_call(
        _prenorm_kernel,
        grid=(t // tm,),
        in_specs=[
            pl.BlockSpec((tm, d), lambda i: (i, 0)),
            pl.BlockSpec((1, 3, d), lambda i: (i // per, 0, 0)),
            pl.BlockSpec((1, d), lambda i: (0, 0)),
        ],
        out_specs=pl.BlockSpec((tm, d), lambda i: (i, 0)),
        out_shape=jax.ShapeDtypeStruct((t, d), BF16),
        compiler_params=_cparams(("parallel",)),
        name="prenorm",
    )(x2, mod, norm_w.reshape(1, d))


def _inproj_kernel(h_ref, w_ref, *refs, riders):
    o_ref = refs[len(riders)]
    step = pl.program_id(0) * pl.num_programs(1) + pl.program_id(1)
    _ride_casts(riders, refs[:len(riders)], refs[len(riders) + 1:], step)
    o_ref[...] = _dot(h_ref[...], w_ref[...]).astype(o_ref.dtype)


def inproj(h_bf, w_bf, tm, tn, riders=()):
    t, d = h_bf.shape
    n = w_bf.shape[1]
    nj = n // tn
    flat = lambda i, j: i * nj + j
    return pl.pallas_call(
        functools.partial(_inproj_kernel, riders=tuple(r.nblk for r in riders)),
        grid=(t // tm, nj),
        in_specs=[
            pl.BlockSpec((tm, d), lambda i, j: (i, 0)),
            pl.BlockSpec((d, tn), lambda i, j: (0, j)),
        ] + [r.in_spec(flat) for r in riders],
        out_specs=[pl.BlockSpec((tm, tn), lambda i, j: (i, j))] + [r.out_spec(flat) for r in riders],
        out_shape=[jax.ShapeDtypeStruct((t, n), BF16)] + [r.out_shape for r in riders],
        compiler_params=_cparams(("arbitrary", "arbitrary")),
        name="inproj",
    )(h_bf, w_bf, *[r.src for r in riders])


def _gated_conv(b_ref, c_ref, v_ref, w_ref, cb_ref, u_ref, s, tm):
    u = c_ref[...].astype(F32) * v_ref[...].astype(F32)

    @pl.when(s == 0)
    def _():
        u_ref[0:SUBLANE, :] = jnp.zeros((SUBLANE, u.shape[1]), F32)

    @pl.when(s > 0)
    def _():
        u_ref[0:SUBLANE, :] = u_ref[tm:tm + SUBLANE, :]

    u_ref[SUBLANE:SUBLANE + tm, :] = u
    conv = (cb_ref[...]
            + u_ref[SUBLANE - 2:SUBLANE - 2 + tm, :] * w_ref[0:1, :]
            + u_ref[SUBLANE - 1:SUBLANE - 1 + tm, :] * w_ref[1:2, :]
            + u * w_ref[2:3, :])
    return b_ref[...].astype(F32) * conv


def _retention_kernel(q_ref, k_ref, v_ref, g_ref, cos_ref, sin_ref, din_ref, dcr_ref, dst_ref,
                      o_ref, *st_refs, nc, ck, dk):
    half = dk // 2
    for st_ref in st_refs:
        st_ref[...] = jnp.zeros(st_ref.shape, F32)

    def rot(x, cos, sin):
        x1, x2 = x[:, :half], x[:, half:]
        return jnp.concatenate([x1 * cos - x2 * sin, x1 * sin + x2 * cos], axis=-1)

    def body(c, carry):
        r0 = pl.multiple_of(c * ck, ck)
        cos = cos_ref[pl.ds(r0, ck), :]
        sin = sin_ref[pl.ds(r0, ck), :]
        for h, st_ref in enumerate(st_refs):
            cols = slice(h * dk, (h + 1) * dk)
            d_cross = dcr_ref[h]
            g_chunk = dcr_ref[h, ck - 1:ck, :]
            q = rot(q_ref[pl.ds(r0, ck), cols].astype(F32), cos, sin)
            k = rot(k_ref[pl.ds(r0, ck), cols].astype(F32), cos, sin) * (dk ** -0.5)
            v = v_ref[pl.ds(r0, ck), cols]
            intra = _dot_nt(q.astype(BF16), k.astype(BF16)) * din_ref[h]
            state = st_ref[...]
            o = _dot(intra.astype(BF16), v) + _dot((q * d_cross).astype(BF16), state.astype(BF16))
            st_ref[...] = state * g_chunk + _dot_tn((k * dst_ref[h]).astype(BF16), v)
            r = lax.rsqrt(jnp.mean(o * o, axis=-1, keepdims=True) + RMS_EPS)
            gate = _silu(g_ref[pl.ds(r0, ck), cols].astype(F32))
            o_ref[pl.ds(r0, ck), cols] = (o * r * gate).astype(o_ref.dtype)
        return carry

    lax.fori_loop(0, nc, body, 0)


def retention(z, bsz, seq, ch, col0):
    t = z.shape[0]
    nh = RET_HEADS
    dk = ch // nh
    ck = min(RET_CHUNK, seq)
    nc = seq // ck
    half = dk // 2
    pos = jnp.arange(seq, dtype=F32)
    inv = 1.0 / (10000.0 ** jnp.linspace(0.0, 1.0, half, dtype=F32))
    ang = pos[:, None] * inv[None, :]
    cos, sin = jnp.cos(ang), jnp.sin(ang)
    log_g = jnp.log(1.0 - 2.0 ** (-5.0 - jnp.arange(nh, dtype=F32)))
    i = jnp.arange(ck, dtype=F32)
    diff = i[:, None] - i[None, :]
    d_intra = jnp.where(diff >= 0, jnp.exp(log_g[:, None, None] * jnp.maximum(diff, 0.0)), 0.0)
    d_cross = jnp.exp(log_g[:, None] * (i[None, :] + 1.0))[..., None]
    d_state = jnp.exp(log_g[:, None] * (ck - 1.0 - i[None, :]))[..., None]
    cb = col0 // ch
    col = lambda g: (lambda b: (b, cb + g))
    whole = lambda b: (0, 0, 0)
    return pl.pallas_call(
        functools.partial(_retention_kernel, nc=nc, ck=ck, dk=dk),
        grid=(bsz,),
        in_specs=[
            pl.BlockSpec((seq, ch), col(0)),
            pl.BlockSpec((seq, ch), col(1)),
            pl.BlockSpec((seq, ch), col(2)),
            pl.BlockSpec((seq, ch), col(3)),
            pl.BlockSpec((seq, half), lambda b: (0, 0)),
            pl.BlockSpec((seq, half), lambda b: (0, 0)),
            pl.BlockSpec((nh, ck, ck), whole),
            pl.BlockSpec((nh, ck, 1), whole),
            pl.BlockSpec((nh, ck, 1), whole),
        ],
        out_specs=pl.BlockSpec((seq, ch), lambda b: (b, 0)),
        out_shape=jax.ShapeDtypeStruct((t, ch), BF16),
        scratch_shapes=[pltpu.VMEM((dk, dk), F32) for _ in range(nh)],
        compiler_params=_cparams(("parallel",)),
        name="retention",
    )(z, z, z, z, cos, sin, d_intra, d_cross, d_state)


def _route(h, wr_ref, br_ref, ri_ref, rf_ref, cnt_ref, carry_ref, tm):
    logits = _dot_nt(wr_ref[...], h.astype(BF16))
    mx = jnp.max(logits, axis=0, keepdims=True)
    ex = jnp.exp(logits - mx)
    probs = ex / jnp.sum(ex, axis=0, keepdims=True)
    sel = probs + br_ref[...]
    s = [sel[e:e + 1, :] for e in range(N_EXPERTS)]
    p = [probs[e:e + 1, :] for e in range(N_EXPERTS)]
    epg = EXPERTS_PER_GROUP

    def first_argmax(vals, exclude=None):
        best = jnp.full_like(vals[0], -jnp.inf)
        idx = jnp.zeros(vals[0].shape, I32)
        for j, vj in enumerate(vals):
            better = vj > best
            if exclude is not None:
                better = better & (exclude != j)
            idx = jnp.where(better, j, idx)
            best = jnp.where(better, vj, best)
        return idx

    gscore = []
    for g in range(N_GROUPS):
        gs = s[g * epg:(g + 1) * epg]
        best = None
        for a in range(epg):
            for b in range(a + 1, epg):
                pair = gs[a] + gs[b]
                best = pair if best is None else jnp.maximum(best, pair)
        gscore.append(best)
    grp = first_argmax(gscore)

    def pick(rows, index, n):
        out = rows[n - 1]
        for j in range(n - 2, -1, -1):
            out = jnp.where(index == j, rows[j], out)
        return out

    in_s = [pick([s[g * epg + j] for g in range(N_GROUPS)], grp, N_GROUPS) for j in range(epg)]
    in_p = [pick([p[g * epg + j] for g in range(N_GROUPS)], grp, N_GROUPS) for j in range(epg)]
    i1 = first_argmax(in_s)
    i2 = first_argmax(in_s, exclude=i1)
    p1 = pick(in_p, i1, epg)
    p2 = pick(in_p, i2, epg)
    e1 = grp * epg + i1
    e2 = grp * epg + i2
    den = p1 + p2
    g1 = p1 / den
    g2 = p2 / den

    eidx = lax.broadcasted_iota(I32, (N_EXPERTS, tm), 0)
    member = (eidx == e1) | (eidx == e2)
    member_f = jnp.where(member, 1.0, 0.0)
    before = lax.broadcasted_iota(I32, (tm, tm), 0) < lax.broadcasted_iota(I32, (tm, tm), 1)
    prefix = _dot(member_f.astype(BF16), jnp.where(before, 1.0, 0.0).astype(BF16))
    base = prefix + carry_ref[:, 0:1]
    rank1 = jnp.sum(jnp.where(eidx == e1, base, 0.0), axis=0, keepdims=True).astype(I32)
    rank2 = jnp.sum(jnp.where(eidx == e2, base, 0.0), axis=0, keepdims=True).astype(I32)
    carry_ref[...] = carry_ref[...] + jnp.sum(member_f, axis=1, keepdims=True)
    cnt_ref[...] = carry_ref[...].astype(I32)

    zi = jnp.zeros((SUBLANE - 4, tm), I32)
    ri_ref[...] = jnp.concatenate([e1, e2, rank1, rank2, zi], axis=0)
    zf = jnp.zeros((SUBLANE - 2, tm), F32)
    rf_ref[...] = jnp.concatenate([g1, g2, zf], axis=0)


def _mixer_out_kernel(*refs, tm, conv):
    if conv:
        b_ref, c_ref, v_ref, cw_ref, cb_ref, *refs = refs
    else:
        ya_ref, *refs = refs
    (yb_ref, wa_ref, wb_ref, x_ref, modm_ref, modf_ref, nw_ref, wr_ref, br_ref,
     x1_ref, h_ref, ri_ref, rf_ref, cnt_ref, carry_ref, *conv_scratch) = refs
    s = pl.program_id(1)

    @pl.when((pl.program_id(0) == 0) & (s == 0))
    def _():
        carry_ref[...] = jnp.zeros(carry_ref.shape, F32)

    if conv:
        ya = _gated_conv(b_ref, c_ref, v_ref, cw_ref, cb_ref, conv_scratch[0], s, tm).astype(BF16)
    else:
        ya = ya_ref[...]
    y = _dot(ya, wa_ref[...]) + _dot(yb_ref[...], wb_ref[...])
    x1 = x_ref[...] + modm_ref[0, 2:3, :] * y
    x1_ref[...] = x1
    h = _modulated_norm(x1, modf_ref, nw_ref)
    h_ref[...] = _pack_pairs(h)
    _route(h, wr_ref, br_ref, ri_ref, rf_ref, cnt_ref, carry_ref, tm)


def mixer_out(lhs, w_bf, x2, mod_mix, mod_ffn, norm_ffn_w, w_router, b_router, bsz, seq, tm, conv=None):
    t, d = x2.shape
    kh = w_bf.shape[0] // 2
    per = seq // tm
    ne = N_EXPERTS
    row = lambda g: (lambda b, s: (b * per + s, g))
    const = lambda b, s: (0, 0)
    if conv is None:
        (o,) = lhs
        lhs_args = (o, o)
        lhs_specs = [pl.BlockSpec((tm, kh), row(0)), pl.BlockSpec((tm, kh), row(1))]
        scratch = []
    else:
        z, yb = lhs
        conv_w, conv_b = conv
        lhs_args = (z, z, z, conv_w, conv_b.reshape(1, kh), yb)
        lhs_specs = [pl.BlockSpec((tm, kh), row(0)), pl.BlockSpec((tm, kh), row(1)),
                     pl.BlockSpec((tm, kh), row(2)), pl.BlockSpec((CONV_WIDTH, kh), const),
                     pl.BlockSpec((1, kh), const), pl.BlockSpec((tm, kh), row(0))]
        scratch = [pltpu.VMEM((tm + SUBLANE, kh), F32)]
    batch = lambda b, s: (b, 0, 0)
    return pl.pallas_call(
        functools.partial(_mixer_out_kernel, tm=tm, conv=conv is not None),
        grid=(bsz, per),
        in_specs=lhs_specs + [
            pl.BlockSpec((kh, d), const),
            pl.BlockSpec((kh, d), lambda b, s: (1, 0)),
            pl.BlockSpec((tm, d), row(0)),
            pl.BlockSpec((1, 3, d), batch),
            pl.BlockSpec((1, 3, d), batch),
            pl.BlockSpec((1, d), const),
            pl.BlockSpec((ne, d), const),
            pl.BlockSpec((ne, 1), const),
        ],
        out_specs=[
            pl.BlockSpec((tm, d), row(0)),
            pl.BlockSpec((tm, d // 2), row(0)),
            pl.BlockSpec((SUBLANE, tm), lambda b, s: (0, b * per + s)),
            pl.BlockSpec((SUBLANE, tm), lambda b, s: (0, b * per + s)),
            pl.BlockSpec((ne, LANE), const),
        ],
        out_shape=[
            jax.ShapeDtypeStruct((t, d), F32),
            jax.ShapeDtypeStruct((t, d // 2), I32),
            jax.ShapeDtypeStruct((SUBLANE, t), I32),
            jax.ShapeDtypeStruct((SUBLANE, t), F32),
            jax.ShapeDtypeStruct((ne, LANE), I32),
        ],
        scratch_shapes=[pltpu.VMEM((ne, LANE), F32)] + scratch,
        compiler_params=_cparams(("arbitrary", "arbitrary")),
        name="mixer_out",
    )(*lhs_args, w_bf, w_bf, x2, mod_mix, mod_ffn, norm_ffn_w.reshape(1, d),
      w_router.T.astype(BF16), b_router.reshape(ne, 1))


def _dispatch_kernel(dest_ref, pad0_ref, padn_ref, h_ref, xs_ref, hbuf_ref, zero_ref, sem, zsem,
                     *, tm, nt):
    i = pl.program_id(0)
    cur = i % 2
    base = i * tm

    def zero_row(e, r):
        return pltpu.make_async_copy(zero_ref.at[pl.ds(0, 1), :],
                                     xs_ref.at[pl.ds(pad0_ref[e] + r, 1), :], zsem)

    def tile_done(buf):
        for _ in range(2):
            pltpu.make_async_copy(hbuf_ref.at[buf], xs_ref.at[pl.ds(0, tm), :], sem.at[buf]).wait()

    @pl.when(i == 0)
    def _():
        zero_ref[...] = jnp.zeros(zero_ref.shape, zero_ref.dtype)
        for e in range(N_EXPERTS):
            lax.fori_loop(0, padn_ref[e], lambda r, c, e=e: (zero_row(e, r).start(), c)[1], 0)

    hbuf_ref[cur] = h_ref[...]

    def start(r, carry):
        for slot in range(2):
            d = dest_ref[2 * (base + r) + slot]
            pltpu.make_async_copy(hbuf_ref.at[cur, pl.ds(r, 1), :], xs_ref.at[pl.ds(d, 1), :],
                                  sem.at[cur]).start()
        return carry

    lax.fori_loop(0, tm, start, 0, unroll=DMA_UNROLL)

    @pl.when(i > 0)
    def _():
        tile_done(1 - cur)

    @pl.when(i == nt - 1)
    def _():
        tile_done(cur)
        for e in range(N_EXPERTS):
            lax.fori_loop(0, padn_ref[e], lambda r, c, e=e: (zero_row(e, r).wait(), c)[1], 0)


def dispatch(h, dest, pad_start, pad_count, n_rows, tm):
    t, d = h.shape
    return pl.pallas_call(
        functools.partial(_dispatch_kernel, tm=tm, nt=t // tm),
        grid_spec=pltpu.PrefetchScalarGridSpec(
            num_scalar_prefetch=3,
            grid=(t // tm,),
            in_specs=[pl.BlockSpec((tm, d), lambda i, *_: (i, 0))],
            out_specs=pl.BlockSpec(memory_space=pl.ANY),
            scratch_shapes=[pltpu.VMEM((2, tm, d), h.dtype), pltpu.VMEM((SUBLANE, d), h.dtype),
                            pltpu.SemaphoreType.DMA((2,)), pltpu.SemaphoreType.DMA(())],
        ),
        out_shape=jax.ShapeDtypeStruct((n_rows, d), h.dtype),
        compiler_params=_cparams(("arbitrary",)),
        name="moe_dispatch",
    )(dest, pad_start, pad_count, h)


def _ffn_kernel(be_ref, bv_ref, xs_ref, w1_ref, w3_ref, w2_ref, *refs, riders):
    ys_ref = refs[len(riders)]
    _ride_casts(riders, refs[:len(riders)], refs[len(riders) + 1:], pl.program_id(0))
    valid = bv_ref[pl.program_id(0)]

    @pl.when(valid > 0)
    def _():
        rows = lax.broadcasted_iota(I32, xs_ref.shape, 0)
        x = _unpack_pairs(jnp.where(rows < valid, xs_ref[...], 0))
        h1 = _dot(x, w1_ref[0])
        h3 = _dot(x, w3_ref[0])
        a = (_silu(h1) * h3).astype(BF16)
        ys_ref[...] = _dot(a, w2_ref[0])

    @pl.when(valid <= 0)
    def _():
        ys_ref[...] = jnp.zeros(ys_ref.shape, ys_ref.dtype)


def grouped_ffn(xs, block_e, block_valid, w1, w3, w2, bm, riders=()):
    n_rows = xs.shape[0]
    d, f = w1.shape[1], w1.shape[2]
    flat = lambda i, be, bv: i
    return pl.pallas_call(
        functools.partial(_ffn_kernel, riders=tuple(r.nblk for r in riders)),
        grid_spec=pltpu.PrefetchScalarGridSpec(
            num_scalar_prefetch=2,
            grid=(n_rows // bm,),
            in_specs=[
                pl.BlockSpec((bm, d // 2), lambda i, be, bv: (i, 0)),
                pl.BlockSpec((1, d, f), lambda i, be, bv: (be[i], 0, 0)),
                pl.BlockSpec((1, d, f), lambda i, be, bv: (be[i], 0, 0)),
                pl.BlockSpec((1, f, d), lambda i, be, bv: (be[i], 0, 0)),
            ] + [r.in_spec(flat) for r in riders],
            out_specs=[pl.BlockSpec((bm, d), lambda i, be, bv: (i, 0))] + [r.out_spec(flat) for r in riders],
        ),
        out_shape=[jax.ShapeDtypeStruct((n_rows, d), F32)] + [r.out_shape for r in riders],
        compiler_params=_cparams(("arbitrary",)),
        name="moe_ffn",
    )(block_e, block_valid, xs, w1, w3, w2, *[r.src for r in riders])


def _combine_kernel(dest_ref, ys_ref, x_ref, mod_ref, gf_ref, *refs, tm, nt, project):
    if project:
        modn_ref, nwn_ref, w_ref, o_ref, z_ref, a_ref, b_ref, sem = refs
    else:
        o_ref, a_ref, b_ref, sem = refs
    i = pl.program_id(0)
    cur = i % 2

    def fetch_row(base, r, buf):
        for slot, ref in ((0, a_ref), (1, b_ref)):
            d = dest_ref[2 * (base + r) + slot]
            pltpu.make_async_copy(ys_ref.at[pl.ds(d, 1), :], ref.at[buf, pl.ds(r, 1), :],
                                  sem.at[buf]).start()

    def fetch(tile, buf):
        lax.fori_loop(0, tm, lambda r, c: (fetch_row(tile * tm, r, buf), c)[1], 0, unroll=DMA_UNROLL)

    def fetched(buf):
        pltpu.make_async_copy(ys_ref.at[pl.ds(0, tm), :], a_ref.at[buf], sem.at[buf]).wait()
        pltpu.make_async_copy(ys_ref.at[pl.ds(0, tm), :], b_ref.at[buf], sem.at[buf]).wait()

    @pl.when(i == 0)
    def _():
        fetch(0, 0)

    if not project:
        @pl.when(i + 1 < nt)
        def _():
            fetch(i + 1, 1 - cur)

    fetched(cur)
    y = gf_ref[:, 0:1] * a_ref[cur] + gf_ref[:, 1:2] * b_ref[cur]
    out = x_ref[...] + mod_ref[0, 2:3, :] * y
    o_ref[...] = out
    if project:
        h = _modulated_norm(out, modn_ref, nwn_ref).astype(BF16)
        base = jnp.minimum(i + 1, nt - 1) * tm
        for r in range(tm):
            fetch_row(base, r, 1 - cur)
        z_ref[...] = _dot(h, w_ref[...]).astype(z_ref.dtype)

        @pl.when(i == nt - 1)
        def _():
            fetched(1 - cur)


def combine(ys, dest, x2, mod, gates_t, seq, tm, project=None):
    t, d = x2.shape
    per = seq // tm
    row = lambda i, dest: (i, 0)
    batch = lambda i, dest: (i // per, 0, 0)
    const = lambda i, dest: (0, 0)
    in_specs = [pl.BlockSpec(memory_space=pl.ANY), pl.BlockSpec((tm, d), row),
                pl.BlockSpec((1, 3, d), batch), pl.BlockSpec((tm, SUBLANE), row)]
    out_specs = [pl.BlockSpec((tm, d), row)]
    out_shape = [jax.ShapeDtypeStruct((t, d), F32)]
    args = [dest, ys, x2, mod, gates_t]
    if project is not None:
        mod_next, nw_next, w_bf = project
        n = w_bf.shape[1]
        in_specs += [pl.BlockSpec((1, 3, d), batch), pl.BlockSpec((1, d), const), pl.BlockSpec((d, n), const)]
        out_specs += [pl.BlockSpec((tm, n), row)]
        out_shape += [jax.ShapeDtypeStruct((t, n), BF16)]
        args += [mod_next, nw_next.reshape(1, d), w_bf]
    return pl.pallas_call(
        functools.partial(_combine_kernel, tm=tm, nt=t // tm, project=project is not None),
        grid_spec=pltpu.PrefetchScalarGridSpec(
            num_scalar_prefetch=1,
            grid=(t // tm,),
            in_specs=in_specs,
            out_specs=out_specs,
            scratch_shapes=[pltpu.VMEM((2, tm, d), F32), pltpu.VMEM((2, tm, d), F32),
                            pltpu.SemaphoreType.DMA((2,))],
        ),
        out_shape=out_shape,
        compiler_params=_cparams(("arbitrary",)),
        name="moe_combine",
    )(*args)


def _moe_blocks(t):
    return (2 * t) // MOE_ROWS + N_EXPERTS


def moe(x2, h, ri, rf, cnt, mod, w1, w3, w2, seq, project=None, riders=()):
    t, d = x2.shape
    bm = MOE_ROWS
    tm = min(256, seq)
    ne = N_EXPERTS
    counts = cnt[:, 0]
    nblk = (counts + bm - 1) // bm
    blk_end = jnp.cumsum(nblk)
    blk_start = blk_end - nblk
    eids = jnp.arange(ne, dtype=I32)
    e12 = ri[0:2].T
    row0 = jnp.sum(jnp.where(e12[..., None] == eids, blk_start * bm, 0), axis=-1)
    dest = (row0 + ri[2:4].T).reshape(2 * t).astype(I32)
    n_blocks = _moe_blocks(t)
    bidx = jnp.arange(n_blocks, dtype=I32)
    block_e = jnp.minimum(jnp.sum(bidx[:, None] >= blk_end[None, :], axis=1), ne - 1).astype(I32)
    onehot = block_e[:, None] == eids
    cnt_b = jnp.sum(jnp.where(onehot, counts, 0), axis=1)
    start_b = jnp.sum(jnp.where(onehot, blk_start, 0), axis=1)
    block_valid = jnp.clip(cnt_b - (bidx - start_b) * bm, 0, bm).astype(I32)
    pad_start = (blk_start * bm + counts).astype(I32)
    pad_end = jnp.where(eids == ne - 1, n_blocks * bm, blk_end * bm)
    xs = dispatch(h, dest, pad_start, (pad_end - pad_start).astype(I32), n_blocks * bm, tm)
    ys, *casts = grouped_ffn(xs, block_e, block_valid, w1, w3, w2, bm, riders)
    return combine(ys, dest, x2, mod, rf.T, seq, tm, project), casts


def _t5_bucket(n):
    max_exact = N_BUCKETS // 2
    nf = jnp.maximum(n, 1).astype(F32)
    large = max_exact + (jnp.log(nf / max_exact) / math.log(MAX_DISTANCE / max_exact)
                         * (N_BUCKETS - max_exact)).astype(I32)
    large = jnp.minimum(large, N_BUCKETS - 1)
    return jnp.where(n < max_exact, n, large)


def _sortable(score):
    bits = pltpu.bitcast(score, I32)
    return bits ^ ((bits >> 31) & 0x7FFFFFFF)


def _dsa_kernel(q_ref, k_ref, v_ref, qi_ref, kw_ref, wq_ref, tb_ref, qn_ref, kn_ref, o_ref,
                khat_ref, vt_ref, qrt_ref, qirt_ref, wt_ref, mt_ref, hi_ref, lo_ref, lo2_ref, nm_ref, lg_ref,
                *acc_refs,
                nh, dh, nih, di, topk, nb):
    j = pl.program_id(1)
    qb = Q_BLOCK
    kc = 2 * qb
    nch = (j + 2) // 2
    pair = 2 * qb

    def transpose(x):
        return x.astype(F32).T

    @pl.when(j == 0)
    def _():
        k = k_ref[...].astype(F32)
        r = lax.rsqrt(jnp.mean(k * k, axis=-1, keepdims=True) + RMS_EPS)
        khat_ref[...] = (k * r * kn_ref[...]).astype(BF16)
        for c in range(nb // 2):
            for s in range(2):
                blk = v_ref[(2 * c + s) * qb:(2 * c + s + 1) * qb, :]
                vt_ref[c, 0:dh, s * qb:(s + 1) * qb] = transpose(blk).astype(BF16)
            vt_ref[c, dh:dh + ONES_ROWS, :] = jnp.ones((ONES_ROWS, kc), BF16)
        for ref in (hi_ref, lo_ref, lo2_ref):
            ref[...] = jnp.full(ref.shape, I16_MIN, I16)

    for h in range(nh):
        qt = transpose(q_ref[:, h * dh:(h + 1) * dh])
        r = lax.rsqrt(jnp.mean(qt * qt, axis=0, keepdims=True) + RMS_EPS)
        qrt_ref[:, h * qb:(h + 1) * qb] = (qt * r * qn_ref[...]).astype(BF16)
    for g in range(nih * di // LANE):
        two = transpose(qi_ref[:, g * LANE:(g + 1) * LANE])
        for s in range(LANE // di):
            h = g * (LANE // di) + s
            qirt_ref[:, h * qb:(h + 1) * qb] = two[s * di:(s + 1) * di, :].astype(BF16)
    wt_ref[...] = transpose(wq_ref[...])

    key_l = lax.broadcasted_iota(I32, (kc, qb), 0)
    q_pos = j * qb + lax.broadcasted_iota(I32, (kc, qb), 1)

    def score_body(c, carry):
        r0 = pl.multiple_of(c * kc, kc)
        ki = kw_ref[pl.ds(r0, kc), 0:di]
        acc = jnp.zeros((kc, qb), F32)
        for g in range(nih // 2):
            rel = _dot(ki, qirt_ref[:, g * pair:(g + 1) * pair])
            for s in range(2):
                h = 2 * g + s
                acc = acc + wt_ref[di + h:di + h + 1, :] * jnp.maximum(rel[:, s * qb:(s + 1) * qb], 0.0)
        m = jnp.where(r0 + key_l <= q_pos, _sortable(acc), INT_MIN)
        mt_ref[c] = m
        hi_ref[c] = (m >> 16).astype(I16)
        lo_ref[c] = ((m & 0xFFFF) + I16_MIN).astype(I16)
        return carry

    lax.fori_loop(0, nch, score_body, 0)

    def count_ge(cand):
        def cbody(c, acc):
            hit = jnp.where(mt_ref[c] >= cand, 1, 0)
            return acc + jnp.sum(hit.reshape(kc // SUBLANE, SUBLANE, qb), axis=0)
        acc = lax.fori_loop(0, nch, cbody, jnp.zeros((SUBLANE, qb), I32))
        return jnp.sum(acc, axis=0, keepdims=True)

    def fold16(hit):
        parts = [hit[PACK16 * i:PACK16 * (i + 1), :] for i in range(kc // PACK16)]
        while len(parts) > 1:
            parts = [parts[i] + parts[i + 1] for i in range(0, len(parts), 2)]
        return parts[0]

    def total16(acc):
        return jnp.sum(acc.astype(F32), axis=0, keepdims=True).astype(I32)

    one, nil = jnp.ones((), BF16), jnp.zeros((), BF16)

    def count16(src_ref, cand, n):
        c16 = cand.astype(I16)
        acc = fold16(jnp.where(src_ref[0] >= c16, one, nil))
        for c in range(1, n):
            acc = acc + fold16(jnp.where(src_ref[c] >= c16, one, nil))
        return total16(acc)

    def search16(src_ref, kth, n):
        zero = jnp.zeros((1, qb), I32)
        ans0 = jnp.where(count16(src_ref, zero, n) >= kth, zero, I16_MIN)

        def bit_body(bi, ans):
            cand = ans | (1 << (14 - bi))
            return jnp.where(count16(src_ref, cand, n) >= kth, cand, ans)

        return lax.fori_loop(0, 15, bit_body, ans0)

    def search(n):
        top = search16(hi_ref, jnp.full((1, qb), topk, I32), n)
        top16 = top.astype(I16)
        acc = jnp.zeros((PACK16, qb), BF16)
        for c in range(n):
            hi = hi_ref[c]
            lo2_ref[c] = jnp.where(hi == top16, lo_ref[c], jnp.full((), I16_MIN, I16))
            acc = acc + fold16(jnp.where(hi > top16, one, nil))
        above = total16(acc)
        low = search16(lo2_ref, topk - above, n)
        return top * 65536 + (low - I16_MIN), above + count16(lo2_ref, low, n)

    def search_any():
        return lax.cond(nch <= nb // 4, functools.partial(search, nb // 4),
                        functools.partial(search, nb // 2))

    selecting = (j + 1) * qb > topk
    thr, n_ge = lax.cond(selecting, search_any,
                         lambda: (jnp.full((1, qb), INT_MIN + 1, I32), jnp.zeros((1, qb), I32)))

    def plain_mask():
        def body(c, carry):
            nm_ref[c] = jnp.where(mt_ref[c] >= thr, 0.0, NEG_BIG)
            return carry
        lax.fori_loop(0, nch, body, 0)

    def tie_mask():
        need = (topk - count_ge(thr + 1)).astype(F32)
        tri = jnp.where(lax.broadcasted_iota(I32, (kc, kc), 1) <= lax.broadcasted_iota(I32, (kc, kc), 0),
                        1.0, 0.0).astype(BF16)

        def body(c, seen):
            m = mt_ref[c]
            tie = m == thr
            upto = _dot(tri, jnp.where(tie, 1.0, 0.0).astype(BF16)) + seen
            nm_ref[c] = jnp.where((m > thr) | (tie & (upto <= need)), 0.0, NEG_BIG)
            return upto[kc - 1:kc, :]
        lax.fori_loop(0, nch, body, jnp.zeros((1, qb), F32))

    crowded = jnp.max(n_ge) > topk
    lax.cond(crowded, tie_mask, plain_mask)

    npair = nh // 2
    pcols = [slice(pr * pair, (pr + 1) * pair) for pr in range(npair)]
    nfar = jnp.maximum((j - 1) // 2, 0)

    def sweep1(c, mruns, near):
        r0 = pl.multiple_of(c * kc, kc)
        kh = khat_ref[pl.ds(r0, kc), :]
        nm = nm_ref[c]
        nm2 = jnp.concatenate([nm, nm], axis=1)
        if near:
            k0 = jnp.clip(j - 2 * c, 0, 2)
            k1 = jnp.clip(j - 2 * c - 1, 0, 2)
        out = []
        for pr in range(npair):
            lg = _dot(kh, qrt_ref[:, pcols[pr]]) + nm2
            if near:
                lg = lg + jnp.concatenate([tb_ref[k0, :, pcols[pr]], tb_ref[k1, :, pcols[pr]]], axis=0)
            lg_ref[c, :, pcols[pr]] = lg
            out.append(jnp.maximum(mruns[pr], jnp.max(lg.reshape(kc // SUBLANE, SUBLANE, pair), axis=0)))
        return tuple(out)

    mruns = tuple(jnp.full((SUBLANE, pair), NEG_BIG, F32) for _ in range(npair))
    mruns = lax.fori_loop(0, nfar, functools.partial(sweep1, near=False), mruns)
    mruns = lax.fori_loop(nfar, nch, functools.partial(sweep1, near=True), mruns)
    mrows = [jnp.max(m, axis=0, keepdims=True) for m in mruns]
    for acc_ref in acc_refs:
        acc_ref[...] = jnp.zeros(acc_ref.shape, F32)

    def sweep2(c, carry):
        vt = vt_ref[c]
        for pr in range(npair):
            p = jnp.exp2((lg_ref[c, :, pcols[pr]] - mrows[pr]).astype(BF16))
            acc_refs[pr][...] += _dot(vt, p)
        return carry

    lax.fori_loop(0, nch, sweep2, 0)
    for pr in range(npair):
        out = acc_refs[pr][0:dh, :] / acc_refs[pr][dh:dh + 1, :]
        for s in range(2):
            h = 2 * pr + s
            o_ref[:, h * dh:(h + 1) * dh] = transpose(
                out[:, s * qb:(s + 1) * qb].astype(BF16)).astype(o_ref.dtype)


def dsa_attention(z, rel_bias, q_norm, k_norm, bsz, seq, d):
    t = z.shape[0]
    nh, dh, nih, di = ATT_HEADS, d // ATT_HEADS, IDX_HEADS, IDX_DIM
    qb = Q_BLOCK
    nb = seq // qb
    topk = min(DSA_TOPK_MAX, seq // 4)
    assert topk % qb == 0 and dh == LANE and nih * di == d // 2 and nb % 2 == 0 and LANE % di == 0
    assert (nb // 2) * (2 * qb // PACK16) <= 256
    i = jnp.arange(qb)
    dist = i[None, :] - i[:, None]
    def lookup(bucket):
        hot = bucket[..., None] == jnp.arange(N_BUCKETS, dtype=I32)
        return jnp.sum(jnp.where(hot[..., None], rel_bias, 0.0), axis=-2)

    tabs = [lookup(_t5_bucket(jnp.maximum(dist + off * qb, 0))) for off in range(2)]
    far = jnp.broadcast_to(lookup(_t5_bucket(jnp.full((), 2 * qb, I32))), (qb, qb, nh))
    tb = jnp.stack(tabs + [far]).transpose(0, 1, 3, 2).reshape(3, qb, nh * qb).astype(F32)
    tb = (tb - tb[2:3]) * math.log2(math.e)
    qn_rows = jnp.broadcast_to((q_norm * (dh ** -0.5 * math.log2(math.e)))[:, None], (dh, qb)).astype(F32)
    cq = d // LANE
    return pl.pallas_call(
        functools.partial(_dsa_kernel, nh=nh, dh=dh, nih=nih, di=di, topk=topk, nb=nb),
        grid=(bsz, nb),
        in_specs=[
            pl.BlockSpec((qb, d), lambda b, j: (b * nb + j, 0)),
            pl.BlockSpec((seq, LANE), lambda b, j: (b, cq + cq // 2)),
            pl.BlockSpec((seq, LANE), lambda b, j: (b, cq + cq // 2 + 1)),
            pl.BlockSpec((qb, d // 2), lambda b, j: (b * nb + j, 2)),
            pl.BlockSpec((seq, LANE), lambda b, j: (b, cq + cq // 2 + 2)),
            pl.BlockSpec((qb, LANE), lambda b, j: (b * nb + j, cq + cq // 2 + 2)),
            pl.BlockSpec((3, qb, nh * qb), lambda b, j: (0, 0, 0)),
            pl.BlockSpec((dh, qb), lambda b, j: (0, 0)),
            pl.BlockSpec((1, dh), lambda b, j: (0, 0)),
        ],
        out_specs=pl.BlockSpec((qb, d), lambda b, j: (b * nb + j, 0)),
        out_shape=jax.ShapeDtypeStruct((t, d), BF16),
        scratch_shapes=[
            pltpu.VMEM((seq, dh), BF16),
            pltpu.VMEM((nb // 2, dh + ONES_ROWS, 2 * qb), BF16),
            pltpu.VMEM((dh, nh * qb), BF16),
            pltpu.VMEM((di, nih * qb), BF16),
            pltpu.VMEM((LANE, qb), F32),
            pltpu.VMEM((nb // 2, 2 * qb, qb), I32),
            pltpu.VMEM((nb // 2, 2 * qb, qb), I16),
            pltpu.VMEM((nb // 2, 2 * qb, qb), I16),
            pltpu.VMEM((nb // 2, 2 * qb, qb), I16),
            pltpu.VMEM((nb // 2, 2 * qb, qb), F32),
            pltpu.VMEM((nb // 2, 2 * qb, nh * qb), F32),
        ] + [pltpu.VMEM((dh + ONES_ROWS, 2 * qb), F32)
             for _ in range(nh // 2)],
        compiler_params=_cparams(("arbitrary", "arbitrary")),
        name="dsa_attention",
    )(z, z, z, z, z, z, tb, qn_rows, k_norm.reshape(1, dh))


def _dsa_weight(c_w_in, d):
    dh = d // ATT_HEADS
    nqi = IDX_HEADS * IDX_DIM
    q, k, v, qi, ki, wi = jnp.split(
        c_w_in, [d, d + dh, d + 2 * dh, d + 2 * dh + nqi, d + 2 * dh + nqi + IDX_DIM], axis=1)
    used = d + nqi + 2 * dh + IDX_DIM + IDX_HEADS
    total = -(-used // 512) * 512
    pad = jnp.zeros((d, total - used), c_w_in.dtype)
    return jnp.concatenate([q, qi, k, v, ki, wi, pad], axis=1).astype(BF16)


def kernel(x, c, rel_bias, w_router, b_router, norm_mix, ada_w_mix, ada_b_mix, norm_ffn, ada_w_ffn,
           ada_b_ffn, ab_w_in, ab_conv_w, ab_conv_b, ab_w_out, c_w_in, c_q_norm, c_k_norm, c_w_out,
           moe_w1, moe_w3, moe_w2):
    bsz, seq, d = x.shape
    depth = norm_mix.shape[0]
    t = bsz * seq
    ch = d // 2
    tm_out = min(512, seq)
    x2 = x.reshape(t, d)
    mod_mix = ada_mod(c, ada_w_mix, ada_b_mix).reshape(depth, bsz, 3, d)
    mod_ffn = ada_mod(c, ada_w_ffn, ada_b_ffn).reshape(depth, bsz, 3, d)
    experts = (moe_w1, moe_w3, moe_w2)
    as_experts = lambda casts: [cw.reshape(w.shape[1:]) for cw, w in zip(casts, experts)]
    w_in = [ab_w_in[l // 2].astype(BF16) if l % 2 == 0 else _dsa_weight(c_w_in[l // 2], d)
            for l in range(depth)]
    tm_in, tn_in = min(1024, seq), 1024
    z = None
    w_moe = None
    for layer in range(depth):
        i = layer // 2
        if z is None:
            riders = _cast_riders(experts, layer, (t // tm_in) * (w_in[layer].shape[1] // tn_in))
            h_mix = prenorm(x2, mod_mix[layer], norm_mix[layer], seq, tm_out)
            z, *casts = inproj(h_mix, w_in[layer], tm_in, tn_in, riders or ())
            if riders:
                w_moe = as_experts(casts)
        if w_moe is None:
            w_moe = [w[layer].astype(BF16) for w in experts]
        if layer % 2 == 0:
            y_b = retention(z, bsz, seq, ch, 3 * ch)
            lhs, w_out, conv = (z, y_b), ab_w_out[i], (ab_conv_w[i], ab_conv_b[i])
        else:
            o = dsa_attention(z, rel_bias, c_q_norm[i], c_k_norm[i], bsz, seq, d)
            lhs, w_out, conv = (o,), c_w_out[i], None
        x2, h, ri, rf, cnt = mixer_out(lhs, w_out.astype(BF16), x2, mod_mix[layer], mod_ffn[layer],
                                       norm_ffn[layer], w_router, b_router, bsz, seq, tm_out, conv)
        last = layer + 1 == depth
        nxt = None if last else (mod_mix[layer + 1], norm_mix[layer + 1], w_in[layer + 1])
        riders = None if last else _cast_riders(experts, layer + 1, _moe_blocks(t))
        res, casts = moe(x2, h, ri, rf, cnt, mod_ffn[layer], *w_moe, seq, nxt, riders or ())
        x2, z = (res[0], None) if last else (res[0], res[1])
        w_moe = as_experts(casts) if riders else None
    return x2.reshape(bsz, seq, d)
```

```python
import functools
import math
from typing import NamedTuple

import jax
import jax.numpy as jnp
from jax import lax
from jax.experimental import pallas as pl
from jax.experimental.pallas import tpu as pltpu

F32 = jnp.float32
BF16 = jnp.bfloat16
I32 = jnp.int32
I16 = jnp.int16

RMS_EPS = 1e-6
CONV_WIDTH = 3
RET_HEADS = 4
ATT_HEADS = 16
IDX_HEADS = 16
IDX_DIM = 64
DSA_TOPK_MAX = 256
Q_BLOCK = 128
N_BUCKETS = 32
MAX_DISTANCE = 128
N_EXPERTS = 16
N_GROUPS = 4
EXPERTS_PER_GROUP = N_EXPERTS // N_GROUPS
RET_CHUNK = 256
MOE_ROWS = 256
DMA_UNROLL = 8
RIDER_BLOCK_BYTES = 1024 * 1024
ONES_ROWS = 16
LANE = 128
SUBLANE = 8
VMEM_LIMIT = 48 * 1024 * 1024
NEG_BIG = -1e30
INT_MIN = -(2 ** 31)
I16_MIN = -(2 ** 15)
HIGH_HALF = -(2 ** 16)
PACK16 = 16


def _cparams(sem, vmem=VMEM_LIMIT):
    return pltpu.CompilerParams(dimension_semantics=sem, vmem_limit_bytes=vmem)


def _dot(a, b):
    return jnp.dot(a, b, preferred_element_type=F32)


def _dot_nt(a, b):
    return lax.dot_general(a, b, (((1,), (1,)), ((), ())), preferred_element_type=F32)


def _dot_tn(a, b):
    return lax.dot_general(a, b, (((0,), (0,)), ((), ())), preferred_element_type=F32)


def _silu(x):
    return x * jax.nn.sigmoid(x)


def _pack_pairs(x):
    n = x.shape[1] // 2
    xb = x.astype(BF16).astype(F32)
    lo = pltpu.bitcast(xb[:, :n], I32)
    hi = pltpu.bitcast(xb[:, n:], I32)
    return lax.shift_right_logical(lo, 16) | (hi & HIGH_HALF)


def _unpack_pairs(u):
    lo = pltpu.bitcast(u << 16, F32)
    hi = pltpu.bitcast(u & HIGH_HALF, F32)
    return jnp.concatenate([lo, hi], axis=1).astype(BF16)


class _CastRider(NamedTuple):
    src: jax.Array
    rows: int
    first_blk: int
    nblk: int

    def in_spec(self, flat):
        return pl.BlockSpec((self.rows, self.src.shape[1]),
                            lambda *g: (jnp.minimum(flat(*g), self.nblk - 1) + self.first_blk, 0))

    def out_spec(self, flat):
        return pl.BlockSpec((self.rows, self.src.shape[1]),
                            lambda *g: (jnp.minimum(flat(*g), self.nblk - 1), 0))

    @property
    def out_shape(self):
        return jax.ShapeDtypeStruct((self.rows * self.nblk, self.src.shape[1]), BF16)


def _cast_riders(weights, layer, n_steps):
    riders = []
    for w in weights:
        per_layer, cols = w.shape[1] * w.shape[2], w.shape[3]
        rows = PACK16
        while rows * n_steps < per_layer:
            rows *= 2
        if per_layer % rows or rows * cols * 4 > RIDER_BLOCK_BYTES:
            return None
        nblk = per_layer // rows
        riders.append(_CastRider(w.reshape(-1, cols), rows, layer * nblk, nblk))
    return riders


def _ride_casts(nblks, src_refs, dst_refs, step):
    for nblk, src, dst in zip(nblks, src_refs, dst_refs):
        @pl.when(step < nblk)
        def _():
            dst[...] = src[...].astype(BF16)


def _ada_kernel(c_ref, w_ref, b_ref, o_ref):
    cs = _silu(c_ref[...])
    o_ref[0] = _dot(cs.astype(BF16), w_ref[0].astype(BF16)) + b_ref[0]


def ada_mod(c, ada_w, ada_b):
    nl, d, n3 = ada_w.shape
    bsz = c.shape[0]
    tn = min(512, n3)
    return pl.pallas_call(
        _ada_kernel,
        grid=(nl, n3 // tn),
        in_specs=[
            pl.BlockSpec((bsz, d), lambda l, j: (0, 0)),
            pl.BlockSpec((1, d, tn), lambda l, j: (l, 0, j)),
            pl.BlockSpec((1, 1, tn), lambda l, j: (l, 0, j)),
        ],
        out_specs=pl.BlockSpec((1, bsz, tn), lambda l, j: (l, 0, j)),
        out_shape=jax.ShapeDtypeStruct((nl, bsz, n3), F32),
        compiler_params=_cparams(("parallel", "parallel")),
        name="ada_mod",
    )(c, ada_w, ada_b.reshape(nl, 1, n3))


def _modulated_norm(x, mod_ref, nw_ref):
    r = lax.rsqrt(jnp.mean(x * x, axis=-1, keepdims=True) + RMS_EPS)
    return x * r * nw_ref[...] * (1.0 + mod_ref[0, 1:2, :]) + mod_ref[0, 0:1, :]


def _prenorm_kernel(x_ref, mod_ref, nw_ref, h_ref):
    h_ref[...] = _modulated_norm(x_ref[...], mod_ref, nw_ref).astype(h_ref.dtype)


def prenorm(x2, mod, norm_w, seq, tm):
    t, d = x2.shape
    per = seq // tm
    return pl.pallas_call(
        _prenorm_kernel,
        grid=(t // tm,),
        in_specs=[
            pl.BlockSpec((tm, d), lambda i: (i, 0)),
            pl.BlockSpec((1, 3, d), lambda i: (i // per, 0, 0)),
            pl.BlockSpec((1, d), lambda i: (0, 0)),
        ],
        out_specs=pl.BlockSpec((tm, d), lambda i: (i, 0)),
        out_shape=jax.ShapeDtypeStruct((t, d), BF16),
        compiler_params=_cparams(("parallel",)),
        name="prenorm",
    )(x2, mod, norm_w.reshape(1, d))


def _inproj_kernel(h_ref, w_ref, *refs, riders):
    o_ref = refs[len(riders)]
    step = pl.program_id(0) * pl.num_programs(1) + pl.program_id(1)
    _ride_casts(riders, refs[:len(riders)], refs[len(riders) + 1:], step)
    o_ref[...] = _dot(h_ref[...], w_ref[...]).astype(o_ref.dtype)


def inproj(h_bf, w_bf, tm, tn, riders=()):
    t, d = h_bf.shape
    n = w_bf.shape[1]
    nj = n // tn
    flat = lambda i, j: i * nj + j
    return pl.pallas_call(
        functools.partial(_inproj_kernel, riders=tuple(r.nblk for r in riders)),
        grid=(t // tm, nj),
        in_specs=[
            pl.BlockSpec((tm, d), lambda i, j: (i, 0)),
            pl.BlockSpec((d, tn), lambda i, j: (0, j)),
        ] + [r.in_spec(flat) for r in riders],
        out_specs=[pl.BlockSpec((tm, tn), lambda i, j: (i, j))] + [r.out_spec(flat) for r in riders],
        out_shape=[jax.ShapeDtypeStruct((t, n), BF16)] + [r.out_shape for r in riders],
        compiler_params=_cparams(("arbitrary", "arbitrary")),
        name="inproj",
    )(h_bf, w_bf, *[r.src for r in riders])


def _gated_conv(b_ref, c_ref, v_ref, w_ref, cb_ref, u_ref, s, tm):
    u = c_ref[...].astype(F32) * v_ref[...].astype(F32)

    @pl.when(s == 0)
    def _():
        u_ref[0:SUBLANE, :] = jnp.zeros((SUBLANE, u.shape[1]), F32)

    @pl.when(s > 0)
    def _():
        u_ref[0:SUBLANE, :] = u_ref[tm:tm + SUBLANE, :]

    u_ref[SUBLANE:SUBLANE + tm, :] = u
    conv = (cb_ref[...]
            + u_ref[SUBLANE - 2:SUBLANE - 2 + tm, :] * w_ref[0:1, :]
            + u_ref[SUBLANE - 1:SUBLANE - 1 + tm, :] * w_ref[1:2, :]
            + u * w_ref[2:3, :])
    return b_ref[...].astype(F32) * conv


def _retention_kernel(q_ref, k_ref, v_ref, g_ref, cos_ref, sin_ref, din_ref, dcr_ref, dst_ref,
                      o_ref, *st_refs, nc, ck, dk):
    half = dk // 2
    for st_ref in st_refs:
        st_ref[...] = jnp.zeros(st_ref.shape, F32)

    def rot(x, cos, sin):
        x1, x2 = x[:, :half], x[:, half:]
        return jnp.concatenate([x1 * cos - x2 * sin, x1 * sin + x2 * cos], axis=-1)

    def body(c, carry):
        r0 = pl.multiple_of(c * ck, ck)
        cos = cos_ref[pl.ds(r0, ck), :]
        sin = sin_ref[pl.ds(r0, ck), :]
        for h, st_ref in enumerate(st_refs):
            cols = slice(h * dk, (h + 1) * dk)
            d_cross = dcr_ref[h]
            g_chunk = dcr_ref[h, ck - 1:ck, :]
            q = rot(q_ref[pl.ds(r0, ck), cols].astype(F32), cos, sin)
            k = rot(k_ref[pl.ds(r0, ck), cols].astype(F32), cos, sin) * (dk ** -0.5)
            v = v_ref[pl.ds(r0, ck), cols]
            intra = _dot_nt(q.astype(BF16), k.astype(BF16)) * din_ref[h]
            state = st_ref[...]
            o = _dot(intra.astype(BF16), v) + _dot((q * d_cross).astype(BF16), state.astype(BF16))
            st_ref[...] = state * g_chunk + _dot_tn((k * dst_ref[h]).astype(BF16), v)
            r = lax.rsqrt(jnp.mean(o * o, axis=-1, keepdims=True) + RMS_EPS)
            gate = _silu(g_ref[pl.ds(r0, ck), cols].astype(F32))
            o_ref[pl.ds(r0, ck), cols] = (o * r * gate).astype(o_ref.dtype)
        return carry

    lax.fori_loop(0, nc, body, 0)


def retention(z, bsz, seq, ch, col0):
    t = z.shape[0]
    nh = RET_HEADS
    dk = ch // nh
    ck = min(RET_CHUNK, seq)
    nc = seq // ck
    half = dk // 2
    pos = jnp.arange(seq, dtype=F32)
    inv = 1.0 / (10000.0 ** jnp.linspace(0.0, 1.0, half, dtype=F32))
    ang = pos[:, None] * inv[None, :]
    cos, sin = jnp.cos(ang), jnp.sin(ang)
    log_g = jnp.log(1.0 - 2.0 ** (-5.0 - jnp.arange(nh, dtype=F32)))
    i = jnp.arange(ck, dtype=F32)
    diff = i[:, None] - i[None, :]
    d_intra = jnp.where(diff >= 0, jnp.exp(log_g[:, None, None] * jnp.maximum(diff, 0.0)), 0.0)
    d_cross = jnp.exp(log_g[:, None] * (i[None, :] + 1.0))[..., None]
    d_state = jnp.exp(log_g[:, None] * (ck - 1.0 - i[None, :]))[..., None]
    cb = col0 // ch
    col = lambda g: (lambda b: (b, cb + g))
    whole = lambda b: (0, 0, 0)
    return pl.pallas_call(
        functools.partial(_retention_kernel, nc=nc, ck=ck, dk=dk),
        grid=(bsz,),
        in_specs=[
            pl.BlockSpec((seq, ch), col(0)),
            pl.BlockSpec((seq, ch), col(1)),
            pl.BlockSpec((seq, ch), col(2)),
            pl.BlockSpec((seq, ch), col(3)),
            pl.BlockSpec((seq, half), lambda b: (0, 0)),
            pl.BlockSpec((seq, half), lambda b: (0, 0)),
            pl.BlockSpec((nh, ck, ck), whole),
            pl.BlockSpec((nh, ck, 1), whole),
            pl.BlockSpec((nh, ck, 1), whole),
        ],
        out_specs=pl.BlockSpec((seq, ch), lambda b: (b, 0)),
        out_shape=jax.ShapeDtypeStruct((t, ch), BF16),
        scratch_shapes=[pltpu.VMEM((dk, dk), F32) for _ in range(nh)],
        compiler_params=_cparams(("parallel",)),
        name="retention",
    )(z, z, z, z, cos, sin, d_intra, d_cross, d_state)


def _route(h, wr_ref, br_ref, ri_ref, rf_ref, cnt_ref, carry_ref, tm):
    logits = _dot_nt(wr_ref[...], h.astype(BF16))
    mx = jnp.max(logits, axis=0, keepdims=True)
    ex = jnp.exp(logits - mx)
    probs = ex / jnp.sum(ex, axis=0, keepdims=True)
    sel = probs + br_ref[...]
    s = [sel[e:e + 1, :] for e in range(N_EXPERTS)]
    p = [probs[e:e + 1, :] for e in range(N_EXPERTS)]
    epg = EXPERTS_PER_GROUP

    def first_argmax(vals, exclude=None):
        best = jnp.full_like(vals[0], -jnp.inf)
        idx = jnp.zeros(vals[0].shape, I32)
        for j, vj in enumerate(vals):
            better = vj > best
            if exclude is not None:
                better = better & (exclude != j)
            idx = jnp.where(better, j, idx)
            best = jnp.where(better, vj, best)
        return idx

    gscore = []
    for g in range(N_GROUPS):
        gs = s[g * epg:(g + 1) * epg]
        best = None
        for a in range(epg):
            for b in range(a + 1, epg):
                pair = gs[a] + gs[b]
                best = pair if best is None else jnp.maximum(best, pair)
        gscore.append(best)
    grp = first_argmax(gscore)

    def pick(rows, index, n):
        out = rows[n - 1]
        for j in range(n - 2, -1, -1):
            out = jnp.where(index == j, rows[j], out)
        return out

    in_s = [pick([s[g * epg + j] for g in range(N_GROUPS)], grp, N_GROUPS) for j in range(epg)]
    in_p = [pick([p[g * epg + j] for g in range(N_GROUPS)], grp, N_GROUPS) for j in range(epg)]
    i1 = first_argmax(in_s)
    i2 = first_argmax(in_s, exclude=i1)
    p1 = pick(in_p, i1, epg)
    p2 = pick(in_p, i2, epg)
    e1 = grp * epg + i1
    e2 = grp * epg + i2
    den = p1 + p2
    g1 = p1 / den
    g2 = p2 / den

    eidx = lax.broadcasted_iota(I32, (N_EXPERTS, tm), 0)
    member = (eidx == e1) | (eidx == e2)
    member_f = jnp.where(member, 1.0, 0.0)
    before = lax.broadcasted_iota(I32, (tm, tm), 0) < lax.broadcasted_iota(I32, (tm, tm), 1)
    prefix = _dot(member_f.astype(BF16), jnp.where(before, 1.0, 0.0).astype(BF16))
    base = prefix + carry_ref[:, 0:1]
    rank1 = jnp.sum(jnp.where(eidx == e1, base, 0.0), axis=0, keepdims=True).astype(I32)
    rank2 = jnp.sum(jnp.where(eidx == e2, base, 0.0), axis=0, keepdims=True).astype(I32)
    carry_ref[...] = carry_ref[...] + jnp.sum(member_f, axis=1, keepdims=True)
    cnt_ref[...] = carry_ref[...].astype(I32)

    zi = jnp.zeros((SUBLANE - 4, tm), I32)
    ri_ref[...] = jnp.concatenate([e1, e2, rank1, rank2, zi], axis=0)
    zf = jnp.zeros((SUBLANE - 2, tm), F32)
    rf_ref[...] = jnp.concatenate([g1, g2, zf], axis=0)


def _mixer_out_kernel(*refs, tm, conv):
    if conv:
        b_ref, c_ref, v_ref, cw_ref, cb_ref, *refs = refs
    else:
        ya_ref, *refs = refs
    (yb_ref, wa_ref, wb_ref, x_ref, modm_ref, modf_ref, nw_ref, wr_ref, br_ref,
     x1_ref, h_ref, ri_ref, rf_ref, cnt_ref, carry_ref, *conv_scratch) = refs
    s = pl.program_id(1)

    @pl.when((pl.program_id(0) == 0) & (s == 0))
    def _():
        carry_ref[...] = jnp.zeros(carry_ref.shape, F32)

    if conv:
        ya = _gated_conv(b_ref, c_ref, v_ref, cw_ref, cb_ref, conv_scratch[0], s, tm).astype(BF16)
    else:
        ya = ya_ref[...]
    y = _dot(ya, wa_ref[...]) + _dot(yb_ref[...], wb_ref[...])
    x1 = x_ref[...] + modm_ref[0, 2:3, :] * y
    x1_ref[...] = x1
    h = _modulated_norm(x1, modf_ref, nw_ref)
    h_ref[...] = _pack_pairs(h)
    _route(h, wr_ref, br_ref, ri_ref, rf_ref, cnt_ref, carry_ref, tm)


def mixer_out(lhs, w_bf, x2, mod_mix, mod_ffn, norm_ffn_w, w_router, b_router, bsz, seq, tm, conv=None):
    t, d = x2.shape
    kh = w_bf.shape[0] // 2
    per = seq // tm
    ne = N_EXPERTS
    row = lambda g: (lambda b, s: (b * per + s, g))
    const = lambda b, s: (0, 0)
    if conv is None:
        (o,) = lhs
        lhs_args = (o, o)
        lhs_specs = [pl.BlockSpec((tm, kh), row(0)), pl.BlockSpec((tm, kh), row(1))]
        scratch = []
    else:
        z, yb = lhs
        conv_w, conv_b = conv
        lhs_args = (z, z, z, conv_w, conv_b.reshape(1, kh), yb)
        lhs_specs = [pl.BlockSpec((tm, kh), row(0)), pl.BlockSpec((tm, kh), row(1)),
                     pl.BlockSpec((tm, kh), row(2)), pl.BlockSpec((CONV_WIDTH, kh), const),
                     pl.BlockSpec((1, kh), const), pl.BlockSpec((tm, kh), row(0))]
        scratch = [pltpu.VMEM((tm + SUBLANE, kh), F32)]
    batch = lambda b, s: (b, 0, 0)
    return pl.pallas_call(
        functools.partial(_mixer_out_kernel, tm=tm, conv=conv is not None),
        grid=(bsz, per),
        in_specs=lhs_specs + [
            pl.BlockSpec((kh, d), const),
            pl.BlockSpec((kh, d), lambda b, s: (1, 0)),
            pl.BlockSpec((tm, d), row(0)),
            pl.BlockSpec((1, 3, d), batch),
            pl.BlockSpec((1, 3, d), batch),
            pl.BlockSpec((1, d), const),
            pl.BlockSpec((ne, d), const),
            pl.BlockSpec((ne, 1), const),
        ],
        out_specs=[
            pl.BlockSpec((tm, d), row(0)),
            pl.BlockSpec((tm, d // 2), row(0)),
            pl.BlockSpec((SUBLANE, tm), lambda b, s: (0, b * per + s)),
            pl.BlockSpec((SUBLANE, tm), lambda b, s: (0, b * per + s)),
            pl.BlockSpec((ne, LANE), const),
        ],
        out_shape=[
            jax.ShapeDtypeStruct((t, d), F32),
            jax.ShapeDtypeStruct((t, d // 2), I32),
            jax.ShapeDtypeStruct((SUBLANE, t), I32),
            jax.ShapeDtypeStruct((SUBLANE, t), F32),
            jax.ShapeDtypeStruct((ne, LANE), I32),
        ],
        scratch_shapes=[pltpu.VMEM((ne, LANE), F32)] + scratch,
        compiler_params=_cparams(("arbitrary", "arbitrary")),
        name="mixer_out",
    )(*lhs_args, w_bf, w_bf, x2, mod_mix, mod_ffn, norm_ffn_w.reshape(1, d),
      w_router.T.astype(BF16), b_router.reshape(ne, 1))


def _dispatch_kernel(dest_ref, pad0_ref, padn_ref, h_ref, xs_ref, hbuf_ref, zero_ref, sem, zsem,
                     *, tm, nt):
    i = pl.program_id(0)
    cur = i % 2
    base = i * tm

    def zero_row(e, r):
        return pltpu.make_async_copy(zero_ref.at[pl.ds(0, 1), :],
                                     xs_ref.at[pl.ds(pad0_ref[e] + r, 1), :], zsem)

    def tile_done(buf):
        for _ in range(2):
            pltpu.make_async_copy(hbuf_ref.at[buf], xs_ref.at[pl.ds(0, tm), :], sem.at[buf]).wait()

    @pl.when(i == 0)
    def _():
        zero_ref[...] = jnp.zeros(zero_ref.shape, zero_ref.dtype)
        for e in range(N_EXPERTS):
            lax.fori_loop(0, padn_ref[e], lambda r, c, e=e: (zero_row(e, r).start(), c)[1], 0)

    hbuf_ref[cur] = h_ref[...]

    def start(r, carry):
        for slot in range(2):
            d = dest_ref[2 * (base + r) + slot]
            pltpu.make_async_copy(hbuf_ref.at[cur, pl.ds(r, 1), :], xs_ref.at[pl.ds(d, 1), :],
                                  sem.at[cur]).start()
        return carry

    lax.fori_loop(0, tm, start, 0, unroll=DMA_UNROLL)

    @pl.when(i > 0)
    def _():
        tile_done(1 - cur)

    @pl.when(i == nt - 1)
    def _():
        tile_done(cur)
        for e in range(N_EXPERTS):
            lax.fori_loop(0, padn_ref[e], lambda r, c, e=e: (zero_row(e, r).wait(), c)[1], 0)


def dispatch(h, dest, pad_start, pad_count, n_rows, tm):
    t, d = h.shape
    return pl.pallas_call(
        functools.partial(_dispatch_kernel, tm=tm, nt=t // tm),
        grid_spec=pltpu.PrefetchScalarGridSpec(
            num_scalar_prefetch=3,
            grid=(t // tm,),
            in_specs=[pl.BlockSpec((tm, d), lambda i, *_: (i, 0))],
            out_specs=pl.BlockSpec(memory_space=pl.ANY),
            scratch_shapes=[pltpu.VMEM((2, tm, d), h.dtype), pltpu.VMEM((SUBLANE, d), h.dtype),
                            pltpu.SemaphoreType.DMA((2,)), pltpu.SemaphoreType.DMA(())],
        ),
        out_shape=jax.ShapeDtypeStruct((n_rows, d), h.dtype),
        compiler_params=_cparams(("arbitrary",)),
        name="moe_dispatch",
    )(dest, pad_start, pad_count, h)


def _ffn_kernel(be_ref, bv_ref, xs_ref, w1_ref, w3_ref, w2_ref, *refs, riders):
    ys_ref = refs[len(riders)]
    _ride_casts(riders, refs[:len(riders)], refs[len(riders) + 1:], pl.program_id(0))
    valid = bv_ref[pl.program_id(0)]

    @pl.when(valid > 0)
    def _():
        rows = lax.broadcasted_iota(I32, xs_ref.shape, 0)
        x = _unpack_pairs(jnp.where(rows < valid, xs_ref[...], 0))
        h1 = _dot(x, w1_ref[0])
        h3 = _dot(x, w3_ref[0])
        a = (_silu(h1) * h3).astype(BF16)
        ys_ref[...] = _dot(a, w2_ref[0])

    @pl.when(valid <= 0)
    def _():
        ys_ref[...] = jnp.zeros(ys_ref.shape, ys_ref.dtype)


def grouped_ffn(xs, block_e, block_valid, w1, w3, w2, bm, riders=()):
    n_rows = xs.shape[0]
    d, f = w1.shape[1], w1.shape[2]
    flat = lambda i, be, bv: i
    return pl.pallas_call(
        functools.partial(_ffn_kernel, riders=tuple(r.nblk for r in riders)),
        grid_spec=pltpu.PrefetchScalarGridSpec(
            num_scalar_prefetch=2,
            grid=(n_rows // bm,),
            in_specs=[
                pl.BlockSpec((bm, d // 2), lambda i, be, bv: (i, 0)),
                pl.BlockSpec((1, d, f), lambda i, be, bv: (be[i], 0, 0)),
                pl.BlockSpec((1, d, f), lambda i, be, bv: (be[i], 0, 0)),
                pl.BlockSpec((1, f, d), lambda i, be, bv: (be[i], 0, 0)),
            ] + [r.in_spec(flat) for r in riders],
            out_specs=[pl.BlockSpec((bm, d), lambda i, be, bv: (i, 0))] + [r.out_spec(flat) for r in riders],
        ),
        out_shape=[jax.ShapeDtypeStruct((n_rows, d), F32)] + [r.out_shape for r in riders],
        compiler_params=_cparams(("arbitrary",)),
        name="moe_ffn",
    )(block_e, block_valid, xs, w1, w3, w2, *[r.src for r in riders])


def _combine_kernel(dest_ref, ys_ref, x_ref, mod_ref, gf_ref, *refs, tm, nt, project):
    if project:
        modn_ref, nwn_ref, w_ref, o_ref, z_ref, a_ref, b_ref, sem = refs
    else:
        o_ref, a_ref, b_ref, sem = refs
    i = pl.program_id(0)
    cur = i % 2

    def fetch_row(base, r, buf):
        for slot, ref in ((0, a_ref), (1, b_ref)):
            d = dest_ref[2 * (base + r) + slot]
            pltpu.make_async_copy(ys_ref.at[pl.ds(d, 1), :], ref.at[buf, pl.ds(r, 1), :],
                                  sem.at[buf]).start()

    def fetch(tile, buf):
        lax.fori_loop(0, tm, lambda r, c: (fetch_row(tile * tm, r, buf), c)[1], 0, unroll=DMA_UNROLL)

    def fetched(buf):
        pltpu.make_async_copy(ys_ref.at[pl.ds(0, tm), :], a_ref.at[buf], sem.at[buf]).wait()
        pltpu.make_async_copy(ys_ref.at[pl.ds(0, tm), :], b_ref.at[buf], sem.at[buf]).wait()

    @pl.when(i == 0)
    def _():
        fetch(0, 0)

    if not project:
        @pl.when(i + 1 < nt)
        def _():
            fetch(i + 1, 1 - cur)

    fetched(cur)
    y = gf_ref[:, 0:1] * a_ref[cur] + gf_ref[:, 1:2] * b_ref[cur]
    out = x_ref[...] + mod_ref[0, 2:3, :] * y
    o_ref[...] = out
    if project:
        h = _modulated_norm(out, modn_ref, nwn_ref).astype(BF16)
        base = jnp.minimum(i + 1, nt - 1) * tm
        for r in range(tm):
            fetch_row(base, r, 1 - cur)
        z_ref[...] = _dot(h, w_ref[...]).astype(z_ref.dtype)

        @pl.when(i == nt - 1)
        def _():
            fetched(1 - cur)


def combine(ys, dest, x2, mod, gates_t, seq, tm, project=None):
    t, d = x2.shape
    per = seq // tm
    row = lambda i, dest: (i, 0)
    batch = lambda i, dest: (i // per, 0, 0)
    const = lambda i, dest: (0, 0)
    in_specs = [pl.BlockSpec(memory_space=pl.ANY), pl.BlockSpec((tm, d), row),
                pl.BlockSpec((1, 3, d), batch), pl.BlockSpec((tm, SUBLANE), row)]
    out_specs = [pl.BlockSpec((tm, d), row)]
    out_shape = [jax.ShapeDtypeStruct((t, d), F32)]
    args = [dest, ys, x2, mod, gates_t]
    if project is not None:
        mod_next, nw_next, w_bf = project
        n = w_bf.shape[1]
        in_specs += [pl.BlockSpec((1, 3, d), batch), pl.BlockSpec((1, d), const), pl.BlockSpec((d, n), const)]
        out_specs += [pl.BlockSpec((tm, n), row)]
        out_shape += [jax.ShapeDtypeStruct((t, n), BF16)]
        args += [mod_next, nw_next.reshape(1, d), w_bf]
    return pl.pallas_call(
        functools.partial(_combine_kernel, tm=tm, nt=t // tm, project=project is not None),
        grid_spec=pltpu.PrefetchScalarGridSpec(
            num_scalar_prefetch=1,
            grid=(t // tm,),
            in_specs=in_specs,
            out_specs=out_specs,
            scratch_shapes=[pltpu.VMEM((2, tm, d), F32), pltpu.VMEM((2, tm, d), F32),
                            pltpu.SemaphoreType.DMA((2,))],
        ),
        out_shape=out_shape,
        compiler_params=_cparams(("arbitrary",)),
        name="moe_combine",
    )(*args)


def _moe_blocks(t):
    return (2 * t) // MOE_ROWS + N_EXPERTS


def moe(x2, h, ri, rf, cnt, mod, w1, w3, w2, seq, project=None, riders=()):
    t, d = x2.shape
    bm = MOE_ROWS
    tm = min(256, seq)
    ne = N_EXPERTS
    counts = cnt[:, 0]
    nblk = (counts + bm - 1) // bm
    blk_end = jnp.cumsum(nblk)
    blk_start = blk_end - nblk
    eids = jnp.arange(ne, dtype=I32)
    e12 = ri[0:2].T
    row0 = jnp.sum(jnp.where(e12[..., None] == eids, blk_start * bm, 0), axis=-1)
    dest = (row0 + ri[2:4].T).reshape(2 * t).astype(I32)
    n_blocks = _moe_blocks(t)
    bidx = jnp.arange(n_blocks, dtype=I32)
    block_e = jnp.minimum(jnp.sum(bidx[:, None] >= blk_end[None, :], axis=1), ne - 1).astype(I32)
    onehot = block_e[:, None] == eids
    cnt_b = jnp.sum(jnp.where(onehot, counts, 0), axis=1)
    start_b = jnp.sum(jnp.where(onehot, blk_start, 0), axis=1)
    block_valid = jnp.clip(cnt_b - (bidx - start_b) * bm, 0, bm).astype(I32)
    pad_start = (blk_start * bm + counts).astype(I32)
    pad_end = jnp.where(eids == ne - 1, n_blocks * bm, blk_end * bm)
    xs = dispatch(h, dest, pad_start, (pad_end - pad_start).astype(I32), n_blocks * bm, tm)
    ys, *casts = grouped_ffn(xs, block_e, block_valid, w1, w3, w2, bm, riders)
    return combine(ys, dest, x2, mod, rf.T, seq, tm, project), casts


def _t5_bucket(n):
    max_exact = N_BUCKETS // 2
    nf = jnp.maximum(n, 1).astype(F32)
    large = max_exact + (jnp.log(nf / max_exact) / math.log(MAX_DISTANCE / max_exact)
                         * (N_BUCKETS - max_exact)).astype(I32)
    large = jnp.minimum(large, N_BUCKETS - 1)
    return jnp.where(n < max_exact, n, large)


def _sortable(score):
    bits = pltpu.bitcast(score, I32)
    return bits ^ ((bits >> 31) & 0x7FFFFFFF)


def _dsa_kernel(q_ref, k_ref, v_ref, qi_ref, kw_ref, wq_ref, tb_ref, qn_ref, kn_ref, o_ref,
                khat_ref, vt_ref, qrt_ref, qirt_ref, wt_ref, mt_ref, hi_ref, lo_ref, lo2_ref, nm_ref, lg_ref,
                *acc_refs,
                nh, dh, nih, di, topk, nb):
    j = pl.program_id(1)
    qb = Q_BLOCK
    kc = 2 * qb
    nch = (j + 2) // 2
    pair = 2 * qb

    def transpose(x):
        return x.astype(F32).T

    @pl.when(j == 0)
    def _():
        k = k_ref[...].astype(F32)
        r = lax.rsqrt(jnp.mean(k * k, axis=-1, keepdims=True) + RMS_EPS)
        khat_ref[...] = (k * r * kn_ref[...]).astype(BF16)
        for c in range(nb // 2):
            for s in range(2):
                blk = v_ref[(2 * c + s) * qb:(2 * c + s + 1) * qb, :]
                vt_ref[c, 0:dh, s * qb:(s + 1) * qb] = transpose(blk).astype(BF16)
            vt_ref[c, dh:dh + ONES_ROWS, :] = jnp.ones((ONES_ROWS, kc), BF16)
        for ref in (hi_ref, lo_ref, lo2_ref):
            ref[...] = jnp.full(ref.shape, I16_MIN, I16)

    for h in range(nh):
        qt = transpose(q_ref[:, h * dh:(h + 1) * dh])
        r = lax.rsqrt(jnp.mean(qt * qt, axis=0, keepdims=True) + RMS_EPS)
        qrt_ref[:, h * qb:(h + 1) * qb] = (qt * r * qn_ref[...]).astype(BF16)
    for g in range(nih * di // LANE):
        two = transpose(qi_ref[:, g * LANE:(g + 1) * LANE])
        for s in range(LANE // di):
            h = g * (LANE // di) + s
            qirt_ref[:, h * qb:(h + 1) * qb] = two[s * di:(s + 1) * di, :].astype(BF16)
    wt_ref[...] = transpose(wq_ref[...])

    key_l = lax.broadcasted_iota(I32, (kc, qb), 0)
    q_pos = j * qb + lax.broadcasted_iota(I32, (kc, qb), 1)

    def score_body(c, carry):
        r0 = pl.multiple_of(c * kc, kc)
        ki = kw_ref[pl.ds(r0, kc), 0:di]
        acc = jnp.zeros((kc, qb), F32)
        for g in range(nih // 2):
            rel = _dot(ki, qirt_ref[:, g * pair:(g + 1) * pair])
            for s in range(2):
                h = 2 * g + s
                acc = acc + wt_ref[di + h:di + h + 1, :] * jnp.maximum(rel[:, s * qb:(s + 1) * qb], 0.0)
        m = jnp.where(r0 + key_l <= q_pos, _sortable(acc), INT_MIN)
        mt_ref[c] = m
        hi_ref[c] = (m >> 16).astype(I16)
        lo_ref[c] = ((m & 0xFFFF) + I16_MIN).astype(I16)
        return carry

    lax.fori_loop(0, nch, score_body, 0)

    def count_ge(cand):
        def cbody(c, acc):
            hit = jnp.where(mt_ref[c] >= cand, 1, 0)
            return acc + jnp.sum(hit.reshape(kc // SUBLANE, SUBLANE, qb), axis=0)
        acc = lax.fori_loop(0, nch, cbody, jnp.zeros((SUBLANE, qb), I32))
        return jnp.sum(acc, axis=0, keepdims=True)

    def fold16(hit):
        parts = [hit[PACK16 * i:PACK16 * (i + 1), :] for i in range(kc // PACK16)]
        while len(parts) > 1:
            parts = [parts[i] + parts[i + 1] for i in range(0, len(parts), 2)]
        return parts[0]

    def total16(acc):
        return jnp.sum(acc.astype(F32), axis=0, keepdims=True).astype(I32)

    one, nil = jnp.ones((), BF16), jnp.zeros((), BF16)

    def count16(src_ref, cand, n):
        c16 = cand.astype(I16)
        acc = fold16(jnp.where(src_ref[0] >= c16, one, nil))
        for c in range(1, n):
            acc = acc + fold16(jnp.where(src_ref[c] >= c16, one, nil))
        return total16(acc)

    def search16(src_ref, kth, n):
        zero = jnp.zeros((1, qb), I32)
        ans0 = jnp.where(count16(src_ref, zero, n) >= kth, zero, I16_MIN)

        def bit_body(bi, ans):
            cand = ans | (1 << (14 - bi))
            return jnp.where(count16(src_ref, cand, n) >= kth, cand, ans)

        return lax.fori_loop(0, 15, bit_body, ans0)

    def search(n):
        top = search16(hi_ref, jnp.full((1, qb), topk, I32), n)
        top16 = top.astype(I16)
        acc = jnp.zeros((PACK16, qb), BF16)
        for c in range(n):
            hi = hi_ref[c]
            lo2_ref[c] = jnp.where(hi == top16, lo_ref[c], jnp.full((), I16_MIN, I16))
            acc = acc + fold16(jnp.where(hi > top16, one, nil))
        above = total16(acc)
        low = search16(lo2_ref, topk - above, n)
        return top * 65536 + (low - I16_MIN), above + count16(lo2_ref, low, n)

    def search_any():
        return lax.cond(nch <= nb // 4, functools.partial(search, nb // 4),
                        functools.partial(search, nb // 2))

    selecting = (j + 1) * qb > topk
    thr, n_ge = lax.cond(selecting, search_any,
                         lambda: (jnp.full((1, qb), INT_MIN + 1, I32), jnp.zeros((1, qb), I32)))

    def plain_mask():
        def body(c, carry):
            nm_ref[c] = jnp.where(mt_ref[c] >= thr, 0.0, NEG_BIG)
            return carry
        lax.fori_loop(0, nch, body, 0)

    def tie_mask():
        need = (topk - count_ge(thr + 1)).astype(F32)
        tri = jnp.where(lax.broadcasted_iota(I32, (kc, kc), 1) <= lax.broadcasted_iota(I32, (kc, kc), 0),
                        1.0, 0.0).astype(BF16)

        def body(c, seen):
            m = mt_ref[c]
            tie = m == thr
            upto = _dot(tri, jnp.where(tie, 1.0, 0.0).astype(BF16)) + seen
            nm_ref[c] = jnp.where((m > thr) | (tie & (upto <= need)), 0.0, NEG_BIG)
            return upto[kc - 1:kc, :]
        lax.fori_loop(0, nch, body, jnp.zeros((1, qb), F32))

    crowded = jnp.max(n_ge) > topk
    lax.cond(crowded, tie_mask, plain_mask)

    npair = nh // 2
    pcols = [slice(pr * pair, (pr + 1) * pair) for pr in range(npair)]
    nfar = jnp.maximum((j - 1) // 2, 0)

    def sweep1(c, mruns, near):
        r0 = pl.multiple_of(c * kc, kc)
        kh = khat_ref[pl.ds(r0, kc), :]
        nm = nm_ref[c]
        nm2 = jnp.concatenate([nm, nm], axis=1)
        if near:
            k0 = jnp.clip(j - 2 * c, 0, 2)
            k1 = jnp.clip(j - 2 * c - 1, 0, 2)
        out = []
        for pr in range(npair):
            lg = _dot(kh, qrt_ref[:, pcols[pr]]) + nm2
            if near:
                lg = lg + jnp.concatenate([tb_ref[k0, :, pcols[pr]], tb_ref[k1, :, pcols[pr]]], axis=0)
            lg_ref[c, :, pcols[pr]] = lg
            out.append(jnp.maximum(mruns[pr], jnp.max(lg.reshape(kc // SUBLANE, SUBLANE, pair), axis=0)))
        return tuple(out)

    def sweep1_many(t, mruns, n):
        for k in range(n):
            mruns = sweep1(n * t + k, mruns, near=False)
        return mruns

    mruns = tuple(jnp.full((SUBLANE, pair), NEG_BIG, F32) for _ in range(npair))
    mruns = lax.fori_loop(0, nfar // 4, functools.partial(sweep1_many, n=4), mruns)
    mruns = lax.fori_loop(2 * (nfar // 4), nfar // 2, functools.partial(sweep1_many, n=2), mruns)
    mruns = lax.fori_loop(2 * (nfar // 2), nfar, functools.partial(sweep1, near=False), mruns)
    mruns = lax.fori_loop(nfar, nch, functools.partial(sweep1, near=True), mruns)
    mrows = [jnp.max(m, axis=0, keepdims=True) for m in mruns]
    for acc_ref in acc_refs:
        acc_ref[...] = jnp.zeros(acc_ref.shape, F32)

    def sweep2(c, carry):
        vt = vt_ref[c]
        for pr in range(npair):
            p = jnp.exp2((lg_ref[c, :, pcols[pr]] - mrows[pr]).astype(BF16))
            acc_refs[pr][...] += _dot(vt, p)
        return carry

    def sweep2_many(t, carry, n):
        vts = [vt_ref[n * t + k] for k in range(n)]
        for pr in range(npair):
            part = None
            for k in range(n):
                p = jnp.exp2((lg_ref[n * t + k, :, pcols[pr]] - mrows[pr]).astype(BF16))
                part = _dot(vts[k], p) if part is None else part + _dot(vts[k], p)
            acc_refs[pr][...] += part
        return carry

    lax.fori_loop(0, nch // 4, functools.partial(sweep2_many, n=4), 0)
    lax.fori_loop(2 * (nch // 4), nch // 2, functools.partial(sweep2_many, n=2), 0)
    lax.fori_loop(2 * (nch // 2), nch, sweep2, 0)
    for pr in range(npair):
        out = acc_refs[pr][0:dh, :] / acc_refs[pr][dh:dh + 1, :]
        for s in range(2):
            h = 2 * pr + s
            o_ref[:, h * dh:(h + 1) * dh] = transpose(
                out[:, s * qb:(s + 1) * qb].astype(BF16)).astype(o_ref.dtype)


def dsa_attention(z, rel_bias, q_norm, k_norm, bsz, seq, d):
    t = z.shape[0]
    nh, dh, nih, di = ATT_HEADS, d // ATT_HEADS, IDX_HEADS, IDX_DIM
    qb = Q_BLOCK
    nb = seq // qb
    topk = min(DSA_TOPK_MAX, seq // 4)
    assert topk % qb == 0 and dh == LANE and nih * di == d // 2 and nb % 2 == 0 and LANE % di == 0
    assert (nb // 2) * (2 * qb // PACK16) <= 256
    i = jnp.arange(qb)
    dist = i[None, :] - i[:, None]
    def lookup(bucket):
        hot = bucket[..., None] == jnp.arange(N_BUCKETS, dtype=I32)
        return jnp.sum(jnp.where(hot[..., None], rel_bias, 0.0), axis=-2)

    tabs = [lookup(_t5_bucket(jnp.maximum(dist + off * qb, 0))) for off in range(2)]
    far = jnp.broadcast_to(lookup(_t5_bucket(jnp.full((), 2 * qb, I32))), (qb, qb, nh))
    tb = jnp.stack(tabs + [far]).transpose(0, 1, 3, 2).reshape(3, qb, nh * qb).astype(F32)
    tb = (tb - tb[2:3]) * math.log2(math.e)
    qn_rows = jnp.broadcast_to((q_norm * (dh ** -0.5 * math.log2(math.e)))[:, None], (dh, qb)).astype(F32)
    cq = d // LANE
    return pl.pallas_call(
        functools.partial(_dsa_kernel, nh=nh, dh=dh, nih=nih, di=di, topk=topk, nb=nb),
        grid=(bsz, nb),
        in_specs=[
            pl.BlockSpec((qb, d), lambda b, j: (b * nb + j, 0)),
            pl.BlockSpec((seq, LANE), lambda b, j: (b, cq + cq // 2)),
            pl.BlockSpec((seq, LANE), lambda b, j: (b, cq + cq // 2 + 1)),
            pl.BlockSpec((qb, d // 2), lambda b, j: (b * nb + j, 2)),
            pl.BlockSpec((seq, LANE), lambda b, j: (b, cq + cq // 2 + 2)),
            pl.BlockSpec((qb, LANE), lambda b, j: (b * nb + j, cq + cq // 2 + 2)),
            pl.BlockSpec((3, qb, nh * qb), lambda b, j: (0, 0, 0)),
            pl.BlockSpec((dh, qb), lambda b, j: (0, 0)),
            pl.BlockSpec((1, dh), lambda b, j: (0, 0)),
        ],
        out_specs=pl.BlockSpec((qb, d), lambda b, j: (b * nb + j, 0)),
        out_shape=jax.ShapeDtypeStruct((t, d), BF16),
        scratch_shapes=[
            pltpu.VMEM((seq, dh), BF16),
            pltpu.VMEM((nb // 2, dh + ONES_ROWS, 2 * qb), BF16),
            pltpu.VMEM((dh, nh * qb), BF16),
            pltpu.VMEM((di, nih * qb), BF16),
            pltpu.VMEM((LANE, qb), F32),
            pltpu.VMEM((nb // 2, 2 * qb, qb), I32),
            pltpu.VMEM((nb // 2, 2 * qb, qb), I16),
            pltpu.VMEM((nb // 2, 2 * qb, qb), I16),
            pltpu.VMEM((nb // 2, 2 * qb, qb), I16),
            pltpu.VMEM((nb // 2, 2 * qb, qb), F32),
            pltpu.VMEM((nb // 2, 2 * qb, nh * qb), F32),
        ] + [pltpu.VMEM((dh + ONES_ROWS, 2 * qb), F32)
             for _ in range(nh // 2)],
        compiler_params=_cparams(("arbitrary", "arbitrary")),
        name="dsa_attention",
    )(z, z, z, z, z, z, tb, qn_rows, k_norm.reshape(1, dh))


def _dsa_weight(c_w_in, d):
    dh = d // ATT_HEADS
    nqi = IDX_HEADS * IDX_DIM
    q, k, v, qi, ki, wi = jnp.split(
        c_w_in, [d, d + dh, d + 2 * dh, d + 2 * dh + nqi, d + 2 * dh + nqi + IDX_DIM], axis=1)
    used = d + nqi + 2 * dh + IDX_DIM + IDX_HEADS
    total = -(-used // 512) * 512
    pad = jnp.zeros((d, total - used), c_w_in.dtype)
    return jnp.concatenate([q, qi, k, v, ki, wi, pad], axis=1).astype(BF16)


def kernel(x, c, rel_bias, w_router, b_router, norm_mix, ada_w_mix, ada_b_mix, norm_ffn, ada_w_ffn,
           ada_b_ffn, ab_w_in, ab_conv_w, ab_conv_b, ab_w_out, c_w_in, c_q_norm, c_k_norm, c_w_out,
           moe_w1, moe_w3, moe_w2):
    bsz, seq, d = x.shape
    depth = norm_mix.shape[0]
    t = bsz * seq
    ch = d // 2
    tm_out = min(512, seq)
    x2 = x.reshape(t, d)
    mod_mix = ada_mod(c, ada_w_mix, ada_b_mix).reshape(depth, bsz, 3, d)
    mod_ffn = ada_mod(c, ada_w_ffn, ada_b_ffn).reshape(depth, bsz, 3, d)
    experts = (moe_w1, moe_w3, moe_w2)
    as_experts = lambda casts: [cw.reshape(w.shape[1:]) for cw, w in zip(casts, experts)]
    w_in = [ab_w_in[l // 2].astype(BF16) if l % 2 == 0 else _dsa_weight(c_w_in[l // 2], d)
            for l in range(depth)]
    tm_in, tn_in = min(1024, seq), 1024
    z = None
    w_moe = None
    for layer in range(depth):
        i = layer // 2
        if z is None:
            riders = _cast_riders(experts, layer, (t // tm_in) * (w_in[layer].shape[1] // tn_in))
            h_mix = prenorm(x2, mod_mix[layer], norm_mix[layer], seq, tm_out)
            z, *casts = inproj(h_mix, w_in[layer], tm_in, tn_in, riders or ())
            if riders:
                w_moe = as_experts(casts)
        if w_moe is None:
            w_moe = [w[layer].astype(BF16) for w in experts]
        if layer % 2 == 0:
            y_b = retention(z, bsz, seq, ch, 3 * ch)
            lhs, w_out, conv = (z, y_b), ab_w_out[i], (ab_conv_w[i], ab_conv_b[i])
        else:
            o = dsa_attention(z, rel_bias, c_q_norm[i], c_k_norm[i], bsz, seq, d)
            lhs, w_out, conv = (o,), c_w_out[i], None
        x2, h, ri, rf, cnt = mixer_out(lhs, w_out.astype(BF16), x2, mod_mix[layer], mod_ffn[layer],
                                       norm_ffn[layer], w_router, b_router, bsz, seq, tm_out, conv)
        last = layer + 1 == depth
        nxt = None if last else (mod_mix[layer + 1], norm_mix[layer + 1], w_in[layer + 1])
        riders = None if last else _cast_riders(experts, layer + 1, _moe_blocks(t))
        res, casts = moe(x2, h, ri, rf, cnt, mod_ffn[layer], *w_moe, seq, nxt, riders or ())
        x2, z = (res[0], None) if last else (res[0], res[1])
        w_moe = as_experts(casts) if riders else None
    return x2.reshape(bsz, seq, d)
```

```python
import functools
import math
from typing import NamedTuple

import jax
import jax.numpy as jnp
from jax import lax
from jax.experimental import pallas as pl
from jax.experimental.pallas import tpu as pltpu

F32 = jnp.float32
BF16 = jnp.bfloat16
I32 = jnp.int32
I16 = jnp.int16

RMS_EPS = 1e-6
CONV_WIDTH = 3
RET_HEADS = 4
ATT_HEADS = 16
IDX_HEADS = 16
IDX_DIM = 64
DSA_TOPK_MAX = 256
Q_BLOCK = 128
N_BUCKETS = 32
MAX_DISTANCE = 128
N_EXPERTS = 16
N_GROUPS = 4
EXPERTS_PER_GROUP = N_EXPERTS // N_GROUPS
RET_CHUNK = 256
MOE_ROWS = 256
DMA_UNROLL = 8
RIDER_BLOCK_BYTES = 1024 * 1024
ONES_ROWS = 16
LANE = 128
SUBLANE = 8
VMEM_LIMIT = 48 * 1024 * 1024
NEG_BIG = -1e30
INT_MIN = -(2 ** 31)
I16_MIN = -(2 ** 15)
HIGH_HALF = -(2 ** 16)
PACK16 = 16


def _cparams(sem, vmem=VMEM_LIMIT):
    return pltpu.CompilerParams(dimension_semantics=sem, vmem_limit_bytes=vmem)


def _dot(a, b):
    return jnp.dot(a, b, preferred_element_type=F32)


def _dot_nt(a, b):
    return lax.dot_general(a, b, (((1,), (1,)), ((), ())), preferred_element_type=F32)


def _dot_tn(a, b):
    return lax.dot_general(a, b, (((0,), (0,)), ((), ())), preferred_element_type=F32)


def _silu(x):
    return x * jax.nn.sigmoid(x)


def _pack_pairs(x):
    n = x.shape[1] // 2
    xb = x.astype(BF16).astype(F32)
    lo = pltpu.bitcast(xb[:, :n], I32)
    hi = pltpu.bitcast(xb[:, n:], I32)
    return lax.shift_right_logical(lo, 16) | (hi & HIGH_HALF)


def _unpack_pairs(u):
    lo = pltpu.bitcast(u << 16, F32)
    hi = pltpu.bitcast(u & HIGH_HALF, F32)
    return jnp.concatenate([lo, hi], axis=1).astype(BF16)


class _CastRider(NamedTuple):
    src: jax.Array
    rows: int
    first_blk: int
    nblk: int

    def in_spec(self, flat):
        return pl.BlockSpec((self.rows, self.src.shape[1]),
                            lambda *g: (jnp.minimum(flat(*g), self.nblk - 1) + self.first_blk, 0))

    def out_spec(self, flat):
        return pl.BlockSpec((self.rows, self.src.shape[1]),
                            lambda *g: (jnp.minimum(flat(*g), self.nblk - 1), 0))

    @property
    def out_shape(self):
        return jax.ShapeDtypeStruct((self.rows * self.nblk, self.src.shape[1]), BF16)


def _cast_riders(weights, layer, n_steps):
    riders = []
    for w in weights:
        per_layer, cols = w.shape[1] * w.shape[2], w.shape[3]
        rows = PACK16
        while rows * n_steps < per_layer:
            rows *= 2
        if per_layer % rows or rows * cols * 4 > RIDER_BLOCK_BYTES:
            return None
        nblk = per_layer // rows
        riders.append(_CastRider(w.reshape(-1, cols), rows, layer * nblk, nblk))
    return riders


def _ride_casts(nblks, src_refs, dst_refs, step):
    for nblk, src, dst in zip(nblks, src_refs, dst_refs):
        @pl.when(step < nblk)
        def _():
            dst[...] = src[...].astype(BF16)


def _ada_kernel(c_ref, w_ref, b_ref, o_ref):
    cs = _silu(c_ref[...])
    o_ref[0] = _dot(cs.astype(BF16), w_ref[0].astype(BF16)) + b_ref[0]


def ada_mod(c, ada_w, ada_b):
    nl, d, n3 = ada_w.shape
    bsz = c.shape[0]
    tn = min(512, n3)
    return pl.pallas_call(
        _ada_kernel,
        grid=(nl, n3 // tn),
        in_specs=[
            pl.BlockSpec((bsz, d), lambda l, j: (0, 0)),
            pl.BlockSpec((1, d, tn), lambda l, j: (l, 0, j)),
            pl.BlockSpec((1, 1, tn), lambda l, j: (l, 0, j)),
        ],
        out_specs=pl.BlockSpec((1, bsz, tn), lambda l, j: (l, 0, j)),
        out_shape=jax.ShapeDtypeStruct((nl, bsz, n3), F32),
        compiler_params=_cparams(("parallel", "parallel")),
        name="ada_mod",
    )(c, ada_w, ada_b.reshape(nl, 1, n3))


def _modulated_norm(x, mod_ref, nw_ref):
    r = lax.rsqrt(jnp.mean(x * x, axis=-1, keepdims=True) + RMS_EPS)
    return x * r * nw_ref[...] * (1.0 + mod_ref[0, 1:2, :]) + mod_ref[0, 0:1, :]


def _prenorm_kernel(x_ref, mod_ref, nw_ref, h_ref):
    h_ref[...] = _modulated_norm(x_ref[...], mod_ref, nw_ref).astype(h_ref.dtype)


def prenorm(x2, mod, norm_w, seq, tm):
    t, d = x2.shape
    per = seq // tm
    return pl.pallas_call(
        _prenorm_kernel,
        grid=(t // tm,),
        in_specs=[
            pl.BlockSpec((tm, d), lambda i: (i, 0)),
            pl.BlockSpec((1, 3, d), lambda i: (i // per, 0, 0)),
            pl.BlockSpec((1, d), lambda i: (0, 0)),
        ],
        out_specs=pl.BlockSpec((tm, d), lambda i: (i, 0)),
        out_shape=jax.ShapeDtypeStruct((t, d), BF16),
        compiler_params=_cparams(("parallel",)),
        name="prenorm",
    )(x2, mod, norm_w.reshape(1, d))


def _inproj_kernel(h_ref, w_ref, *refs, riders):
    o_ref = refs[len(riders)]
    step = pl.program_id(0) * pl.num_programs(1) + pl.program_id(1)
    _ride_casts(riders, refs[:len(riders)], refs[len(riders) + 1:], step)
    o_ref[...] = _dot(h_ref[...], w_ref[...]).astype(o_ref.dtype)


def inproj(h_bf, w_bf, tm, tn, riders=()):
    t, d = h_bf.shape
    n = w_bf.shape[1]
    nj = n // tn
    flat = lambda i, j: i * nj + j
    return pl.pallas_call(
        functools.partial(_inproj_kernel, riders=tuple(r.nblk for r in riders)),
        grid=(t // tm, nj),
        in_specs=[
            pl.BlockSpec((tm, d), lambda i, j: (i, 0)),
            pl.BlockSpec((d, tn), lambda i, j: (0, j)),
        ] + [r.in_spec(flat) for r in riders],
        out_specs=[pl.BlockSpec((tm, tn), lambda i, j: (i, j))] + [r.out_spec(flat) for r in riders],
        out_shape=[jax.ShapeDtypeStruct((t, n), BF16)] + [r.out_shape for r in riders],
        compiler_params=_cparams(("arbitrary", "arbitrary")),
        name="inproj",
    )(h_bf, w_bf, *[r.src for r in riders])


def _gated_conv(b_ref, c_ref, v_ref, w_ref, cb_ref, u_ref, s, tm):
    u = c_ref[...].astype(F32) * v_ref[...].astype(F32)

    @pl.when(s == 0)
    def _():
        u_ref[0:SUBLANE, :] = jnp.zeros((SUBLANE, u.shape[1]), F32)

    @pl.when(s > 0)
    def _():
        u_ref[0:SUBLANE, :] = u_ref[tm:tm + SUBLANE, :]

    u_ref[SUBLANE:SUBLANE + tm, :] = u
    conv = (cb_ref[...]
            + u_ref[SUBLANE - 2:SUBLANE - 2 + tm, :] * w_ref[0:1, :]
            + u_ref[SUBLANE - 1:SUBLANE - 1 + tm, :] * w_ref[1:2, :]
            + u * w_ref[2:3, :])
    return b_ref[...].astype(F32) * conv


def _retention_kernel(q_ref, k_ref, v_ref, g_ref, cos_ref, sin_ref, din_ref, dcr_ref, dst_ref,
                      o_ref, *st_refs, nc, ck, dk):
    half = dk // 2
    for st_ref in st_refs:
        st_ref[...] = jnp.zeros(st_ref.shape, F32)

    def rot(x, cos, sin):
        x1, x2 = x[:, :half], x[:, half:]
        return jnp.concatenate([x1 * cos - x2 * sin, x1 * sin + x2 * cos], axis=-1)

    def body(c, carry):
        r0 = pl.multiple_of(c * ck, ck)
        cos = cos_ref[pl.ds(r0, ck), :]
        sin = sin_ref[pl.ds(r0, ck), :]
        for h, st_ref in enumerate(st_refs):
            cols = slice(h * dk, (h + 1) * dk)
            d_cross = dcr_ref[h]
            g_chunk = dcr_ref[h, ck - 1:ck, :]
            q = rot(q_ref[pl.ds(r0, ck), cols].astype(F32), cos, sin)
            k = rot(k_ref[pl.ds(r0, ck), cols].astype(F32), cos, sin) * (dk ** -0.5)
            v = v_ref[pl.ds(r0, ck), cols]
            intra = _dot_nt(q.astype(BF16), k.astype(BF16)) * din_ref[h]
            state = st_ref[...]
            o = _dot(intra.astype(BF16), v) + _dot((q * d_cross).astype(BF16), state.astype(BF16))
            st_ref[...] = state * g_chunk + _dot_tn((k * dst_ref[h]).astype(BF16), v)
            r = lax.rsqrt(jnp.mean(o * o, axis=-1, keepdims=True) + RMS_EPS)
            gate = _silu(g_ref[pl.ds(r0, ck), cols].astype(F32))
            o_ref[pl.ds(r0, ck), cols] = (o * r * gate).astype(o_ref.dtype)
        return carry

    lax.fori_loop(0, nc, body, 0)


def retention(z, bsz, seq, ch, col0):
    t = z.shape[0]
    nh = RET_HEADS
    dk = ch // nh
    ck = min(RET_CHUNK, seq)
    nc = seq // ck
    half = dk // 2
    pos = jnp.arange(seq, dtype=F32)
    inv = 1.0 / (10000.0 ** jnp.linspace(0.0, 1.0, half, dtype=F32))
    ang = pos[:, None] * inv[None, :]
    cos, sin = jnp.cos(ang), jnp.sin(ang)
    log_g = jnp.log(1.0 - 2.0 ** (-5.0 - jnp.arange(nh, dtype=F32)))
    i = jnp.arange(ck, dtype=F32)
    diff = i[:, None] - i[None, :]
    d_intra = jnp.where(diff >= 0, jnp.exp(log_g[:, None, None] * jnp.maximum(diff, 0.0)), 0.0)
    d_cross = jnp.exp(log_g[:, None] * (i[None, :] + 1.0))[..., None]
    d_state = jnp.exp(log_g[:, None] * (ck - 1.0 - i[None, :]))[..., None]
    cb = col0 // ch
    col = lambda g: (lambda b: (b, cb + g))
    whole = lambda b: (0, 0, 0)
    return pl.pallas_call(
        functools.partial(_retention_kernel, nc=nc, ck=ck, dk=dk),
        grid=(bsz,),
        in_specs=[
            pl.BlockSpec((seq, ch), col(0)),
            pl.BlockSpec((seq, ch), col(1)),
            pl.BlockSpec((seq, ch), col(2)),
            pl.BlockSpec((seq, ch), col(3)),
            pl.BlockSpec((seq, half), lambda b: (0, 0)),
            pl.BlockSpec((seq, half), lambda b: (0, 0)),
            pl.BlockSpec((nh, ck, ck), whole),
            pl.BlockSpec((nh, ck, 1), whole),
            pl.BlockSpec((nh, ck, 1), whole),
        ],
        out_specs=pl.BlockSpec((seq, ch), lambda b: (b, 0)),
        out_shape=jax.ShapeDtypeStruct((t, ch), BF16),
        scratch_shapes=[pltpu.VMEM((dk, dk), F32) for _ in range(nh)],
        compiler_params=_cparams(("parallel",)),
        name="retention",
    )(z, z, z, z, cos, sin, d_intra, d_cross, d_state)


def _route(h, wr_ref, br_ref, ri_ref, rf_ref, cnt_ref, carry_ref, tm):
    logits = _dot_nt(wr_ref[...], h.astype(BF16))
    mx = jnp.max(logits, axis=0, keepdims=True)
    ex = jnp.exp(logits - mx)
    probs = ex / jnp.sum(ex, axis=0, keepdims=True)
    sel = probs + br_ref[...]
    s = [sel[e:e + 1, :] for e in range(N_EXPERTS)]
    p = [probs[e:e + 1, :] for e in range(N_EXPERTS)]
    epg = EXPERTS_PER_GROUP

    def first_argmax(vals, exclude=None):
        best = jnp.full_like(vals[0], -jnp.inf)
        idx = jnp.zeros(vals[0].shape, I32)
        for j, vj in enumerate(vals):
            better = vj > best
            if exclude is not None:
                better = better & (exclude != j)
            idx = jnp.where(better, j, idx)
            best = jnp.where(better, vj, best)
        return idx

    gscore = []
    for g in range(N_GROUPS):
        gs = s[g * epg:(g + 1) * epg]
        best = None
        for a in range(epg):
            for b in range(a + 1, epg):
                pair = gs[a] + gs[b]
                best = pair if best is None else jnp.maximum(best, pair)
        gscore.append(best)
    grp = first_argmax(gscore)

    def pick(rows, index, n):
        out = rows[n - 1]
        for j in range(n - 2, -1, -1):
            out = jnp.where(index == j, rows[j], out)
        return out

    in_s = [pick([s[g * epg + j] for g in range(N_GROUPS)], grp, N_GROUPS) for j in range(epg)]
    in_p = [pick([p[g * epg + j] for g in range(N_GROUPS)], grp, N_GROUPS) for j in range(epg)]
    i1 = first_argmax(in_s)
    i2 = first_argmax(in_s, exclude=i1)
    p1 = pick(in_p, i1, epg)
    p2 = pick(in_p, i2, epg)
    e1 = grp * epg + i1
    e2 = grp * epg + i2
    den = p1 + p2
    g1 = p1 / den
    g2 = p2 / den

    eidx = lax.broadcasted_iota(I32, (N_EXPERTS, tm), 0)
    member = (eidx == e1) | (eidx == e2)
    member_f = jnp.where(member, 1.0, 0.0)
    before = lax.broadcasted_iota(I32, (tm, tm), 0) < lax.broadcasted_iota(I32, (tm, tm), 1)
    prefix = _dot(member_f.astype(BF16), jnp.where(before, 1.0, 0.0).astype(BF16))
    base = prefix + carry_ref[:, 0:1]
    rank1 = jnp.sum(jnp.where(eidx == e1, base, 0.0), axis=0, keepdims=True).astype(I32)
    rank2 = jnp.sum(jnp.where(eidx == e2, base, 0.0), axis=0, keepdims=True).astype(I32)
    carry_ref[...] = carry_ref[...] + jnp.sum(member_f, axis=1, keepdims=True)
    cnt_ref[...] = carry_ref[...].astype(I32)

    zi = jnp.zeros((SUBLANE - 4, tm), I32)
    ri_ref[...] = jnp.concatenate([e1, e2, rank1, rank2, zi], axis=0)
    zf = jnp.zeros((SUBLANE - 2, tm), F32)
    rf_ref[...] = jnp.concatenate([g1, g2, zf], axis=0)


def _mixer_out_kernel(*refs, tm, conv):
    if conv:
        b_ref, c_ref, v_ref, cw_ref, cb_ref, *refs = refs
    else:
        ya_ref, *refs = refs
    (yb_ref, wa_ref, wb_ref, x_ref, modm_ref, modf_ref, nw_ref, wr_ref, br_ref,
     x1_ref, h_ref, ri_ref, rf_ref, cnt_ref, carry_ref, *conv_scratch) = refs
    s = pl.program_id(1)

    @pl.when((pl.program_id(0) == 0) & (s == 0))
    def _():
        carry_ref[...] = jnp.zeros(carry_ref.shape, F32)

    if conv:
        ya = _gated_conv(b_ref, c_ref, v_ref, cw_ref, cb_ref, conv_scratch[0], s, tm).astype(BF16)
    else:
        ya = ya_ref[...]
    y = _dot(ya, wa_ref[...]) + _dot(yb_ref[...], wb_ref[...])
    x1 = x_ref[...] + modm_ref[0, 2:3, :] * y
    x1_ref[...] = x1
    h = _modulated_norm(x1, modf_ref, nw_ref)
    h_ref[...] = _pack_pairs(h)
    _route(h, wr_ref, br_ref, ri_ref, rf_ref, cnt_ref, carry_ref, tm)


def mixer_out(lhs, w_bf, x2, mod_mix, mod_ffn, norm_ffn_w, w_router, b_router, bsz, seq, tm, conv=None):
    t, d = x2.shape
    kh = w_bf.shape[0] // 2
    per = seq // tm
    ne = N_EXPERTS
    row = lambda g: (lambda b, s: (b * per + s, g))
    const = lambda b, s: (0, 0)
    if conv is None:
        (o,) = lhs
        lhs_args = (o, o)
        lhs_specs = [pl.BlockSpec((tm, kh), row(0)), pl.BlockSpec((tm, kh), row(1))]
        scratch = []
    else:
        z, yb = lhs
        conv_w, conv_b = conv
        lhs_args = (z, z, z, conv_w, conv_b.reshape(1, kh), yb)
        lhs_specs = [pl.BlockSpec((tm, kh), row(0)), pl.BlockSpec((tm, kh), row(1)),
                     pl.BlockSpec((tm, kh), row(2)), pl.BlockSpec((CONV_WIDTH, kh), const),
                     pl.BlockSpec((1, kh), const), pl.BlockSpec((tm, kh), row(0))]
        scratch = [pltpu.VMEM((tm + SUBLANE, kh), F32)]
    batch = lambda b, s: (b, 0, 0)
    return pl.pallas_call(
        functools.partial(_mixer_out_kernel, tm=tm, conv=conv is not None),
        grid=(bsz, per),
        in_specs=lhs_specs + [
            pl.BlockSpec((kh, d), const),
            pl.BlockSpec((kh, d), lambda b, s: (1, 0)),
            pl.BlockSpec((tm, d), row(0)),
            pl.BlockSpec((1, 3, d), batch),
            pl.BlockSpec((1, 3, d), batch),
            pl.BlockSpec((1, d), const),
            pl.BlockSpec((ne, d), const),
            pl.BlockSpec((ne, 1), const),
        ],
        out_specs=[
            pl.BlockSpec((tm, d), row(0)),
            pl.BlockSpec((tm, d // 2), row(0)),
            pl.BlockSpec((SUBLANE, tm), lambda b, s: (0, b * per + s)),
            pl.BlockSpec((SUBLANE, tm), lambda b, s: (0, b * per + s)),
            pl.BlockSpec((ne, LANE), const),
        ],
        out_shape=[
            jax.ShapeDtypeStruct((t, d), F32),
            jax.ShapeDtypeStruct((t, d // 2), I32),
            jax.ShapeDtypeStruct((SUBLANE, t), I32),
            jax.ShapeDtypeStruct((SUBLANE, t), F32),
            jax.ShapeDtypeStruct((ne, LANE), I32),
        ],
        scratch_shapes=[pltpu.VMEM((ne, LANE), F32)] + scratch,
        compiler_params=_cparams(("arbitrary", "arbitrary")),
        name="mixer_out",
    )(*lhs_args, w_bf, w_bf, x2, mod_mix, mod_ffn, norm_ffn_w.reshape(1, d),
      w_router.T.astype(BF16), b_router.reshape(ne, 1))


def _dispatch_kernel(dest_ref, pad0_ref, padn_ref, h_ref, xs_ref, hbuf_ref, zero_ref, sem, zsem,
                     *, tm, nt):
    i = pl.program_id(0)
    cur = i % 2
    base = i * tm

    def zero_row(e, r):
        return pltpu.make_async_copy(zero_ref.at[pl.ds(0, 1), :],
                                     xs_ref.at[pl.ds(pad0_ref[e] + r, 1), :], zsem)

    def tile_done(buf):
        for _ in range(2):
            pltpu.make_async_copy(hbuf_ref.at[buf], xs_ref.at[pl.ds(0, tm), :], sem.at[buf]).wait()

    @pl.when(i == 0)
    def _():
        zero_ref[...] = jnp.zeros(zero_ref.shape, zero_ref.dtype)
        for e in range(N_EXPERTS):
            lax.fori_loop(0, padn_ref[e], lambda r, c, e=e: (zero_row(e, r).start(), c)[1], 0)

    hbuf_ref[cur] = h_ref[...]

    def start(r, carry):
        for slot in range(2):
            d = dest_ref[2 * (base + r) + slot]
            pltpu.make_async_copy(hbuf_ref.at[cur, pl.ds(r, 1), :], xs_ref.at[pl.ds(d, 1), :],
                                  sem.at[cur]).start()
        return carry

    lax.fori_loop(0, tm, start, 0, unroll=DMA_UNROLL)

    @pl.when(i > 0)
    def _():
        tile_done(1 - cur)

    @pl.when(i == nt - 1)
    def _():
        tile_done(cur)
        for e in range(N_EXPERTS):
            lax.fori_loop(0, padn_ref[e], lambda r, c, e=e: (zero_row(e, r).wait(), c)[1], 0)


def dispatch(h, dest, pad_start, pad_count, n_rows, tm):
    t, d = h.shape
    return pl.pallas_call(
        functools.partial(_dispatch_kernel, tm=tm, nt=t // tm),
        grid_spec=pltpu.PrefetchScalarGridSpec(
            num_scalar_prefetch=3,
            grid=(t // tm,),
            in_specs=[pl.BlockSpec((tm, d), lambda i, *_: (i, 0))],
            out_specs=pl.BlockSpec(memory_space=pl.ANY),
            scratch_shapes=[pltpu.VMEM((2, tm, d), h.dtype), pltpu.VMEM((SUBLANE, d), h.dtype),
                            pltpu.SemaphoreType.DMA((2,)), pltpu.SemaphoreType.DMA(())],
        ),
        out_shape=jax.ShapeDtypeStruct((n_rows, d), h.dtype),
        compiler_params=_cparams(("arbitrary",)),
        name="moe_dispatch",
    )(dest, pad_start, pad_count, h)


def _ffn_kernel(be_ref, bv_ref, xs_ref, w1_ref, w3_ref, w2_ref, *refs, riders):
    ys_ref = refs[len(riders)]
    _ride_casts(riders, refs[:len(riders)], refs[len(riders) + 1:], pl.program_id(0))
    valid = bv_ref[pl.program_id(0)]

    @pl.when(valid > 0)
    def _():
        rows = lax.broadcasted_iota(I32, xs_ref.shape, 0)
        x = _unpack_pairs(jnp.where(rows < valid, xs_ref[...], 0))
        h1 = _dot(x, w1_ref[0])
        h3 = _dot(x, w3_ref[0])
        a = (_silu(h1) * h3).astype(BF16)
        ys_ref[...] = _dot(a, w2_ref[0])

    @pl.when(valid <= 0)
    def _():
        ys_ref[...] = jnp.zeros(ys_ref.shape, ys_ref.dtype)


def grouped_ffn(xs, block_e, block_valid, w1, w3, w2, bm, riders=()):
    n_rows = xs.shape[0]
    d, f = w1.shape[1], w1.shape[2]
    flat = lambda i, be, bv: i
    return pl.pallas_call(
        functools.partial(_ffn_kernel, riders=tuple(r.nblk for r in riders)),
        grid_spec=pltpu.PrefetchScalarGridSpec(
            num_scalar_prefetch=2,
            grid=(n_rows // bm,),
            in_specs=[
                pl.BlockSpec((bm, d // 2), lambda i, be, bv: (i, 0)),
                pl.BlockSpec((1, d, f), lambda i, be, bv: (be[i], 0, 0)),
                pl.BlockSpec((1, d, f), lambda i, be, bv: (be[i], 0, 0)),
                pl.BlockSpec((1, f, d), lambda i, be, bv: (be[i], 0, 0)),
            ] + [r.in_spec(flat) for r in riders],
            out_specs=[pl.BlockSpec((bm, d), lambda i, be, bv: (i, 0))] + [r.out_spec(flat) for r in riders],
        ),
        out_shape=[jax.ShapeDtypeStruct((n_rows, d), F32)] + [r.out_shape for r in riders],
        compiler_params=_cparams(("arbitrary",)),
        name="moe_ffn",
    )(block_e, block_valid, xs, w1, w3, w2, *[r.src for r in riders])


def _combine_kernel(dest_ref, ys_ref, x_ref, mod_ref, gf_ref, *refs, tm, nt, project):
    if project:
        modn_ref, nwn_ref, w_ref, o_ref, z_ref, a_ref, b_ref, sem = refs
    else:
        o_ref, a_ref, b_ref, sem = refs
    i = pl.program_id(0)
    cur = i % 2

    def fetch_row(base, r, buf):
        for slot, ref in ((0, a_ref), (1, b_ref)):
            d = dest_ref[2 * (base + r) + slot]
            pltpu.make_async_copy(ys_ref.at[pl.ds(d, 1), :], ref.at[buf, pl.ds(r, 1), :],
                                  sem.at[buf]).start()

    def fetch(tile, buf):
        lax.fori_loop(0, tm, lambda r, c: (fetch_row(tile * tm, r, buf), c)[1], 0, unroll=DMA_UNROLL)

    def fetched(buf):
        pltpu.make_async_copy(ys_ref.at[pl.ds(0, tm), :], a_ref.at[buf], sem.at[buf]).wait()
        pltpu.make_async_copy(ys_ref.at[pl.ds(0, tm), :], b_ref.at[buf], sem.at[buf]).wait()

    @pl.when(i == 0)
    def _():
        fetch(0, 0)

    if not project:
        @pl.when(i + 1 < nt)
        def _():
            fetch(i + 1, 1 - cur)

    fetched(cur)
    y = gf_ref[:, 0:1] * a_ref[cur] + gf_ref[:, 1:2] * b_ref[cur]
    out = x_ref[...] + mod_ref[0, 2:3, :] * y
    o_ref[...] = out
    if project:
        h = _modulated_norm(out, modn_ref, nwn_ref).astype(BF16)
        base = jnp.minimum(i + 1, nt - 1) * tm
        for r in range(tm):
            fetch_row(base, r, 1 - cur)
        z_ref[...] = _dot(h, w_ref[...]).astype(z_ref.dtype)

        @pl.when(i == nt - 1)
        def _():
            fetched(1 - cur)


def combine(ys, dest, x2, mod, gates_t, seq, tm, project=None):
    t, d = x2.shape
    per = seq // tm
    row = lambda i, dest: (i, 0)
    batch = lambda i, dest: (i // per, 0, 0)
    const = lambda i, dest: (0, 0)
    in_specs = [pl.BlockSpec(memory_space=pl.ANY), pl.BlockSpec((tm, d), row),
                pl.BlockSpec((1, 3, d), batch), pl.BlockSpec((tm, SUBLANE), row)]
    out_specs = [pl.BlockSpec((tm, d), row)]
    out_shape = [jax.ShapeDtypeStruct((t, d), F32)]
    args = [dest, ys, x2, mod, gates_t]
    if project is not None:
        mod_next, nw_next, w_bf = project
        n = w_bf.shape[1]
        in_specs += [pl.BlockSpec((1, 3, d), batch), pl.BlockSpec((1, d), const), pl.BlockSpec((d, n), const)]
        out_specs += [pl.BlockSpec((tm, n), row)]
        out_shape += [jax.ShapeDtypeStruct((t, n), BF16)]
        args += [mod_next, nw_next.reshape(1, d), w_bf]
    return pl.pallas_call(
        functools.partial(_combine_kernel, tm=tm, nt=t // tm, project=project is not None),
        grid_spec=pltpu.PrefetchScalarGridSpec(
            num_scalar_prefetch=1,
            grid=(t // tm,),
            in_specs=in_specs,
            out_specs=out_specs,
            scratch_shapes=[pltpu.VMEM((2, tm, d), F32), pltpu.VMEM((2, tm, d), F32),
                            pltpu.SemaphoreType.DMA((2,))],
        ),
        out_shape=out_shape,
        compiler_params=_cparams(("arbitrary",)),
        name="moe_combine",
    )(*args)


def _moe_blocks(t):
    return (2 * t) // MOE_ROWS + N_EXPERTS


def moe(x2, h, ri, rf, cnt, mod, w1, w3, w2, seq, project=None, riders=()):
    t, d = x2.shape
    bm = MOE_ROWS
    tm = min(256, seq)
    ne = N_EXPERTS
    counts = cnt[:, 0]
    nblk = (counts + bm - 1) // bm
    blk_end = jnp.cumsum(nblk)
    blk_start = blk_end - nblk
    eids = jnp.arange(ne, dtype=I32)
    e12 = ri[0:2].T
    row0 = jnp.sum(jnp.where(e12[..., None] == eids, blk_start * bm, 0), axis=-1)
    dest = (row0 + ri[2:4].T).reshape(2 * t).astype(I32)
    n_blocks = _moe_blocks(t)
    bidx = jnp.arange(n_blocks, dtype=I32)
    block_e = jnp.minimum(jnp.sum(bidx[:, None] >= blk_end[None, :], axis=1), ne - 1).astype(I32)
    onehot = block_e[:, None] == eids
    cnt_b = jnp.sum(jnp.where(onehot, counts, 0), axis=1)
    start_b = jnp.sum(jnp.where(onehot, blk_start, 0), axis=1)
    block_valid = jnp.clip(cnt_b - (bidx - start_b) * bm, 0, bm).astype(I32)
    pad_start = (blk_start * bm + counts).astype(I32)
    pad_end = jnp.where(eids == ne - 1, n_blocks * bm, blk_end * bm)
    xs = dispatch(h, dest, pad_start, (pad_end - pad_start).astype(I32), n_blocks * bm, tm)
    ys, *casts = grouped_ffn(xs, block_e, block_valid, w1, w3, w2, bm, riders)
    return combine(ys, dest, x2, mod, rf.T, seq, tm, project), casts


def _t5_bucket(n):
    max_exact = N_BUCKETS // 2
    nf = jnp.maximum(n, 1).astype(F32)
    large = max_exact + (jnp.log(nf / max_exact) / math.log(MAX_DISTANCE / max_exact)
                         * (N_BUCKETS - max_exact)).astype(I32)
    large = jnp.minimum(large, N_BUCKETS - 1)
    return jnp.where(n < max_exact, n, large)


def _sortable(score):
    bits = pltpu.bitcast(score, I32)
    return bits ^ ((bits >> 31) & 0x7FFFFFFF)


def _dsa_kernel(q_ref, k_ref, v_ref, qi_ref, kw_ref, wq_ref, tb_ref, qn_ref, kn_ref, o_ref,
                khat_ref, vt_ref, qrt_ref, qirt_ref, wt_ref, mt_ref, hi_ref, lo_ref, lo2_ref, nm_ref, lg_ref,
                *acc_refs,
                nh, dh, nih, di, topk, nb):
    j = pl.program_id(1)
    qb = Q_BLOCK
    kc = 2 * qb
    nch = (j + 2) // 2
    pair = 2 * qb

    def transpose(x):
        return x.astype(F32).T

    @pl.when(j == 0)
    def _():
        k = k_ref[...].astype(F32)
        r = lax.rsqrt(jnp.mean(k * k, axis=-1, keepdims=True) + RMS_EPS)
        khat_ref[...] = (k * r * kn_ref[...]).astype(BF16)
        for c in range(nb // 2):
            for s in range(2):
                blk = v_ref[(2 * c + s) * qb:(2 * c + s + 1) * qb, :]
                vt_ref[c, 0:dh, s * qb:(s + 1) * qb] = transpose(blk).astype(BF16)
            vt_ref[c, dh:dh + ONES_ROWS, :] = jnp.ones((ONES_ROWS, kc), BF16)
        for ref in (hi_ref, lo_ref, lo2_ref):
            ref[...] = jnp.full(ref.shape, I16_MIN, I16)

    for h in range(nh):
        qt = transpose(q_ref[:, h * dh:(h + 1) * dh])
        r = lax.rsqrt(jnp.mean(qt * qt, axis=0, keepdims=True) + RMS_EPS)
        qrt_ref[:, h * qb:(h + 1) * qb] = (qt * r * qn_ref[...]).astype(BF16)
    for g in range(nih * di // LANE):
        two = transpose(qi_ref[:, g * LANE:(g + 1) * LANE])
        for s in range(LANE // di):
            h = g * (LANE // di) + s
            qirt_ref[:, h * qb:(h + 1) * qb] = two[s * di:(s + 1) * di, :].astype(BF16)
    wt_ref[...] = transpose(wq_ref[...])

    key_l = lax.broadcasted_iota(I32, (kc, qb), 0)
    q_pos = j * qb + lax.broadcasted_iota(I32, (kc, qb), 1)

    def score_body(c, carry):
        r0 = pl.multiple_of(c * kc, kc)
        ki = kw_ref[pl.ds(r0, kc), 0:di]
        acc = jnp.zeros((kc, qb), F32)
        for g in range(nih // 2):
            rel = _dot(ki, qirt_ref[:, g * pair:(g + 1) * pair])
            for s in range(2):
                h = 2 * g + s
                acc = acc + wt_ref[di + h:di + h + 1, :] * jnp.maximum(rel[:, s * qb:(s + 1) * qb], 0.0)
        m = jnp.where(r0 + key_l <= q_pos, _sortable(acc), INT_MIN)
        mt_ref[c] = m
        hi_ref[c] = (m >> 16).astype(I16)
        lo_ref[c] = ((m & 0xFFFF) + I16_MIN).astype(I16)
        return carry

    def score_many(t, carry, n):
        for k in range(n):
            score_body(n * t + k, carry)
        return carry

    lax.fori_loop(0, nch // 4, functools.partial(score_many, n=4), 0)
    lax.fori_loop(2 * (nch // 4), nch // 2, functools.partial(score_many, n=2), 0)
    lax.fori_loop(2 * (nch // 2), nch, score_body, 0)

    def count_ge(cand):
        def cbody(c, acc):
            hit = jnp.where(mt_ref[c] >= cand, 1, 0)
            return acc + jnp.sum(hit.reshape(kc // SUBLANE, SUBLANE, qb), axis=0)
        acc = lax.fori_loop(0, nch, cbody, jnp.zeros((SUBLANE, qb), I32))
        return jnp.sum(acc, axis=0, keepdims=True)

    def fold16(hit):
        parts = [hit[PACK16 * i:PACK16 * (i + 1), :] for i in range(kc // PACK16)]
        while len(parts) > 1:
            parts = [parts[i] + parts[i + 1] for i in range(0, len(parts), 2)]
        return parts[0]

    def total16(acc):
        return jnp.sum(acc.astype(F32), axis=0, keepdims=True).astype(I32)

    one, nil = jnp.ones((), BF16), jnp.zeros((), BF16)

    def count16(src_ref, cand, n):
        c16 = cand.astype(I16)
        acc = fold16(jnp.where(src_ref[0] >= c16, one, nil))
        for c in range(1, n):
            acc = acc + fold16(jnp.where(src_ref[c] >= c16, one, nil))
        return total16(acc)

    def search16(src_ref, kth, n):
        zero = jnp.zeros((1, qb), I32)
        ans0 = jnp.where(count16(src_ref, zero, n) >= kth, zero, I16_MIN)

        def bit_body(bi, ans):
            cand = ans | (1 << (14 - bi))
            return jnp.where(count16(src_ref, cand, n) >= kth, cand, ans)

        return lax.fori_loop(0, 15, bit_body, ans0)

    def search(n):
        top = search16(hi_ref, jnp.full((1, qb), topk, I32), n)
        top16 = top.astype(I16)
        acc = jnp.zeros((PACK16, qb), BF16)
        for c in range(n):
            hi = hi_ref[c]
            lo2_ref[c] = jnp.where(hi == top16, lo_ref[c], jnp.full((), I16_MIN, I16))
            acc = acc + fold16(jnp.where(hi > top16, one, nil))
        above = total16(acc)
        low = search16(lo2_ref, topk - above, n)
        return top * 65536 + (low - I16_MIN), above + count16(lo2_ref, low, n)

    def search_any():
        return lax.cond(nch <= nb // 4, functools.partial(search, nb // 4),
                        functools.partial(search, nb // 2))

    selecting = (j + 1) * qb > topk
    thr, n_ge = lax.cond(selecting, search_any,
                         lambda: (jnp.full((1, qb), INT_MIN + 1, I32), jnp.zeros((1, qb), I32)))

    def plain_mask():
        def body(c, carry):
            nm_ref[c] = jnp.where(mt_ref[c] >= thr, 0.0, NEG_BIG)
            return carry
        lax.fori_loop(0, nch, body, 0)

    def tie_mask():
        need = (topk - count_ge(thr + 1)).astype(F32)
        tri = jnp.where(lax.broadcasted_iota(I32, (kc, kc), 1) <= lax.broadcasted_iota(I32, (kc, kc), 0),
                        1.0, 0.0).astype(BF16)

        def body(c, seen):
            m = mt_ref[c]
            tie = m == thr
            upto = _dot(tri, jnp.where(tie, 1.0, 0.0).astype(BF16)) + seen
            nm_ref[c] = jnp.where((m > thr) | (tie & (upto <= need)), 0.0, NEG_BIG)
            return upto[kc - 1:kc, :]
        lax.fori_loop(0, nch, body, jnp.zeros((1, qb), F32))

    crowded = jnp.max(n_ge) > topk
    lax.cond(crowded, tie_mask, plain_mask)

    npair = nh // 2
    pcols = [slice(pr * pair, (pr + 1) * pair) for pr in range(npair)]
    nfar = jnp.maximum((j - 1) // 2, 0)

    def sweep1(c, mruns, near):
        r0 = pl.multiple_of(c * kc, kc)
        kh = khat_ref[pl.ds(r0, kc), :]
        nm = nm_ref[c]
        nm2 = jnp.concatenate([nm, nm], axis=1)
        if near:
            k0 = jnp.clip(j - 2 * c, 0, 2)
            k1 = jnp.clip(j - 2 * c - 1, 0, 2)
        out = []
        for pr in range(npair):
            lg = _dot(kh, qrt_ref[:, pcols[pr]]) + nm2
            if near:
                lg = lg + jnp.concatenate([tb_ref[k0, :, pcols[pr]], tb_ref[k1, :, pcols[pr]]], axis=0)
            lg_ref[c, :, pcols[pr]] = lg
            out.append(jnp.maximum(mruns[pr], jnp.max(lg.reshape(kc // SUBLANE, SUBLANE, pair), axis=0)))
        return tuple(out)

    def sweep1_many(t, mruns, n):
        for k in range(n):
            mruns = sweep1(n * t + k, mruns, near=False)
        return mruns

    mruns = tuple(jnp.full((SUBLANE, pair), NEG_BIG, F32) for _ in range(npair))
    mruns = lax.fori_loop(0, nfar // 4, functools.partial(sweep1_many, n=4), mruns)
    mruns = lax.fori_loop(2 * (nfar // 4), nfar // 2, functools.partial(sweep1_many, n=2), mruns)
    mruns = lax.fori_loop(2 * (nfar // 2), nfar, functools.partial(sweep1, near=False), mruns)
    mruns = lax.fori_loop(nfar, nch, functools.partial(sweep1, near=True), mruns)
    mrows = [jnp.max(m, axis=0, keepdims=True) for m in mruns]
    for acc_ref in acc_refs:
        acc_ref[...] = jnp.zeros(acc_ref.shape, F32)

    def sweep2(c, carry):
        vt = vt_ref[c]
        for pr in range(npair):
            p = jnp.exp2((lg_ref[c, :, pcols[pr]] - mrows[pr]).astype(BF16))
            acc_refs[pr][...] += _dot(vt, p)
        return carry

    def sweep2_many(t, carry, n):
        vts = [vt_ref[n * t + k] for k in range(n)]
        for pr in range(npair):
            part = None
            for k in range(n):
                p = jnp.exp2((lg_ref[n * t + k, :, pcols[pr]] - mrows[pr]).astype(BF16))
                part = _dot(vts[k], p) if part is None else part + _dot(vts[k], p)
            acc_refs[pr][...] += part
        return carry

    lax.fori_loop(0, nch // 4, functools.partial(sweep2_many, n=4), 0)
    lax.fori_loop(2 * (nch // 4), nch // 2, functools.partial(sweep2_many, n=2), 0)
    lax.fori_loop(2 * (nch // 2), nch, sweep2, 0)
    for pr in range(npair):
        out = acc_refs[pr][0:dh, :] / acc_refs[pr][dh:dh + 1, :]
        for s in range(2):
            h = 2 * pr + s
            o_ref[:, h * dh:(h + 1) * dh] = transpose(
                out[:, s * qb:(s + 1) * qb].astype(BF16)).astype(o_ref.dtype)


def dsa_attention(z, rel_bias, q_norm, k_norm, bsz, seq, d):
    t = z.shape[0]
    nh, dh, nih, di = ATT_HEADS, d // ATT_HEADS, IDX_HEADS, IDX_DIM
    qb = Q_BLOCK
    nb = seq // qb
    topk = min(DSA_TOPK_MAX, seq // 4)
    assert topk % qb == 0 and dh == LANE and nih * di == d // 2 and nb % 2 == 0 and LANE % di == 0
    assert (nb // 2) * (2 * qb // PACK16) <= 256
    i = jnp.arange(qb)
    dist = i[None, :] - i[:, None]
    def lookup(bucket):
        hot = bucket[..., None] == jnp.arange(N_BUCKETS, dtype=I32)
        return jnp.sum(jnp.where(hot[..., None], rel_bias, 0.0), axis=-2)

    tabs = [lookup(_t5_bucket(jnp.maximum(dist + off * qb, 0))) for off in range(2)]
    far = jnp.broadcast_to(lookup(_t5_bucket(jnp.full((), 2 * qb, I32))), (qb, qb, nh))
    tb = jnp.stack(tabs + [far]).transpose(0, 1, 3, 2).reshape(3, qb, nh * qb).astype(F32)
    tb = (tb - tb[2:3]) * math.log2(math.e)
    qn_rows = jnp.broadcast_to((q_norm * (dh ** -0.5 * math.log2(math.e)))[:, None], (dh, qb)).astype(F32)
    cq = d // LANE
    return pl.pallas_call(
        functools.partial(_dsa_kernel, nh=nh, dh=dh, nih=nih, di=di, topk=topk, nb=nb),
        grid=(bsz, nb),
        in_specs=[
            pl.BlockSpec((qb, d), lambda b, j: (b * nb + j, 0)),
            pl.BlockSpec((seq, LANE), lambda b, j: (b, cq + cq // 2)),
            pl.BlockSpec((seq, LANE), lambda b, j: (b, cq + cq // 2 + 1)),
            pl.BlockSpec((qb, d // 2), lambda b, j: (b * nb + j, 2)),
            pl.BlockSpec((seq, LANE), lambda b, j: (b, cq + cq // 2 + 2)),
            pl.BlockSpec((qb, LANE), lambda b, j: (b * nb + j, cq + cq // 2 + 2)),
            pl.BlockSpec((3, qb, nh * qb), lambda b, j: (0, 0, 0)),
            pl.BlockSpec((dh, qb), lambda b, j: (0, 0)),
            pl.BlockSpec((1, dh), lambda b, j: (0, 0)),
        ],
        out_specs=pl.BlockSpec((qb, d), lambda b, j: (b * nb + j, 0)),
        out_shape=jax.ShapeDtypeStruct((t, d), BF16),
        scratch_shapes=[
            pltpu.VMEM((seq, dh), BF16),
            pltpu.VMEM((nb // 2, dh + ONES_ROWS, 2 * qb), BF16),
            pltpu.VMEM((dh, nh * qb), BF16),
            pltpu.VMEM((di, nih * qb), BF16),
            pltpu.VMEM((LANE, qb), F32),
            pltpu.VMEM((nb // 2, 2 * qb, qb), I32),
            pltpu.VMEM((nb // 2, 2 * qb, qb), I16),
            pltpu.VMEM((nb // 2, 2 * qb, qb), I16),
            pltpu.VMEM((nb // 2, 2 * qb, qb), I16),
            pltpu.VMEM((nb // 2, 2 * qb, qb), F32),
            pltpu.VMEM((nb // 2, 2 * qb, nh * qb), F32),
        ] + [pltpu.VMEM((dh + ONES_ROWS, 2 * qb), F32)
             for _ in range(nh // 2)],
        compiler_params=_cparams(("arbitrary", "arbitrary")),
        name="dsa_attention",
    )(z, z, z, z, z, z, tb, qn_rows, k_norm.reshape(1, dh))


def _dsa_weight(c_w_in, d):
    dh = d // ATT_HEADS
    nqi = IDX_HEADS * IDX_DIM
    q, k, v, qi, ki, wi = jnp.split(
        c_w_in, [d, d + dh, d + 2 * dh, d + 2 * dh + nqi, d + 2 * dh + nqi + IDX_DIM], axis=1)
    used = d + nqi + 2 * dh + IDX_DIM + IDX_HEADS
    total = -(-used // 512) * 512
    pad = jnp.zeros((d, total - used), c_w_in.dtype)
    return jnp.concatenate([q, qi, k, v, ki, wi, pad], axis=1).astype(BF16)


def kernel(x, c, rel_bias, w_router, b_router, norm_mix, ada_w_mix, ada_b_mix, norm_ffn, ada_w_ffn,
           ada_b_ffn, ab_w_in, ab_conv_w, ab_conv_b, ab_w_out, c_w_in, c_q_norm, c_k_norm, c_w_out,
           moe_w1, moe_w3, moe_w2):
    bsz, seq, d = x.shape
    depth = norm_mix.shape[0]
    t = bsz * seq
    ch = d // 2
    tm_out = min(512, seq)
    x2 = x.reshape(t, d)
    mod_mix = ada_mod(c, ada_w_mix, ada_b_mix).reshape(depth, bsz, 3, d)
    mod_ffn = ada_mod(c, ada_w_ffn, ada_b_ffn).reshape(depth, bsz, 3, d)
    experts = (moe_w1, moe_w3, moe_w2)
    as_experts = lambda casts: [cw.reshape(w.shape[1:]) for cw, w in zip(casts, experts)]
    w_in = [ab_w_in[l // 2].astype(BF16) if l % 2 == 0 else _dsa_weight(c_w_in[l // 2], d)
            for l in range(depth)]
    tm_in, tn_in = min(1024, seq), 1024
    z = None
    w_moe = None
    for layer in range(depth):
        i = layer // 2
        if z is None:
            riders = _cast_riders(experts, layer, (t // tm_in) * (w_in[layer].shape[1] // tn_in))
            h_mix = prenorm(x2, mod_mix[layer], norm_mix[layer], seq, tm_out)
            z, *casts = inproj(h_mix, w_in[layer], tm_in, tn_in, riders or ())
            if riders:
                w_moe = as_experts(casts)
        if w_moe is None:
            w_moe = [w[layer].astype(BF16) for w in experts]
        if layer % 2 == 0:
            y_b = retention(z, bsz, seq, ch, 3 * ch)
            lhs, w_out, conv = (z, y_b), ab_w_out[i], (ab_conv_w[i], ab_conv_b[i])
        else:
            o = dsa_attention(z, rel_bias, c_q_norm[i], c_k_norm[i], bsz, seq, d)
            lhs, w_out, conv = (o,), c_w_out[i], None
        x2, h, ri, rf, cnt = mixer_out(lhs, w_out.astype(BF16), x2, mod_mix[layer], mod_ffn[layer],
                                       norm_ffn[layer], w_router, b_router, bsz, seq, tm_out, conv)
        last = layer + 1 == depth
        nxt = None if last else (mod_mix[layer + 1], norm_mix[layer + 1], w_in[layer + 1])
        riders = None if last else _cast_riders(experts, layer + 1, _moe_blocks(t))
        res, casts = moe(x2, h, ri, rf, cnt, mod_ffn[layer], *w_moe, seq, nxt, riders or ())
        x2, z = (res[0], None) if last else (res[0], res[1])
        w_moe = as_experts(casts) if riders else None
    return x2.reshape(bsz, seq, d)
```

```python
import functools
import math
from typing import NamedTuple

import jax
import jax.numpy as jnp
from jax import lax
from jax.experimental import pallas as pl
from jax.experimental.pallas import tpu as pltpu

F32 = jnp.float32
BF16 = jnp.bfloat16
I32 = jnp.int32
I16 = jnp.int16

RMS_EPS = 1e-6
CONV_WIDTH = 3
RET_HEADS = 4
ATT_HEADS = 16
IDX_HEADS = 16
IDX_DIM = 64
DSA_TOPK_MAX = 256
Q_BLOCK = 128
N_BUCKETS = 32
MAX_DISTANCE = 128
N_EXPERTS = 16
N_GROUPS = 4
EXPERTS_PER_GROUP = N_EXPERTS // N_GROUPS
RET_CHUNK = 256
MOE_ROWS = 256
DMA_UNROLL = 8
RIDER_BLOCK_BYTES = 1024 * 1024
ONES_ROWS = 16
LANE = 128
SUBLANE = 8
VMEM_LIMIT = 48 * 1024 * 1024
NEG_BIG = -1e30
INT_MIN = -(2 ** 31)
I16_MIN = -(2 ** 15)
HIGH_HALF = -(2 ** 16)
PACK16 = 16


def _cparams(sem, vmem=VMEM_LIMIT):
    return pltpu.CompilerParams(dimension_semantics=sem, vmem_limit_bytes=vmem)


def _dot(a, b):
    return jnp.dot(a, b, preferred_element_type=F32)


def _dot_nt(a, b):
    return lax.dot_general(a, b, (((1,), (1,)), ((), ())), preferred_element_type=F32)


def _dot_tn(a, b):
    return lax.dot_general(a, b, (((0,), (0,)), ((), ())), preferred_element_type=F32)


def _silu(x):
    return x * jax.nn.sigmoid(x)


def _pack_pairs(x):
    n = x.shape[1] // 2
    xb = x.astype(BF16).astype(F32)
    lo = pltpu.bitcast(xb[:, :n], I32)
    hi = pltpu.bitcast(xb[:, n:], I32)
    return lax.shift_right_logical(lo, 16) | (hi & HIGH_HALF)


def _unpack_pairs(u):
    lo = pltpu.bitcast(u << 16, F32)
    hi = pltpu.bitcast(u & HIGH_HALF, F32)
    return jnp.concatenate([lo, hi], axis=1).astype(BF16)


class _CastRider(NamedTuple):
    src: jax.Array
    rows: int
    first_blk: int
    nblk: int

    def in_spec(self, flat):
        return pl.BlockSpec((self.rows, self.src.shape[1]),
                            lambda *g: (jnp.minimum(flat(*g), self.nblk - 1) + self.first_blk, 0))

    def out_spec(self, flat):
        return pl.BlockSpec((self.rows, self.src.shape[1]),
                            lambda *g: (jnp.minimum(flat(*g), self.nblk - 1), 0))

    @property
    def out_shape(self):
        return jax.ShapeDtypeStruct((self.rows * self.nblk, self.src.shape[1]), BF16)


def _cast_riders(weights, layer, n_steps):
    riders = []
    for w in weights:
        per_layer, cols = w.shape[1] * w.shape[2], w.shape[3]
        rows = PACK16
        while rows * n_steps < per_layer:
            rows *= 2
        if per_layer % rows or rows * cols * 4 > RIDER_BLOCK_BYTES:
            return None
        nblk = per_layer // rows
        riders.append(_CastRider(w.reshape(-1, cols), rows, layer * nblk, nblk))
    return riders


def _ride_casts(nblks, src_refs, dst_refs, step):
    for nblk, src, dst in zip(nblks, src_refs, dst_refs):
        @pl.when(step < nblk)
        def _():
            dst[...] = src[...].astype(BF16)


def _ada_kernel(c_ref, w_ref, b_ref, o_ref):
    cs = _silu(c_ref[...])
    o_ref[0] = _dot(cs.astype(BF16), w_ref[0].astype(BF16)) + b_ref[0]


def ada_mod(c, ada_w, ada_b):
    nl, d, n3 = ada_w.shape
    bsz = c.shape[0]
    tn = min(512, n3)
    return pl.pallas_call(
        _ada_kernel,
        grid=(nl, n3 // tn),
        in_specs=[
            pl.BlockSpec((bsz, d), lambda l, j: (0, 0)),
            pl.BlockSpec((1, d, tn), lambda l, j: (l, 0, j)),
            pl.BlockSpec((1, 1, tn), lambda l, j: (l, 0, j)),
        ],
        out_specs=pl.BlockSpec((1, bsz, tn), lambda l, j: (l, 0, j)),
        out_shape=jax.ShapeDtypeStruct((nl, bsz, n3), F32),
        compiler_params=_cparams(("parallel", "parallel")),
        name="ada_mod",
    )(c, ada_w, ada_b.reshape(nl, 1, n3))


def _modulated_norm(x, mod_ref, nw_ref):
    r = lax.rsqrt(jnp.mean(x * x, axis=-1, keepdims=True) + RMS_EPS)
    return x * r * nw_ref[...] * (1.0 + mod_ref[0, 1:2, :]) + mod_ref[0, 0:1, :]


def _prenorm_kernel(x_ref, mod_ref, nw_ref, h_ref):
    h_ref[...] = _modulated_norm(x_ref[...], mod_ref, nw_ref).astype(h_ref.dtype)


def prenorm(x2, mod, norm_w, seq, tm):
    t, d = x2.shape
    per = seq // tm
    return pl.pallas_call(
        _prenorm_kernel,
        grid=(t // tm,),
        in_specs=[
            pl.BlockSpec((tm, d), lambda i: (i, 0)),
            pl.BlockSpec((1, 3, d), lambda i: (i // per, 0, 0)),
            pl.BlockSpec((1, d), lambda i: (0, 0)),
        ],
        out_specs=pl.BlockSpec((tm, d), lambda i: (i, 0)),
        out_shape=jax.ShapeDtypeStruct((t, d), BF16),
        compiler_params=_cparams(("parallel",)),
        name="prenorm",
    )(x2, mod, norm_w.reshape(1, d))


def _inproj_kernel(h_ref, w_ref, *refs, riders):
    o_ref = refs[len(riders)]
    step = pl.program_id(0) * pl.num_programs(1) + pl.program_id(1)
    _ride_casts(riders, refs[:len(riders)], refs[len(riders) + 1:], step)
    o_ref[...] = _dot(h_ref[...], w_ref[...]).astype(o_ref.dtype)


def inproj(h_bf, w_bf, tm, tn, riders=()):
    t, d = h_bf.shape
    n = w_bf.shape[1]
    nj = n // tn
    flat = lambda i, j: i * nj + j
    return pl.pallas_call(
        functools.partial(_inproj_kernel, riders=tuple(r.nblk for r in riders)),
        grid=(t // tm, nj),
        in_specs=[
            pl.BlockSpec((tm, d), lambda i, j: (i, 0)),
            pl.BlockSpec((d, tn), lambda i, j: (0, j)),
        ] + [r.in_spec(flat) for r in riders],
        out_specs=[pl.BlockSpec((tm, tn), lambda i, j: (i, j))] + [r.out_spec(flat) for r in riders],
        out_shape=[jax.ShapeDtypeStruct((t, n), BF16)] + [r.out_shape for r in riders],
        compiler_params=_cparams(("arbitrary", "arbitrary")),
        name="inproj",
    )(h_bf, w_bf, *[r.src for r in riders])


def _gated_conv(b_ref, c_ref, v_ref, w_ref, cb_ref, u_ref, s, tm):
    u = c_ref[...].astype(F32) * v_ref[...].astype(F32)

    @pl.when(s == 0)
    def _():
        u_ref[0:SUBLANE, :] = jnp.zeros((SUBLANE, u.shape[1]), F32)

    @pl.when(s > 0)
    def _():
        u_ref[0:SUBLANE, :] = u_ref[tm:tm + SUBLANE, :]

    u_ref[SUBLANE:SUBLANE + tm, :] = u
    conv = (cb_ref[...]
            + u_ref[SUBLANE - 2:SUBLANE - 2 + tm, :] * w_ref[0:1, :]
            + u_ref[SUBLANE - 1:SUBLANE - 1 + tm, :] * w_ref[1:2, :]
            + u * w_ref[2:3, :])
    return b_ref[...].astype(F32) * conv


def _retention_kernel(q_ref, k_ref, v_ref, g_ref, cos_ref, sin_ref, din_ref, dcr_ref, dst_ref,
                      o_ref, *st_refs, nc, ck, dk):
    half = dk // 2
    for st_ref in st_refs:
        st_ref[...] = jnp.zeros(st_ref.shape, F32)

    def rot(x, cos, sin):
        x1, x2 = x[:, :half], x[:, half:]
        return jnp.concatenate([x1 * cos - x2 * sin, x1 * sin + x2 * cos], axis=-1)

    def body(c, carry):
        r0 = pl.multiple_of(c * ck, ck)
        cos = cos_ref[pl.ds(r0, ck), :]
        sin = sin_ref[pl.ds(r0, ck), :]
        for h, st_ref in enumerate(st_refs):
            cols = slice(h * dk, (h + 1) * dk)
            d_cross = dcr_ref[h]
            g_chunk = dcr_ref[h, ck - 1:ck, :]
            q = rot(q_ref[pl.ds(r0, ck), cols].astype(F32), cos, sin)
            k = rot(k_ref[pl.ds(r0, ck), cols].astype(F32), cos, sin) * (dk ** -0.5)
            v = v_ref[pl.ds(r0, ck), cols]
            intra = _dot_nt(q.astype(BF16), k.astype(BF16)) * din_ref[h]
            state = st_ref[...]
            o = _dot(intra.astype(BF16), v) + _dot((q * d_cross).astype(BF16), state.astype(BF16))
            st_ref[...] = state * g_chunk + _dot_tn((k * dst_ref[h]).astype(BF16), v)
            r = lax.rsqrt(jnp.mean(o * o, axis=-1, keepdims=True) + RMS_EPS)
            gate = _silu(g_ref[pl.ds(r0, ck), cols].astype(F32))
            o_ref[pl.ds(r0, ck), cols] = (o * r * gate).astype(o_ref.dtype)
        return carry

    lax.fori_loop(0, nc, body, 0)


def retention(z, bsz, seq, ch, col0):
    t = z.shape[0]
    nh = RET_HEADS
    dk = ch // nh
    ck = min(RET_CHUNK, seq)
    nc = seq // ck
    half = dk // 2
    pos = jnp.arange(seq, dtype=F32)
    inv = 1.0 / (10000.0 ** jnp.linspace(0.0, 1.0, half, dtype=F32))
    ang = pos[:, None] * inv[None, :]
    cos, sin = jnp.cos(ang), jnp.sin(ang)
    log_g = jnp.log(1.0 - 2.0 ** (-5.0 - jnp.arange(nh, dtype=F32)))
    i = jnp.arange(ck, dtype=F32)
    diff = i[:, None] - i[None, :]
    d_intra = jnp.where(diff >= 0, jnp.exp(log_g[:, None, None] * jnp.maximum(diff, 0.0)), 0.0)
    d_cross = jnp.exp(log_g[:, None] * (i[None, :] + 1.0))[..., None]
    d_state = jnp.exp(log_g[:, None] * (ck - 1.0 - i[None, :]))[..., None]
    cb = col0 // ch
    col = lambda g: (lambda b: (b, cb + g))
    whole = lambda b: (0, 0, 0)
    return pl.pallas_call(
        functools.partial(_retention_kernel, nc=nc, ck=ck, dk=dk),
        grid=(bsz,),
        in_specs=[
            pl.BlockSpec((seq, ch), col(0)),
            pl.BlockSpec((seq, ch), col(1)),
            pl.BlockSpec((seq, ch), col(2)),
            pl.BlockSpec((seq, ch), col(3)),
            pl.BlockSpec((seq, half), lambda b: (0, 0)),
            pl.BlockSpec((seq, half), lambda b: (0, 0)),
            pl.BlockSpec((nh, ck, ck), whole),
            pl.BlockSpec((nh, ck, 1), whole),
            pl.BlockSpec((nh, ck, 1), whole),
        ],
        out_specs=pl.BlockSpec((seq, ch), lambda b: (b, 0)),
        out_shape=jax.ShapeDtypeStruct((t, ch), BF16),
        scratch_shapes=[pltpu.VMEM((dk, dk), F32) for _ in range(nh)],
        compiler_params=_cparams(("parallel",)),
        name="retention",
    )(z, z, z, z, cos, sin, d_intra, d_cross, d_state)


def _route(h, wr_ref, br_ref, ri_ref, rf_ref, cnt_ref, carry_ref, tm):
    logits = _dot_nt(wr_ref[...], h.astype(BF16))
    mx = jnp.max(logits, axis=0, keepdims=True)
    ex = jnp.exp(logits - mx)
    probs = ex / jnp.sum(ex, axis=0, keepdims=True)
    sel = probs + br_ref[...]
    s = [sel[e:e + 1, :] for e in range(N_EXPERTS)]
    p = [probs[e:e + 1, :] for e in range(N_EXPERTS)]
    epg = EXPERTS_PER_GROUP

    def first_argmax(vals, exclude=None):
        best = jnp.full_like(vals[0], -jnp.inf)
        idx = jnp.zeros(vals[0].shape, I32)
        for j, vj in enumerate(vals):
            better = vj > best
            if exclude is not None:
                better = better & (exclude != j)
            idx = jnp.where(better, j, idx)
            best = jnp.where(better, vj, best)
        return idx

    gscore = []
    for g in range(N_GROUPS):
        gs = s[g * epg:(g + 1) * epg]
        best = None
        for a in range(epg):
            for b in range(a + 1, epg):
                pair = gs[a] + gs[b]
                best = pair if best is None else jnp.maximum(best, pair)
        gscore.append(best)
    grp = first_argmax(gscore)

    def pick(rows, index, n):
        out = rows[n - 1]
        for j in range(n - 2, -1, -1):
            out = jnp.where(index == j, rows[j], out)
        return out

    in_s = [pick([s[g * epg + j] for g in range(N_GROUPS)], grp, N_GROUPS) for j in range(epg)]
    in_p = [pick([p[g * epg + j] for g in range(N_GROUPS)], grp, N_GROUPS) for j in range(epg)]
    i1 = first_argmax(in_s)
    i2 = first_argmax(in_s, exclude=i1)
    p1 = pick(in_p, i1, epg)
    p2 = pick(in_p, i2, epg)
    e1 = grp * epg + i1
    e2 = grp * epg + i2
    den = p1 + p2
    g1 = p1 / den
    g2 = p2 / den

    eidx = lax.broadcasted_iota(I32, (N_EXPERTS, tm), 0)
    member = (eidx == e1) | (eidx == e2)
    member_f = jnp.where(member, 1.0, 0.0)
    before = lax.broadcasted_iota(I32, (tm, tm), 0) < lax.broadcasted_iota(I32, (tm, tm), 1)
    prefix = _dot(member_f.astype(BF16), jnp.where(before, 1.0, 0.0).astype(BF16))
    base = prefix + carry_ref[:, 0:1]
    rank1 = jnp.sum(jnp.where(eidx == e1, base, 0.0), axis=0, keepdims=True).astype(I32)
    rank2 = jnp.sum(jnp.where(eidx == e2, base, 0.0), axis=0, keepdims=True).astype(I32)
    carry_ref[...] = carry_ref[...] + jnp.sum(member_f, axis=1, keepdims=True)
    cnt_ref[...] = carry_ref[...].astype(I32)

    zi = jnp.zeros((SUBLANE - 4, tm), I32)
    ri_ref[...] = jnp.concatenate([e1, e2, rank1, rank2, zi], axis=0)
    zf = jnp.zeros((SUBLANE - 2, tm), F32)
    rf_ref[...] = jnp.concatenate([g1, g2, zf], axis=0)


def _mixer_out_kernel(*refs, tm, conv):
    if conv:
        b_ref, c_ref, v_ref, cw_ref, cb_ref, *refs = refs
    else:
        ya_ref, *refs = refs
    (yb_ref, wa_ref, wb_ref, x_ref, modm_ref, modf_ref, nw_ref, wr_ref, br_ref,
     x1_ref, h_ref, ri_ref, rf_ref, cnt_ref, carry_ref, *conv_scratch) = refs
    s = pl.program_id(1)

    @pl.when((pl.program_id(0) == 0) & (s == 0))
    def _():
        carry_ref[...] = jnp.zeros(carry_ref.shape, F32)

    if conv:
        ya = _gated_conv(b_ref, c_ref, v_ref, cw_ref, cb_ref, conv_scratch[0], s, tm).astype(BF16)
    else:
        ya = ya_ref[...]
    y = _dot(ya, wa_ref[...]) + _dot(yb_ref[...], wb_ref[...])
    x1 = x_ref[...] + modm_ref[0, 2:3, :] * y
    x1_ref[...] = x1
    h = _modulated_norm(x1, modf_ref, nw_ref)
    h_ref[...] = _pack_pairs(h)
    _route(h, wr_ref, br_ref, ri_ref, rf_ref, cnt_ref, carry_ref, tm)


def mixer_out(lhs, w_bf, x2, mod_mix, mod_ffn, norm_ffn_w, w_router, b_router, bsz, seq, tm, conv=None):
    t, d = x2.shape
    kh = w_bf.shape[0] // 2
    per = seq // tm
    ne = N_EXPERTS
    row = lambda g: (lambda b, s: (b * per + s, g))
    const = lambda b, s: (0, 0)
    if conv is None:
        (o,) = lhs
        lhs_args = (o, o)
        lhs_specs = [pl.BlockSpec((tm, kh), row(0)), pl.BlockSpec((tm, kh), row(1))]
        scratch = []
    else:
        z, yb = lhs
        conv_w, conv_b = conv
        lhs_args = (z, z, z, conv_w, conv_b.reshape(1, kh), yb)
        lhs_specs = [pl.BlockSpec((tm, kh), row(0)), pl.BlockSpec((tm, kh), row(1)),
                     pl.BlockSpec((tm, kh), row(2)), pl.BlockSpec((CONV_WIDTH, kh), const),
                     pl.BlockSpec((1, kh), const), pl.BlockSpec((tm, kh), row(0))]
        scratch = [pltpu.VMEM((tm + SUBLANE, kh), F32)]
    batch = lambda b, s: (b, 0, 0)
    return pl.pallas_call(
        functools.partial(_mixer_out_kernel, tm=tm, conv=conv is not None),
        grid=(bsz, per),
        in_specs=lhs_specs + [
            pl.BlockSpec((kh, d), const),
            pl.BlockSpec((kh, d), lambda b, s: (1, 0)),
            pl.BlockSpec((tm, d), row(0)),
            pl.BlockSpec((1, 3, d), batch),
            pl.BlockSpec((1, 3, d), batch),
            pl.BlockSpec((1, d), const),
            pl.BlockSpec((ne, d), const),
            pl.BlockSpec((ne, 1), const),
        ],
        out_specs=[
            pl.BlockSpec((tm, d), row(0)),
            pl.BlockSpec((tm, d // 2), row(0)),
            pl.BlockSpec((SUBLANE, tm), lambda b, s: (0, b * per + s)),
            pl.BlockSpec((SUBLANE, tm), lambda b, s: (0, b * per + s)),
            pl.BlockSpec((ne, LANE), const),
        ],
        out_shape=[
            jax.ShapeDtypeStruct((t, d), F32),
            jax.ShapeDtypeStruct((t, d // 2), I32),
            jax.ShapeDtypeStruct((SUBLANE, t), I32),
            jax.ShapeDtypeStruct((SUBLANE, t), F32),
            jax.ShapeDtypeStruct((ne, LANE), I32),
        ],
        scratch_shapes=[pltpu.VMEM((ne, LANE), F32)] + scratch,
        compiler_params=_cparams(("arbitrary", "arbitrary")),
        name="mixer_out",
    )(*lhs_args, w_bf, w_bf, x2, mod_mix, mod_ffn, norm_ffn_w.reshape(1, d),
      w_router.T.astype(BF16), b_router.reshape(ne, 1))


def _dispatch_kernel(dest_ref, pad0_ref, padn_ref, h_ref, xs_ref, hbuf_ref, zero_ref, sem, zsem,
                     *, tm, nt):
    i = pl.program_id(0)
    cur = i % 2
    base = i * tm

    def zero_row(e, r):
        return pltpu.make_async_copy(zero_ref.at[pl.ds(0, 1), :],
                                     xs_ref.at[pl.ds(pad0_ref[e] + r, 1), :], zsem)

    def tile_done(buf):
        for _ in range(2):
            pltpu.make_async_copy(hbuf_ref.at[buf], xs_ref.at[pl.ds(0, tm), :], sem.at[buf]).wait()

    @pl.when(i == 0)
    def _():
        zero_ref[...] = jnp.zeros(zero_ref.shape, zero_ref.dtype)
        for e in range(N_EXPERTS):
            lax.fori_loop(0, padn_ref[e], lambda r, c, e=e: (zero_row(e, r).start(), c)[1], 0)

    hbuf_ref[cur] = h_ref[...]

    def start(r, carry):
        for slot in range(2):
            d = dest_ref[2 * (base + r) + slot]
            pltpu.make_async_copy(hbuf_ref.at[cur, pl.ds(r, 1), :], xs_ref.at[pl.ds(d, 1), :],
                                  sem.at[cur]).start(priority=slot)
        return carry

    lax.fori_loop(0, tm, start, 0, unroll=DMA_UNROLL)

    @pl.when(i > 0)
    def _():
        tile_done(1 - cur)

    @pl.when(i == nt - 1)
    def _():
        tile_done(cur)
        for e in range(N_EXPERTS):
            lax.fori_loop(0, padn_ref[e], lambda r, c, e=e: (zero_row(e, r).wait(), c)[1], 0)


def dispatch(h, dest, pad_start, pad_count, n_rows, tm):
    t, d = h.shape
    return pl.pallas_call(
        functools.partial(_dispatch_kernel, tm=tm, nt=t // tm),
        grid_spec=pltpu.PrefetchScalarGridSpec(
            num_scalar_prefetch=3,
            grid=(t // tm,),
            in_specs=[pl.BlockSpec((tm, d), lambda i, *_: (i, 0))],
            out_specs=pl.BlockSpec(memory_space=pl.ANY),
            scratch_shapes=[pltpu.VMEM((2, tm, d), h.dtype), pltpu.VMEM((SUBLANE, d), h.dtype),
                            pltpu.SemaphoreType.DMA((2,)), pltpu.SemaphoreType.DMA(())],
        ),
        out_shape=jax.ShapeDtypeStruct((n_rows, d), h.dtype),
        compiler_params=_cparams(("arbitrary",)),
        name="moe_dispatch",
    )(dest, pad_start, pad_count, h)


def _ffn_kernel(be_ref, bv_ref, xs_ref, w1_ref, w3_ref, w2_ref, *refs, riders):
    ys_ref = refs[len(riders)]
    _ride_casts(riders, refs[:len(riders)], refs[len(riders) + 1:], pl.program_id(0))
    valid = bv_ref[pl.program_id(0)]

    @pl.when(valid > 0)
    def _():
        rows = lax.broadcasted_iota(I32, xs_ref.shape, 0)
        x = _unpack_pairs(jnp.where(rows < valid, xs_ref[...], 0))
        h1 = _dot(x, w1_ref[0])
        h3 = _dot(x, w3_ref[0])
        a = (_silu(h1) * h3).astype(BF16)
        ys_ref[...] = _dot(a, w2_ref[0])

    @pl.when(valid <= 0)
    def _():
        ys_ref[...] = jnp.zeros(ys_ref.shape, ys_ref.dtype)


def grouped_ffn(xs, block_e, block_valid, w1, w3, w2, bm, riders=()):
    n_rows = xs.shape[0]
    d, f = w1.shape[1], w1.shape[2]
    flat = lambda i, be, bv: i
    return pl.pallas_call(
        functools.partial(_ffn_kernel, riders=tuple(r.nblk for r in riders)),
        grid_spec=pltpu.PrefetchScalarGridSpec(
            num_scalar_prefetch=2,
            grid=(n_rows // bm,),
            in_specs=[
                pl.BlockSpec((bm, d // 2), lambda i, be, bv: (i, 0)),
                pl.BlockSpec((1, d, f), lambda i, be, bv: (be[i], 0, 0)),
                pl.BlockSpec((1, d, f), lambda i, be, bv: (be[i], 0, 0)),
                pl.BlockSpec((1, f, d), lambda i, be, bv: (be[i], 0, 0)),
            ] + [r.in_spec(flat) for r in riders],
            out_specs=[pl.BlockSpec((bm, d), lambda i, be, bv: (i, 0))] + [r.out_spec(flat) for r in riders],
        ),
        out_shape=[jax.ShapeDtypeStruct((n_rows, d), F32)] + [r.out_shape for r in riders],
        compiler_params=_cparams(("arbitrary",)),
        name="moe_ffn",
    )(block_e, block_valid, xs, w1, w3, w2, *[r.src for r in riders])


def _combine_kernel(dest_ref, ys_ref, x_ref, mod_ref, gf_ref, *refs, tm, nt, project):
    if project:
        modn_ref, nwn_ref, w_ref, o_ref, z_ref, a_ref, b_ref, sem = refs
    else:
        o_ref, a_ref, b_ref, sem = refs
    i = pl.program_id(0)
    cur = i % 2

    def fetch_row(base, r, buf):
        for slot, ref in ((0, a_ref), (1, b_ref)):
            d = dest_ref[2 * (base + r) + slot]
            pltpu.make_async_copy(ys_ref.at[pl.ds(d, 1), :], ref.at[buf, pl.ds(r, 1), :],
                                  sem.at[buf]).start(priority=0 if project else slot)

    def fetch(tile, buf):
        lax.fori_loop(0, tm, lambda r, c: (fetch_row(tile * tm, r, buf), c)[1], 0, unroll=DMA_UNROLL)

    def fetched(buf):
        pltpu.make_async_copy(ys_ref.at[pl.ds(0, tm), :], a_ref.at[buf], sem.at[buf]).wait()
        pltpu.make_async_copy(ys_ref.at[pl.ds(0, tm), :], b_ref.at[buf], sem.at[buf]).wait()

    @pl.when(i == 0)
    def _():
        fetch(0, 0)

    if not project:
        @pl.when(i + 1 < nt)
        def _():
            fetch(i + 1, 1 - cur)

    fetched(cur)
    y = gf_ref[:, 0:1] * a_ref[cur] + gf_ref[:, 1:2] * b_ref[cur]
    out = x_ref[...] + mod_ref[0, 2:3, :] * y
    o_ref[...] = out
    if project:
        h = _modulated_norm(out, modn_ref, nwn_ref).astype(BF16)
        base = jnp.minimum(i + 1, nt - 1) * tm
        for r in range(tm):
            fetch_row(base, r, 1 - cur)
        z_ref[...] = _dot(h, w_ref[...]).astype(z_ref.dtype)

        @pl.when(i == nt - 1)
        def _():
            fetched(1 - cur)


def combine(ys, dest, x2, mod, gates_t, seq, tm, project=None):
    t, d = x2.shape
    per = seq // tm
    row = lambda i, dest: (i, 0)
    batch = lambda i, dest: (i // per, 0, 0)
    const = lambda i, dest: (0, 0)
    in_specs = [pl.BlockSpec(memory_space=pl.ANY), pl.BlockSpec((tm, d), row),
                pl.BlockSpec((1, 3, d), batch), pl.BlockSpec((tm, SUBLANE), row)]
    out_specs = [pl.BlockSpec((tm, d), row)]
    out_shape = [jax.ShapeDtypeStruct((t, d), F32)]
    args = [dest, ys, x2, mod, gates_t]
    if project is not None:
        mod_next, nw_next, w_bf = project
        n = w_bf.shape[1]
        in_specs += [pl.BlockSpec((1, 3, d), batch), pl.BlockSpec((1, d), const), pl.BlockSpec((d, n), const)]
        out_specs += [pl.BlockSpec((tm, n), row)]
        out_shape += [jax.ShapeDtypeStruct((t, n), BF16)]
        args += [mod_next, nw_next.reshape(1, d), w_bf]
    return pl.pallas_call(
        functools.partial(_combine_kernel, tm=tm, nt=t // tm, project=project is not None),
        grid_spec=pltpu.PrefetchScalarGridSpec(
            num_scalar_prefetch=1,
            grid=(t // tm,),
            in_specs=in_specs,
            out_specs=out_specs,
            scratch_shapes=[pltpu.VMEM((2, tm, d), F32), pltpu.VMEM((2, tm, d), F32),
                            pltpu.SemaphoreType.DMA((2,))],
        ),
        out_shape=out_shape,
        compiler_params=_cparams(("arbitrary",)),
        name="moe_combine",
    )(*args)


def _moe_blocks(t):
    return (2 * t) // MOE_ROWS + N_EXPERTS


def moe(x2, h, ri, rf, cnt, mod, w1, w3, w2, seq, project=None, riders=()):
    t, d = x2.shape
    bm = MOE_ROWS
    tm = min(256, seq)
    ne = N_EXPERTS
    counts = cnt[:, 0]
    nblk = (counts + bm - 1) // bm
    blk_end = jnp.cumsum(nblk)
    blk_start = blk_end - nblk
    eids = jnp.arange(ne, dtype=I32)
    e12 = ri[0:2].T
    row0 = jnp.sum(jnp.where(e12[..., None] == eids, blk_start * bm, 0), axis=-1)
    dest = (row0 + ri[2:4].T).reshape(2 * t).astype(I32)
    n_blocks = _moe_blocks(t)
    bidx = jnp.arange(n_blocks, dtype=I32)
    block_e = jnp.minimum(jnp.sum(bidx[:, None] >= blk_end[None, :], axis=1), ne - 1).astype(I32)
    onehot = block_e[:, None] == eids
    cnt_b = jnp.sum(jnp.where(onehot, counts, 0), axis=1)
    start_b = jnp.sum(jnp.where(onehot, blk_start, 0), axis=1)
    block_valid = jnp.clip(cnt_b - (bidx - start_b) * bm, 0, bm).astype(I32)
    pad_start = (blk_start * bm + counts).astype(I32)
    pad_end = jnp.where(eids == ne - 1, n_blocks * bm, blk_end * bm)
    xs = dispatch(h, dest, pad_start, (pad_end - pad_start).astype(I32), n_blocks * bm, tm)
    ys, *casts = grouped_ffn(xs, block_e, block_valid, w1, w3, w2, bm, riders)
    return combine(ys, dest, x2, mod, rf.T, seq, tm, project), casts


def _t5_bucket(n):
    max_exact = N_BUCKETS // 2
    nf = jnp.maximum(n, 1).astype(F32)
    large = max_exact + (jnp.log(nf / max_exact) / math.log(MAX_DISTANCE / max_exact)
                         * (N_BUCKETS - max_exact)).astype(I32)
    large = jnp.minimum(large, N_BUCKETS - 1)
    return jnp.where(n < max_exact, n, large)


def _sortable(score):
    bits = pltpu.bitcast(score, I32)
    return bits ^ ((bits >> 31) & 0x7FFFFFFF)


def _dsa_kernel(q_ref, k_ref, v_ref, qi_ref, kw_ref, wq_ref, tb_ref, qn_ref, kn_ref, o_ref,
                khat_ref, vt_ref, qrt_ref, qirt_ref, wt_ref, mt_ref, hi_ref, lo_ref, lo2_ref, nm_ref, lg_ref,
                *acc_refs,
                nh, dh, nih, di, topk, nb):
    j = pl.program_id(1)
    qb = Q_BLOCK
    kc = 2 * qb
    nch = (j + 2) // 2
    pair = 2 * qb

    def transpose(x):
        return x.astype(F32).T

    @pl.when(j == 0)
    def _():
        k = k_ref[...].astype(F32)
        r = lax.rsqrt(jnp.mean(k * k, axis=-1, keepdims=True) + RMS_EPS)
        khat_ref[...] = (k * r * kn_ref[...]).astype(BF16)
        for c in range(nb // 2):
            for s in range(2):
                blk = v_ref[(2 * c + s) * qb:(2 * c + s + 1) * qb, :]
                vt_ref[c, 0:dh, s * qb:(s + 1) * qb] = transpose(blk).astype(BF16)
            vt_ref[c, dh:dh + ONES_ROWS, :] = jnp.ones((ONES_ROWS, kc), BF16)
        for ref in (hi_ref, lo_ref, lo2_ref):
            ref[...] = jnp.full(ref.shape, I16_MIN, I16)

    for h in range(nh):
        qt = transpose(q_ref[:, h * dh:(h + 1) * dh])
        r = lax.rsqrt(jnp.mean(qt * qt, axis=0, keepdims=True) + RMS_EPS)
        qrt_ref[:, h * qb:(h + 1) * qb] = (qt * r * qn_ref[...]).astype(BF16)
    for g in range(nih * di // LANE):
        two = transpose(qi_ref[:, g * LANE:(g + 1) * LANE])
        for s in range(LANE // di):
            h = g * (LANE // di) + s
            qirt_ref[:, h * qb:(h + 1) * qb] = two[s * di:(s + 1) * di, :].astype(BF16)
    wt_ref[...] = transpose(wq_ref[...])

    key_l = lax.broadcasted_iota(I32, (kc, qb), 0)
    q_pos = j * qb + lax.broadcasted_iota(I32, (kc, qb), 1)

    def score_body(c, carry):
        r0 = pl.multiple_of(c * kc, kc)
        ki = kw_ref[pl.ds(r0, kc), 0:di]
        acc = jnp.zeros((kc, qb), F32)
        for g in range(nih // 2):
            rel = _dot(ki, qirt_ref[:, g * pair:(g + 1) * pair])
            for s in range(2):
                h = 2 * g + s
                acc = acc + wt_ref[di + h:di + h + 1, :] * jnp.maximum(rel[:, s * qb:(s + 1) * qb], 0.0)
        m = jnp.where(r0 + key_l <= q_pos, _sortable(acc), INT_MIN)
        mt_ref[c] = m
        hi_ref[c] = (m >> 16).astype(I16)
        lo_ref[c] = ((m & 0xFFFF) + I16_MIN).astype(I16)
        return carry

    def score_many(t, carry, n):
        for k in range(n):
            score_body(n * t + k, carry)
        return carry

    lax.fori_loop(0, nch // 4, functools.partial(score_many, n=4), 0)
    lax.fori_loop(2 * (nch // 4), nch // 2, functools.partial(score_many, n=2), 0)
    lax.fori_loop(2 * (nch // 2), nch, score_body, 0)

    def count_ge(cand):
        def cbody(c, acc):
            hit = jnp.where(mt_ref[c] >= cand, 1, 0)
            return acc + jnp.sum(hit.reshape(kc // SUBLANE, SUBLANE, qb), axis=0)
        acc = lax.fori_loop(0, nch, cbody, jnp.zeros((SUBLANE, qb), I32))
        return jnp.sum(acc, axis=0, keepdims=True)

    def fold16(hit):
        parts = [hit[PACK16 * i:PACK16 * (i + 1), :] for i in range(kc // PACK16)]
        while len(parts) > 1:
            parts = [parts[i] + parts[i + 1] for i in range(0, len(parts), 2)]
        return parts[0]

    def total16(acc):
        return jnp.sum(acc.astype(F32), axis=0, keepdims=True).astype(I32)

    one, nil = jnp.ones((), BF16), jnp.zeros((), BF16)

    def count16(src_ref, cand, n):
        c16 = cand.astype(I16)
        acc = fold16(jnp.where(src_ref[0] >= c16, one, nil))
        for c in range(1, n):
            acc = acc + fold16(jnp.where(src_ref[c] >= c16, one, nil))
        return total16(acc)

    def search16(src_ref, kth, n):
        zero = jnp.zeros((1, qb), I32)
        ans0 = jnp.where(count16(src_ref, zero, n) >= kth, zero, I16_MIN)

        def bit_body(bi, ans):
            cand = ans | (1 << (14 - bi))
            return jnp.where(count16(src_ref, cand, n) >= kth, cand, ans)

        return lax.fori_loop(0, 15, bit_body, ans0)

    def search(n):
        top = search16(hi_ref, jnp.full((1, qb), topk, I32), n)
        top16 = top.astype(I16)
        acc = jnp.zeros((PACK16, qb), BF16)
        for c in range(n):
            hi = hi_ref[c]
            lo2_ref[c] = jnp.where(hi == top16, lo_ref[c], jnp.full((), I16_MIN, I16))
            acc = acc + fold16(jnp.where(hi > top16, one, nil))
        above = total16(acc)
        low = search16(lo2_ref, topk - above, n)
        return top * 65536 + (low - I16_MIN), above + count16(lo2_ref, low, n)

    def search_any():
        return lax.cond(nch <= nb // 4, functools.partial(search, nb // 4),
                        functools.partial(search, nb // 2))

    selecting = (j + 1) * qb > topk
    thr, n_ge = lax.cond(selecting, search_any,
                         lambda: (jnp.full((1, qb), INT_MIN + 1, I32), jnp.zeros((1, qb), I32)))

    def plain_mask():
        def body(c, carry):
            nm_ref[c] = jnp.where(mt_ref[c] >= thr, 0.0, NEG_BIG)
            return carry
        lax.fori_loop(0, nch, body, 0)

    def tie_mask():
        need = (topk - count_ge(thr + 1)).astype(F32)
        tri = jnp.where(lax.broadcasted_iota(I32, (kc, kc), 1) <= lax.broadcasted_iota(I32, (kc, kc), 0),
                        1.0, 0.0).astype(BF16)

        def body(c, seen):
            m = mt_ref[c]
            tie = m == thr
            upto = _dot(tri, jnp.where(tie, 1.0, 0.0).astype(BF16)) + seen
            nm_ref[c] = jnp.where((m > thr) | (tie & (upto <= need)), 0.0, NEG_BIG)
            return upto[kc - 1:kc, :]
        lax.fori_loop(0, nch, body, jnp.zeros((1, qb), F32))

    crowded = jnp.max(n_ge) > topk
    lax.cond(crowded, tie_mask, plain_mask)

    npair = nh // 2
    pcols = [slice(pr * pair, (pr + 1) * pair) for pr in range(npair)]
    nfar = jnp.maximum((j - 1) // 2, 0)

    def sweep1(c, mruns, near):
        r0 = pl.multiple_of(c * kc, kc)
        kh = khat_ref[pl.ds(r0, kc), :]
        nm = nm_ref[c]
        nm2 = jnp.concatenate([nm, nm], axis=1)
        if near:
            k0 = jnp.clip(j - 2 * c, 0, 2)
            k1 = jnp.clip(j - 2 * c - 1, 0, 2)
        out = []
        for pr in range(npair):
            lg = _dot(kh, qrt_ref[:, pcols[pr]]) + nm2
            if near:
                lg = lg + jnp.concatenate([tb_ref[k0, :, pcols[pr]], tb_ref[k1, :, pcols[pr]]], axis=0)
            lg_ref[c, :, pcols[pr]] = lg
            out.append(jnp.maximum(mruns[pr], jnp.max(lg.reshape(kc // SUBLANE, SUBLANE, pair), axis=0)))
        return tuple(out)

    def sweep1_many(t, mruns, n):
        for k in range(n):
            mruns = sweep1(n * t + k, mruns, near=False)
        return mruns

    mruns = tuple(jnp.full((SUBLANE, pair), NEG_BIG, F32) for _ in range(npair))
    mruns = lax.fori_loop(0, nfar // 4, functools.partial(sweep1_many, n=4), mruns)
    mruns = lax.fori_loop(2 * (nfar // 4), nfar // 2, functools.partial(sweep1_many, n=2), mruns)
    mruns = lax.fori_loop(2 * (nfar // 2), nfar, functools.partial(sweep1, near=False), mruns)
    mruns = lax.fori_loop(nfar, nch, functools.partial(sweep1, near=True), mruns)
    mrows = [jnp.max(m, axis=0, keepdims=True) for m in mruns]
    for acc_ref in acc_refs:
        acc_ref[...] = jnp.zeros(acc_ref.shape, F32)

    def sweep2(c, carry):
        vt = vt_ref[c]
        for pr in range(npair):
            p = jnp.exp2((lg_ref[c, :, pcols[pr]] - mrows[pr]).astype(BF16))
            acc_refs[pr][...] += _dot(vt, p)
        return carry

    def sweep2_many(t, carry, n):
        vts = [vt_ref[n * t + k] for k in range(n)]
        for pr in range(npair):
            part = None
            for k in range(n):
                p = jnp.exp2((lg_ref[n * t + k, :, pcols[pr]] - mrows[pr]).astype(BF16))
                part = _dot(vts[k], p) if part is None else part + _dot(vts[k], p)
            acc_refs[pr][...] += part
        return carry

    lax.fori_loop(0, nch // 4, functools.partial(sweep2_many, n=4), 0)
    lax.fori_loop(2 * (nch // 4), nch // 2, functools.partial(sweep2_many, n=2), 0)
    lax.fori_loop(2 * (nch // 2), nch, sweep2, 0)
    for pr in range(npair):
        out = acc_refs[pr][0:dh, :] / acc_refs[pr][dh:dh + 1, :]
        for s in range(2):
            h = 2 * pr + s
            o_ref[:, h * dh:(h + 1) * dh] = transpose(
                out[:, s * qb:(s + 1) * qb].astype(BF16)).astype(o_ref.dtype)


def dsa_attention(z, rel_bias, q_norm, k_norm, bsz, seq, d):
    t = z.shape[0]
    nh, dh, nih, di = ATT_HEADS, d // ATT_HEADS, IDX_HEADS, IDX_DIM
    qb = Q_BLOCK
    nb = seq // qb
    topk = min(DSA_TOPK_MAX, seq // 4)
    assert topk % qb == 0 and dh == LANE and nih * di == d // 2 and nb % 2 == 0 and LANE % di == 0
    assert (nb // 2) * (2 * qb // PACK16) <= 256
    i = jnp.arange(qb)
    dist = i[None, :] - i[:, None]
    def lookup(bucket):
        hot = bucket[..., None] == jnp.arange(N_BUCKETS, dtype=I32)
        return jnp.sum(jnp.where(hot[..., None], rel_bias, 0.0), axis=-2)

    tabs = [lookup(_t5_bucket(jnp.maximum(dist + off * qb, 0))) for off in range(2)]
    far = jnp.broadcast_to(lookup(_t5_bucket(jnp.full((), 2 * qb, I32))), (qb, qb, nh))
    tb = jnp.stack(tabs + [far]).transpose(0, 1, 3, 2).reshape(3, qb, nh * qb).astype(F32)
    tb = (tb - tb[2:3]) * math.log2(math.e)
    qn_rows = jnp.broadcast_to((q_norm * (dh ** -0.5 * math.log2(math.e)))[:, None], (dh, qb)).astype(F32)
    cq = d // LANE
    return pl.pallas_call(
        functools.partial(_dsa_kernel, nh=nh, dh=dh, nih=nih, di=di, topk=topk, nb=nb),
        grid=(bsz, nb),
        in_specs=[
            pl.BlockSpec((qb, d), lambda b, j: (b * nb + j, 0)),
            pl.BlockSpec((seq, LANE), lambda b, j: (b, cq + cq // 2)),
            pl.BlockSpec((seq, LANE), lambda b, j: (b, cq + cq // 2 + 1)),
            pl.BlockSpec((qb, d // 2), lambda b, j: (b * nb + j, 2)),
            pl.BlockSpec((seq, LANE), lambda b, j: (b, cq + cq // 2 + 2)),
            pl.BlockSpec((qb, LANE), lambda b, j: (b * nb + j, cq + cq // 2 + 2)),
            pl.BlockSpec((3, qb, nh * qb), lambda b, j: (0, 0, 0)),
            pl.BlockSpec((dh, qb), lambda b, j: (0, 0)),
            pl.BlockSpec((1, dh), lambda b, j: (0, 0)),
        ],
        out_specs=pl.BlockSpec((qb, d), lambda b, j: (b * nb + j, 0)),
        out_shape=jax.ShapeDtypeStruct((t, d), BF16),
        scratch_shapes=[
            pltpu.VMEM((seq, dh), BF16),
            pltpu.VMEM((nb // 2, dh + ONES_ROWS, 2 * qb), BF16),
            pltpu.VMEM((dh, nh * qb), BF16),
            pltpu.VMEM((di, nih * qb), BF16),
            pltpu.VMEM((LANE, qb), F32),
            pltpu.VMEM((nb // 2, 2 * qb, qb), I32),
            pltpu.VMEM((nb // 2, 2 * qb, qb), I16),
            pltpu.VMEM((nb // 2, 2 * qb, qb), I16),
            pltpu.VMEM((nb // 2, 2 * qb, qb), I16),
            pltpu.VMEM((nb // 2, 2 * qb, qb), F32),
            pltpu.VMEM((nb // 2, 2 * qb, nh * qb), F32),
        ] + [pltpu.VMEM((dh + ONES_ROWS, 2 * qb), F32)
             for _ in range(nh // 2)],
        compiler_params=_cparams(("arbitrary", "arbitrary")),
        name="dsa_attention",
    )(z, z, z, z, z, z, tb, qn_rows, k_norm.reshape(1, dh))


def _dsa_weight(c_w_in, d):
    dh = d // ATT_HEADS
    nqi = IDX_HEADS * IDX_DIM
    q, k, v, qi, ki, wi = jnp.split(
        c_w_in, [d, d + dh, d + 2 * dh, d + 2 * dh + nqi, d + 2 * dh + nqi + IDX_DIM], axis=1)
    used = d + nqi + 2 * dh + IDX_DIM + IDX_HEADS
    total = -(-used // 512) * 512
    pad = jnp.zeros((d, total - used), c_w_in.dtype)
    return jnp.concatenate([q, qi, k, v, ki, wi, pad], axis=1).astype(BF16)


def kernel(x, c, rel_bias, w_router, b_router, norm_mix, ada_w_mix, ada_b_mix, norm_ffn, ada_w_ffn,
           ada_b_ffn, ab_w_in, ab_conv_w, ab_conv_b, ab_w_out, c_w_in, c_q_norm, c_k_norm, c_w_out,
           moe_w1, moe_w3, moe_w2):
    bsz, seq, d = x.shape
    depth = norm_mix.shape[0]
    t = bsz * seq
    ch = d // 2
    tm_out = min(512, seq)
    x2 = x.reshape(t, d)
    mod_mix = ada_mod(c, ada_w_mix, ada_b_mix).reshape(depth, bsz, 3, d)
    mod_ffn = ada_mod(c, ada_w_ffn, ada_b_ffn).reshape(depth, bsz, 3, d)
    experts = (moe_w1, moe_w3, moe_w2)
    as_experts = lambda casts: [cw.reshape(w.shape[1:]) for cw, w in zip(casts, experts)]
    w_in = [ab_w_in[l // 2].astype(BF16) if l % 2 == 0 else _dsa_weight(c_w_in[l // 2], d)
            for l in range(depth)]
    tm_in, tn_in = min(1024, seq), 1024
    z = None
    w_moe = None
    for layer in range(depth):
        i = layer // 2
        if z is None:
            riders = _cast_riders(experts, layer, (t // tm_in) * (w_in[layer].shape[1] // tn_in))
            h_mix = prenorm(x2, mod_mix[layer], norm_mix[layer], seq, tm_out)
            z, *casts = inproj(h_mix, w_in[layer], tm_in, tn_in, riders or ())
            if riders:
                w_moe = as_experts(casts)
        if w_moe is None:
            w_moe = [w[layer].astype(BF16) for w in experts]
        if layer % 2 == 0:
            y_b = retention(z, bsz, seq, ch, 3 * ch)
            lhs, w_out, conv = (z, y_b), ab_w_out[i], (ab_conv_w[i], ab_conv_b[i])
        else:
            o = dsa_attention(z, rel_bias, c_q_norm[i], c_k_norm[i], bsz, seq, d)
            lhs, w_out, conv = (o,), c_w_out[i], None
        x2, h, ri, rf, cnt = mixer_out(lhs, w_out.astype(BF16), x2, mod_mix[layer], mod_ffn[layer],
                                       norm_ffn[layer], w_router, b_router, bsz, seq, tm_out, conv)
        last = layer + 1 == depth
        nxt = None if last else (mod_mix[layer + 1], norm_mix[layer + 1], w_in[layer + 1])
        riders = None if last else _cast_riders(experts, layer + 1, _moe_blocks(t))
        res, casts = moe(x2, h, ri, rf, cnt, mod_ffn[layer], *w_moe, seq, nxt, riders or ())
        x2, z = (res[0], None) if last else (res[0], res[1])
        w_moe = as_experts(casts) if riders else None
    return x2.reshape(bsz, seq, d)
```
